```python
import jax
import jax.numpy as jnp
from jax import lax
import numpy as np

D_MODEL = 1024
BATCH = 8
SEQ = 4096
DEPTH = 2

GRID_W = 64
CTX_LEN = 256
N_EVEN = (DEPTH + 1) // 2
N_ODD = DEPTH // 2

MLA_HEADS = 8
MLA_Q_LORA = 256
MLA_KV_LORA = 128
MLA_NOPE = 64
MLA_ROPE = 32
MLA_V = 64
MLA_WIDTH = MLA_HEADS * MLA_V
ML_HEADS = 4
ML_WIDTH = D_MODEL // 2
ML_HEAD_DIM = ML_WIDTH // ML_HEADS
ML_CONV = 3
ML_CHUNK = 128
AB_IN = MLA_Q_LORA + MLA_KV_LORA + MLA_ROPE + MLA_WIDTH + 2 * ML_WIDTH
AB_MIX = MLA_WIDTH + ML_WIDTH

NA_HEADS = 8
NA_HEAD_DIM = 64
NA_WIDTH = NA_HEADS * NA_HEAD_DIM
WIN_R = 8
WIN_C = 16
GQA_HEADS = 8
GQA_KV_HEADS = 2
GQA_HEAD_DIM = 64
GQA_WIDTH = GQA_HEADS * GQA_HEAD_DIM
GQA_KV_WIDTH = GQA_KV_HEADS * GQA_HEAD_DIM
CD_IN = 4 * NA_WIDTH + GQA_WIDTH + 2 * GQA_KV_WIDTH + GQA_WIDTH
CD_MIX = NA_WIDTH + GQA_WIDTH

Q_BLOCK = 128
ROPE_BASE = 10000.0
EPS = 1e-6

kernel_name = 'hybrid_mla_mlstm_natten_gqa_dit'


def rmsnorm(x, g):
    xf = x.astype(jnp.float32)
    y = xf * lax.rsqrt(jnp.mean(xf * xf, axis=-1, keepdims=True) + EPS)
    return (y * g.astype(jnp.float32)).astype(x.dtype)


def axial_rope_tables(n_tokens, rot_dim):
    t = jnp.arange(n_tokens)
    pos = jnp.stack([t // GRID_W, t % GRID_W], axis=-1).astype(jnp.float32)
    n_freq = rot_dim // 4
    inv = ROPE_BASE ** (-jnp.arange(n_freq, dtype=jnp.float32) / n_freq)
    ang = pos[:, :, None] * inv
    return jnp.cos(ang), jnp.sin(ang)


def apply_axial_rope(x, cos, sin):
    b, t, h, r = x.shape
    nq = r // 4
    xs = x.reshape(b, t, h, 2, r // 2)
    x1, x2 = xs[..., :nq], xs[..., nq:]
    cs = cos[:, None].astype(x.dtype)
    sn = sin[:, None].astype(x.dtype)
    out = jnp.concatenate([x1 * cs - x2 * sn, x2 * cs + x1 * sn], axis=-1)
    return out.reshape(b, t, h, r)


def blocked_attention(q, k, v):
    b, t, hk, g, d = q.shape
    qb = min(Q_BLOCK, t)
    nb = t // qb
    scale = d ** -0.5
    q_blocks = jnp.moveaxis(q.reshape(b, nb, qb, hk, g, d), 1, 0)

    def one_block(qblk):
        s = jnp.einsum('bqhgd,bkhd->bhgqk', qblk, k).astype(jnp.float32) * scale
        p = jax.nn.softmax(s, axis=-1).astype(v.dtype)
        return jnp.einsum('bhgqk,bkhd->bqhgd', p, v)

    o = lax.map(one_block, q_blocks)
    return jnp.moveaxis(o, 0, 1).reshape(b, t, hk, g, v.shape[-1])


def mla_qkv(p, g_q, w_uq, g_kv, w_ukv, rope):
    b, t, _ = p.shape
    o1 = MLA_Q_LORA
    o2 = o1 + MLA_KV_LORA
    cq = rmsnorm(p[..., :o1], g_q)
    ckv = rmsnorm(p[..., o1:o2], g_kv)
    k_r = p[..., o2:o2 + MLA_ROPE][:, :, None]
    q = (cq @ w_uq).reshape(b, t, MLA_HEADS, MLA_NOPE + MLA_ROPE)
    kv = (ckv @ w_ukv).reshape(b, t, MLA_HEADS, MLA_NOPE + MLA_V)
    q_n, q_r = q[..., :MLA_NOPE], q[..., MLA_NOPE:]
    if rope is not None:
        q_r = apply_axial_rope(q_r, rope[0], rope[1])
        k_r = apply_axial_rope(k_r, rope[0], rope[1])
    q = jnp.concatenate([q_n, q_r], axis=-1)
    k = jnp.concatenate([kv[..., :MLA_NOPE], jnp.broadcast_to(k_r, (b, t, MLA_HEADS, MLA_ROPE))], axis=-1)
    v = kv[..., MLA_NOPE:]
    return q, k, v


def short_conv(u, w, bias):
    t = u.shape[1]
    pad = ML_CONV // 2
    up = jnp.pad(u, ((0, 0), (pad, ML_CONV - 1 - pad), (0, 0)))
    acc = up[:, 0:t] * w[0]
    for j in range(1, ML_CONV):
        acc = acc + up[:, j:j + t] * w[j]
    return acc + bias


def mlstm_inputs(u, conv_w, conv_b, w_q, w_k, w_v):
    b, t, _ = u.shape
    xc = jax.nn.silu(short_conv(u, conv_w, conv_b))
    xh = xc.reshape(b, t, ML_HEADS, ML_HEAD_DIM)
    q = jnp.einsum('bthd,hde->bthe', xh, w_q)
    k = jnp.einsum('bthd,hde->bthe', xh, w_k)
    v = jnp.einsum('bthd,hde->bthe', u.reshape(b, t, ML_HEADS, ML_HEAD_DIM), w_v)
    gate_in = jnp.concatenate([q, k, v], axis=-1).reshape(b, t, 3 * ML_WIDTH)
    q_h = jnp.swapaxes(q, 1, 2).astype(jnp.float32)
    k_h = jnp.swapaxes(k, 1, 2).astype(jnp.float32) * (ML_HEAD_DIM ** -0.5)
    v_h = jnp.swapaxes(v, 1, 2).astype(jnp.float32)
    return xc, q_h, k_h, v_h, gate_in


def mlstm_gates(gate_in, w, bias):
    pre = (gate_in @ w + bias).astype(jnp.float32)
    ig = pre[..., :ML_HEADS]
    lf = jax.nn.log_sigmoid(pre[..., ML_HEADS:])
    return jnp.swapaxes(ig, 1, 2), jnp.swapaxes(lf, 1, 2)


def mlstm_chunkwise(q, k, v, ig, lf, state):
    b, h, t, d = q.shape
    L = min(ML_CHUNK, t)
    nc = t // L

    def chunks(a):
        return jnp.moveaxis(a.reshape(a.shape[:2] + (nc, L) + a.shape[3:]), 2, 0)

    tril = jnp.tril(jnp.ones((L, L), dtype=bool))

    def step(carry, inp):
        C, n, m = carry
        qc, kc, vc, ic, fc = inp
        bcum = jnp.cumsum(fc, axis=-1)
        dmat = jnp.where(tril, bcum[..., :, None] - bcum[..., None, :] + ic[..., None, :], -jnp.inf)
        m_inter = bcum + m[..., None]
        m_t = jnp.maximum(m_inter, jnp.max(dmat, axis=-1))
        w_inter = jnp.exp(m_inter - m_t)
        s = jnp.exp(dmat - m_t[..., None]) * jnp.einsum('bhld,bhsd->bhls', qc, kc)
        num = w_inter[..., None] * jnp.einsum('bhvk,bhlk->bhlv', C, qc) + jnp.einsum('bhls,bhsv->bhlv', s, vc)
        den = w_inter * jnp.einsum('bhk,bhlk->bhl', n, qc) + jnp.sum(s, axis=-1)
        h_out = num / jnp.maximum(jnp.abs(den), jnp.exp(-m_t))[..., None]
        b_last = bcum[..., -1]
        g = b_last[..., None] - bcum + ic
        m_new = jnp.maximum(b_last + m, jnp.max(g, axis=-1))
        a = jnp.exp(b_last + m - m_new)
        w = jnp.exp(g - m_new[..., None])
        C_new = a[..., None, None] * C + jnp.einsum('bhl,bhlv,bhlk->bhvk', w, vc, kc)
        n_new = a[..., None] * n + jnp.einsum('bhl,bhlk->bhk', w, kc)
        return (C_new, n_new, m_new), h_out

    xs = (chunks(q), chunks(k), chunks(v), chunks(ig), chunks(lf))
    state_out, hs = lax.scan(step, state, xs)
    return jnp.moveaxis(hs, 0, 2).reshape(b, h, t, d), state_out


def mlstm_output(h, xc, z, g_head, skip):
    b, nh, t, d = h.shape
    h = jnp.swapaxes(h, 1, 2)
    mu = jnp.mean(h, axis=-1, keepdims=True)
    var = jnp.mean(jnp.square(h - mu), axis=-1, keepdims=True)
    hn = (h - mu) * lax.rsqrt(var + EPS) * g_head.astype(jnp.float32).reshape(nh, d)
    hn = hn.reshape(b, t, nh * d).astype(xc.dtype)
    return (hn + skip * xc) * jax.nn.silu(z)


def mlstm_branch(uc, zc, ux, zx, conv_w, conv_b, w_q, w_k, w_v, w_gate, b_gate, g_head, skip, need_ctx):
    xc_c, q_c, k_c, v_c, gin_c = mlstm_inputs(uc, conv_w, conv_b, w_q, w_k, w_v)
    xc_x, q_x, k_x, v_x, gin_x = mlstm_inputs(ux, conv_w, conv_b, w_q, w_k, w_v)
    b = ux.shape[0]
    d = ML_HEAD_DIM
    h_ctx_dirs = []
    h_lat_dirs = []
    for direction in range(2):
        i_c, f_c = mlstm_gates(gin_c, w_gate[direction], b_gate[direction])
        i_x, f_x = mlstm_gates(gin_x, w_gate[direction], b_gate[direction])
        seq_c = [q_c, k_c, v_c, i_c, f_c]
        seq_x = [q_x, k_x, v_x, i_x, f_x]
        if direction == 1:
            seq_c = [jnp.flip(a, axis=2) for a in seq_c]
            seq_x = [jnp.flip(a, axis=2) for a in seq_x]
        state0 = (jnp.zeros((b, ML_HEADS, d, d), jnp.float32),
                  jnp.zeros((b, ML_HEADS, d), jnp.float32),
                  jnp.zeros((b, ML_HEADS), jnp.float32))
        h_c, state_ctx = mlstm_chunkwise(seq_c[0], seq_c[1], seq_c[2], seq_c[3], seq_c[4], state0)
        h_x, _ = mlstm_chunkwise(seq_x[0], seq_x[1], seq_x[2], seq_x[3], seq_x[4], state_ctx)
        if direction == 1:
            h_c = jnp.flip(h_c, axis=2)
            h_x = jnp.flip(h_x, axis=2)
        h_ctx_dirs.append(h_c)
        h_lat_dirs.append(h_x)
    out_x = mlstm_output(h_lat_dirs[0] + h_lat_dirs[1], xc_x, zx, g_head, skip)
    out_c = mlstm_output(h_ctx_dirs[0] + h_ctx_dirs[1], xc_c, zc, g_head, skip) if need_ctx else None
    return out_x, out_c


def even_layer(hx, hc, w_in, w_out, mla_g_q, mla_w_uq, mla_g_kv, mla_w_ukv,
               ml_conv_w, ml_conv_b, ml_w_q, ml_w_k, ml_w_v, ml_w_gate, ml_b_gate, ml_g_head, ml_skip,
               rope, need_ctx):
    b, s, _ = hx.shape
    px = hx @ w_in
    pc = hc @ w_in
    s1 = MLA_Q_LORA + MLA_KV_LORA + MLA_ROPE
    s2 = s1 + MLA_WIDTH
    s3 = s2 + ML_WIDTH
    qx, kx, vx = mla_qkv(px[..., :s1], mla_g_q, mla_w_uq, mla_g_kv, mla_w_ukv, rope)
    qc, kc, vc = mla_qkv(pc[..., :s1], mla_g_q, mla_w_uq, mla_g_kv, mla_w_ukv, None)
    k_all = jnp.concatenate([kx, kc], axis=1)
    v_all = jnp.concatenate([vx, vc], axis=1)
    ax = blocked_attention(qx[:, :, :, None], k_all, v_all).reshape(b, s, MLA_WIDTH) * jax.nn.silu(px[..., s1:s2])
    bx, bc = mlstm_branch(pc[..., s2:s3], pc[..., s3:], px[..., s2:s3], px[..., s3:],
                          ml_conv_w, ml_conv_b, ml_w_q, ml_w_k, ml_w_v, ml_w_gate, ml_b_gate,
                          ml_g_head, ml_skip, need_ctx)
    yx = jnp.concatenate([ax, bx], axis=-1) @ w_out
    yc = None
    if need_ctx:
        nctx = hc.shape[1]
        ac = blocked_attention(qc[:, :, :, None], kc, vc).reshape(b, nctx, MLA_WIDTH) * jax.nn.silu(pc[..., s1:s2])
        yc = jnp.concatenate([ac, bc], axis=-1) @ w_out
    return yx, yc


def neighborhood_attention(q, k, v, k_ctx, v_ctx, rpb):
    b, s, h, d = q.shape
    rows = s // GRID_W
    kr = min(WIN_R, rows)
    kc_ = WIN_C
    n_loc = kr * kc_
    qg = q.reshape(b, rows, GRID_W, h, d)
    kg = k.reshape(b, rows, GRID_W, h, d)
    vg = v.reshape(b, rows, GRID_W, h, d)
    col = jnp.arange(GRID_W)
    col_idx = jnp.clip(col - kc_ // 2, 0, GRID_W - kc_)[:, None] + jnp.arange(kc_)[None]
    col_rel = col_idx - col[:, None] + (WIN_C - 1)
    rpb_f = rpb.astype(jnp.float32)
    scale = d ** -0.5

    def one_row(r):
        rs = jnp.clip(r - kr // 2, 0, rows - kr)
        q_r = lax.dynamic_index_in_dim(qg, r, axis=1, keepdims=False)
        k_win = lax.dynamic_slice_in_dim(kg, rs, kr, axis=1)[:, :, col_idx]
        v_win = lax.dynamic_slice_in_dim(vg, rs, kr, axis=1)[:, :, col_idx]
        row_rel = rs + jnp.arange(kr) - r + (WIN_R - 1)
        bias = jnp.transpose(rpb_f[:, row_rel][:, :, col_rel], (0, 2, 1, 3))
        s_loc = jnp.einsum('bqhd,brqkhd->bhqrk', q_r, k_win).astype(jnp.float32) * scale + bias
        s_ctx = jnp.einsum('bqhd,bchd->bhqc', q_r, k_ctx).astype(jnp.float32) * scale
        p = jax.nn.softmax(jnp.concatenate([s_loc.reshape(b, h, GRID_W, n_loc), s_ctx], axis=-1), axis=-1)
        p = p.astype(v.dtype)
        o = jnp.einsum('bhqrk,brqkhd->bqhd', p[..., :n_loc].reshape(b, h, GRID_W, kr, kc_), v_win)
        return o + jnp.einsum('bhqc,bchd->bqhd', p[..., n_loc:], v_ctx)

    out = lax.map(one_row, jnp.arange(rows))
    return jnp.moveaxis(out, 0, 1).reshape(b, s, h, d)


def gqa_qkv(p, g_q, g_k, rope):
    b, t, _ = p.shape
    q = rmsnorm(p[..., :GQA_WIDTH].reshape(b, t, GQA_HEADS, GQA_HEAD_DIM), g_q)
    k = rmsnorm(p[..., GQA_WIDTH:GQA_WIDTH + GQA_KV_WIDTH].reshape(b, t, GQA_KV_HEADS, GQA_HEAD_DIM), g_k)
    v = p[..., GQA_WIDTH + GQA_KV_WIDTH:].reshape(b, t, GQA_KV_HEADS, GQA_HEAD_DIM)
    if rope is not None:
        q = apply_axial_rope(q, rope[0], rope[1])
        k = apply_axial_rope(k, rope[0], rope[1])
    q = q.reshape(b, t, GQA_KV_HEADS, GQA_HEADS // GQA_KV_HEADS, GQA_HEAD_DIM)
    return q, k, v


def odd_layer(hx, hc, w_in, w_out, rpb, g_q, g_k, rope, need_ctx):
    b, s, _ = hx.shape
    px = hx @ w_in
    pc = hc @ w_in
    n = NA_WIDTH
    g0 = 4 * n
    g1 = g0 + GQA_WIDTH + 2 * GQA_KV_WIDTH

    def na_heads(p, j):
        return p[..., j * n:(j + 1) * n].reshape(p.shape[0], p.shape[1], NA_HEADS, NA_HEAD_DIM)

    qx_c, kx_c, vx_c = na_heads(px, 0), na_heads(px, 1), na_heads(px, 2)
    qc_c, kc_c, vc_c = na_heads(pc, 0), na_heads(pc, 1), na_heads(pc, 2)
    cx = neighborhood_attention(qx_c, kx_c, vx_c, kc_c, vc_c, rpb).reshape(b, s, NA_WIDTH)
    cx = cx * jax.nn.silu(px[..., 3 * n:g0])
    qx_d, kx_d, vx_d = gqa_qkv(px[..., g0:g1], g_q, g_k, rope)
    qc_d, kc_d, vc_d = gqa_qkv(pc[..., g0:g1], g_q, g_k, None)
    dx = blocked_attention(qx_d, jnp.concatenate([kx_d, kc_d], axis=1), jnp.concatenate([vx_d, vc_d], axis=1))
    dx = dx.reshape(b, s, GQA_WIDTH) * jax.nn.silu(px[..., g1:])
    yx = jnp.concatenate([cx, dx], axis=-1) @ w_out
    yc = None
    if need_ctx:
        nctx = hc.shape[1]
        cc = blocked_attention(qc_c[:, :, :, None], kc_c, vc_c).reshape(b, nctx, NA_WIDTH) * jax.nn.silu(pc[..., 3 * n:g0])
        dc = blocked_attention(qc_d, kc_d, vc_d).reshape(b, nctx, GQA_WIDTH) * jax.nn.silu(pc[..., g1:])
        yc = jnp.concatenate([cc, dc], axis=-1) @ w_out
    return yx, yc


def setup_inputs(seed: int = 0) -> dict:
    key = jax.random.key(seed)
    keys = iter(jax.random.split(key, 40))

    def nrm(shape, scale):
        return jax.random.normal(next(keys), shape, jnp.float32) * scale

    def gain(shape):
        return 1.0 + nrm(shape, 0.05)

    D = D_MODEL
    b_gate = jnp.concatenate([nrm((N_EVEN, 2, ML_HEADS), 0.1),
                              jnp.linspace(3.0, 6.0, ML_HEADS) + nrm((N_EVEN, 2, ML_HEADS), 0.1)], axis=-1)
    return {
        'x': nrm((BATCH, SEQ, D), 1.0),
        'c': nrm((BATCH, D), 1.0),
        'ctx': nrm((BATCH, CTX_LEN, D), 1.0),
        'c_ctx': nrm((D,), 1.0),
        'w_mod': nrm((DEPTH, D, 3 * D), 0.5 * D ** -0.5),
        'b_mod': nrm((DEPTH, 3 * D), 0.02),
        'g_norm': gain((DEPTH, D)),
        'ab_w_in': nrm((N_EVEN, D, AB_IN), D ** -0.5),
        'ab_w_out': nrm((N_EVEN, AB_MIX, D), AB_MIX ** -0.5),
        'mla_g_q': gain((N_EVEN, MLA_Q_LORA)),
        'mla_w_uq': nrm((N_EVEN, MLA_Q_LORA, MLA_HEADS * (MLA_NOPE + MLA_ROPE)), MLA_Q_LORA ** -0.5),
        'mla_g_kv': gain((N_EVEN, MLA_KV_LORA)),
        'mla_w_ukv': nrm((N_EVEN, MLA_KV_LORA, MLA_HEADS * (MLA_NOPE + MLA_V)), MLA_KV_LORA ** -0.5),
        'ml_conv_w': nrm((N_EVEN, ML_CONV, ML_WIDTH), ML_CONV ** -0.5),
        'ml_conv_b': nrm((N_EVEN, ML_WIDTH), 0.02),
        'ml_w_q': nrm((N_EVEN, ML_HEADS, ML_HEAD_DIM, ML_HEAD_DIM), ML_HEAD_DIM ** -0.5),
        'ml_w_k': nrm((N_EVEN, ML_HEADS, ML_HEAD_DIM, ML_HEAD_DIM), ML_HEAD_DIM ** -0.5),
        'ml_w_v': nrm((N_EVEN, ML_HEADS, ML_HEAD_DIM, ML_HEAD_DIM), ML_HEAD_DIM ** -0.5),
        'ml_w_gate': nrm((N_EVEN, 2, 3 * ML_WIDTH, 2 * ML_HEADS), 0.3 * (3 * ML_WIDTH) ** -0.5),
        'ml_b_gate': b_gate,
        'ml_g_head': gain((N_EVEN, ML_WIDTH)),
        'ml_skip': gain((N_EVEN, ML_WIDTH)),
        'cd_w_in': nrm((N_ODD, D, CD_IN), D ** -0.5),
        'cd_w_out': nrm((N_ODD, CD_MIX, D), CD_MIX ** -0.5),
        'na_rpb': nrm((N_ODD, NA_HEADS, 2 * WIN_R - 1, 2 * WIN_C - 1), 0.5),
        'gqa_g_q': gain((N_ODD, GQA_HEAD_DIM)),
        'gqa_g_k': gain((N_ODD, GQA_HEAD_DIM)),
        'g_final': gain((D,)),
    }


def reference(x, c, ctx, c_ctx, w_mod, b_mod, g_norm, ab_w_in, ab_w_out, mla_g_q, mla_w_uq, mla_g_kv,
              mla_w_ukv, ml_conv_w, ml_conv_b, ml_w_q, ml_w_k, ml_w_v, ml_w_gate, ml_b_gate, ml_g_head,
              ml_skip, cd_w_in, cd_w_out, na_rpb, gqa_g_q, gqa_g_k, g_final):
    s = x.shape[1]
    rope_mla = axial_rope_tables(s, MLA_ROPE)
    rope_gqa = axial_rope_tables(s, GQA_HEAD_DIM)
    sc = jax.nn.silu(c)
    sctx = jax.nn.silu(c_ctx)
    ctx_stream = ctx
    for layer in range(DEPTH):
        need_ctx = layer < DEPTH - 1
        shift, scale, gate = jnp.split(sc @ w_mod[layer] + b_mod[layer], 3, axis=-1)
        shift_c, scale_c, gate_c = jnp.split(sctx @ w_mod[layer] + b_mod[layer], 3, axis=-1)
        hx = rmsnorm(x, g_norm[layer]) * (1.0 + scale[:, None]) + shift[:, None]
        hc = rmsnorm(ctx_stream, g_norm[layer]) * (1.0 + scale_c) + shift_c
        i = layer // 2
        if layer % 2 == 0:
            yx, yc = even_layer(hx, hc, ab_w_in[i], ab_w_out[i], mla_g_q[i], mla_w_uq[i], mla_g_kv[i],
                                mla_w_ukv[i], ml_conv_w[i], ml_conv_b[i], ml_w_q[i], ml_w_k[i], ml_w_v[i],
                                ml_w_gate[i], ml_b_gate[i], ml_g_head[i], ml_skip[i], rope_mla, need_ctx)
        else:
            yx, yc = odd_layer(hx, hc, cd_w_in[i], cd_w_out[i], na_rpb[i], gqa_g_q[i], gqa_g_k[i],
                               rope_gqa, need_ctx)
        x = x + gate[:, None] * yx
        if need_ctx:
            ctx_stream = ctx_stream + gate_c * yc
    return rmsnorm(x, g_final)
```

```python
import functools

import jax
import jax.numpy as jnp
from jax import lax
from jax.experimental import pallas as pl
from jax.experimental.pallas import tpu as pltpu

F32 = jnp.float32
BF16 = jnp.bfloat16

LANES = 128
TOKEN_TILE = 256
KEY_CHUNK = 512
GRID_W = 64
WIN_R = 8
WIN_C = 16
ML_CHUNK = 128
ML_HEADS = 4
EPS = 1e-6
ROPE_BASE = 10000.0
LOG2E = 1.4426950408889634
NEG = -1e30
VMEM_LIMIT = 56 * 1024 * 1024

_NT = (((1,), (1,)), ((), ()))


def _dot(a, b):
    return jnp.dot(a, b, preferred_element_type=F32)


def _dot_nt(a, b):
    return lax.dot_general(a, b, _NT, preferred_element_type=F32)


def _silu(v):
    return v * (1.0 / (1.0 + jnp.exp(-v)))


def _log_sigmoid(v):
    return -(jnp.maximum(-v, 0.0) + jnp.log1p(jnp.exp(-jnp.abs(v))))


def _params(n_axes):
    return pltpu.CompilerParams(dimension_semantics=("arbitrary",) * n_axes, vmem_limit_bytes=VMEM_LIMIT)


def _full(shape):
    nd = len(shape)
    return pl.BlockSpec(shape, lambda *_: (0,) * nd)


def _mod_body(c_ref, w_ref, b_ref, o_ref):
    s = _silu(c_ref[...])
    o_ref[0] = _dot(s.astype(BF16), w_ref[0].astype(BF16)) + b_ref[0]


def _modulation(cvec, w_mod, b_mod):
    depth, d, n = w_mod.shape
    rows = cvec.shape[0]
    tn = n // 4
    return pl.pallas_call(
        _mod_body,
        out_shape=jax.ShapeDtypeStruct((depth, rows, n), F32),
        grid=(depth, n // tn),
        in_specs=[_full((rows, d)),
                  pl.BlockSpec((1, d, tn), lambda l, j: (l, 0, j)),
                  pl.BlockSpec((1, 1, tn), lambda l, j: (l, 0, j))],
        out_specs=pl.BlockSpec((1, rows, tn), lambda l, j: (l, 0, j)),
        compiler_params=_params(2), name="modulation",
    )(cvec, w_mod, b_mod.reshape(depth, 1, n))


class _Tokens:
    def __init__(self, arrays, batch, seq, ctx_len):
        self.arrays = arrays
        self.split = len(arrays) == 2
        self.n_lat = batch * seq // TOKEN_TILE
        self.n_ctx = batch * ctx_len // TOKEN_TILE
        self.d = arrays[0].shape[-1]

    def specs(self):
        blk = (TOKEN_TILE, self.d)
        if not self.split:
            return [pl.BlockSpec(blk, lambda i: (i, 0))]
        n_lat = self.n_lat
        return [pl.BlockSpec(blk, lambda i: (jnp.minimum(i, n_lat - 1), 0)),
                pl.BlockSpec(blk, lambda i: (jnp.maximum(i - n_lat, 0), 0))]

    def load(self, refs, i):
        if not self.split:
            return refs[0][...]
        return jnp.where(i < self.n_lat, refs[0][...], refs[1][...])


def _mod_spec(n_lat, nj, batch, width):
    return pl.BlockSpec((1, 1, width), lambda i: (jnp.where(i < n_lat, i // nj, batch), 0, 0))


def _in_proj_body(*refs, tok, outs, d):
    n_tok = len(tok.arrays)
    mod_ref, g_ref, w_ref = refs[n_tok:n_tok + 3]
    o_refs = refs[n_tok + 3:]
    i = pl.program_id(0)
    x = tok.load(refs[:n_tok], i)
    y = x * lax.rsqrt(jnp.mean(x * x, axis=-1, keepdims=True) + EPS) * g_ref[...]
    mod = mod_ref[0]
    h = y * (1.0 + mod[:, d:2 * d]) + mod[:, :d]
    acc = _dot(h.astype(BF16), w_ref[...])
    for o_ref, (c0, width, _, scale, transposed) in zip(o_refs, outs):
        v = acc[:, c0:c0 + width]
        if scale != 1.0:
            v = v * scale
        if transposed:
            o_ref[0] = v.T.astype(o_ref.dtype)
        else:
            o_ref[...] = v.astype(o_ref.dtype)


def _in_proj(tok, mod_l, g, w, outs, batch, seq, ctx_len):
    d = tok.d
    nj = seq // TOKEN_TILE
    n_lat, n_all = tok.n_lat, tok.n_lat + tok.n_ctx
    rows = n_all * TOKEN_TILE
    t_len = seq + ctx_len
    out_shape, out_specs = [], []
    for (_, width, dtype, _, transposed) in outs:
        if transposed:
            out_shape.append(jax.ShapeDtypeStruct((batch, width, t_len), dtype))
            out_specs.append(pl.BlockSpec(
                (1, width, TOKEN_TILE),
                lambda i: (jnp.where(i < n_lat, i // nj, i - n_lat), 0, jnp.where(i < n_lat, i % nj, nj))))
        else:
            out_shape.append(jax.ShapeDtypeStruct((rows, width), dtype))
            out_specs.append(pl.BlockSpec((TOKEN_TILE, width), lambda i: (i, 0)))
    body = functools.partial(_in_proj_body, tok=tok, outs=outs, d=d)
    return pl.pallas_call(
        body, out_shape=out_shape, grid=(n_all,),
        in_specs=tok.specs() + [_mod_spec(n_lat, nj, batch, 3 * d), _full((1, d)), _full(w.shape)],
        out_specs=out_specs, compiler_params=_params(1), name="in_proj",
    )(*tok.arrays, mod_l, g.reshape(1, d), w)


def _out_proj_body(*refs, tok, d, final):
    n_tok = len(tok.arrays)
    ma_ref, mb_ref, wa_ref, wb_ref, mod_ref = refs[n_tok:n_tok + 5]
    i = pl.program_id(0)
    x = tok.load(refs[:n_tok], i)
    acc = _dot(ma_ref[...], wa_ref[...]) + _dot(mb_ref[...], wb_ref[...])
    xn = x + mod_ref[0][:, 2 * d:] * acc
    if final:
        gf_ref, o_ref = refs[n_tok + 5:]
        xn = xn * lax.rsqrt(jnp.mean(xn * xn, axis=-1, keepdims=True) + EPS) * gf_ref[...]
    else:
        o_ref = refs[n_tok + 5]
    o_ref[...] = xn


def _out_proj(tok, mix_a, mix_b, w_out, mod_l, batch, seq, g_final=None):
    d = tok.d
    half = mix_a.shape[1]
    nj = seq // TOKEN_TILE
    n_lat = tok.n_lat
    final = g_final is not None
    n_tiles = n_lat if final else n_lat + tok.n_ctx
    wa, wb = w_out[:half].astype(BF16), w_out[half:].astype(BF16)
    tile = lambda width: pl.BlockSpec((TOKEN_TILE, width), lambda i: (i, 0))
    in_specs = tok.specs() + [tile(half), tile(half), _full(wa.shape), _full(wb.shape),
                              _mod_spec(n_lat, nj, batch, 3 * d)]
    args = list(tok.arrays) + [mix_a, mix_b, wa, wb, mod_l]
    if final:
        in_specs.append(_full((1, d)))
        args.append(g_final.reshape(1, d))
    body = functools.partial(_out_proj_body, tok=tok, d=d, final=final)
    return pl.pallas_call(
        body, out_shape=jax.ShapeDtypeStruct((n_tiles * TOKEN_TILE, d), F32), grid=(n_tiles,),
        in_specs=in_specs, out_specs=tile(d), compiler_params=_params(1), name="out_proj",
    )(*args)


def _rope_tables(seq, ctx_len, rot_dim, lane_pattern):
    t = jnp.arange(seq)
    pos = jnp.stack([t // GRID_W, t % GRID_W], axis=-1).astype(F32)
    n_freq = rot_dim // 4
    inv = ROPE_BASE ** (-jnp.arange(n_freq, dtype=F32) / n_freq)
    ang = pos[:, :, None] * inv
    cos, sin = jnp.cos(ang), jnp.sin(ang)
    cos_row = jnp.concatenate([cos[:, 0], cos[:, 0], cos[:, 1], cos[:, 1]], axis=-1)
    sin_row = jnp.concatenate([-sin[:, 0], sin[:, 0], -sin[:, 1], sin[:, 1]], axis=-1)
    cos_t, sin_t = lane_pattern(cos_row, 1.0), lane_pattern(sin_row, 0.0)
    pad = lambda a, v: jnp.concatenate([a, jnp.full((ctx_len, LANES), v, F32)], axis=0)
    return pad(cos_t, 1.0), pad(sin_t, 0.0)


def _rope(x, cos, sin, dist):
    lane = lax.broadcasted_iota(jnp.int32, x.shape, 1)
    first = (lane % (2 * dist)) < dist
    partner = jnp.where(first, pltpu.roll(x, LANES - dist, 1), pltpu.roll(x, dist, 1))
    return x * cos + partner * sin


def _tok_block(b, j, nj, batch):
    return jnp.where(j < nj, b * nj + j, batch * nj + b)


def _mla_prep_body(pa_ref, cos_ref, sin_ref, gq_ref, gkv_ref, wuq_ref, wuk_ref, wuv_ref,
                   q_ref, k_ref, vt_ref, *, heads, q_lora, kv_lora, qscale):
    pa = pa_ref[...]
    cos, sin = cos_ref[...], sin_ref[...]

    def norm(v, g_ref):
        return (v * lax.rsqrt(jnp.mean(v * v, axis=-1, keepdims=True) + EPS) * g_ref[...]).astype(BF16)

    cq = norm(pa[:, :q_lora], gq_ref)
    q_all = _dot(cq, wuq_ref[...])
    for h in range(heads):
        qh = _rope(q_all[:, h * LANES:(h + 1) * LANES], cos, sin, 8)
        q_ref[0, h] = (qh * qscale).astype(BF16)
    ckv = norm(pa[:, q_lora:q_lora + kv_lora], gkv_ref)
    k_nope = _dot(ckv, wuk_ref[...])
    k_rope = _rope(pa[:, q_lora + kv_lora:], cos, sin, 8)
    for h in range(heads):
        k_ref[0, h] = (k_nope[:, h * LANES:(h + 1) * LANES] + k_rope).astype(BF16)
    vt_ref[0] = _dot(ckv, wuv_ref[...]).T.astype(BF16)


def _mla_prep(pa, cos, sin, g_q, g_kv, w_uq, w_ukv, batch, seq, ctx_len, heads, nope, rope, v_dim):
    q_lora, kv_lora = g_q.shape[0], g_kv.shape[0]
    nj = seq // TOKEN_TILE
    t_len = seq + ctx_len
    pad = LANES - nope - rope
    wuq = jnp.pad(w_uq.reshape(q_lora, heads, nope + rope), ((0, 0), (0, 0), (0, pad)))
    wuq = wuq.reshape(q_lora, heads * LANES).astype(BF16)
    wkv = w_ukv.reshape(kv_lora, heads, nope + v_dim)
    wuk = jnp.pad(wkv[..., :nope], ((0, 0), (0, 0), (0, LANES - nope))).reshape(kv_lora, heads * LANES).astype(BF16)
    wuv = wkv[..., nope:].reshape(kv_lora, heads * v_dim).astype(BF16)
    body = functools.partial(_mla_prep_body, heads=heads, q_lora=q_lora, kv_lora=kv_lora,
                             qscale=(nope + rope) ** -0.5 * LOG2E)
    head_major = pl.BlockSpec((1, heads, TOKEN_TILE, LANES), lambda b, j: (b, 0, j, 0))
    return pl.pallas_call(
        body,
        out_shape=[jax.ShapeDtypeStruct((batch, heads, t_len, LANES), BF16),
                   jax.ShapeDtypeStruct((batch, heads, t_len, LANES), BF16),
                   jax.ShapeDtypeStruct((batch, heads * v_dim, t_len), BF16)],
        grid=(batch, nj + 1),
        in_specs=[pl.BlockSpec((TOKEN_TILE, pa.shape[1]), lambda b, j: (_tok_block(b, j, nj, batch), 0)),
                  pl.BlockSpec((TOKEN_TILE, LANES), lambda b, j: (j, 0)),
                  pl.BlockSpec((TOKEN_TILE, LANES), lambda b, j: (j, 0)),
                  _full((1, q_lora)), _full((1, kv_lora)), _full(wuq.shape), _full(wuk.shape), _full(wuv.shape)],
        out_specs=[head_major, head_major,
                   pl.BlockSpec((1, heads * v_dim, TOKEN_TILE), lambda b, j: (b, 0, j))],
        compiler_params=_params(2), name="mla_prep",
    )(pa, cos, sin, g_q.reshape(1, -1), g_kv.reshape(1, -1), wuq, wuk, wuv)


def _gqa_prep_body(pd_ref, cos_ref, sin_ref, gq_ref, gk_ref, bd_ref, q_ref, k_ref, *, q_width, qscale):
    pd = pd_ref[...]
    cos, sin = cos_ref[...], sin_ref[...]
    bd = bd_ref[...]
    lane = lax.broadcasted_iota(jnp.int32, (TOKEN_TILE, LANES), 1)
    upper = lane >= LANES // 2

    def head_norm(v, g_ref):
        ms = jnp.dot(v * v, bd, preferred_element_type=F32, precision=lax.Precision.HIGHEST)
        return v * lax.rsqrt(ms + EPS) * g_ref[...]

    n_blocks = q_width // LANES
    for i in range(n_blocks):
        xq = _rope(head_norm(pd[:, i * LANES:(i + 1) * LANES], gq_ref), cos, sin, 16) * qscale
        t = (2 * i) // n_blocks
        in_half = upper if t == 1 else jnp.logical_not(upper)
        swapped = pltpu.roll(xq, LANES // 2, 1)
        q_ref[0, 2 * i + t] = jnp.where(in_half, xq, 0.0).astype(BF16)
        q_ref[0, 2 * i + 1 - t] = jnp.where(in_half, swapped, 0.0).astype(BF16)
    xk = _rope(head_norm(pd[:, q_width:], gk_ref), cos, sin, 16)
    k_ref[0, 0] = xk.astype(BF16)


def _gqa_prep(pd, cos, sin, g_q, g_k, batch, seq, ctx_len, heads, kv_heads, head_dim):
    assert kv_heads * head_dim == LANES and heads % (2 * kv_heads) == 0
    nj = seq // TOKEN_TILE
    t_len = seq + ctx_len
    q_width = heads * head_dim
    grp = jnp.arange(LANES) // head_dim
    bd = (grp[:, None] == grp[None, :]).astype(F32) / head_dim
    rep = LANES // head_dim
    body = functools.partial(_gqa_prep_body, q_width=q_width, qscale=head_dim ** -0.5 * LOG2E)
    return pl.pallas_call(
        body,
        out_shape=[jax.ShapeDtypeStruct((batch, heads, t_len, LANES), BF16),
                   jax.ShapeDtypeStruct((batch, 1, t_len, LANES), BF16)],
        grid=(batch, nj + 1),
        in_specs=[pl.BlockSpec((TOKEN_TILE, pd.shape[1]), lambda b, j: (_tok_block(b, j, nj, batch), 0)),
                  pl.BlockSpec((TOKEN_TILE, LANES), lambda b, j: (j, 0)),
                  pl.BlockSpec((TOKEN_TILE, LANES), lambda b, j: (j, 0)),
                  _full((1, LANES)), _full((1, LANES)), _full((LANES, LANES))],
        out_specs=[pl.BlockSpec((1, heads, TOKEN_TILE, LANES), lambda b, j: (b, 0, j, 0)),
                   pl.BlockSpec((1, 1, TOKEN_TILE, LANES), lambda b, j: (b, 0, j, 0))],
        compiler_params=_params(2), name="gqa_prep",
    )(pd, cos, sin, jnp.tile(g_q, rep).reshape(1, LANES), jnp.tile(g_k, rep).reshape(1, LANES), bd)


def _flash_body(q_ref, k_ref, vt_ref, g_ref, o_ref, *, k_sel, v_off, v_dim, seq, ctx_len, nj, ctx_queries):
    j = pl.program_id(2)
    n_chunks = seq // KEY_CHUNK
    n_dyn = jnp.where(j == nj, 0, n_chunks) if ctx_queries else n_chunks
    outs = []
    for a in range(2):
        q = q_ref[0, a]

        def step(carry, kc, vc, q=q):
            m, l, acc = carry
            s_t = _dot_nt(kc, q)
            m_new = jnp.maximum(m, jnp.max(s_t, axis=0, keepdims=True))
            alpha = jnp.exp2(m - m_new)
            p = jnp.exp2(s_t - m_new)
            l = alpha * l + jnp.sum(p, axis=0, keepdims=True)
            acc = alpha * acc + _dot(vc, p.astype(BF16))
            return m_new, l, acc

        def chunk(c, carry, a=a, step=step):
            off = pl.multiple_of(c * KEY_CHUNK, KEY_CHUNK)
            kc = k_ref[0, k_sel[a], pl.ds(off, KEY_CHUNK), :]
            vc = vt_ref[0, v_off[a]:v_off[a] + v_dim, pl.ds(off, KEY_CHUNK)]
            return step(carry, kc, vc)

        init = (jnp.full((1, TOKEN_TILE), NEG, F32), jnp.zeros((1, TOKEN_TILE), F32),
                jnp.zeros((v_dim, TOKEN_TILE), F32))
        carry = lax.fori_loop(0, n_dyn, chunk, init)
        kc = k_ref[0, k_sel[a], seq:seq + ctx_len, :]
        vc = vt_ref[0, v_off[a]:v_off[a] + v_dim, seq:seq + ctx_len]
        _, l, acc = step(carry, kc, vc)
        outs.append(acc / l)
    o2 = jnp.concatenate(outs, axis=0).T
    o_ref[...] = (o2 * _silu(g_ref[...].astype(F32))).astype(o_ref.dtype)


def _flash(q, k, vt, gate, batch, seq, ctx_len, k_heads_per_pair, v_rows_per_pair, pairs_per_kv, ctx_queries):
    heads = q.shape[1]
    t_len = seq + ctx_len
    nj = seq // TOKEN_TILE
    v_dim = LANES // 2
    k_sel = (0, 1) if k_heads_per_pair == 2 else (0, 0)
    v_off = (0, v_dim) if v_rows_per_pair == 2 * v_dim else (0, 0)
    k_blocks = k.shape[1] // k_heads_per_pair
    body = functools.partial(_flash_body, k_sel=k_sel, v_off=v_off, v_dim=v_dim, seq=seq, ctx_len=ctx_len,
                             nj=nj, ctx_queries=ctx_queries)
    tok = lambda b, p, j: (_tok_block(b, j, nj, batch), p)
    out_rows = gate.shape[0] if ctx_queries else batch * seq
    return pl.pallas_call(
        body, out_shape=jax.ShapeDtypeStruct((out_rows, gate.shape[1]), BF16),
        grid=(batch, heads // 2, nj + 1 if ctx_queries else nj),
        in_specs=[pl.BlockSpec((1, 2, TOKEN_TILE, LANES), lambda b, p, j: (b, p, j, 0)),
                  pl.BlockSpec((1, k_heads_per_pair, t_len, LANES),
                               lambda b, p, j: (b, (p // pairs_per_kv) % k_blocks, 0, 0)),
                  pl.BlockSpec((1, v_rows_per_pair, t_len), lambda b, p, j: (b, p // pairs_per_kv, 0)),
                  pl.BlockSpec((TOKEN_TILE, LANES), tok)],
        out_specs=pl.BlockSpec((TOKEN_TILE, LANES), tok),
        compiler_params=_params(3), name="flash_attention",
    )(q, k, vt, gate)


NA_KEY_ROWS = 10
NA_BIAS_ROWS = 19


def _na_bias_body(rpb_ref, o_ref):
    p = pl.program_id(0)
    shape = (GRID_W, LANES)
    kc = lax.broadcasted_iota(jnp.int32, shape, 0)
    lane = lax.broadcasted_iota(jnp.int32, shape, 1)
    qc = lane % GRID_W
    upper = lane >= GRID_W
    rel = kc - qc + (WIN_C - 1)
    c0 = jnp.clip(qc - WIN_C // 2, 0, GRID_W - WIN_C)
    col_ok = (kc >= c0) & (kc < c0 + WIN_C)
    n_rel_r, n_rel_c = 2 * WIN_R - 1, 2 * WIN_C - 1
    o_ref[...] = jnp.zeros(o_ref.shape, F32)

    def block(dd, carry):
        base0 = (2 * p) * (n_rel_r * n_rel_c) + dd * n_rel_c
        base1 = base0 + n_rel_r * n_rel_c
        val = jnp.zeros(shape, F32)
        for jj in range(n_rel_c):
            val = jnp.where(rel == jj, jnp.where(upper, rpb_ref[base1 + jj], rpb_ref[base0 + jj]), val)
        off = pl.multiple_of((dd + 2) * GRID_W, GRID_W)
        o_ref[0, pl.ds(off, GRID_W), :] = jnp.where(col_ok, val * LOG2E, NEG)
        return carry

    lax.fori_loop(0, n_rel_r, block, 0)


def _na_bias(rpb):
    heads = rpb.shape[0]
    return pl.pallas_call(
        _na_bias_body,
        out_shape=jax.ShapeDtypeStruct((heads // 2, NA_BIAS_ROWS * GRID_W, LANES), F32),
        grid=(heads // 2,),
        in_specs=[pl.BlockSpec(memory_space=pltpu.SMEM)],
        out_specs=pl.BlockSpec((1, NA_BIAS_ROWS * GRID_W, LANES), lambda p: (p, 0, 0)),
        compiler_params=_params(1), name="na_bias",
    )(rpb.reshape(-1))


def _na_body(q_ref, kl_ref, kc_ref, vt_ref, bias_ref, g_ref, o_ref, *, seq, ctx_len, rows_per_step):
    rblk = pl.program_id(2)
    n_rows = seq // GRID_W
    span = NA_KEY_ROWS * GRID_W
    lane = lax.broadcasted_iota(jnp.int32, (GRID_W, LANES), 1)
    lower = lane < GRID_W
    key_row = lax.broadcasted_iota(jnp.int32, (span, LANES), 0) // GRID_W
    k_ctx = kc_ref[...]
    v_ctx = vt_ref[0, :, seq:seq + ctx_len]
    for rr in range(rows_per_step):
        r = rblk * rows_per_step + rr
        rs = jnp.clip(r - WIN_R // 2, 0, n_rows - WIN_R)
        rs_al = 2 * jnp.minimum(rs // 2, (n_rows - NA_KEY_ROWS) // 2)
        q_r = q_ref[rr * GRID_W:(rr + 1) * GRID_W, :]
        zero = jnp.zeros_like(q_r)
        q2 = jnp.concatenate([jnp.where(lower, q_r, zero), jnp.where(lower, zero, q_r)], axis=0)
        k_off = pl.multiple_of(rs_al * GRID_W, LANES)
        b_off = pl.multiple_of((rs_al - r + 9) * GRID_W, GRID_W)
        s_t = _dot_nt(kl_ref[pl.ds(k_off, span), :], q2) + bias_ref[0, pl.ds(b_off, span), :]
        a0 = rs - rs_al
        s_t = jnp.where((key_row >= a0) & (key_row < a0 + WIN_R), s_t, NEG)
        s_c = _dot_nt(k_ctx, q2)
        m = jnp.maximum(jnp.max(s_t, axis=0, keepdims=True), jnp.max(s_c, axis=0, keepdims=True))
        p_t = jnp.exp2(s_t - m)
        p_c = jnp.exp2(s_c - m)
        l = jnp.sum(p_t, axis=0, keepdims=True) + jnp.sum(p_c, axis=0, keepdims=True)
        r_t = _dot(vt_ref[0, :, pl.ds(k_off, span)], p_t.astype(BF16)) + _dot(v_ctx, p_c.astype(BF16))
        r_n = (r_t / l).T
        o = jnp.where(lower, r_n[:GRID_W], r_n[GRID_W:])
        g = g_ref[rr * GRID_W:(rr + 1) * GRID_W, :].astype(F32)
        o_ref[rr * GRID_W:(rr + 1) * GRID_W, :] = (o * _silu(g)).astype(o_ref.dtype)


def _na_attention(q, k, vt, bias, gate, batch, seq, ctx_len):
    pairs = q.shape[1] // LANES
    t_len = seq + ctx_len
    rows_per_step = TOKEN_TILE // GRID_W
    n_steps = seq // TOKEN_TILE
    lat_tile = lambda b, p, s: (b * n_steps + s, p)
    body = functools.partial(_na_body, seq=seq, ctx_len=ctx_len, rows_per_step=rows_per_step)
    return pl.pallas_call(
        body, out_shape=jax.ShapeDtypeStruct((batch * seq, q.shape[1]), BF16),
        grid=(batch, pairs, n_steps),
        in_specs=[pl.BlockSpec((TOKEN_TILE, LANES), lat_tile),
                  pl.BlockSpec((seq, LANES), lambda b, p, s: (b, p)),
                  pl.BlockSpec((ctx_len, LANES), lambda b, p, s: (batch * seq // ctx_len + b, p)),
                  pl.BlockSpec((1, LANES, t_len), lambda b, p, s: (b, p, 0)),
                  pl.BlockSpec((1, NA_BIAS_ROWS * GRID_W, LANES), lambda b, p, s: (p, 0, 0)),
                  pl.BlockSpec((TOKEN_TILE, LANES), lat_tile)],
        out_specs=pl.BlockSpec((TOKEN_TILE, LANES), lat_tile),
        compiler_params=_params(3), name="neighborhood_attention",
    )(q, k, k, vt, bias, gate)


def _seg_scan(v, reverse):
    n = v.shape[1]
    lane = lax.broadcasted_iota(jnp.int32, v.shape, 1) % ML_CHUNK
    k = 1
    while k < ML_CHUNK:
        if reverse:
            v = v + jnp.where(lane < ML_CHUNK - k, pltpu.roll(v, n - k, 1), 0.0)
        else:
            v = v + jnp.where(lane >= k, pltpu.roll(v, k, 1), 0.0)
        k *= 2
    return v


def _mlstm_prep_body(u_ref, up_ref, un_ref, cw_ref, cb_ref, wqk_ref, wv_ref, wg_ref, bg_ref,
                     xc_ref, q_ref, k_ref, v_ref, vt_ref, g_ref, *, nj, width, kscale):
    j = pl.program_id(1)
    u = u_ref[...]
    row = lax.broadcasted_iota(jnp.int32, u.shape, 0)
    prev = jnp.where((j > 0) & (j < nj), up_ref[7:8, :], 0.0)
    nxt = jnp.where(j < nj - 1, un_ref[0:1, :], 0.0)
    u_m1 = jnp.where(row == 0, prev, pltpu.roll(u, 1, 0))
    u_p1 = jnp.where(row == TOKEN_TILE - 1, nxt, pltpu.roll(u, TOKEN_TILE - 1, 0))
    cw = cw_ref[...]
    xc = _silu(u_m1 * cw[0:1] + u * cw[1:2] + u_p1 * cw[2:3] + cb_ref[...])
    xcb = xc.astype(BF16)
    xc_ref[...] = xcb
    qk = _dot(xcb, wqk_ref[...])
    v = _dot(u.astype(BF16), wv_ref[...])
    qb, kb, vb = qk[:, :width].astype(BF16), qk[:, width:].astype(BF16), v.astype(BF16)
    q_ref[...] = qb
    k_ref[...] = (qk[:, width:] * kscale).astype(BF16)
    v_ref[...] = vb
    vt_ref[0] = v.T.astype(BF16)
    pre = _dot_nt(wg_ref[0], qb) + _dot_nt(wg_ref[1], kb) + _dot_nt(wg_ref[2], vb) + bg_ref[...]
    grow = lax.broadcasted_iota(jnp.int32, pre.shape, 0) % 16
    gates = jnp.where(grow < ML_HEADS, pre, jnp.where(grow < 3 * ML_HEADS, _log_sigmoid(pre), 0.0))
    scanned = jnp.concatenate([_seg_scan(gates[:16], False), _seg_scan(gates[16:], True)], axis=0)
    g_ref[0] = jnp.where((grow >= 2 * ML_HEADS) & (grow < 3 * ML_HEADS), scanned, gates)


def _mlstm_prep(u, conv_w, conv_b, w_q, w_k, w_v, w_gate, b_gate, batch, seq, ctx_len):
    heads, hd = w_q.shape[0], w_q.shape[1]
    assert heads == ML_HEADS and hd == ML_CHUNK
    width = heads * hd
    nj = seq // TOKEN_TILE
    t_len = seq + ctx_len
    rows = u.shape[0]

    def block_diag(w):
        eye = jnp.eye(heads, dtype=w.dtype)
        return (eye[:, None, :, None] * w[:, :, None, :]).reshape(width, width)

    wqk = jnp.concatenate([block_diag(w_q), block_diag(w_k)], axis=1).astype(BF16)
    wv = block_diag(w_v).astype(BF16)
    wg = w_gate.reshape(2, heads, 3, hd, 2 * heads).transpose(2, 0, 4, 1, 3).reshape(3, 2, 2 * heads, width)
    zeros = jnp.zeros((3, 2, heads, width), w_gate.dtype)
    wg = jnp.concatenate([wg, wg[:, :, heads:], zeros], axis=2).reshape(3, 32, width).astype(BF16)
    bzero = jnp.zeros((2, heads), b_gate.dtype)
    bg = jnp.concatenate([b_gate, b_gate[:, heads:], bzero], axis=1).reshape(32, 1)
    n_halo = rows // 8
    tokb = lambda b, j: _tok_block(b, j, nj, batch)
    tile = pl.BlockSpec((TOKEN_TILE, width), lambda b, j: (tokb(b, j), 0))
    body = functools.partial(_mlstm_prep_body, nj=nj, width=width, kscale=hd ** -0.5)
    per_tile = TOKEN_TILE // 8
    return pl.pallas_call(
        body,
        out_shape=[jax.ShapeDtypeStruct((rows, width), BF16)] * 4
        + [jax.ShapeDtypeStruct((batch, width, t_len), BF16), jax.ShapeDtypeStruct((batch, 32, t_len), F32)],
        grid=(batch, nj + 1),
        in_specs=[tile,
                  pl.BlockSpec((8, width), lambda b, j: (jnp.maximum(tokb(b, j) * per_tile - 1, 0), 0)),
                  pl.BlockSpec((8, width), lambda b, j: (jnp.minimum((tokb(b, j) + 1) * per_tile, n_halo - 1), 0)),
                  _full((3, width)), _full((1, width)), _full(wqk.shape), _full(wv.shape),
                  _full(wg.shape), _full((32, 1))],
        out_specs=[tile, tile, tile, tile,
                   pl.BlockSpec((1, width, TOKEN_TILE), lambda b, j: (b, 0, j)),
                   pl.BlockSpec((1, 32, TOKEN_TILE), lambda b, j: (b, 0, j))],
        compiler_params=_params(2), name="mlstm_prep",
    )(u, u, u, conv_w, conv_b.reshape(1, width), wqk, wv, wg, bg)


def _mlstm_seq_body(qf, kf, vf, vtf, gf, qb, kb, vb, vtb, gb, hf_ref, hb_ref, c_s, n_s, m_s):
    t = pl.program_id(1)
    L = ML_CHUNK

    @pl.when(t == 0)
    def _():
        c_s[...] = jnp.zeros(c_s.shape, F32)
        n_s[...] = jnp.zeros(n_s.shape, F32)
        m_s[...] = jnp.zeros(m_s.shape, F32)

    li = lax.broadcasted_iota(jnp.int32, (L, L), 0)
    si = lax.broadcasted_iota(jnp.int32, (L, L), 1)
    streams = ((qf, kf, vf, vtf, gf, hf_ref, si <= li), (qb, kb, vb, vtb, gb, hb_ref, si >= li))
    for d, (q_ref, k_ref, v_ref, vt_ref, g_ref, h_ref, incl) in enumerate(streams):
        g = g_ref[0]
        for h in range(ML_HEADS):
            idx = d * ML_HEADS + h
            cols = slice(h * L, (h + 1) * L)
            q, k, v, vt = q_ref[:, cols], k_ref[:, cols], v_ref[:, cols], vt_ref[0, cols, :]
            i_row, f_row, b_row = g[h:h + 1], g[ML_HEADS + h:ML_HEADS + h + 1], g[2 * ML_HEADS + h:2 * ML_HEADS + h + 1]
            c_st, n_st, m_st = c_s[idx], n_s[idx], m_s[idx][:, 0:1]
            b_col = jnp.sum(jnp.where(incl, f_row, 0.0), axis=1, keepdims=True)
            dmat = jnp.where(incl, b_col - b_row + i_row, NEG)
            m_inter = b_col + m_st
            m_t = jnp.maximum(m_inter, jnp.max(dmat, axis=1, keepdims=True))
            w_inter = jnp.exp(m_inter - m_t)
            s_mat = jnp.exp(dmat - m_t) * _dot_nt(q, k)
            num = w_inter * _dot_nt(q, c_st.astype(BF16)) + _dot(s_mat.astype(BF16), v)
            qn = jnp.sum(q.astype(F32) * n_st, axis=1, keepdims=True)
            den = w_inter * qn + jnp.sum(s_mat, axis=1, keepdims=True)
            h_ref[:, cols] = num / jnp.maximum(jnp.abs(den), jnp.exp(-m_t))
            b_last = jnp.sum(f_row, axis=1, keepdims=True)
            g_row = b_last - b_row + i_row
            m_new = jnp.maximum(b_last + m_st, jnp.max(g_row, axis=1, keepdims=True))
            decay = jnp.exp(b_last + m_st - m_new)
            w_row = jnp.exp(g_row - m_new)
            c_s[idx] = decay * c_st + _dot((vt.astype(F32) * w_row).astype(BF16), k)
            w8 = jnp.broadcast_to(w_row, (8, L)).astype(BF16)
            n_s[idx] = decay * n_st + _dot(w8, k)[0:1]
            m_s[idx] = jnp.broadcast_to(m_new, (1, L))


def _mlstm_seq(q, k, v, vt, gates, batch, seq, ctx_len):
    width = q.shape[1]
    L = ML_CHUNK
    n_lat, n_ctx = seq // L, ctx_len // L
    n_chunks = n_lat + n_ctx
    fwd = lambda t: (t + n_lat) % n_chunks
    bwd = lambda t: n_chunks - 1 - t
    rowblk = lambda b, c: jnp.where(c < n_lat, b * n_lat + c, batch * n_lat + b * n_ctx + (c - n_lat))

    def stream(chunk_of, d):
        tok = pl.BlockSpec((L, width), lambda b, t: (rowblk(b, chunk_of(t)), 0))
        return [tok, tok, tok,
                pl.BlockSpec((1, width, L), lambda b, t: (b, 0, chunk_of(t))),
                pl.BlockSpec((1, 16, L), lambda b, t: (b, d, chunk_of(t)))]

    out_f = pl.BlockSpec((L, width), lambda b, t: (rowblk(b, fwd(t)), 0))
    out_b = pl.BlockSpec((L, width), lambda b, t: (rowblk(b, bwd(t)), 0))
    n_state = 2 * ML_HEADS
    return pl.pallas_call(
        _mlstm_seq_body,
        out_shape=[jax.ShapeDtypeStruct(q.shape, F32)] * 2,
        grid=(batch, n_chunks),
        in_specs=stream(fwd, 0) + stream(bwd, 1),
        out_specs=[out_f, out_b],
        scratch_shapes=[pltpu.VMEM((n_state, L, L), F32), pltpu.VMEM((n_state, 1, L), F32),
                        pltpu.VMEM((n_state, 1, L), F32)],
        compiler_params=_params(2), name="mlstm_recurrence",
    )(q, k, v, vt, gates, q, k, v, vt, gates)


def _mlstm_out_body(hf_ref, hb_ref, xc_ref, z_ref, gh_ref, sk_ref, o_ref):
    h = hf_ref[...] + hb_ref[...]
    L = ML_CHUNK
    for hd in range(ML_HEADS):
        cols = slice(hd * L, (hd + 1) * L)
        hh = h[:, cols]
        mu = jnp.mean(hh, axis=-1, keepdims=True)
        var = jnp.mean(jnp.square(hh - mu), axis=-1, keepdims=True)
        hn = (hh - mu) * lax.rsqrt(var + EPS) * gh_ref[:, cols]
        o = (hn + sk_ref[:, cols] * xc_ref[:, cols].astype(F32)) * _silu(z_ref[:, cols].astype(F32))
        o_ref[:, cols] = o.astype(o_ref.dtype)


def _mlstm_out(hf, hb, xc, z, g_head, skip):
    rows, width = hf.shape
    tile = pl.BlockSpec((TOKEN_TILE, width), lambda i: (i, 0))
    return pl.pallas_call(
        _mlstm_out_body, out_shape=jax.ShapeDtypeStruct((rows, width), BF16), grid=(rows // TOKEN_TILE,),
        in_specs=[tile, tile, tile, tile, _full((1, width)), _full((1, width))],
        out_specs=tile, compiler_params=_params(1), name="mlstm_out",
    )(hf, hb, xc, z, g_head.reshape(1, width), skip.reshape(1, width))


def kernel(x, c, ctx, c_ctx, w_mod, b_mod, g_norm, ab_w_in, ab_w_out, mla_g_q, mla_w_uq, mla_g_kv, mla_w_ukv,
           ml_conv_w, ml_conv_b, ml_w_q, ml_w_k, ml_w_v, ml_w_gate, ml_b_gate, ml_g_head, ml_skip,
           cd_w_in, cd_w_out, na_rpb, gqa_g_q, gqa_g_k, g_final):
    batch, seq, d = x.shape
    ctx_len = ctx.shape[1]
    assert ctx_len == TOKEN_TILE and seq % KEY_CHUNK == 0 and seq // GRID_W >= NA_KEY_ROWS
    dims = (batch, seq, ctx_len)

    mla_heads, mla_rope, mla_v = 8, 32, 64
    mla_nope = mla_w_uq.shape[2] // mla_heads - mla_rope
    q_lora, kv_lora = mla_g_q.shape[1], mla_g_kv.shape[1]
    ml_width = ml_conv_w.shape[2]
    mla_width = mla_heads * mla_v
    gqa_heads, gqa_dim = 8, gqa_g_q.shape[1]
    gqa_kv = (cd_w_in.shape[2] - 4 * 512 - 2 * gqa_heads * gqa_dim) // (2 * gqa_dim)
    na_width = na_rpb.shape[1] * 64

    mod_rows = -(-(batch + 1) // 8) * 8
    cvec = jnp.concatenate([c, c_ctx[None], jnp.zeros((mod_rows - batch - 1, d), F32)], axis=0)
    mod = _modulation(cvec, w_mod, b_mod)
    mod0 = mod[0].reshape(mod_rows, 1, 3 * d)
    mod1 = mod[1].reshape(mod_rows, 1, 3 * d)

    tok0 = _Tokens((x.reshape(batch * seq, d), ctx.reshape(batch * ctx_len, d)), *dims)
    w_in = ab_w_in[0]
    s1 = q_lora + kv_lora
    zcol = lambda n: jnp.zeros((d, n), w_in.dtype)
    w0 = jnp.concatenate([w_in[:, :s1], zcol(mla_nope), w_in[:, s1:s1 + mla_rope],
                          zcol(LANES - mla_nope - mla_rope), w_in[:, s1 + mla_rope:]], axis=1).astype(BF16)
    o_pa = s1 + LANES
    outs0 = ((0, o_pa, F32, 1.0, False), (o_pa, mla_width, BF16, 1.0, False),
             (o_pa + mla_width, ml_width, F32, 1.0, False), (o_pa + mla_width + ml_width, ml_width, BF16, 1.0, False))
    pa, gate_a, u, z = _in_proj(tok0, mod0, g_norm[0], w0, outs0, *dims)

    def mla_lanes(row, fill):
        n = row.shape[0]
        return jnp.concatenate([jnp.full((n, mla_nope), fill, F32), row,
                                jnp.full((n, LANES - mla_nope - mla_rope), fill, F32)], axis=-1)

    cos_a, sin_a = _rope_tables(seq, ctx_len, mla_rope, mla_lanes)
    q_a, k_a, vt_a = _mla_prep(pa, cos_a, sin_a, mla_g_q[0], mla_g_kv[0], mla_w_uq[0], mla_w_ukv[0],
                               *dims, mla_heads, mla_nope, mla_rope, mla_v)
    mix_a = _flash(q_a, k_a, vt_a, gate_a, *dims, k_heads_per_pair=2, v_rows_per_pair=2 * mla_v,
                   pairs_per_kv=1, ctx_queries=True)

    xc, q_m, k_m, v_m, vt_m, gates = _mlstm_prep(u, ml_conv_w[0], ml_conv_b[0], ml_w_q[0], ml_w_k[0], ml_w_v[0],
                                                 ml_w_gate[0], ml_b_gate[0], *dims)
    h_f, h_b = _mlstm_seq(q_m, k_m, v_m, vt_m, gates, *dims)
    mix_b = _mlstm_out(h_f, h_b, xc, z, ml_g_head[0], ml_skip[0])
    x1 = _out_proj(tok0, mix_a, mix_b, ab_w_out[0], mod0, batch, seq)

    tok1 = _Tokens((x1,), *dims)
    w1 = cd_w_in[0].astype(BF16)
    gq_w, gkv_w = gqa_heads * gqa_dim, gqa_kv * gqa_dim
    o_d = 4 * na_width
    outs1 = ((0, na_width, BF16, 64 ** -0.5 * LOG2E, False), (na_width, na_width, BF16, 1.0, False),
             (2 * na_width, na_width, BF16, 1.0, True), (3 * na_width, na_width, BF16, 1.0, False),
             (o_d, gq_w + gkv_w, F32, 1.0, False), (o_d + gq_w + gkv_w, gkv_w, BF16, 1.0, True),
             (o_d + gq_w + 2 * gkv_w, gq_w, BF16, 1.0, False))
    q_c, k_c, vt_c, gate_c, pd, vt_d, gate_d = _in_proj(tok1, mod1, g_norm[1], w1, outs1, *dims)

    mix_c = _na_attention(q_c, k_c, vt_c, _na_bias(na_rpb[0]), gate_c, *dims)

    gqa_lanes = lambda row, fill: jnp.tile(row, (1, LANES // gqa_dim))
    cos_d, sin_d = _rope_tables(seq, ctx_len, gqa_dim, gqa_lanes)
    q_d, k_d = _gqa_prep(pd, cos_d, sin_d, gqa_g_q[0], gqa_g_k[0], *dims, gqa_heads, gqa_kv, gqa_dim)
    mix_d = _flash(q_d, k_d, vt_d, gate_d, *dims, k_heads_per_pair=1, v_rows_per_pair=gqa_dim,
                   pairs_per_kv=gqa_heads // (2 * gqa_kv), ctx_queries=False)

    out = _out_proj(tok1, mix_c, mix_d, cd_w_out[0], mod1, batch, seq, g_final=g_final)
    return out.reshape(batch, seq, d)
```

```python
import functools

import jax
import jax.numpy as jnp
from jax import lax
from jax.experimental import pallas as pl
from jax.experimental.pallas import tpu as pltpu

F32 = jnp.float32
BF16 = jnp.bfloat16

LANES = 128
TOKEN_TILE = 256
KEY_CHUNK = 512
GRID_W = 64
WIN_R = 8
WIN_C = 16
ML_CHUNK = 128
ML_HEADS = 4
EPS = 1e-6
ROPE_BASE = 10000.0
LOG2E = 1.4426950408889634
NEG = -1e30
VMEM_LIMIT = 56 * 1024 * 1024

_NT = (((1,), (1,)), ((), ()))


def _dot(a, b):
    return jnp.dot(a, b, preferred_element_type=F32)


def _dot_nt(a, b):
    return lax.dot_general(a, b, _NT, preferred_element_type=F32)


def _silu(v):
    return v * (1.0 / (1.0 + jnp.exp(-v)))


def _log_sigmoid(v):
    return -(jnp.maximum(-v, 0.0) + jnp.log1p(jnp.exp(-jnp.abs(v))))


def _params(n_axes):
    return pltpu.CompilerParams(dimension_semantics=("arbitrary",) * n_axes, vmem_limit_bytes=VMEM_LIMIT)


def _full(shape):
    nd = len(shape)
    return pl.BlockSpec(shape, lambda *_: (0,) * nd)


def _mod_body(c_ref, w_ref, b_ref, o_ref):
    s = _silu(c_ref[...])
    o_ref[0] = _dot(s.astype(BF16), w_ref[0].astype(BF16)) + b_ref[0]


def _modulation(cvec, w_mod, b_mod):
    depth, d, n = w_mod.shape
    rows = cvec.shape[0]
    tn = n // 4
    return pl.pallas_call(
        _mod_body,
        out_shape=jax.ShapeDtypeStruct((depth, rows, n), F32),
        grid=(depth, n // tn),
        in_specs=[_full((rows, d)),
                  pl.BlockSpec((1, d, tn), lambda l, j: (l, 0, j)),
                  pl.BlockSpec((1, 1, tn), lambda l, j: (l, 0, j))],
        out_specs=pl.BlockSpec((1, rows, tn), lambda l, j: (l, 0, j)),
        compiler_params=_params(2), name="modulation",
    )(cvec, w_mod, b_mod.reshape(depth, 1, n))


class _Tokens:
    def __init__(self, arrays, batch, seq, ctx_len):
        self.arrays = arrays
        self.split = len(arrays) == 2
        self.n_lat = batch * seq // TOKEN_TILE
        self.n_ctx = batch * ctx_len // TOKEN_TILE
        self.d = arrays[0].shape[-1]

    def specs(self):
        blk = (TOKEN_TILE, self.d)
        if not self.split:
            return [pl.BlockSpec(blk, lambda i: (i, 0))]
        n_lat = self.n_lat
        return [pl.BlockSpec(blk, lambda i: (jnp.minimum(i, n_lat - 1), 0)),
                pl.BlockSpec(blk, lambda i: (jnp.maximum(i - n_lat, 0), 0))]

    def load(self, refs, i):
        if not self.split:
            return refs[0][...]
        return jnp.where(i < self.n_lat, refs[0][...], refs[1][...])


def _mod_spec(n_lat, nj, batch, width):
    return pl.BlockSpec((1, 1, width), lambda i: (jnp.where(i < n_lat, i // nj, batch), 0, 0))


def _in_proj_body(*refs, tok, outs, d):
    n_tok = len(tok.arrays)
    mod_ref, g_ref, w_ref = refs[n_tok:n_tok + 3]
    o_refs = refs[n_tok + 3:]
    i = pl.program_id(0)
    x = tok.load(refs[:n_tok], i)
    y = x * lax.rsqrt(jnp.mean(x * x, axis=-1, keepdims=True) + EPS) * g_ref[...]
    mod = mod_ref[0]
    h = y * (1.0 + mod[:, d:2 * d]) + mod[:, :d]
    acc = _dot(h.astype(BF16), w_ref[...])
    for o_ref, (c0, width, _, scale, transposed) in zip(o_refs, outs):
        v = acc[:, c0:c0 + width]
        if scale != 1.0:
            v = v * scale
        if transposed:
            o_ref[0] = v.T.astype(o_ref.dtype)
        else:
            o_ref[...] = v.astype(o_ref.dtype)


def _in_proj(tok, mod_l, g, w, outs, batch, seq, ctx_len):
    d = tok.d
    nj = seq // TOKEN_TILE
    n_lat, n_all = tok.n_lat, tok.n_lat + tok.n_ctx
    rows = n_all * TOKEN_TILE
    t_len = seq + ctx_len
    out_shape, out_specs = [], []
    for (_, width, dtype, _, transposed) in outs:
        if transposed:
            out_shape.append(jax.ShapeDtypeStruct((batch, width, t_len), dtype))
            out_specs.append(pl.BlockSpec(
                (1, width, TOKEN_TILE),
                lambda i: (jnp.where(i < n_lat, i // nj, i - n_lat), 0, jnp.where(i < n_lat, i % nj, nj))))
        else:
            out_shape.append(jax.ShapeDtypeStruct((rows, width), dtype))
            out_specs.append(pl.BlockSpec((TOKEN_TILE, width), lambda i: (i, 0)))
    body = functools.partial(_in_proj_body, tok=tok, outs=outs, d=d)
    return pl.pallas_call(
        body, out_shape=out_shape, grid=(n_all,),
        in_specs=tok.specs() + [_mod_spec(n_lat, nj, batch, 3 * d), _full((1, d)), _full(w.shape)],
        out_specs=out_specs, compiler_params=_params(1), name="in_proj",
    )(*tok.arrays, mod_l, g.reshape(1, d), w)


def _out_proj_body(*refs, tok, d, final):
    n_tok = len(tok.arrays)
    ma_ref, mb_ref, wa_ref, wb_ref, mod_ref = refs[n_tok:n_tok + 5]
    i = pl.program_id(0)
    x = tok.load(refs[:n_tok], i)
    acc = _dot(ma_ref[...], wa_ref[...]) + _dot(mb_ref[...], wb_ref[...])
    xn = x + mod_ref[0][:, 2 * d:] * acc
    if final:
        gf_ref, o_ref = refs[n_tok + 5:]
        xn = xn * lax.rsqrt(jnp.mean(xn * xn, axis=-1, keepdims=True) + EPS) * gf_ref[...]
    else:
        o_ref = refs[n_tok + 5]
    o_ref[...] = xn


def _out_proj(tok, mix_a, mix_b, w_out, mod_l, batch, seq, g_final=None):
    d = tok.d
    half = mix_a.shape[1]
    nj = seq // TOKEN_TILE
    n_lat = tok.n_lat
    final = g_final is not None
    n_tiles = n_lat if final else n_lat + tok.n_ctx
    wa, wb = w_out[:half].astype(BF16), w_out[half:].astype(BF16)
    tile = lambda width: pl.BlockSpec((TOKEN_TILE, width), lambda i: (i, 0))
    in_specs = tok.specs() + [tile(half), tile(half), _full(wa.shape), _full(wb.shape),
                              _mod_spec(n_lat, nj, batch, 3 * d)]
    args = list(tok.arrays) + [mix_a, mix_b, wa, wb, mod_l]
    if final:
        in_specs.append(_full((1, d)))
        args.append(g_final.reshape(1, d))
    body = functools.partial(_out_proj_body, tok=tok, d=d, final=final)
    return pl.pallas_call(
        body, out_shape=jax.ShapeDtypeStruct((n_tiles * TOKEN_TILE, d), F32), grid=(n_tiles,),
        in_specs=in_specs, out_specs=tile(d), compiler_params=_params(1), name="out_proj",
    )(*args)


def _rope_tables(seq, ctx_len, rot_dim, lane_pattern):
    t = jnp.arange(seq)
    pos = jnp.stack([t // GRID_W, t % GRID_W], axis=-1).astype(F32)
    n_freq = rot_dim // 4
    inv = ROPE_BASE ** (-jnp.arange(n_freq, dtype=F32) / n_freq)
    ang = pos[:, :, None] * inv
    cos, sin = jnp.cos(ang), jnp.sin(ang)
    cos_row = jnp.concatenate([cos[:, 0], cos[:, 0], cos[:, 1], cos[:, 1]], axis=-1)
    sin_row = jnp.concatenate([-sin[:, 0], sin[:, 0], -sin[:, 1], sin[:, 1]], axis=-1)
    cos_t, sin_t = lane_pattern(cos_row, 1.0), lane_pattern(sin_row, 0.0)
    pad = lambda a, v: jnp.concatenate([a, jnp.full((ctx_len, LANES), v, F32)], axis=0)
    return pad(cos_t, 1.0), pad(sin_t, 0.0)


def _rope(x, cos, sin, dist):
    lane = lax.broadcasted_iota(jnp.int32, x.shape, 1)
    first = (lane % (2 * dist)) < dist
    partner = jnp.where(first, pltpu.roll(x, LANES - dist, 1), pltpu.roll(x, dist, 1))
    return x * cos + partner * sin


def _tok_block(b, j, nj, batch):
    return jnp.where(j < nj, b * nj + j, batch * nj + b)


def _mla_prep_body(pa_ref, cos_ref, sin_ref, gq_ref, gkv_ref, wuq_ref, wuk_ref, wuv_ref,
                   q_ref, k_ref, vt_ref, *, heads, q_lora, kv_lora, qscale):
    pa = pa_ref[...]
    cos, sin = cos_ref[...], sin_ref[...]

    def norm(v, g_ref):
        return (v * lax.rsqrt(jnp.mean(v * v, axis=-1, keepdims=True) + EPS) * g_ref[...]).astype(BF16)

    cq = norm(pa[:, :q_lora], gq_ref)
    q_all = _dot(cq, wuq_ref[...])
    for h in range(heads):
        qh = _rope(q_all[:, h * LANES:(h + 1) * LANES], cos, sin, 8)
        q_ref[0, h] = (qh * qscale).astype(BF16)
    ckv = norm(pa[:, q_lora:q_lora + kv_lora], gkv_ref)
    k_nope = _dot(ckv, wuk_ref[...])
    k_rope = _rope(pa[:, q_lora + kv_lora:], cos, sin, 8)
    for h in range(heads):
        k_ref[0, h] = (k_nope[:, h * LANES:(h + 1) * LANES] + k_rope).astype(BF16)
    vt_ref[0] = _dot(ckv, wuv_ref[...]).T.astype(BF16)


def _mla_prep(pa, cos, sin, g_q, g_kv, w_uq, w_ukv, batch, seq, ctx_len, heads, nope, rope, v_dim):
    q_lora, kv_lora = g_q.shape[0], g_kv.shape[0]
    nj = seq // TOKEN_TILE
    t_len = seq + ctx_len
    pad = LANES - nope - rope
    wuq = jnp.pad(w_uq.reshape(q_lora, heads, nope + rope), ((0, 0), (0, 0), (0, pad)))
    wuq = wuq.reshape(q_lora, heads * LANES).astype(BF16)
    wkv = w_ukv.reshape(kv_lora, heads, nope + v_dim)
    wuk = jnp.pad(wkv[..., :nope], ((0, 0), (0, 0), (0, LANES - nope))).reshape(kv_lora, heads * LANES).astype(BF16)
    wuv = wkv[..., nope:].reshape(kv_lora, heads * v_dim).astype(BF16)
    body = functools.partial(_mla_prep_body, heads=heads, q_lora=q_lora, kv_lora=kv_lora,
                             qscale=(nope + rope) ** -0.5 * LOG2E)
    head_major = pl.BlockSpec((1, heads, TOKEN_TILE, LANES), lambda b, j: (b, 0, j, 0))
    return pl.pallas_call(
        body,
        out_shape=[jax.ShapeDtypeStruct((batch, heads, t_len, LANES), BF16),
                   jax.ShapeDtypeStruct((batch, heads, t_len, LANES), BF16),
                   jax.ShapeDtypeStruct((batch, heads * v_dim, t_len), BF16)],
        grid=(batch, nj + 1),
        in_specs=[pl.BlockSpec((TOKEN_TILE, pa.shape[1]), lambda b, j: (_tok_block(b, j, nj, batch), 0)),
                  pl.BlockSpec((TOKEN_TILE, LANES), lambda b, j: (j, 0)),
                  pl.BlockSpec((TOKEN_TILE, LANES), lambda b, j: (j, 0)),
                  _full((1, q_lora)), _full((1, kv_lora)), _full(wuq.shape), _full(wuk.shape), _full(wuv.shape)],
        out_specs=[head_major, head_major,
                   pl.BlockSpec((1, heads * v_dim, TOKEN_TILE), lambda b, j: (b, 0, j))],
        compiler_params=_params(2), name="mla_prep",
    )(pa, cos, sin, g_q.reshape(1, -1), g_kv.reshape(1, -1), wuq, wuk, wuv)


def _gqa_prep_body(pd_ref, cos_ref, sin_ref, gq_ref, gk_ref, bd_ref, q_ref, k_ref, *, q_width, qscale):
    pd = pd_ref[...]
    cos, sin = cos_ref[...], sin_ref[...]
    bd = bd_ref[...]
    lane = lax.broadcasted_iota(jnp.int32, (TOKEN_TILE, LANES), 1)
    upper = lane >= LANES // 2

    def head_norm(v, g_ref):
        ms = jnp.dot(v * v, bd, preferred_element_type=F32, precision=lax.Precision.HIGHEST)
        return v * lax.rsqrt(ms + EPS) * g_ref[...]

    n_blocks = q_width // LANES
    for i in range(n_blocks):
        xq = _rope(head_norm(pd[:, i * LANES:(i + 1) * LANES], gq_ref), cos, sin, 16) * qscale
        t = (2 * i) // n_blocks
        in_half = upper if t == 1 else jnp.logical_not(upper)
        swapped = pltpu.roll(xq, LANES // 2, 1)
        q_ref[0, 2 * i + t] = jnp.where(in_half, xq, 0.0).astype(BF16)
        q_ref[0, 2 * i + 1 - t] = jnp.where(in_half, swapped, 0.0).astype(BF16)
    xk = _rope(head_norm(pd[:, q_width:], gk_ref), cos, sin, 16)
    k_ref[0, 0] = xk.astype(BF16)


def _gqa_prep(pd, cos, sin, g_q, g_k, batch, seq, ctx_len, heads, kv_heads, head_dim):
    assert kv_heads * head_dim == LANES and heads % (2 * kv_heads) == 0
    nj = seq // TOKEN_TILE
    t_len = seq + ctx_len
    q_width = heads * head_dim
    grp = jnp.arange(LANES) // head_dim
    bd = (grp[:, None] == grp[None, :]).astype(F32) / head_dim
    rep = LANES // head_dim
    body = functools.partial(_gqa_prep_body, q_width=q_width, qscale=head_dim ** -0.5 * LOG2E)
    return pl.pallas_call(
        body,
        out_shape=[jax.ShapeDtypeStruct((batch, heads, t_len, LANES), BF16),
                   jax.ShapeDtypeStruct((batch, 1, t_len, LANES), BF16)],
        grid=(batch, nj + 1),
        in_specs=[pl.BlockSpec((TOKEN_TILE, pd.shape[1]), lambda b, j: (_tok_block(b, j, nj, batch), 0)),
                  pl.BlockSpec((TOKEN_TILE, LANES), lambda b, j: (j, 0)),
                  pl.BlockSpec((TOKEN_TILE, LANES), lambda b, j: (j, 0)),
                  _full((1, LANES)), _full((1, LANES)), _full((LANES, LANES))],
        out_specs=[pl.BlockSpec((1, heads, TOKEN_TILE, LANES), lambda b, j: (b, 0, j, 0)),
                   pl.BlockSpec((1, 1, TOKEN_TILE, LANES), lambda b, j: (b, 0, j, 0))],
        compiler_params=_params(2), name="gqa_prep",
    )(pd, cos, sin, jnp.tile(g_q, rep).reshape(1, LANES), jnp.tile(g_k, rep).reshape(1, LANES), bd)


def _flash_body(q_ref, k_ref, vt_ref, g_ref, o_ref, s_scr, *, k_sel, v_off, v_dim, seq, ctx_len, nj, ctx_queries):
    j = pl.program_id(2)
    lat_chunks = tuple((c * KEY_CHUNK, KEY_CHUNK) for c in range(seq // KEY_CHUNK))
    ctx_chunk = ((seq, ctx_len),)

    def attend(chunks):
        outs = []
        for a in range(2):
            q = q_ref[0, a]
            m = None
            for (st, sz) in chunks:
                s_t = _dot_nt(k_ref[0, k_sel[a], st:st + sz, :], q)
                s_scr[a, st:st + sz, :] = s_t
                cm = jnp.max(s_t, axis=0, keepdims=True)
                m = cm if m is None else jnp.maximum(m, cm)
            l = jnp.zeros((1, TOKEN_TILE), F32)
            acc = jnp.zeros((v_dim, TOKEN_TILE), F32)
            for (st, sz) in chunks:
                p = jnp.exp2(s_scr[a, st:st + sz, :] - m)
                l = l + jnp.sum(p, axis=0, keepdims=True)
                acc = acc + _dot(vt_ref[0, v_off[a]:v_off[a] + v_dim, st:st + sz], p.astype(BF16))
            outs.append(acc / l)
        o2 = jnp.concatenate(outs, axis=0).T
        o_ref[...] = (o2 * _silu(g_ref[...].astype(F32))).astype(o_ref.dtype)

    if ctx_queries:
        @pl.when(j < nj)
        def _():
            attend(lat_chunks + ctx_chunk)

        @pl.when(j == nj)
        def _():
            attend(ctx_chunk)
    else:
        attend(lat_chunks + ctx_chunk)


def _flash(q, k, vt, gate, batch, seq, ctx_len, k_heads_per_pair, v_rows_per_pair, pairs_per_kv, ctx_queries):
    heads = q.shape[1]
    t_len = seq + ctx_len
    nj = seq // TOKEN_TILE
    v_dim = LANES // 2
    k_sel = (0, 1) if k_heads_per_pair == 2 else (0, 0)
    v_off = (0, v_dim) if v_rows_per_pair == 2 * v_dim else (0, 0)
    k_blocks = k.shape[1] // k_heads_per_pair
    body = functools.partial(_flash_body, k_sel=k_sel, v_off=v_off, v_dim=v_dim, seq=seq, ctx_len=ctx_len,
                             nj=nj, ctx_queries=ctx_queries)
    tok = lambda b, p, j: (_tok_block(b, j, nj, batch), p)
    out_rows = gate.shape[0] if ctx_queries else batch * seq
    return pl.pallas_call(
        body, out_shape=jax.ShapeDtypeStruct((out_rows, gate.shape[1]), BF16),
        grid=(batch, heads // 2, nj + 1 if ctx_queries else nj),
        in_specs=[pl.BlockSpec((1, 2, TOKEN_TILE, LANES), lambda b, p, j: (b, p, j, 0)),
                  pl.BlockSpec((1, k_heads_per_pair, t_len, LANES),
                               lambda b, p, j: (b, (p // pairs_per_kv) % k_blocks, 0, 0)),
                  pl.BlockSpec((1, v_rows_per_pair, t_len), lambda b, p, j: (b, p // pairs_per_kv, 0)),
                  pl.BlockSpec((TOKEN_TILE, LANES), tok)],
        out_specs=pl.BlockSpec((TOKEN_TILE, LANES), tok),
        scratch_shapes=[pltpu.VMEM((2, t_len, TOKEN_TILE), F32)],
        compiler_params=_params(3), name="flash_attention",
    )(q, k, vt, gate)


NA_KEY_ROWS = 10
NA_BIAS_ROWS = 19


def _na_bias_body(rpb_ref, o_ref):
    p = pl.program_id(0)
    shape = (GRID_W, LANES)
    kc = lax.broadcasted_iota(jnp.int32, shape, 0)
    lane = lax.broadcasted_iota(jnp.int32, shape, 1)
    qc = lane % GRID_W
    upper = lane >= GRID_W
    rel = kc - qc + (WIN_C - 1)
    c0 = jnp.clip(qc - WIN_C // 2, 0, GRID_W - WIN_C)
    col_ok = (kc >= c0) & (kc < c0 + WIN_C)
    n_rel_r, n_rel_c = 2 * WIN_R - 1, 2 * WIN_C - 1
    o_ref[...] = jnp.zeros(o_ref.shape, F32)

    def block(dd, carry):
        base0 = (2 * p) * (n_rel_r * n_rel_c) + dd * n_rel_c
        base1 = base0 + n_rel_r * n_rel_c
        val = jnp.zeros(shape, F32)
        for jj in range(n_rel_c):
            val = jnp.where(rel == jj, jnp.where(upper, rpb_ref[base1 + jj], rpb_ref[base0 + jj]), val)
        off = pl.multiple_of((dd + 2) * GRID_W, GRID_W)
        o_ref[0, pl.ds(off, GRID_W), :] = jnp.where(col_ok, val * LOG2E, NEG)
        return carry

    lax.fori_loop(0, n_rel_r, block, 0)


def _na_bias(rpb):
    heads = rpb.shape[0]
    return pl.pallas_call(
        _na_bias_body,
        out_shape=jax.ShapeDtypeStruct((heads // 2, NA_BIAS_ROWS * GRID_W, LANES), F32),
        grid=(heads // 2,),
        in_specs=[pl.BlockSpec(memory_space=pltpu.SMEM)],
        out_specs=pl.BlockSpec((1, NA_BIAS_ROWS * GRID_W, LANES), lambda p: (p, 0, 0)),
        compiler_params=_params(1), name="na_bias",
    )(rpb.reshape(-1))


def _na_body(q_ref, kl_ref, kc_ref, vt_ref, bias_ref, g_ref, o_ref, *, seq, ctx_len, rows_per_step):
    rblk = pl.program_id(2)
    n_rows = seq // GRID_W
    span = NA_KEY_ROWS * GRID_W
    lane = lax.broadcasted_iota(jnp.int32, (GRID_W, LANES), 1)
    lower = lane < GRID_W
    key_row = lax.broadcasted_iota(jnp.int32, (span, LANES), 0) // GRID_W
    k_ctx = kc_ref[...]
    v_ctx = vt_ref[0, :, seq:seq + ctx_len]
    for rr in range(rows_per_step):
        r = rblk * rows_per_step + rr
        rs = jnp.clip(r - WIN_R // 2, 0, n_rows - WIN_R)
        rs_al = 2 * jnp.minimum(rs // 2, (n_rows - NA_KEY_ROWS) // 2)
        q_r = q_ref[rr * GRID_W:(rr + 1) * GRID_W, :]
        zero = jnp.zeros_like(q_r)
        q2 = jnp.concatenate([jnp.where(lower, q_r, zero), jnp.where(lower, zero, q_r)], axis=0)
        k_off = pl.multiple_of(rs_al * GRID_W, LANES)
        b_off = pl.multiple_of((rs_al - r + 9) * GRID_W, GRID_W)
        s_t = _dot_nt(kl_ref[pl.ds(k_off, span), :], q2) + bias_ref[0, pl.ds(b_off, span), :]
        a0 = rs - rs_al
        s_t = jnp.where((key_row >= a0) & (key_row < a0 + WIN_R), s_t, NEG)
        s_c = _dot_nt(k_ctx, q2)
        m = jnp.maximum(jnp.max(s_t, axis=0, keepdims=True), jnp.max(s_c, axis=0, keepdims=True))
        p_t = jnp.exp2(s_t - m)
        p_c = jnp.exp2(s_c - m)
        l = jnp.sum(p_t, axis=0, keepdims=True) + jnp.sum(p_c, axis=0, keepdims=True)
        r_t = _dot(vt_ref[0, :, pl.ds(k_off, span)], p_t.astype(BF16)) + _dot(v_ctx, p_c.astype(BF16))
        r_n = (r_t / l).T
        o = jnp.where(lower, r_n[:GRID_W], r_n[GRID_W:])
        g = g_ref[rr * GRID_W:(rr + 1) * GRID_W, :].astype(F32)
        o_ref[rr * GRID_W:(rr + 1) * GRID_W, :] = (o * _silu(g)).astype(o_ref.dtype)


def _na_attention(q, k, vt, bias, gate, batch, seq, ctx_len):
    pairs = q.shape[1] // LANES
    t_len = seq + ctx_len
    rows_per_step = TOKEN_TILE // GRID_W
    n_steps = seq // TOKEN_TILE
    lat_tile = lambda b, p, s: (b * n_steps + s, p)
    body = functools.partial(_na_body, seq=seq, ctx_len=ctx_len, rows_per_step=rows_per_step)
    return pl.pallas_call(
        body, out_shape=jax.ShapeDtypeStruct((batch * seq, q.shape[1]), BF16),
        grid=(batch, pairs, n_steps),
        in_specs=[pl.BlockSpec((TOKEN_TILE, LANES), lat_tile),
                  pl.BlockSpec((seq, LANES), lambda b, p, s: (b, p)),
                  pl.BlockSpec((ctx_len, LANES), lambda b, p, s: (batch * seq // ctx_len + b, p)),
                  pl.BlockSpec((1, LANES, t_len), lambda b, p, s: (b, p, 0)),
                  pl.BlockSpec((1, NA_BIAS_ROWS * GRID_W, LANES), lambda b, p, s: (p, 0, 0)),
                  pl.BlockSpec((TOKEN_TILE, LANES), lat_tile)],
        out_specs=pl.BlockSpec((TOKEN_TILE, LANES), lat_tile),
        compiler_params=_params(3), name="neighborhood_attention",
    )(q, k, k, vt, bias, gate)


def _seg_scan(v, reverse):
    n = v.shape[1]
    lane = lax.broadcasted_iota(jnp.int32, v.shape, 1) % ML_CHUNK
    k = 1
    while k < ML_CHUNK:
        if reverse:
            v = v + jnp.where(lane < ML_CHUNK - k, pltpu.roll(v, n - k, 1), 0.0)
        else:
            v = v + jnp.where(lane >= k, pltpu.roll(v, k, 1), 0.0)
        k *= 2
    return v


def _mlstm_prep_body(u_ref, up_ref, un_ref, cw_ref, cb_ref, wqk_ref, wv_ref, wg_ref, bg_ref,
                     xc_ref, q_ref, k_ref, v_ref, vt_ref, g_ref, *, nj, width, kscale):
    j = pl.program_id(1)
    u = u_ref[...]
    row = lax.broadcasted_iota(jnp.int32, u.shape, 0)
    prev = jnp.where((j > 0) & (j < nj), up_ref[7:8, :], 0.0)
    nxt = jnp.where(j < nj - 1, un_ref[0:1, :], 0.0)
    u_m1 = jnp.where(row == 0, prev, pltpu.roll(u, 1, 0))
    u_p1 = jnp.where(row == TOKEN_TILE - 1, nxt, pltpu.roll(u, TOKEN_TILE - 1, 0))
    cw = cw_ref[...]
    xc = _silu(u_m1 * cw[0:1] + u * cw[1:2] + u_p1 * cw[2:3] + cb_ref[...])
    xcb = xc.astype(BF16)
    xc_ref[...] = xcb
    qk = _dot(xcb, wqk_ref[...])
    v = _dot(u.astype(BF16), wv_ref[...])
    qb, kb, vb = qk[:, :width].astype(BF16), qk[:, width:].astype(BF16), v.astype(BF16)
    q_ref[...] = qb
    k_ref[...] = (qk[:, width:] * kscale).astype(BF16)
    v_ref[...] = vb
    vt_ref[0] = v.T.astype(BF16)
    pre = _dot_nt(wg_ref[0], qb) + _dot_nt(wg_ref[1], kb) + _dot_nt(wg_ref[2], vb) + bg_ref[...]
    grow = lax.broadcasted_iota(jnp.int32, pre.shape, 0) % 16
    gates = jnp.where(grow < ML_HEADS, pre, jnp.where(grow < 3 * ML_HEADS, _log_sigmoid(pre), 0.0))
    scanned = jnp.concatenate([_seg_scan(gates[:16], False), _seg_scan(gates[16:], True)], axis=0)
    g_ref[0] = jnp.where((grow >= 2 * ML_HEADS) & (grow < 3 * ML_HEADS), scanned, gates)


def _mlstm_prep(u, conv_w, conv_b, w_q, w_k, w_v, w_gate, b_gate, batch, seq, ctx_len):
    heads, hd = w_q.shape[0], w_q.shape[1]
    assert heads == ML_HEADS and hd == ML_CHUNK
    width = heads * hd
    nj = seq // TOKEN_TILE
    t_len = seq + ctx_len
    rows = u.shape[0]

    def block_diag(w):
        eye = jnp.eye(heads, dtype=w.dtype)
        return (eye[:, None, :, None] * w[:, :, None, :]).reshape(width, width)

    wqk = jnp.concatenate([block_diag(w_q), block_diag(w_k)], axis=1).astype(BF16)
    wv = block_diag(w_v).astype(BF16)
    wg = w_gate.reshape(2, heads, 3, hd, 2 * heads).transpose(2, 0, 4, 1, 3).reshape(3, 2, 2 * heads, width)
    zeros = jnp.zeros((3, 2, heads, width), w_gate.dtype)
    wg = jnp.concatenate([wg, wg[:, :, heads:], zeros], axis=2).reshape(3, 32, width).astype(BF16)
    bzero = jnp.zeros((2, heads), b_gate.dtype)
    bg = jnp.concatenate([b_gate, b_gate[:, heads:], bzero], axis=1).reshape(32, 1)
    n_halo = rows // 8
    tokb = lambda b, j: _tok_block(b, j, nj, batch)
    tile = pl.BlockSpec((TOKEN_TILE, width), lambda b, j: (tokb(b, j), 0))
    body = functools.partial(_mlstm_prep_body, nj=nj, width=width, kscale=hd ** -0.5)
    per_tile = TOKEN_TILE // 8
    return pl.pallas_call(
        body,
        out_shape=[jax.ShapeDtypeStruct((rows, width), BF16)] * 4
        + [jax.ShapeDtypeStruct((batch, width, t_len), BF16), jax.ShapeDtypeStruct((batch, 32, t_len), F32)],
        grid=(batch, nj + 1),
        in_specs=[tile,
                  pl.BlockSpec((8, width), lambda b, j: (jnp.maximum(tokb(b, j) * per_tile - 1, 0), 0)),
                  pl.BlockSpec((8, width), lambda b, j: (jnp.minimum((tokb(b, j) + 1) * per_tile, n_halo - 1), 0)),
                  _full((3, width)), _full((1, width)), _full(wqk.shape), _full(wv.shape),
                  _full(wg.shape), _full((32, 1))],
        out_specs=[tile, tile, tile, tile,
                   pl.BlockSpec((1, width, TOKEN_TILE), lambda b, j: (b, 0, j)),
                   pl.BlockSpec((1, 32, TOKEN_TILE), lambda b, j: (b, 0, j))],
        compiler_params=_params(2), name="mlstm_prep",
    )(u, u, u, conv_w, conv_b.reshape(1, width), wqk, wv, wg, bg)


def _mlstm_seq_body(qf, kf, vf, vtf, gf, qb, kb, vb, vtb, gb, hf_ref, hb_ref, c_s, n_s, m_s):
    t = pl.program_id(1)
    L = ML_CHUNK

    @pl.when(t == 0)
    def _():
        c_s[...] = jnp.zeros(c_s.shape, F32)
        n_s[...] = jnp.zeros(n_s.shape, F32)
        m_s[...] = jnp.zeros(m_s.shape, F32)

    li = lax.broadcasted_iota(jnp.int32, (L, L), 0)
    si = lax.broadcasted_iota(jnp.int32, (L, L), 1)
    streams = ((qf, kf, vf, vtf, gf, hf_ref, si <= li), (qb, kb, vb, vtb, gb, hb_ref, si >= li))
    for d, (q_ref, k_ref, v_ref, vt_ref, g_ref, h_ref, incl) in enumerate(streams):
        g = g_ref[0]
        for h in range(ML_HEADS):
            idx = d * ML_HEADS + h
            cols = slice(h * L, (h + 1) * L)
            q, k, v, vt = q_ref[:, cols], k_ref[:, cols], v_ref[:, cols], vt_ref[0, cols, :]
            i_row, f_row, b_row = g[h:h + 1], g[ML_HEADS + h:ML_HEADS + h + 1], g[2 * ML_HEADS + h:2 * ML_HEADS + h + 1]
            c_st, n_st, m_st = c_s[idx], n_s[idx], m_s[idx][:, 0:1]
            b_col = jnp.sum(jnp.where(incl, f_row, 0.0), axis=1, keepdims=True)
            dmat = jnp.where(incl, b_col - b_row + i_row, NEG)
            m_inter = b_col + m_st
            m_t = jnp.maximum(m_inter, jnp.max(dmat, axis=1, keepdims=True))
            w_inter = jnp.exp(m_inter - m_t)
            s_mat = jnp.exp(dmat - m_t) * _dot_nt(q, k)
            num = w_inter * _dot_nt(q, c_st.astype(BF16)) + _dot(s_mat.astype(BF16), v)
            qn = jnp.sum(q.astype(F32) * n_st, axis=1, keepdims=True)
            den = w_inter * qn + jnp.sum(s_mat, axis=1, keepdims=True)
            h_ref[:, cols] = num / jnp.maximum(jnp.abs(den), jnp.exp(-m_t))
            b_last = jnp.sum(f_row, axis=1, keepdims=True)
            g_row = b_last - b_row + i_row
            m_new = jnp.maximum(b_last + m_st, jnp.max(g_row, axis=1, keepdims=True))
            decay = jnp.exp(b_last + m_st - m_new)
            w_row = jnp.exp(g_row - m_new)
            c_s[idx] = decay * c_st + _dot((vt.astype(F32) * w_row).astype(BF16), k)
            w8 = jnp.broadcast_to(w_row, (8, L)).astype(BF16)
            n_s[idx] = decay * n_st + _dot(w8, k)[0:1]
            m_s[idx] = jnp.broadcast_to(m_new, (1, L))


def _mlstm_seq(q, k, v, vt, gates, batch, seq, ctx_len):
    width = q.shape[1]
    L = ML_CHUNK
    n_lat, n_ctx = seq // L, ctx_len // L
    n_chunks = n_lat + n_ctx
    fwd = lambda t: (t + n_lat) % n_chunks
    bwd = lambda t: n_chunks - 1 - t
    rowblk = lambda b, c: jnp.where(c < n_lat, b * n_lat + c, batch * n_lat + b * n_ctx + (c - n_lat))

    def stream(chunk_of, d):
        tok = pl.BlockSpec((L, width), lambda b, t: (rowblk(b, chunk_of(t)), 0))
        return [tok, tok, tok,
                pl.BlockSpec((1, width, L), lambda b, t: (b, 0, chunk_of(t))),
                pl.BlockSpec((1, 16, L), lambda b, t: (b, d, chunk_of(t)))]

    out_f = pl.BlockSpec((L, width), lambda b, t: (rowblk(b, fwd(t)), 0))
    out_b = pl.BlockSpec((L, width), lambda b, t: (rowblk(b, bwd(t)), 0))
    n_state = 2 * ML_HEADS
    return pl.pallas_call(
        _mlstm_seq_body,
        out_shape=[jax.ShapeDtypeStruct(q.shape, F32)] * 2,
        grid=(batch, n_chunks),
        in_specs=stream(fwd, 0) + stream(bwd, 1),
        out_specs=[out_f, out_b],
        scratch_shapes=[pltpu.VMEM((n_state, L, L), F32), pltpu.VMEM((n_state, 1, L), F32),
                        pltpu.VMEM((n_state, 1, L), F32)],
        compiler_params=_params(2), name="mlstm_recurrence",
    )(q, k, v, vt, gates, q, k, v, vt, gates)


def _mlstm_out_body(hf_ref, hb_ref, xc_ref, z_ref, gh_ref, sk_ref, o_ref):
    h = hf_ref[...] + hb_ref[...]
    L = ML_CHUNK
    for hd in range(ML_HEADS):
        cols = slice(hd * L, (hd + 1) * L)
        hh = h[:, cols]
        mu = jnp.mean(hh, axis=-1, keepdims=True)
        var = jnp.mean(jnp.square(hh - mu), axis=-1, keepdims=True)
        hn = (hh - mu) * lax.rsqrt(var + EPS) * gh_ref[:, cols]
        o = (hn + sk_ref[:, cols] * xc_ref[:, cols].astype(F32)) * _silu(z_ref[:, cols].astype(F32))
        o_ref[:, cols] = o.astype(o_ref.dtype)


def _mlstm_out(hf, hb, xc, z, g_head, skip):
    rows, width = hf.shape
    tile = pl.BlockSpec((TOKEN_TILE, width), lambda i: (i, 0))
    return pl.pallas_call(
        _mlstm_out_body, out_shape=jax.ShapeDtypeStruct((rows, width), BF16), grid=(rows // TOKEN_TILE,),
        in_specs=[tile, tile, tile, tile, _full((1, width)), _full((1, width))],
        out_specs=tile, compiler_params=_params(1), name="mlstm_out",
    )(hf, hb, xc, z, g_head.reshape(1, width), skip.reshape(1, width))


def kernel(x, c, ctx, c_ctx, w_mod, b_mod, g_norm, ab_w_in, ab_w_out, mla_g_q, mla_w_uq, mla_g_kv, mla_w_ukv,
           ml_conv_w, ml_conv_b, ml_w_q, ml_w_k, ml_w_v, ml_w_gate, ml_b_gate, ml_g_head, ml_skip,
           cd_w_in, cd_w_out, na_rpb, gqa_g_q, gqa_g_k, g_final):
    batch, seq, d = x.shape
    ctx_len = ctx.shape[1]
    assert ctx_len == TOKEN_TILE and seq % KEY_CHUNK == 0 and seq // GRID_W >= NA_KEY_ROWS
    dims = (batch, seq, ctx_len)

    mla_heads, mla_rope, mla_v = 8, 32, 64
    mla_nope = mla_w_uq.shape[2] // mla_heads - mla_rope
    q_lora, kv_lora = mla_g_q.shape[1], mla_g_kv.shape[1]
    ml_width = ml_conv_w.shape[2]
    mla_width = mla_heads * mla_v
    gqa_heads, gqa_dim = 8, gqa_g_q.shape[1]
    gqa_kv = (cd_w_in.shape[2] - 4 * 512 - 2 * gqa_heads * gqa_dim) // (2 * gqa_dim)
    na_width = na_rpb.shape[1] * 64

    mod_rows = -(-(batch + 1) // 8) * 8
    cvec = jnp.concatenate([c, c_ctx[None], jnp.zeros((mod_rows - batch - 1, d), F32)], axis=0)
    mod = _modulation(cvec, w_mod, b_mod)
    mod0 = mod[0].reshape(mod_rows, 1, 3 * d)
    mod1 = mod[1].reshape(mod_rows, 1, 3 * d)

    tok0 = _Tokens((x.reshape(batch * seq, d), ctx.reshape(batch * ctx_len, d)), *dims)
    w_in = ab_w_in[0]
    s1 = q_lora + kv_lora
    zcol = lambda n: jnp.zeros((d, n), w_in.dtype)
    w0 = jnp.concatenate([w_in[:, :s1], zcol(mla_nope), w_in[:, s1:s1 + mla_rope],
                          zcol(LANES - mla_nope - mla_rope), w_in[:, s1 + mla_rope:]], axis=1).astype(BF16)
    o_pa = s1 + LANES
    outs0 = ((0, o_pa, F32, 1.0, False), (o_pa, mla_width, BF16, 1.0, False),
             (o_pa + mla_width, ml_width, F32, 1.0, False), (o_pa + mla_width + ml_width, ml_width, BF16, 1.0, False))
    pa, gate_a, u, z = _in_proj(tok0, mod0, g_norm[0], w0, outs0, *dims)

    def mla_lanes(row, fill):
        n = row.shape[0]
        return jnp.concatenate([jnp.full((n, mla_nope), fill, F32), row,
                                jnp.full((n, LANES - mla_nope - mla_rope), fill, F32)], axis=-1)

    cos_a, sin_a = _rope_tables(seq, ctx_len, mla_rope, mla_lanes)
    q_a, k_a, vt_a = _mla_prep(pa, cos_a, sin_a, mla_g_q[0], mla_g_kv[0], mla_w_uq[0], mla_w_ukv[0],
                               *dims, mla_heads, mla_nope, mla_rope, mla_v)
    mix_a = _flash(q_a, k_a, vt_a, gate_a, *dims, k_heads_per_pair=2, v_rows_per_pair=2 * mla_v,
                   pairs_per_kv=1, ctx_queries=True)

    xc, q_m, k_m, v_m, vt_m, gates = _mlstm_prep(u, ml_conv_w[0], ml_conv_b[0], ml_w_q[0], ml_w_k[0], ml_w_v[0],
                                                 ml_w_gate[0], ml_b_gate[0], *dims)
    h_f, h_b = _mlstm_seq(q_m, k_m, v_m, vt_m, gates, *dims)
    mix_b = _mlstm_out(h_f, h_b, xc, z, ml_g_head[0], ml_skip[0])
    x1 = _out_proj(tok0, mix_a, mix_b, ab_w_out[0], mod0, batch, seq)

    tok1 = _Tokens((x1,), *dims)
    w1 = cd_w_in[0].astype(BF16)
    gq_w, gkv_w = gqa_heads * gqa_dim, gqa_kv * gqa_dim
    o_d = 4 * na_width
    outs1 = ((0, na_width, BF16, 64 ** -0.5 * LOG2E, False), (na_width, na_width, BF16, 1.0, False),
             (2 * na_width, na_width, BF16, 1.0, True), (3 * na_width, na_width, BF16, 1.0, False),
             (o_d, gq_w + gkv_w, F32, 1.0, False), (o_d + gq_w + gkv_w, gkv_w, BF16, 1.0, True),
             (o_d + gq_w + 2 * gkv_w, gq_w, BF16, 1.0, False))
    q_c, k_c, vt_c, gate_c, pd, vt_d, gate_d = _in_proj(tok1, mod1, g_norm[1], w1, outs1, *dims)

    mix_c = _na_attention(q_c, k_c, vt_c, _na_bias(na_rpb[0]), gate_c, *dims)

    gqa_lanes = lambda row, fill: jnp.tile(row, (1, LANES // gqa_dim))
    cos_d, sin_d = _rope_tables(seq, ctx_len, gqa_dim, gqa_lanes)
    q_d, k_d = _gqa_prep(pd, cos_d, sin_d, gqa_g_q[0], gqa_g_k[0], *dims, gqa_heads, gqa_kv, gqa_dim)
    mix_d = _flash(q_d, k_d, vt_d, gate_d, *dims, k_heads_per_pair=1, v_rows_per_pair=gqa_dim,
                   pairs_per_kv=gqa_heads // (2 * gqa_kv), ctx_queries=False)

    out = _out_proj(tok1, mix_c, mix_d, cd_w_out[0], mod1, batch, seq, g_final=g_final)
    return out.reshape(batch, seq, d)
```

```python
import functools

import jax
import jax.numpy as jnp
from jax import lax
from jax.experimental import pallas as pl
from jax.experimental.pallas import tpu as pltpu

F32 = jnp.float32
BF16 = jnp.bfloat16

LANES = 128
TOKEN_TILE = 256
KEY_CHUNK = 512
GRID_W = 64
WIN_R = 8
WIN_C = 16
ML_CHUNK = 128
ML_HEADS = 4
EPS = 1e-6
ROPE_BASE = 10000.0
LOG2E = 1.4426950408889634
NEG = -1e30
VMEM_LIMIT = 56 * 1024 * 1024

_NT = (((1,), (1,)), ((), ()))


def _dot(a, b):
    return jnp.dot(a, b, preferred_element_type=F32)


def _dot_nt(a, b):
    return lax.dot_general(a, b, _NT, preferred_element_type=F32)


def _silu(v):
    return v * (1.0 / (1.0 + jnp.exp(-v)))


def _log_sigmoid(v):
    return -(jnp.maximum(-v, 0.0) + jnp.log1p(jnp.exp(-jnp.abs(v))))


def _params(n_axes):
    return pltpu.CompilerParams(dimension_semantics=("arbitrary",) * n_axes, vmem_limit_bytes=VMEM_LIMIT)


def _full(shape):
    nd = len(shape)
    return pl.BlockSpec(shape, lambda *_: (0,) * nd)


def _mod_body(c_ref, w_ref, b_ref, o_ref):
    s = _silu(c_ref[...])
    o_ref[0] = _dot(s.astype(BF16), w_ref[0].astype(BF16)) + b_ref[0]


def _modulation(cvec, w_mod, b_mod):
    depth, d, n = w_mod.shape
    rows = cvec.shape[0]
    tn = n // 4
    return pl.pallas_call(
        _mod_body,
        out_shape=jax.ShapeDtypeStruct((depth, rows, n), F32),
        grid=(depth, n // tn),
        in_specs=[_full((rows, d)),
                  pl.BlockSpec((1, d, tn), lambda l, j: (l, 0, j)),
                  pl.BlockSpec((1, 1, tn), lambda l, j: (l, 0, j))],
        out_specs=pl.BlockSpec((1, rows, tn), lambda l, j: (l, 0, j)),
        compiler_params=_params(2), name="modulation",
    )(cvec, w_mod, b_mod.reshape(depth, 1, n))


class _Tokens:
    def __init__(self, arrays, batch, seq, ctx_len):
        self.arrays = arrays
        self.split = len(arrays) == 2
        self.n_lat = batch * seq // TOKEN_TILE
        self.n_ctx = batch * ctx_len // TOKEN_TILE
        self.d = arrays[0].shape[-1]

    def specs(self):
        blk = (TOKEN_TILE, self.d)
        if not self.split:
            return [pl.BlockSpec(blk, lambda i: (i, 0))]
        n_lat = self.n_lat
        return [pl.BlockSpec(blk, lambda i: (jnp.minimum(i, n_lat - 1), 0)),
                pl.BlockSpec(blk, lambda i: (jnp.maximum(i - n_lat, 0), 0))]

    def load(self, refs, i):
        if not self.split:
            return refs[0][...]
        return jnp.where(i < self.n_lat, refs[0][...], refs[1][...])


def _mod_spec(n_lat, nj, batch, width):
    return pl.BlockSpec((1, 1, width), lambda i: (jnp.where(i < n_lat, i // nj, batch), 0, 0))


def _in_proj_body(*refs, tok, outs, d):
    n_tok = len(tok.arrays)
    mod_ref, g_ref, w_ref = refs[n_tok:n_tok + 3]
    o_refs = refs[n_tok + 3:]
    i = pl.program_id(0)
    x = tok.load(refs[:n_tok], i)
    y = x * lax.rsqrt(jnp.mean(x * x, axis=-1, keepdims=True) + EPS) * g_ref[...]
    mod = mod_ref[0]
    h = y * (1.0 + mod[:, d:2 * d]) + mod[:, :d]
    acc = _dot(h.astype(BF16), w_ref[...])
    for o_ref, (c0, width, _, scale, transposed) in zip(o_refs, outs):
        v = acc[:, c0:c0 + width]
        if scale != 1.0:
            v = v * scale
        if transposed:
            o_ref[0] = v.T.astype(o_ref.dtype)
        else:
            o_ref[...] = v.astype(o_ref.dtype)


def _in_proj(tok, mod_l, g, w, outs, batch, seq, ctx_len):
    d = tok.d
    nj = seq // TOKEN_TILE
    n_lat, n_all = tok.n_lat, tok.n_lat + tok.n_ctx
    rows = n_all * TOKEN_TILE
    t_len = seq + ctx_len
    out_shape, out_specs = [], []
    for (_, width, dtype, _, transposed) in outs:
        if transposed:
            out_shape.append(jax.ShapeDtypeStruct((batch, width, t_len), dtype))
            out_specs.append(pl.BlockSpec(
                (1, width, TOKEN_TILE),
                lambda i: (jnp.where(i < n_lat, i // nj, i - n_lat), 0, jnp.where(i < n_lat, i % nj, nj))))
        else:
            out_shape.append(jax.ShapeDtypeStruct((rows, width), dtype))
            out_specs.append(pl.BlockSpec((TOKEN_TILE, width), lambda i: (i, 0)))
    body = functools.partial(_in_proj_body, tok=tok, outs=outs, d=d)
    return pl.pallas_call(
        body, out_shape=out_shape, grid=(n_all,),
        in_specs=tok.specs() + [_mod_spec(n_lat, nj, batch, 3 * d), _full((1, d)), _full(w.shape)],
        out_specs=out_specs, compiler_params=_params(1), name="in_proj",
    )(*tok.arrays, mod_l, g.reshape(1, d), w)


def _out_proj_body(*refs, tok, d, final):
    n_tok = len(tok.arrays)
    ma_ref, mb_ref, wa_ref, wb_ref, mod_ref = refs[n_tok:n_tok + 5]
    i = pl.program_id(0)
    x = tok.load(refs[:n_tok], i)
    acc = _dot(ma_ref[...], wa_ref[...]) + _dot(mb_ref[...], wb_ref[...])
    xn = x + mod_ref[0][:, 2 * d:] * acc
    if final:
        gf_ref, o_ref = refs[n_tok + 5:]
        xn = xn * lax.rsqrt(jnp.mean(xn * xn, axis=-1, keepdims=True) + EPS) * gf_ref[...]
    else:
        o_ref = refs[n_tok + 5]
    o_ref[...] = xn


def _out_proj(tok, mix_a, mix_b, w_out, mod_l, batch, seq, g_final=None):
    d = tok.d
    half = mix_a.shape[1]
    nj = seq // TOKEN_TILE
    n_lat = tok.n_lat
    final = g_final is not None
    n_tiles = n_lat if final else n_lat + tok.n_ctx
    wa, wb = w_out[:half].astype(BF16), w_out[half:].astype(BF16)
    tile = lambda width: pl.BlockSpec((TOKEN_TILE, width), lambda i: (i, 0))
    in_specs = tok.specs() + [tile(half), tile(half), _full(wa.shape), _full(wb.shape),
                              _mod_spec(n_lat, nj, batch, 3 * d)]
    args = list(tok.arrays) + [mix_a, mix_b, wa, wb, mod_l]
    if final:
        in_specs.append(_full((1, d)))
        args.append(g_final.reshape(1, d))
    body = functools.partial(_out_proj_body, tok=tok, d=d, final=final)
    return pl.pallas_call(
        body, out_shape=jax.ShapeDtypeStruct((n_tiles * TOKEN_TILE, d), F32), grid=(n_tiles,),
        in_specs=in_specs, out_specs=tile(d), compiler_params=_params(1), name="out_proj",
    )(*args)


def _rope_tables(seq, ctx_len, rot_dim, lane_pattern):
    t = jnp.arange(seq)
    pos = jnp.stack([t // GRID_W, t % GRID_W], axis=-1).astype(F32)
    n_freq = rot_dim // 4
    inv = ROPE_BASE ** (-jnp.arange(n_freq, dtype=F32) / n_freq)
    ang = pos[:, :, None] * inv
    cos, sin = jnp.cos(ang), jnp.sin(ang)
    cos_row = jnp.concatenate([cos[:, 0], cos[:, 0], cos[:, 1], cos[:, 1]], axis=-1)
    sin_row = jnp.concatenate([-sin[:, 0], sin[:, 0], -sin[:, 1], sin[:, 1]], axis=-1)
    cos_t, sin_t = lane_pattern(cos_row, 1.0), lane_pattern(sin_row, 0.0)
    pad = lambda a, v: jnp.concatenate([a, jnp.full((ctx_len, LANES), v, F32)], axis=0)
    return pad(cos_t, 1.0), pad(sin_t, 0.0)


def _rope(x, cos, sin, dist):
    lane = lax.broadcasted_iota(jnp.int32, x.shape, 1)
    first = (lane % (2 * dist)) < dist
    partner = jnp.where(first, pltpu.roll(x, LANES - dist, 1), pltpu.roll(x, dist, 1))
    return x * cos + partner * sin


def _tok_block(b, j, nj, batch):
    return jnp.where(j < nj, b * nj + j, batch * nj + b)


def _mla_prep_body(pa_ref, cos_ref, sin_ref, gq_ref, gkv_ref, wuq_ref, wuk_ref, wuv_ref,
                   q_ref, k_ref, vt_ref, *, heads, q_lora, kv_lora, qscale):
    pa = pa_ref[...]
    cos, sin = cos_ref[...], sin_ref[...]

    def norm(v, g_ref):
        return (v * lax.rsqrt(jnp.mean(v * v, axis=-1, keepdims=True) + EPS) * g_ref[...]).astype(BF16)

    cq = norm(pa[:, :q_lora], gq_ref)
    q_all = _dot(cq, wuq_ref[...])
    for h in range(heads):
        qh = _rope(q_all[:, h * LANES:(h + 1) * LANES], cos, sin, 8)
        q_ref[0, h] = (qh * qscale).astype(BF16)
    ckv = norm(pa[:, q_lora:q_lora + kv_lora], gkv_ref)
    k_nope = _dot(ckv, wuk_ref[...])
    k_rope = _rope(pa[:, q_lora + kv_lora:], cos, sin, 8)
    for h in range(heads):
        k_ref[0, h] = (k_nope[:, h * LANES:(h + 1) * LANES] + k_rope).astype(BF16)
    vt_ref[0] = _dot(ckv, wuv_ref[...]).T.astype(BF16)


def _mla_prep(pa, cos, sin, g_q, g_kv, w_uq, w_ukv, batch, seq, ctx_len, heads, nope, rope, v_dim):
    q_lora, kv_lora = g_q.shape[0], g_kv.shape[0]
    nj = seq // TOKEN_TILE
    t_len = seq + ctx_len
    pad = LANES - nope - rope
    wuq = jnp.pad(w_uq.reshape(q_lora, heads, nope + rope), ((0, 0), (0, 0), (0, pad)))
    wuq = wuq.reshape(q_lora, heads * LANES).astype(BF16)
    wkv = w_ukv.reshape(kv_lora, heads, nope + v_dim)
    wuk = jnp.pad(wkv[..., :nope], ((0, 0), (0, 0), (0, LANES - nope))).reshape(kv_lora, heads * LANES).astype(BF16)
    wuv = wkv[..., nope:].reshape(kv_lora, heads * v_dim).astype(BF16)
    body = functools.partial(_mla_prep_body, heads=heads, q_lora=q_lora, kv_lora=kv_lora,
                             qscale=(nope + rope) ** -0.5 * LOG2E)
    head_major = pl.BlockSpec((1, heads, TOKEN_TILE, LANES), lambda b, j: (b, 0, j, 0))
    return pl.pallas_call(
        body,
        out_shape=[jax.ShapeDtypeStruct((batch, heads, t_len, LANES), BF16),
                   jax.ShapeDtypeStruct((batch, heads, t_len, LANES), BF16),
                   jax.ShapeDtypeStruct((batch, heads * v_dim, t_len), BF16)],
        grid=(batch, nj + 1),
        in_specs=[pl.BlockSpec((TOKEN_TILE, pa.shape[1]), lambda b, j: (_tok_block(b, j, nj, batch), 0)),
                  pl.BlockSpec((TOKEN_TILE, LANES), lambda b, j: (j, 0)),
                  pl.BlockSpec((TOKEN_TILE, LANES), lambda b, j: (j, 0)),
                  _full((1, q_lora)), _full((1, kv_lora)), _full(wuq.shape), _full(wuk.shape), _full(wuv.shape)],
        out_specs=[head_major, head_major,
                   pl.BlockSpec((1, heads * v_dim, TOKEN_TILE), lambda b, j: (b, 0, j))],
        compiler_params=_params(2), name="mla_prep",
    )(pa, cos, sin, g_q.reshape(1, -1), g_kv.reshape(1, -1), wuq, wuk, wuv)


def _gqa_prep_body(pd_ref, cos_ref, sin_ref, gq_ref, gk_ref, bd_ref, q_ref, k_ref, *, q_width, qscale):
    pd = pd_ref[...]
    cos, sin = cos_ref[...], sin_ref[...]
    bd = bd_ref[...]
    lane = lax.broadcasted_iota(jnp.int32, (TOKEN_TILE, LANES), 1)
    upper = lane >= LANES // 2

    def head_norm(v, g_ref):
        ms = jnp.dot(v * v, bd, preferred_element_type=F32, precision=lax.Precision.HIGHEST)
        return v * lax.rsqrt(ms + EPS) * g_ref[...]

    n_blocks = q_width // LANES
    for i in range(n_blocks):
        xq = _rope(head_norm(pd[:, i * LANES:(i + 1) * LANES], gq_ref), cos, sin, 16) * qscale
        t = (2 * i) // n_blocks
        in_half = upper if t == 1 else jnp.logical_not(upper)
        swapped = pltpu.roll(xq, LANES // 2, 1)
        q_ref[0, 2 * i + t] = jnp.where(in_half, xq, 0.0).astype(BF16)
        q_ref[0, 2 * i + 1 - t] = jnp.where(in_half, swapped, 0.0).astype(BF16)
    xk = _rope(head_norm(pd[:, q_width:], gk_ref), cos, sin, 16)
    k_ref[0, 0] = xk.astype(BF16)


def _gqa_prep(pd, cos, sin, g_q, g_k, batch, seq, ctx_len, heads, kv_heads, head_dim):
    assert kv_heads * head_dim == LANES and heads % (2 * kv_heads) == 0
    nj = seq // TOKEN_TILE
    t_len = seq + ctx_len
    q_width = heads * head_dim
    grp = jnp.arange(LANES) // head_dim
    bd = (grp[:, None] == grp[None, :]).astype(F32) / head_dim
    rep = LANES // head_dim
    body = functools.partial(_gqa_prep_body, q_width=q_width, qscale=head_dim ** -0.5 * LOG2E)
    return pl.pallas_call(
        body,
        out_shape=[jax.ShapeDtypeStruct((batch, heads, t_len, LANES), BF16),
                   jax.ShapeDtypeStruct((batch, 1, t_len, LANES), BF16)],
        grid=(batch, nj + 1),
        in_specs=[pl.BlockSpec((TOKEN_TILE, pd.shape[1]), lambda b, j: (_tok_block(b, j, nj, batch), 0)),
                  pl.BlockSpec((TOKEN_TILE, LANES), lambda b, j: (j, 0)),
                  pl.BlockSpec((TOKEN_TILE, LANES), lambda b, j: (j, 0)),
                  _full((1, LANES)), _full((1, LANES)), _full((LANES, LANES))],
        out_specs=[pl.BlockSpec((1, heads, TOKEN_TILE, LANES), lambda b, j: (b, 0, j, 0)),
                   pl.BlockSpec((1, 1, TOKEN_TILE, LANES), lambda b, j: (b, 0, j, 0))],
        compiler_params=_params(2), name="gqa_prep",
    )(pd, cos, sin, jnp.tile(g_q, rep).reshape(1, LANES), jnp.tile(g_k, rep).reshape(1, LANES), bd)


def _score_pass(q, k_ref, k_head, chunks, s_ref):
    m = None
    for (st, sz) in chunks:
        s_t = _dot_nt(k_ref[0, k_head, st:st + sz, :], q)
        s_ref[st:st + sz, :] = s_t
        cm = jnp.max(s_t, axis=0, keepdims=True)
        m = cm if m is None else jnp.maximum(m, cm)
    return m


def _value_pass(s_ref, m, vt_ref, v_rows, chunks):
    l = jnp.zeros((1, TOKEN_TILE), F32)
    acc = jnp.zeros((v_rows.stop - v_rows.start, TOKEN_TILE), F32)
    for (st, sz) in chunks:
        p = jnp.exp2(s_ref[st:st + sz, :] - m)
        l = l + jnp.sum(p, axis=0, keepdims=True)
        acc = acc + _dot(vt_ref[0, v_rows, st:st + sz], p.astype(BF16))
    return acc / l


def _gated_store(outs, g_ref, o_ref):
    o2 = jnp.concatenate(outs, axis=0).T
    o_ref[...] = (o2 * _silu(g_ref[...].astype(F32))).astype(o_ref.dtype)


def _flash_body(q_ref, k_ref, vt_ref, g_ref, o_ref, s_scr, m_scr, *, k_sel, v_off, v_dim, chunks):
    j = pl.program_id(2)

    @pl.when((pl.program_id(0) == 0) & (pl.program_id(1) == 0) & (j == 0))
    def _():
        s_scr[...] = jnp.zeros(s_scr.shape, F32)
        m_scr[...] = jnp.zeros(m_scr.shape, F32)

    def step(slot, prev):
        heads = (0, 1)
        q = [q_ref[0, a] for a in heads]
        m_prev = [m_scr[prev, a] for a in heads]
        m = [None, None]
        l = [jnp.zeros((1, TOKEN_TILE), F32) for _ in heads]
        acc = [jnp.zeros((v_dim, TOKEN_TILE), F32) for _ in heads]
        for (st, sz) in chunks:
            for a in heads:
                s_t = _dot_nt(k_ref[0, k_sel[a], st:st + sz, :], q[a])
                s_scr[slot, a, st:st + sz, :] = s_t
                cm = jnp.max(s_t, axis=0, keepdims=True)
                m[a] = cm if m[a] is None else jnp.maximum(m[a], cm)
                p = jnp.exp2(s_scr[prev, a, st:st + sz, :] - m_prev[a])
                l[a] = l[a] + jnp.sum(p, axis=0, keepdims=True)
                acc[a] = acc[a] + _dot(vt_ref[0, v_off[a]:v_off[a] + v_dim, st:st + sz], p.astype(BF16))
        for a in heads:
            m_scr[slot, a] = m[a]
        _gated_store([acc[a] / l[a] for a in heads], g_ref, o_ref)

    for parity in (0, 1):
        pl.when(j % 2 == parity)(functools.partial(step, parity, 1 - parity))


def _flash_ctx_body(q_ref, k_ref, vt_ref, g_ref, prev_ref, o_ref, s_scr, *, k_sel, v_off, v_dim, chunks):
    del prev_ref
    outs = []
    for a in range(2):
        m = _score_pass(q_ref[0, a], k_ref, k_sel[a], chunks, s_scr.at[a])
        outs.append(_value_pass(s_scr.at[a], m, vt_ref, slice(v_off[a], v_off[a] + v_dim), chunks))
    _gated_store(outs, g_ref, o_ref)


def _flash(q, k, vt, gate, batch, seq, ctx_len, k_heads_per_pair, v_rows_per_pair, pairs_per_kv, ctx_queries):
    heads = q.shape[1]
    t_len = seq + ctx_len
    nj = seq // TOKEN_TILE
    v_dim = LANES // 2
    k_sel = (0, 1) if k_heads_per_pair == 2 else (0, 0)
    v_off = (0, v_dim) if v_rows_per_pair == 2 * v_dim else (0, 0)
    k_blocks = k.shape[1] // k_heads_per_pair
    k_block = lambda p: (p // pairs_per_kv) % k_blocks
    chunks = tuple((c * KEY_CHUNK, KEY_CHUNK) for c in range(seq // KEY_CHUNK)) + ((seq, ctx_len),)
    out_rows = gate.shape[0] if ctx_queries else batch * seq
    out_shape = jax.ShapeDtypeStruct((out_rows, gate.shape[1]), BF16)
    static = dict(k_sel=k_sel, v_off=v_off, v_dim=v_dim)

    done = lambda b, p, j: (b * nj + jnp.maximum(j - 1, 0), p)
    out = pl.pallas_call(
        functools.partial(_flash_body, chunks=chunks, **static), out_shape=out_shape,
        grid=(batch, heads // 2, nj + 1),
        in_specs=[pl.BlockSpec((1, 2, TOKEN_TILE, LANES), lambda b, p, j: (b, p, jnp.minimum(j, nj - 1), 0)),
                  pl.BlockSpec((1, k_heads_per_pair, t_len, LANES), lambda b, p, j: (b, k_block(p), 0, 0)),
                  pl.BlockSpec((1, v_rows_per_pair, t_len), lambda b, p, j: (b, p // pairs_per_kv, 0)),
                  pl.BlockSpec((TOKEN_TILE, LANES), done)],
        out_specs=pl.BlockSpec((TOKEN_TILE, LANES), done),
        scratch_shapes=[pltpu.VMEM((2, 2, t_len, TOKEN_TILE), F32), pltpu.VMEM((2, 2, 1, TOKEN_TILE), F32)],
        compiler_params=_params(3), name="flash_attention",
    )(q, k, vt, gate)
    if not ctx_queries:
        return out

    ctx_tile = lambda b, p: (batch * nj + b, p)
    return pl.pallas_call(
        functools.partial(_flash_ctx_body, chunks=((0, ctx_len),), **static), out_shape=out_shape,
        grid=(batch, heads // 2),
        in_specs=[pl.BlockSpec((1, 2, TOKEN_TILE, LANES), lambda b, p: (b, p, nj, 0)),
                  pl.BlockSpec((1, k_heads_per_pair, ctx_len, LANES), lambda b, p: (b, k_block(p), seq // ctx_len, 0)),
                  pl.BlockSpec((1, v_rows_per_pair, ctx_len), lambda b, p: (b, p // pairs_per_kv, seq // ctx_len)),
                  pl.BlockSpec((TOKEN_TILE, LANES), ctx_tile),
                  pl.BlockSpec(memory_space=pl.ANY)],
        out_specs=pl.BlockSpec((TOKEN_TILE, LANES), ctx_tile),
        scratch_shapes=[pltpu.VMEM((2, ctx_len, TOKEN_TILE), F32)],
        input_output_aliases={4: 0},
        compiler_params=_params(2), name="flash_attention_ctx",
    )(q, k, vt, gate, out)


NA_KEY_ROWS = 12
NA_BIAS_ROWS = 23


def _na_bias_body(rpb_ref, o_ref):
    p = pl.program_id(0)
    shape = (GRID_W, LANES)
    kc = lax.broadcasted_iota(jnp.int32, shape, 0)
    lane = lax.broadcasted_iota(jnp.int32, shape, 1)
    qc = lane % GRID_W
    upper = lane >= GRID_W
    rel = kc - qc + (WIN_C - 1)
    c0 = jnp.clip(qc - WIN_C // 2, 0, GRID_W - WIN_C)
    col_ok = (kc >= c0) & (kc < c0 + WIN_C)
    n_rel_r, n_rel_c = 2 * WIN_R - 1, 2 * WIN_C - 1
    o_ref[...] = jnp.zeros(o_ref.shape, F32)

    def block(dd, carry):
        base0 = (2 * p) * (n_rel_r * n_rel_c) + dd * n_rel_c
        base1 = base0 + n_rel_r * n_rel_c
        val = jnp.zeros(shape, F32)
        for jj in range(n_rel_c):
            val = jnp.where(rel == jj, jnp.where(upper, rpb_ref[base1 + jj], rpb_ref[base0 + jj]), val)
        off = pl.multiple_of((dd - (WIN_R - 1) + NA_BIAS_ROWS // 2) * GRID_W, GRID_W)
        o_ref[0, pl.ds(off, GRID_W), :] = jnp.where(col_ok, val * LOG2E, NEG)
        return carry

    lax.fori_loop(0, n_rel_r, block, 0)


def _na_bias(rpb):
    heads = rpb.shape[0]
    return pl.pallas_call(
        _na_bias_body,
        out_shape=jax.ShapeDtypeStruct((heads // 2, NA_BIAS_ROWS * GRID_W, LANES), F32),
        grid=(heads // 2,),
        in_specs=[pl.BlockSpec(memory_space=pltpu.SMEM)],
        out_specs=pl.BlockSpec((1, NA_BIAS_ROWS * GRID_W, LANES), lambda p: (p, 0, 0)),
        compiler_params=_params(1), name="na_bias",
    )(rpb.reshape(-1))


def _na_body(q_ref, kl_ref, kc_ref, vt_ref, bias_ref, g_ref, o_ref, *, seq, ctx_len, rows_per_step):
    rblk = pl.program_id(2)
    n_rows = seq // GRID_W
    span = NA_KEY_ROWS * GRID_W
    lane = lax.broadcasted_iota(jnp.int32, (GRID_W, LANES), 1)
    lower = lane < GRID_W
    key_row = lax.broadcasted_iota(jnp.int32, (span, LANES), 0) // GRID_W
    r0 = rblk * rows_per_step
    u0 = jnp.minimum(jnp.clip(r0 - WIN_R // 2, 0, n_rows - WIN_R), n_rows - NA_KEY_ROWS)
    q = q_ref[...]
    zero = jnp.zeros((GRID_W, LANES), q.dtype)
    parts = []
    for rr in range(rows_per_step):
        q_r = q[rr * GRID_W:(rr + 1) * GRID_W]
        parts += [jnp.where(lower, q_r, zero), jnp.where(lower, zero, q_r)]
    q2 = jnp.concatenate(parts, axis=0)
    k_off = pl.multiple_of(u0 * GRID_W, TOKEN_TILE)
    s_loc = _dot_nt(kl_ref[pl.ds(k_off, span), :], q2)
    s_ctx = _dot_nt(kc_ref[...], q2)
    cols = []
    for rr in range(rows_per_step):
        r = r0 + rr
        a0 = jnp.clip(r - WIN_R // 2, 0, n_rows - WIN_R) - u0
        b_off = pl.multiple_of((u0 - r + NA_BIAS_ROWS // 2) * GRID_W, GRID_W)
        blk = s_loc[:, rr * LANES:(rr + 1) * LANES] + bias_ref[0, pl.ds(b_off, span), :]
        cols.append(jnp.where((key_row >= a0) & (key_row < a0 + WIN_R), blk, NEG))
    s_loc = jnp.concatenate(cols, axis=1)
    m = jnp.maximum(jnp.max(s_loc, axis=0, keepdims=True), jnp.max(s_ctx, axis=0, keepdims=True))
    p_loc = jnp.exp2(s_loc - m)
    p_ctx = jnp.exp2(s_ctx - m)
    l = jnp.sum(p_loc, axis=0, keepdims=True) + jnp.sum(p_ctx, axis=0, keepdims=True)
    r_t = (_dot(vt_ref[0, :, pl.ds(k_off, span)], p_loc.astype(BF16))
           + _dot(vt_ref[0, :, seq:seq + ctx_len], p_ctx.astype(BF16)))
    r_n = (r_t / l).T
    outs = [jnp.where(lower, r_n[rr * LANES:rr * LANES + GRID_W], r_n[rr * LANES + GRID_W:(rr + 1) * LANES])
            for rr in range(rows_per_step)]
    o = jnp.concatenate(outs, axis=0)
    o_ref[...] = (o * _silu(g_ref[...].astype(F32))).astype(o_ref.dtype)


def _na_attention(q, k, vt, bias, gate, batch, seq, ctx_len):
    pairs = q.shape[1] // LANES
    t_len = seq + ctx_len
    rows_per_step = TOKEN_TILE // GRID_W
    n_steps = seq // TOKEN_TILE
    lat_tile = lambda b, p, s: (b * n_steps + s, p)
    body = functools.partial(_na_body, seq=seq, ctx_len=ctx_len, rows_per_step=rows_per_step)
    return pl.pallas_call(
        body, out_shape=jax.ShapeDtypeStruct((batch * seq, q.shape[1]), BF16),
        grid=(batch, pairs, n_steps),
        in_specs=[pl.BlockSpec((TOKEN_TILE, LANES), lat_tile),
                  pl.BlockSpec((seq, LANES), lambda b, p, s: (b, p)),
                  pl.BlockSpec((ctx_len, LANES), lambda b, p, s: (batch * seq // ctx_len + b, p)),
                  pl.BlockSpec((1, LANES, t_len), lambda b, p, s: (b, p, 0)),
                  pl.BlockSpec((1, NA_BIAS_ROWS * GRID_W, LANES), lambda b, p, s: (p, 0, 0)),
                  pl.BlockSpec((TOKEN_TILE, LANES), lat_tile)],
        out_specs=pl.BlockSpec((TOKEN_TILE, LANES), lat_tile),
        compiler_params=_params(3), name="neighborhood_attention",
    )(q, k, k, vt, bias, gate)


def _seg_scan(v, reverse):
    n = v.shape[1]
    lane = lax.broadcasted_iota(jnp.int32, v.shape, 1) % ML_CHUNK
    k = 1
    while k < ML_CHUNK:
        if reverse:
            v = v + jnp.where(lane < ML_CHUNK - k, pltpu.roll(v, n - k, 1), 0.0)
        else:
            v = v + jnp.where(lane >= k, pltpu.roll(v, k, 1), 0.0)
        k *= 2
    return v


def _mlstm_prep_body(u_ref, up_ref, un_ref, cw_ref, cb_ref, wqk_ref, wv_ref, wg_ref, bg_ref,
                     xc_ref, q_ref, k_ref, v_ref, vt_ref, g_ref, *, nj, width, kscale):
    j = pl.program_id(1)
    u = u_ref[...]
    row = lax.broadcasted_iota(jnp.int32, u.shape, 0)
    prev = jnp.where((j > 0) & (j < nj), up_ref[7:8, :], 0.0)
    nxt = jnp.where(j < nj - 1, un_ref[0:1, :], 0.0)
    u_m1 = jnp.where(row == 0, prev, pltpu.roll(u, 1, 0))
    u_p1 = jnp.where(row == TOKEN_TILE - 1, nxt, pltpu.roll(u, TOKEN_TILE - 1, 0))
    cw = cw_ref[...]
    xc = _silu(u_m1 * cw[0:1] + u * cw[1:2] + u_p1 * cw[2:3] + cb_ref[...])
    xcb = xc.astype(BF16)
    xc_ref[...] = xcb
    qk = _dot(xcb, wqk_ref[...])
    v = _dot(u.astype(BF16), wv_ref[...])
    qb, kb, vb = qk[:, :width].astype(BF16), qk[:, width:].astype(BF16), v.astype(BF16)
    q_ref[...] = qb
    k_ref[...] = (qk[:, width:] * kscale).astype(BF16)
    v_ref[...] = vb
    vt_ref[0] = v.T.astype(BF16)
    pre = _dot_nt(wg_ref[0], qb) + _dot_nt(wg_ref[1], kb) + _dot_nt(wg_ref[2], vb) + bg_ref[...]
    grow = lax.broadcasted_iota(jnp.int32, pre.shape, 0) % 16
    gates = jnp.where(grow < ML_HEADS, pre, jnp.where(grow < 3 * ML_HEADS, _log_sigmoid(pre), 0.0))
    scanned = jnp.concatenate([_seg_scan(gates[:16], False), _seg_scan(gates[16:], True)], axis=0)
    g_ref[0] = jnp.where((grow >= 2 * ML_HEADS) & (grow < 3 * ML_HEADS), scanned, gates)


def _mlstm_prep(u, conv_w, conv_b, w_q, w_k, w_v, w_gate, b_gate, batch, seq, ctx_len):
    heads, hd = w_q.shape[0], w_q.shape[1]
    assert heads == ML_HEADS and hd == ML_CHUNK
    width = heads * hd
    nj = seq // TOKEN_TILE
    t_len = seq + ctx_len
    rows = u.shape[0]

    def block_diag(w):
        eye = jnp.eye(heads, dtype=w.dtype)
        return (eye[:, None, :, None] * w[:, :, None, :]).reshape(width, width)

    wqk = jnp.concatenate([block_diag(w_q), block_diag(w_k)], axis=1).astype(BF16)
    wv = block_diag(w_v).astype(BF16)
    wg = w_gate.reshape(2, heads, 3, hd, 2 * heads).transpose(2, 0, 4, 1, 3).reshape(3, 2, 2 * heads, width)
    zeros = jnp.zeros((3, 2, heads, width), w_gate.dtype)
    wg = jnp.concatenate([wg, wg[:, :, heads:], zeros], axis=2).reshape(3, 32, width).astype(BF16)
    bzero = jnp.zeros((2, heads), b_gate.dtype)
    bg = jnp.concatenate([b_gate, b_gate[:, heads:], bzero], axis=1).reshape(32, 1)
    n_halo = rows // 8
    tokb = lambda b, j: _tok_block(b, j, nj, batch)
    tile = pl.BlockSpec((TOKEN_TILE, width), lambda b, j: (tokb(b, j), 0))
    body = functools.partial(_mlstm_prep_body, nj=nj, width=width, kscale=hd ** -0.5)
    per_tile = TOKEN_TILE // 8
    return pl.pallas_call(
        body,
        out_shape=[jax.ShapeDtypeStruct((rows, width), BF16)] * 4
        + [jax.ShapeDtypeStruct((batch, width, t_len), BF16), jax.ShapeDtypeStruct((batch, 32, t_len), F32)],
        grid=(batch, nj + 1),
        in_specs=[tile,
                  pl.BlockSpec((8, width), lambda b, j: (jnp.maximum(tokb(b, j) * per_tile - 1, 0), 0)),
                  pl.BlockSpec((8, width), lambda b, j: (jnp.minimum((tokb(b, j) + 1) * per_tile, n_halo - 1), 0)),
                  _full((3, width)), _full((1, width)), _full(wqk.shape), _full(wv.shape),
                  _full(wg.shape), _full((32, 1))],
        out_specs=[tile, tile, tile, tile,
                   pl.BlockSpec((1, width, TOKEN_TILE), lambda b, j: (b, 0, j)),
                   pl.BlockSpec((1, 32, TOKEN_TILE), lambda b, j: (b, 0, j))],
        compiler_params=_params(2), name="mlstm_prep",
    )(u, u, u, conv_w, conv_b.reshape(1, width), wqk, wv, wg, bg)


def _mlstm_seq_body(qf, kf, vf, vtf, gf, qb, kb, vb, vtb, gb, hf_ref, hb_ref, c_s, n_s, m_s):
    t = pl.program_id(1)
    L = ML_CHUNK

    @pl.when(t == 0)
    def _():
        c_s[...] = jnp.zeros(c_s.shape, F32)
        n_s[...] = jnp.zeros(n_s.shape, F32)
        m_s[...] = jnp.zeros(m_s.shape, F32)

    li = lax.broadcasted_iota(jnp.int32, (L, L), 0)
    si = lax.broadcasted_iota(jnp.int32, (L, L), 1)
    streams = ((qf, kf, vf, vtf, gf, hf_ref, si <= li), (qb, kb, vb, vtb, gb, hb_ref, si >= li))
    for d, (q_ref, k_ref, v_ref, vt_ref, g_ref, h_ref, incl) in enumerate(streams):
        g = g_ref[0]
        for h in range(ML_HEADS):
            idx = d * ML_HEADS + h
            cols = slice(h * L, (h + 1) * L)
            q, k, v, vt = q_ref[:, cols], k_ref[:, cols], v_ref[:, cols], vt_ref[0, cols, :]
            i_row, f_row, b_row = g[h:h + 1], g[ML_HEADS + h:ML_HEADS + h + 1], g[2 * ML_HEADS + h:2 * ML_HEADS + h + 1]
            c_st, n_st, m_st = c_s[idx], n_s[idx], m_s[idx][:, 0:1]
            b_col = jnp.sum(jnp.where(incl, f_row, 0.0), axis=1, keepdims=True)
            dmat = jnp.where(incl, b_col - b_row + i_row, NEG)
            m_inter = b_col + m_st
            m_t = jnp.maximum(m_inter, jnp.max(dmat, axis=1, keepdims=True))
            w_inter = jnp.exp(m_inter - m_t)
            s_mat = jnp.exp(dmat - m_t) * _dot_nt(q, k)
            num = w_inter * _dot_nt(q, c_st.astype(BF16)) + _dot(s_mat.astype(BF16), v)
            qn = jnp.sum(q.astype(F32) * n_st, axis=1, keepdims=True)
            den = w_inter * qn + jnp.sum(s_mat, axis=1, keepdims=True)
            h_ref[:, cols] = num / jnp.maximum(jnp.abs(den), jnp.exp(-m_t))
            b_last = jnp.sum(f_row, axis=1, keepdims=True)
            g_row = b_last - b_row + i_row
            m_new = jnp.maximum(b_last + m_st, jnp.max(g_row, axis=1, keepdims=True))
            decay = jnp.exp(b_last + m_st - m_new)
            w_row = jnp.exp(g_row - m_new)
            c_s[idx] = decay * c_st + _dot((vt.astype(F32) * w_row).astype(BF16), k)
            w8 = jnp.broadcast_to(w_row, (8, L)).astype(BF16)
            n_s[idx] = decay * n_st + _dot(w8, k)[0:1]
            m_s[idx] = jnp.broadcast_to(m_new, (1, L))


def _mlstm_seq(q, k, v, vt, gates, batch, seq, ctx_len):
    width = q.shape[1]
    L = ML_CHUNK
    n_lat, n_ctx = seq // L, ctx_len // L
    n_chunks = n_lat + n_ctx
    fwd = lambda t: (t + n_lat) % n_chunks
    bwd = lambda t: n_chunks - 1 - t
    rowblk = lambda b, c: jnp.where(c < n_lat, b * n_lat + c, batch * n_lat + b * n_ctx + (c - n_lat))

    def stream(chunk_of, d):
        tok = pl.BlockSpec((L, width), lambda b, t: (rowblk(b, chunk_of(t)), 0))
        return [tok, tok, tok,
                pl.BlockSpec((1, width, L), lambda b, t: (b, 0, chunk_of(t))),
                pl.BlockSpec((1, 16, L), lambda b, t: (b, d, chunk_of(t)))]

    out_f = pl.BlockSpec((L, width), lambda b, t: (rowblk(b, fwd(t)), 0))
    out_b = pl.BlockSpec((L, width), lambda b, t: (rowblk(b, bwd(t)), 0))
    n_state = 2 * ML_HEADS
    return pl.pallas_call(
        _mlstm_seq_body,
        out_shape=[jax.ShapeDtypeStruct(q.shape, F32)] * 2,
        grid=(batch, n_chunks),
        in_specs=stream(fwd, 0) + stream(bwd, 1),
        out_specs=[out_f, out_b],
        scratch_shapes=[pltpu.VMEM((n_state, L, L), F32), pltpu.VMEM((n_state, 1, L), F32),
                        pltpu.VMEM((n_state, 1, L), F32)],
        compiler_params=_params(2), name="mlstm_recurrence",
    )(q, k, v, vt, gates, q, k, v, vt, gates)


def _mlstm_out_body(hf_ref, hb_ref, xc_ref, z_ref, gh_ref, sk_ref, o_ref):
    h = hf_ref[...] + hb_ref[...]
    L = ML_CHUNK
    for hd in range(ML_HEADS):
        cols = slice(hd * L, (hd + 1) * L)
        hh = h[:, cols]
        mu = jnp.mean(hh, axis=-1, keepdims=True)
        var = jnp.mean(jnp.square(hh - mu), axis=-1, keepdims=True)
        hn = (hh - mu) * lax.rsqrt(var + EPS) * gh_ref[:, cols]
        o = (hn + sk_ref[:, cols] * xc_ref[:, cols].astype(F32)) * _silu(z_ref[:, cols].astype(F32))
        o_ref[:, cols] = o.astype(o_ref.dtype)


def _mlstm_out(hf, hb, xc, z, g_head, skip):
    rows, width = hf.shape
    tile = pl.BlockSpec((TOKEN_TILE, width), lambda i: (i, 0))
    return pl.pallas_call(
        _mlstm_out_body, out_shape=jax.ShapeDtypeStruct((rows, width), BF16), grid=(rows // TOKEN_TILE,),
        in_specs=[tile, tile, tile, tile, _full((1, width)), _full((1, width))],
        out_specs=tile, compiler_params=_params(1), name="mlstm_out",
    )(hf, hb, xc, z, g_head.reshape(1, width), skip.reshape(1, width))


def kernel(x, c, ctx, c_ctx, w_mod, b_mod, g_norm, ab_w_in, ab_w_out, mla_g_q, mla_w_uq, mla_g_kv, mla_w_ukv,
           ml_conv_w, ml_conv_b, ml_w_q, ml_w_k, ml_w_v, ml_w_gate, ml_b_gate, ml_g_head, ml_skip,
           cd_w_in, cd_w_out, na_rpb, gqa_g_q, gqa_g_k, g_final):
    batch, seq, d = x.shape
    ctx_len = ctx.shape[1]
    assert ctx_len == TOKEN_TILE and seq % KEY_CHUNK == 0 and seq // GRID_W >= NA_KEY_ROWS
    dims = (batch, seq, ctx_len)

    mla_heads, mla_rope, mla_v = 8, 32, 64
    mla_nope = mla_w_uq.shape[2] // mla_heads - mla_rope
    q_lora, kv_lora = mla_g_q.shape[1], mla_g_kv.shape[1]
    ml_width = ml_conv_w.shape[2]
    mla_width = mla_heads * mla_v
    gqa_heads, gqa_dim = 8, gqa_g_q.shape[1]
    gqa_kv = (cd_w_in.shape[2] - 4 * 512 - 2 * gqa_heads * gqa_dim) // (2 * gqa_dim)
    na_width = na_rpb.shape[1] * 64

    mod_rows = -(-(batch + 1) // 8) * 8
    cvec = jnp.concatenate([c, c_ctx[None], jnp.zeros((mod_rows - batch - 1, d), F32)], axis=0)
    mod = _modulation(cvec, w_mod, b_mod)
    mod0 = mod[0].reshape(mod_rows, 1, 3 * d)
    mod1 = mod[1].reshape(mod_rows, 1, 3 * d)

    tok0 = _Tokens((x.reshape(batch * seq, d), ctx.reshape(batch * ctx_len, d)), *dims)
    w_in = ab_w_in[0]
    s1 = q_lora + kv_lora
    zcol = lambda n: jnp.zeros((d, n), w_in.dtype)
    w0 = jnp.concatenate([w_in[:, :s1], zcol(mla_nope), w_in[:, s1:s1 + mla_rope],
                          zcol(LANES - mla_nope - mla_rope), w_in[:, s1 + mla_rope:]], axis=1).astype(BF16)
    o_pa = s1 + LANES
    outs0 = ((0, o_pa, F32, 1.0, False), (o_pa, mla_width, BF16, 1.0, False),
             (o_pa + mla_width, ml_width, F32, 1.0, False), (o_pa + mla_width + ml_width, ml_width, BF16, 1.0, False))
    pa, gate_a, u, z = _in_proj(tok0, mod0, g_norm[0], w0, outs0, *dims)

    def mla_lanes(row, fill):
        n = row.shape[0]
        return jnp.concatenate([jnp.full((n, mla_nope), fill, F32), row,
                                jnp.full((n, LANES - mla_nope - mla_rope), fill, F32)], axis=-1)

    cos_a, sin_a = _rope_tables(seq, ctx_len, mla_rope, mla_lanes)
    q_a, k_a, vt_a = _mla_prep(pa, cos_a, sin_a, mla_g_q[0], mla_g_kv[0], mla_w_uq[0], mla_w_ukv[0],
                               *dims, mla_heads, mla_nope, mla_rope, mla_v)
    mix_a = _flash(q_a, k_a, vt_a, gate_a, *dims, k_heads_per_pair=2, v_rows_per_pair=2 * mla_v,
                   pairs_per_kv=1, ctx_queries=True)

    xc, q_m, k_m, v_m, vt_m, gates = _mlstm_prep(u, ml_conv_w[0], ml_conv_b[0], ml_w_q[0], ml_w_k[0], ml_w_v[0],
                                                 ml_w_gate[0], ml_b_gate[0], *dims)
    h_f, h_b = _mlstm_seq(q_m, k_m, v_m, vt_m, gates, *dims)
    mix_b = _mlstm_out(h_f, h_b, xc, z, ml_g_head[0], ml_skip[0])
    x1 = _out_proj(tok0, mix_a, mix_b, ab_w_out[0], mod0, batch, seq)

    tok1 = _Tokens((x1,), *dims)
    w1 = cd_w_in[0].astype(BF16)
    gq_w, gkv_w = gqa_heads * gqa_dim, gqa_kv * gqa_dim
    o_d = 4 * na_width
    outs1 = ((0, na_width, BF16, 64 ** -0.5 * LOG2E, False), (na_width, na_width, BF16, 1.0, False),
             (2 * na_width, na_width, BF16, 1.0, True), (3 * na_width, na_width, BF16, 1.0, False),
             (o_d, gq_w + gkv_w, F32, 1.0, False), (o_d + gq_w + gkv_w, gkv_w, BF16, 1.0, True),
             (o_d + gq_w + 2 * gkv_w, gq_w, BF16, 1.0, False))
    q_c, k_c, vt_c, gate_c, pd, vt_d, gate_d = _in_proj(tok1, mod1, g_norm[1], w1, outs1, *dims)

    mix_c = _na_attention(q_c, k_c, vt_c, _na_bias(na_rpb[0]), gate_c, *dims)

    gqa_lanes = lambda row, fill: jnp.tile(row, (1, LANES // gqa_dim))
    cos_d, sin_d = _rope_tables(seq, ctx_len, gqa_dim, gqa_lanes)
    q_d, k_d = _gqa_prep(pd, cos_d, sin_d, gqa_g_q[0], gqa_g_k[0], *dims, gqa_heads, gqa_kv, gqa_dim)
    mix_d = _flash(q_d, k_d, vt_d, gate_d, *dims, k_heads_per_pair=1, v_rows_per_pair=gqa_dim,
                   pairs_per_kv=gqa_heads // (2 * gqa_kv), ctx_queries=False)

    out = _out_proj(tok1, mix_c, mix_d, cd_w_out[0], mod1, batch, seq, g_final=g_final)
    return out.reshape(batch, seq, d)
```

```python
import functools

import jax
import jax.numpy as jnp
from jax import lax
from jax.experimental import pallas as pl
from jax.experimental.pallas import tpu as pltpu

F32 = jnp.float32
BF16 = jnp.bfloat16

LANES = 128
TOKEN_TILE = 256
KEY_CHUNK = 512
GRID_W = 64
WIN_R = 8
WIN_C = 16
ML_CHUNK = 128
ML_HEADS = 4
EPS = 1e-6
ROPE_BASE = 10000.0
LOG2E = 1.4426950408889634
NEG = -1e30
VMEM_LIMIT = 56 * 1024 * 1024

_NT = (((1,), (1,)), ((), ()))


def _dot(a, b):
    return jnp.dot(a, b, preferred_element_type=F32)


def _dot_nt(a, b):
    return lax.dot_general(a, b, _NT, preferred_element_type=F32)


def _silu(v):
    return v * (1.0 / (1.0 + jnp.exp(-v)))


def _log_sigmoid(v):
    return -(jnp.maximum(-v, 0.0) + jnp.log1p(jnp.exp(-jnp.abs(v))))


def _params(n_axes):
    return pltpu.CompilerParams(dimension_semantics=("arbitrary",) * n_axes, vmem_limit_bytes=VMEM_LIMIT)


def _full(shape):
    nd = len(shape)
    return pl.BlockSpec(shape, lambda *_: (0,) * nd)


def _mod_body(c_ref, w_ref, b_ref, o_ref):
    s = _silu(c_ref[...])
    o_ref[0] = _dot(s.astype(BF16), w_ref[0].astype(BF16)) + b_ref[0]


def _modulation(cvec, w_mod, b_mod):
    depth, d, n = w_mod.shape
    rows = cvec.shape[0]
    tn = n // 4
    return pl.pallas_call(
        _mod_body,
        out_shape=jax.ShapeDtypeStruct((depth, rows, n), F32),
        grid=(depth, n // tn),
        in_specs=[_full((rows, d)),
                  pl.BlockSpec((1, d, tn), lambda l, j: (l, 0, j)),
                  pl.BlockSpec((1, 1, tn), lambda l, j: (l, 0, j))],
        out_specs=pl.BlockSpec((1, rows, tn), lambda l, j: (l, 0, j)),
        compiler_params=_params(2), name="modulation",
    )(cvec, w_mod, b_mod.reshape(depth, 1, n))


class _Tokens:
    def __init__(self, arrays, batch, seq, ctx_len):
        self.arrays = arrays
        self.split = len(arrays) == 2
        self.n_lat = batch * seq // TOKEN_TILE
        self.n_ctx = batch * ctx_len // TOKEN_TILE
        self.d = arrays[0].shape[-1]

    def specs(self):
        blk = (TOKEN_TILE, self.d)
        if not self.split:
            return [pl.BlockSpec(blk, lambda i: (i, 0))]
        n_lat = self.n_lat
        return [pl.BlockSpec(blk, lambda i: (jnp.minimum(i, n_lat - 1), 0)),
                pl.BlockSpec(blk, lambda i: (jnp.maximum(i - n_lat, 0), 0))]

    def load(self, refs, i):
        if not self.split:
            return refs[0][...]
        return jnp.where(i < self.n_lat, refs[0][...], refs[1][...])


def _mod_spec(n_lat, nj, batch, width):
    return pl.BlockSpec((1, 1, width), lambda i: (jnp.where(i < n_lat, i // nj, batch), 0, 0))


def _in_proj_body(*refs, tok, outs, d):
    n_tok = len(tok.arrays)
    mod_ref, g_ref, w_ref = refs[n_tok:n_tok + 3]
    o_refs = refs[n_tok + 3:]
    i = pl.program_id(0)
    x = tok.load(refs[:n_tok], i)
    y = x * lax.rsqrt(jnp.mean(x * x, axis=-1, keepdims=True) + EPS) * g_ref[...]
    mod = mod_ref[0]
    h = y * (1.0 + mod[:, d:2 * d]) + mod[:, :d]
    acc = _dot(h.astype(BF16), w_ref[...])
    for o_ref, (c0, width, _, scale, transposed) in zip(o_refs, outs):
        v = acc[:, c0:c0 + width]
        if scale != 1.0:
            v = v * scale
        if transposed:
            o_ref[0] = v.T.astype(o_ref.dtype)
        else:
            o_ref[...] = v.astype(o_ref.dtype)


def _in_proj(tok, mod_l, g, w, outs, batch, seq, ctx_len):
    d = tok.d
    nj = seq // TOKEN_TILE
    n_lat, n_all = tok.n_lat, tok.n_lat + tok.n_ctx
    rows = n_all * TOKEN_TILE
    t_len = seq + ctx_len
    out_shape, out_specs = [], []
    for (_, width, dtype, _, transposed) in outs:
        if transposed:
            out_shape.append(jax.ShapeDtypeStruct((batch, width, t_len), dtype))
            out_specs.append(pl.BlockSpec(
                (1, width, TOKEN_TILE),
                lambda i: (jnp.where(i < n_lat, i // nj, i - n_lat), 0, jnp.where(i < n_lat, i % nj, nj))))
        else:
            out_shape.append(jax.ShapeDtypeStruct((rows, width), dtype))
            out_specs.append(pl.BlockSpec((TOKEN_TILE, width), lambda i: (i, 0)))
    body = functools.partial(_in_proj_body, tok=tok, outs=outs, d=d)
    return pl.pallas_call(
        body, out_shape=out_shape, grid=(n_all,),
        in_specs=tok.specs() + [_mod_spec(n_lat, nj, batch, 3 * d), _full((1, d)), _full(w.shape)],
        out_specs=out_specs, compiler_params=_params(1), name="in_proj",
    )(*tok.arrays, mod_l, g.reshape(1, d), w)


def _out_proj_body(*refs, tok, mix_a, d, final):
    n_tok, n_a = len(tok.arrays), len(mix_a.arrays)
    mb_ref, wa_ref, wb_ref, mod_ref = refs[n_tok + n_a:n_tok + n_a + 4]
    rest = refs[n_tok + n_a + 4:]
    i = pl.program_id(0)
    x = tok.load(refs[:n_tok], i)
    acc = _dot(mix_a.load(refs[n_tok:n_tok + n_a], i), wa_ref[...]) + _dot(mb_ref[...], wb_ref[...])
    xn = x + mod_ref[0][:, 2 * d:] * acc
    if final:
        gf_ref, o_ref = rest
        xn = xn * lax.rsqrt(jnp.mean(xn * xn, axis=-1, keepdims=True) + EPS) * gf_ref[...]
    else:
        (o_ref,) = rest
    o_ref[...] = xn


def _out_proj(tok, mix_a, mix_b, w_out, mod_l, batch, seq, g_final=None):
    d = tok.d
    half = mix_a.d
    nj = seq // TOKEN_TILE
    n_lat = tok.n_lat
    final = g_final is not None
    n_tiles = n_lat if final else n_lat + tok.n_ctx
    wa, wb = w_out[:half].astype(BF16), w_out[half:].astype(BF16)
    tile = lambda width: pl.BlockSpec((TOKEN_TILE, width), lambda i: (i, 0))
    in_specs = tok.specs() + mix_a.specs() + [tile(half), _full(wa.shape), _full(wb.shape),
                                              _mod_spec(n_lat, nj, batch, 3 * d)]
    args = list(tok.arrays) + list(mix_a.arrays) + [mix_b, wa, wb, mod_l]
    if final:
        in_specs.append(_full((1, d)))
        args.append(g_final.reshape(1, d))
    body = functools.partial(_out_proj_body, tok=tok, mix_a=mix_a, d=d, final=final)
    return pl.pallas_call(
        body, out_shape=jax.ShapeDtypeStruct((n_tiles * TOKEN_TILE, d), F32), grid=(n_tiles,),
        in_specs=in_specs, out_specs=tile(d), compiler_params=_params(1), name="out_proj",
    )(*args)


def _rope_tables(seq, ctx_len, rot_dim, lane_pattern):
    t = jnp.arange(seq)
    pos = jnp.stack([t // GRID_W, t % GRID_W], axis=-1).astype(F32)
    n_freq = rot_dim // 4
    inv = ROPE_BASE ** (-jnp.arange(n_freq, dtype=F32) / n_freq)
    ang = pos[:, :, None] * inv
    cos, sin = jnp.cos(ang), jnp.sin(ang)
    cos_row = jnp.concatenate([cos[:, 0], cos[:, 0], cos[:, 1], cos[:, 1]], axis=-1)
    sin_row = jnp.concatenate([-sin[:, 0], sin[:, 0], -sin[:, 1], sin[:, 1]], axis=-1)
    cos_t, sin_t = lane_pattern(cos_row, 1.0), lane_pattern(sin_row, 0.0)
    pad = lambda a, v: jnp.concatenate([a, jnp.full((ctx_len, LANES), v, F32)], axis=0)
    return pad(cos_t, 1.0), pad(sin_t, 0.0)


def _rope(x, cos, sin, dist):
    lane = lax.broadcasted_iota(jnp.int32, x.shape, 1)
    first = (lane % (2 * dist)) < dist
    partner = jnp.where(first, pltpu.roll(x, LANES - dist, 1), pltpu.roll(x, dist, 1))
    return x * cos + partner * sin


def _tok_block(b, j, nj, batch):
    return jnp.where(j < nj, b * nj + j, batch * nj + b)


def _mla_prep_body(pa_ref, cos_ref, sin_ref, gq_ref, gkv_ref, wuq_ref, wuk_ref, wuv_ref,
                   q_ref, k_ref, vt_ref, *, heads, q_lora, kv_lora, qscale):
    pa = pa_ref[...]
    cos, sin = cos_ref[...], sin_ref[...]

    def norm(v, g_ref):
        return (v * lax.rsqrt(jnp.mean(v * v, axis=-1, keepdims=True) + EPS) * g_ref[...]).astype(BF16)

    cq = norm(pa[:, :q_lora], gq_ref)
    q_all = _dot(cq, wuq_ref[...])
    for h in range(heads):
        qh = _rope(q_all[:, h * LANES:(h + 1) * LANES], cos, sin, 8)
        q_ref[0, h] = (qh * qscale).astype(BF16)
    ckv = norm(pa[:, q_lora:q_lora + kv_lora], gkv_ref)
    k_nope = _dot(ckv, wuk_ref[...])
    k_rope = _rope(pa[:, q_lora + kv_lora:], cos, sin, 8)
    for h in range(heads):
        k_ref[0, h] = (k_nope[:, h * LANES:(h + 1) * LANES] + k_rope).astype(BF16)
    vt_ref[0] = _dot(ckv, wuv_ref[...]).T.astype(BF16)


def _mla_prep(pa, cos, sin, g_q, g_kv, w_uq, w_ukv, batch, seq, ctx_len, heads, nope, rope, v_dim):
    q_lora, kv_lora = g_q.shape[0], g_kv.shape[0]
    nj = seq // TOKEN_TILE
    t_len = seq + ctx_len
    pad = LANES - nope - rope
    wuq = jnp.pad(w_uq.reshape(q_lora, heads, nope + rope), ((0, 0), (0, 0), (0, pad)))
    wuq = wuq.reshape(q_lora, heads * LANES).astype(BF16)
    wkv = w_ukv.reshape(kv_lora, heads, nope + v_dim)
    wuk = jnp.pad(wkv[..., :nope], ((0, 0), (0, 0), (0, LANES - nope))).reshape(kv_lora, heads * LANES).astype(BF16)
    wuv = wkv[..., nope:].reshape(kv_lora, heads * v_dim).astype(BF16)
    body = functools.partial(_mla_prep_body, heads=heads, q_lora=q_lora, kv_lora=kv_lora,
                             qscale=(nope + rope) ** -0.5 * LOG2E)
    head_major = pl.BlockSpec((1, heads, TOKEN_TILE, LANES), lambda b, j: (b, 0, j, 0))
    return pl.pallas_call(
        body,
        out_shape=[jax.ShapeDtypeStruct((batch, heads, t_len, LANES), BF16),
                   jax.ShapeDtypeStruct((batch, heads, t_len, LANES), BF16),
                   jax.ShapeDtypeStruct((batch, heads * v_dim, t_len), BF16)],
        grid=(batch, nj + 1),
        in_specs=[pl.BlockSpec((TOKEN_TILE, pa.shape[1]), lambda b, j: (_tok_block(b, j, nj, batch), 0)),
                  pl.BlockSpec((TOKEN_TILE, LANES), lambda b, j: (j, 0)),
                  pl.BlockSpec((TOKEN_TILE, LANES), lambda b, j: (j, 0)),
                  _full((1, q_lora)), _full((1, kv_lora)), _full(wuq.shape), _full(wuk.shape), _full(wuv.shape)],
        out_specs=[head_major, head_major,
                   pl.BlockSpec((1, heads * v_dim, TOKEN_TILE), lambda b, j: (b, 0, j))],
        compiler_params=_params(2), name="mla_prep",
    )(pa, cos, sin, g_q.reshape(1, -1), g_kv.reshape(1, -1), wuq, wuk, wuv)


def _gqa_prep_body(pd_ref, cos_ref, sin_ref, gq_ref, gk_ref, bd_ref, q_ref, k_ref, *, q_width, qscale):
    pd = pd_ref[...]
    cos, sin = cos_ref[...], sin_ref[...]
    bd = bd_ref[...]
    lane = lax.broadcasted_iota(jnp.int32, (TOKEN_TILE, LANES), 1)
    upper = lane >= LANES // 2

    def head_norm(v, g_ref):
        ms = jnp.dot(v * v, bd, preferred_element_type=F32, precision=lax.Precision.HIGHEST)
        return v * lax.rsqrt(ms + EPS) * g_ref[...]

    n_blocks = q_width // LANES
    for i in range(n_blocks):
        xq = _rope(head_norm(pd[:, i * LANES:(i + 1) * LANES], gq_ref), cos, sin, 16) * qscale
        t = (2 * i) // n_blocks
        in_half = upper if t == 1 else jnp.logical_not(upper)
        swapped = pltpu.roll(xq, LANES // 2, 1)
        q_ref[0, 2 * i + t] = jnp.where(in_half, xq, 0.0).astype(BF16)
        q_ref[0, 2 * i + 1 - t] = jnp.where(in_half, swapped, 0.0).astype(BF16)
    xk = _rope(head_norm(pd[:, q_width:], gk_ref), cos, sin, 16)
    k_ref[0, 0] = xk.astype(BF16)


def _gqa_prep(pd, cos, sin, g_q, g_k, batch, seq, ctx_len, heads, kv_heads, head_dim):
    assert kv_heads * head_dim == LANES and heads % (2 * kv_heads) == 0
    nj = seq // TOKEN_TILE
    t_len = seq + ctx_len
    q_width = heads * head_dim
    grp = jnp.arange(LANES) // head_dim
    bd = (grp[:, None] == grp[None, :]).astype(F32) / head_dim
    rep = LANES // head_dim
    body = functools.partial(_gqa_prep_body, q_width=q_width, qscale=head_dim ** -0.5 * LOG2E)
    return pl.pallas_call(
        body,
        out_shape=[jax.ShapeDtypeStruct((batch, heads, t_len, LANES), BF16),
                   jax.ShapeDtypeStruct((batch, 1, t_len, LANES), BF16)],
        grid=(batch, nj + 1),
        in_specs=[pl.BlockSpec((TOKEN_TILE, pd.shape[1]), lambda b, j: (_tok_block(b, j, nj, batch), 0)),
                  pl.BlockSpec((TOKEN_TILE, LANES), lambda b, j: (j, 0)),
                  pl.BlockSpec((TOKEN_TILE, LANES), lambda b, j: (j, 0)),
                  _full((1, LANES)), _full((1, LANES)), _full((LANES, LANES))],
        out_specs=[pl.BlockSpec((1, heads, TOKEN_TILE, LANES), lambda b, j: (b, 0, j, 0)),
                   pl.BlockSpec((1, 1, TOKEN_TILE, LANES), lambda b, j: (b, 0, j, 0))],
        compiler_params=_params(2), name="gqa_prep",
    )(pd, cos, sin, jnp.tile(g_q, rep).reshape(1, LANES), jnp.tile(g_k, rep).reshape(1, LANES), bd)


def _score_pass(q, k_ref, k_head, chunks, s_ref):
    m = None
    for (st, sz) in chunks:
        s_t = _dot_nt(k_ref[0, k_head, st:st + sz, :], q)
        s_ref[st:st + sz, :] = s_t
        cm = jnp.max(s_t, axis=0, keepdims=True)
        m = cm if m is None else jnp.maximum(m, cm)
    return m


def _value_pass(s_ref, m, vt_ref, v_rows, chunks):
    l = jnp.zeros((1, TOKEN_TILE), F32)
    acc = jnp.zeros((v_rows.stop - v_rows.start, TOKEN_TILE), F32)
    for (st, sz) in chunks:
        p = jnp.exp2(s_ref[st:st + sz, :] - m)
        l = l + jnp.sum(p, axis=0, keepdims=True)
        acc = acc + _dot(vt_ref[0, v_rows, st:st + sz], p.astype(BF16))
    return acc / l


def _gated_store(outs, g_ref, o_ref):
    o2 = jnp.concatenate(outs, axis=0).T
    o_ref[...] = (o2 * _silu(g_ref[...].astype(F32))).astype(o_ref.dtype)


def _flash_body(q_ref, k_ref, vt_ref, g_ref, o_ref, s_scr, m_scr, *, k_sel, v_off, v_dim, chunks):
    j = pl.program_id(2)

    @pl.when((pl.program_id(0) == 0) & (pl.program_id(1) == 0) & (j == 0))
    def _():
        s_scr[...] = jnp.zeros(s_scr.shape, F32)
        m_scr[...] = jnp.zeros(m_scr.shape, F32)

    def step(slot, prev):
        heads = (0, 1)
        q = [q_ref[0, a] for a in heads]
        m_prev = [m_scr[prev, a] for a in heads]
        m = [None, None]
        l = [jnp.zeros((1, TOKEN_TILE), F32) for _ in heads]
        acc = [jnp.zeros((v_dim, TOKEN_TILE), F32) for _ in heads]
        for (st, sz) in chunks:
            for a in heads:
                s_t = _dot_nt(k_ref[0, k_sel[a], st:st + sz, :], q[a])
                s_scr[slot, a, st:st + sz, :] = s_t
                cm = jnp.max(s_t, axis=0, keepdims=True)
                m[a] = cm if m[a] is None else jnp.maximum(m[a], cm)
                p = jnp.exp2(s_scr[prev, a, st:st + sz, :] - m_prev[a])
                l[a] = l[a] + jnp.sum(p, axis=0, keepdims=True)
                acc[a] = acc[a] + _dot(vt_ref[0, v_off[a]:v_off[a] + v_dim, st:st + sz], p.astype(BF16))
        for a in heads:
            m_scr[slot, a] = m[a]
        _gated_store([acc[a] / l[a] for a in heads], g_ref, o_ref)

    for parity in (0, 1):
        pl.when(j % 2 == parity)(functools.partial(step, parity, 1 - parity))


def _flash_ctx_body(q_ref, k_ref, vt_ref, g_ref, o_ref, s_scr, *, k_sel, v_off, v_dim, chunks):
    outs = []
    for a in range(2):
        m = _score_pass(q_ref[0, a], k_ref, k_sel[a], chunks, s_scr.at[a])
        outs.append(_value_pass(s_scr.at[a], m, vt_ref, slice(v_off[a], v_off[a] + v_dim), chunks))
    _gated_store(outs, g_ref, o_ref)


def _flash(q, k, vt, gate, batch, seq, ctx_len, k_heads_per_pair, v_rows_per_pair, pairs_per_kv, ctx_queries):
    heads = q.shape[1]
    t_len = seq + ctx_len
    nj = seq // TOKEN_TILE
    v_dim = LANES // 2
    k_sel = (0, 1) if k_heads_per_pair == 2 else (0, 0)
    v_off = (0, v_dim) if v_rows_per_pair == 2 * v_dim else (0, 0)
    k_blocks = k.shape[1] // k_heads_per_pair
    k_block = lambda p: (p // pairs_per_kv) % k_blocks
    chunks = tuple((c * KEY_CHUNK, KEY_CHUNK) for c in range(seq // KEY_CHUNK)) + ((seq, ctx_len),)
    static = dict(k_sel=k_sel, v_off=v_off, v_dim=v_dim)

    width = gate.shape[1]
    done = lambda b, p, j: (b * nj + jnp.maximum(j - 1, 0), p)
    out = pl.pallas_call(
        functools.partial(_flash_body, chunks=chunks, **static),
        out_shape=jax.ShapeDtypeStruct((batch * seq, width), BF16),
        grid=(batch, heads // 2, nj + 1),
        in_specs=[pl.BlockSpec((1, 2, TOKEN_TILE, LANES), lambda b, p, j: (b, p, jnp.minimum(j, nj - 1), 0)),
                  pl.BlockSpec((1, k_heads_per_pair, t_len, LANES), lambda b, p, j: (b, k_block(p), 0, 0)),
                  pl.BlockSpec((1, v_rows_per_pair, t_len), lambda b, p, j: (b, p // pairs_per_kv, 0)),
                  pl.BlockSpec((TOKEN_TILE, LANES), done)],
        out_specs=pl.BlockSpec((TOKEN_TILE, LANES), done),
        scratch_shapes=[pltpu.VMEM((2, 2, t_len, TOKEN_TILE), F32), pltpu.VMEM((2, 2, 1, TOKEN_TILE), F32)],
        compiler_params=_params(3), name="flash_attention",
    )(q, k, vt, gate)
    if not ctx_queries:
        return (out,)

    out_ctx = pl.pallas_call(
        functools.partial(_flash_ctx_body, chunks=((0, ctx_len),), **static),
        out_shape=jax.ShapeDtypeStruct((batch * ctx_len, width), BF16),
        grid=(batch, heads // 2),
        in_specs=[pl.BlockSpec((1, 2, TOKEN_TILE, LANES), lambda b, p: (b, p, nj, 0)),
                  pl.BlockSpec((1, k_heads_per_pair, ctx_len, LANES), lambda b, p: (b, k_block(p), seq // ctx_len, 0)),
                  pl.BlockSpec((1, v_rows_per_pair, ctx_len), lambda b, p: (b, p // pairs_per_kv, seq // ctx_len)),
                  pl.BlockSpec((TOKEN_TILE, LANES), lambda b, p: (batch * nj + b, p))],
        out_specs=pl.BlockSpec((TOKEN_TILE, LANES), lambda b, p: (b, p)),
        scratch_shapes=[pltpu.VMEM((2, ctx_len, TOKEN_TILE), F32)],
        compiler_params=_params(2), name="flash_attention_ctx",
    )(q, k, vt, gate)
    return out, out_ctx


NA_KEY_ROWS = 12
NA_BIAS_ROWS = 23


def _na_bias_body(rpb_ref, o_ref):
    p = pl.program_id(0)
    shape = (GRID_W, LANES)
    kc = lax.broadcasted_iota(jnp.int32, shape, 0)
    lane = lax.broadcasted_iota(jnp.int32, shape, 1)
    qc = lane % GRID_W
    upper = lane >= GRID_W
    rel = kc - qc + (WIN_C - 1)
    c0 = jnp.clip(qc - WIN_C // 2, 0, GRID_W - WIN_C)
    col_ok = (kc >= c0) & (kc < c0 + WIN_C)
    n_rel_r, n_rel_c = 2 * WIN_R - 1, 2 * WIN_C - 1
    o_ref[...] = jnp.zeros(o_ref.shape, F32)

    def block(dd, carry):
        base0 = (2 * p) * (n_rel_r * n_rel_c) + dd * n_rel_c
        base1 = base0 + n_rel_r * n_rel_c
        val = jnp.zeros(shape, F32)
        for jj in range(n_rel_c):
            val = jnp.where(rel == jj, jnp.where(upper, rpb_ref[base1 + jj], rpb_ref[base0 + jj]), val)
        off = pl.multiple_of((dd - (WIN_R - 1) + NA_BIAS_ROWS // 2) * GRID_W, GRID_W)
        o_ref[0, pl.ds(off, GRID_W), :] = jnp.where(col_ok, val * LOG2E, NEG)
        return carry

    lax.fori_loop(0, n_rel_r, block, 0)


def _na_bias(rpb):
    heads = rpb.shape[0]
    return pl.pallas_call(
        _na_bias_body,
        out_shape=jax.ShapeDtypeStruct((heads // 2, NA_BIAS_ROWS * GRID_W, LANES), F32),
        grid=(heads // 2,),
        in_specs=[pl.BlockSpec(memory_space=pltpu.SMEM)],
        out_specs=pl.BlockSpec((1, NA_BIAS_ROWS * GRID_W, LANES), lambda p: (p, 0, 0)),
        compiler_params=_params(1), name="na_bias",
    )(rpb.reshape(-1))


def _na_body(q_ref, kl_ref, kc_ref, vt_ref, bias_ref, g_ref, o_ref, *, seq, ctx_len, rows_per_step):
    rblk = pl.program_id(2)
    n_rows = seq // GRID_W
    span = NA_KEY_ROWS * GRID_W
    lane = lax.broadcasted_iota(jnp.int32, (GRID_W, LANES), 1)
    lower = lane < GRID_W
    key_row = lax.broadcasted_iota(jnp.int32, (span, LANES), 0) // GRID_W
    r0 = rblk * rows_per_step
    u0 = jnp.minimum(jnp.clip(r0 - WIN_R // 2, 0, n_rows - WIN_R), n_rows - NA_KEY_ROWS)
    q = q_ref[...]
    zero = jnp.zeros((GRID_W, LANES), q.dtype)
    parts = []
    for rr in range(rows_per_step):
        q_r = q[rr * GRID_W:(rr + 1) * GRID_W]
        parts += [jnp.where(lower, q_r, zero), jnp.where(lower, zero, q_r)]
    q2 = jnp.concatenate(parts, axis=0)
    k_off = pl.multiple_of(u0 * GRID_W, TOKEN_TILE)
    s_loc = _dot_nt(kl_ref[pl.ds(k_off, span), :], q2)
    s_ctx = _dot_nt(kc_ref[...], q2)
    cols = []
    for rr in range(rows_per_step):
        r = r0 + rr
        a0 = jnp.clip(r - WIN_R // 2, 0, n_rows - WIN_R) - u0
        b_off = pl.multiple_of((u0 - r + NA_BIAS_ROWS // 2) * GRID_W, GRID_W)
        blk = s_loc[:, rr * LANES:(rr + 1) * LANES] + bias_ref[0, pl.ds(b_off, span), :]
        cols.append(jnp.where((key_row >= a0) & (key_row < a0 + WIN_R), blk, NEG))
    s_loc = jnp.concatenate(cols, axis=1)
    m = jnp.maximum(jnp.max(s_loc, axis=0, keepdims=True), jnp.max(s_ctx, axis=0, keepdims=True))
    p_loc = jnp.exp2(s_loc - m)
    p_ctx = jnp.exp2(s_ctx - m)
    l = jnp.sum(p_loc, axis=0, keepdims=True) + jnp.sum(p_ctx, axis=0, keepdims=True)
    r_t = (_dot(vt_ref[0, :, pl.ds(k_off, span)], p_loc.astype(BF16))
           + _dot(vt_ref[0, :, seq:seq + ctx_len], p_ctx.astype(BF16)))
    r_n = (r_t / l).T
    outs = [jnp.where(lower, r_n[rr * LANES:rr * LANES + GRID_W], r_n[rr * LANES + GRID_W:(rr + 1) * LANES])
            for rr in range(rows_per_step)]
    o = jnp.concatenate(outs, axis=0)
    o_ref[...] = (o * _silu(g_ref[...].astype(F32))).astype(o_ref.dtype)


def _na_attention(q, k, vt, bias, gate, batch, seq, ctx_len):
    pairs = q.shape[1] // LANES
    t_len = seq + ctx_len
    rows_per_step = TOKEN_TILE // GRID_W
    n_steps = seq // TOKEN_TILE
    lat_tile = lambda b, p, s: (b * n_steps + s, p)
    body = functools.partial(_na_body, seq=seq, ctx_len=ctx_len, rows_per_step=rows_per_step)
    return pl.pallas_call(
        body, out_shape=jax.ShapeDtypeStruct((batch * seq, q.shape[1]), BF16),
        grid=(batch, pairs, n_steps),
        in_specs=[pl.BlockSpec((TOKEN_TILE, LANES), lat_tile),
                  pl.BlockSpec((seq, LANES), lambda b, p, s: (b, p)),
                  pl.BlockSpec((ctx_len, LANES), lambda b, p, s: (batch * seq // ctx_len + b, p)),
                  pl.BlockSpec((1, LANES, t_len), lambda b, p, s: (b, p, 0)),
                  pl.BlockSpec((1, NA_BIAS_ROWS * GRID_W, LANES), lambda b, p, s: (p, 0, 0)),
                  pl.BlockSpec((TOKEN_TILE, LANES), lat_tile)],
        out_specs=pl.BlockSpec((TOKEN_TILE, LANES), lat_tile),
        compiler_params=_params(3), name="neighborhood_attention",
    )(q, k, k, vt, bias, gate)


def _seg_scan(v, reverse, use_max=False):
    n = v.shape[1]
    lane = lax.broadcasted_iota(jnp.int32, v.shape, 1) % ML_CHUNK
    k = 1
    while k < ML_CHUNK:
        if reverse:
            ok, shifted = lane < ML_CHUNK - k, pltpu.roll(v, n - k, 1)
        else:
            ok, shifted = lane >= k, pltpu.roll(v, k, 1)
        v = jnp.maximum(v, jnp.where(ok, shifted, NEG)) if use_max else v + jnp.where(ok, shifted, 0.0)
        k *= 2
    return v


ML_GATE_ROWS = 40


def _mlstm_prep_body(u_ref, up_ref, un_ref, cw_ref, cb_ref, wqk_ref, wv_ref, wg_ref, bg_ref,
                     xc_ref, k_ref, qt_ref, vt_ref, g_ref, gc_ref, *, nj, width, kscale):
    j = pl.program_id(1)
    u = u_ref[...]
    row = lax.broadcasted_iota(jnp.int32, u.shape, 0)
    prev = jnp.where((j > 0) & (j < nj), up_ref[7:8, :], 0.0)
    nxt = jnp.where(j < nj - 1, un_ref[0:1, :], 0.0)
    u_m1 = jnp.where(row == 0, prev, pltpu.roll(u, 1, 0))
    u_p1 = jnp.where(row == TOKEN_TILE - 1, nxt, pltpu.roll(u, TOKEN_TILE - 1, 0))
    cw = cw_ref[...]
    xc = _silu(u_m1 * cw[0:1] + u * cw[1:2] + u_p1 * cw[2:3] + cb_ref[...])
    xcb = xc.astype(BF16)
    xc_ref[...] = xcb
    qk = _dot(xcb, wqk_ref[...])
    v = _dot(u.astype(BF16), wv_ref[...])
    qb, kb, vb = qk[:, :width].astype(BF16), qk[:, width:].astype(BF16), v.astype(BF16)
    k_ref[...] = (qk[:, width:] * kscale).astype(BF16)
    qt_ref[0] = qk[:, :width].T.astype(BF16)
    vt_ref[0] = v.T.astype(BF16)
    pre = _dot_nt(wg_ref[0], qb) + _dot_nt(wg_ref[1], kb) + _dot_nt(wg_ref[2], vb) + bg_ref[...]
    r_cols = []
    for d in range(2):
        rev = d == 1
        i8 = pre[16 * d:16 * d + 8]
        f8 = _log_sigmoid(pre[16 * d + 8:16 * d + 16])
        b8 = _seg_scan(f8, rev)
        b_last = b8 + _seg_scan(f8, not rev) - f8
        r8 = i8 - b8
        c8 = _seg_scan(r8, rev, use_max=True)
        r_max = jnp.maximum(c8, _seg_scan(r8, not rev, use_max=True))
        g_ref[0, d * ML_GATE_ROWS:(d + 1) * ML_GATE_ROWS] = jnp.concatenate(
            [-c8, b8 + c8, jnp.exp(r8 - r_max), b_last, b_last + r_max], axis=0)
        r_cols.append(r8)
    pad = jnp.zeros((LANES - 16, TOKEN_TILE), F32)
    gc_ref[...] = jnp.concatenate(r_cols + [pad], axis=0).T


def _mlstm_prep(u, conv_w, conv_b, w_q, w_k, w_v, w_gate, b_gate, batch, seq, ctx_len):
    heads, hd = w_q.shape[0], w_q.shape[1]
    assert heads == ML_HEADS and hd == ML_CHUNK
    width = heads * hd
    nj = seq // TOKEN_TILE
    t_len = seq + ctx_len
    rows = u.shape[0]

    def block_diag(w):
        eye = jnp.eye(heads, dtype=w.dtype)
        return (eye[:, None, :, None] * w[:, :, None, :]).reshape(width, width)

    wqk = jnp.concatenate([block_diag(w_q), block_diag(w_k)], axis=1).astype(BF16)
    wv = block_diag(w_v).astype(BF16)
    wg = w_gate.reshape(2, heads, 3, hd, 2, heads).transpose(2, 0, 4, 5, 1, 3).reshape(3, 2, 2, heads, width)
    wg = jnp.pad(wg, ((0, 0), (0, 0), (0, 0), (0, 8 - heads), (0, 0))).reshape(3, 32, width).astype(BF16)
    bg = jnp.pad(b_gate.reshape(2, 2, heads), ((0, 0), (0, 0), (0, 8 - heads))).reshape(32, 1)
    n_halo = rows // 8
    tokb = lambda b, j: _tok_block(b, j, nj, batch)
    tile = pl.BlockSpec((TOKEN_TILE, width), lambda b, j: (tokb(b, j), 0))
    feat = pl.BlockSpec((1, width, TOKEN_TILE), lambda b, j: (b, 0, j))
    body = functools.partial(_mlstm_prep_body, nj=nj, width=width, kscale=hd ** -0.5)
    per_tile = TOKEN_TILE // 8
    return pl.pallas_call(
        body,
        out_shape=[jax.ShapeDtypeStruct((rows, width), BF16)] * 2
        + [jax.ShapeDtypeStruct((batch, width, t_len), BF16)] * 2
        + [jax.ShapeDtypeStruct((batch, 2 * ML_GATE_ROWS, t_len), F32), jax.ShapeDtypeStruct((rows, LANES), F32)],
        grid=(batch, nj + 1),
        in_specs=[tile,
                  pl.BlockSpec((8, width), lambda b, j: (jnp.maximum(tokb(b, j) * per_tile - 1, 0), 0)),
                  pl.BlockSpec((8, width), lambda b, j: (jnp.minimum((tokb(b, j) + 1) * per_tile, n_halo - 1), 0)),
                  _full((3, width)), _full((1, width)), _full(wqk.shape), _full(wv.shape),
                  _full(wg.shape), _full((32, 1))],
        out_specs=[tile, tile, feat, feat,
                   pl.BlockSpec((1, 2 * ML_GATE_ROWS, TOKEN_TILE), lambda b, j: (b, 0, j)),
                   pl.BlockSpec((TOKEN_TILE, LANES), lambda b, j: (tokb(b, j), 0))],
        compiler_params=_params(2), name="mlstm_prep",
    )(u, u, u, conv_w, conv_b.reshape(1, width), wqk, wv, wg, bg)


def _mlstm_seq_body(kf, qtf, vtf, gf, gcf, kb, qtb, vtb, gb, gcb, hf_ref, hb_ref, c_s, n_s, m_s):
    t = pl.program_id(1)
    L = ML_CHUNK

    @pl.when(t == 0)
    def _():
        c_s[...] = jnp.zeros(c_s.shape, F32)
        n_s[...] = jnp.zeros(n_s.shape, F32)
        m_s[...] = jnp.zeros(m_s.shape, F32)

    si = lax.broadcasted_iota(jnp.int32, (L, L), 0)
    li = lax.broadcasted_iota(jnp.int32, (L, L), 1)
    streams = ((kf, qtf, vtf, gf, gcf, hf_ref, si <= li), (kb, qtb, vtb, gb, gcb, hb_ref, si >= li))
    for d, (k_ref, qt_ref, vt_ref, g_ref, gc_ref, h_ref, incl) in enumerate(streams):
        g = g_ref[0]
        gc = gc_ref[...]
        for h in range(ML_HEADS):
            idx = d * ML_HEADS + h
            cols = slice(h * L, (h + 1) * L)
            k, qt, vt = k_ref[:, cols], qt_ref[0, cols, :], vt_ref[0, cols, :]
            neg_c, m_loc, w0, b_last, g_max = (g[8 * i + h:8 * i + h + 1] for i in range(5))
            r_col = gc[:, 8 * d + h:8 * d + h + 1]
            c_st, n_st, m_st = c_s[idx], n_s[idx], m_s[idx]
            p0 = jnp.where(incl, jnp.exp(r_col + neg_c), 0.0) * _dot(k, qt)
            s_sum = jnp.sum(p0, axis=0, keepdims=True)
            intra = _dot(vt, p0.astype(BF16))
            cn = _dot(jnp.concatenate([c_st, n_st], axis=0).astype(BF16), qt)
            delta = jnp.maximum(m_st + neg_c, 0.0)
            e_intra = jnp.exp(-delta)
            w_inter = jnp.exp(m_st + neg_c - delta)
            num = w_inter * cn[:L] + e_intra * intra
            den = w_inter * cn[L:L + 1] + e_intra * s_sum
            h_ref[0, cols, :] = num / jnp.maximum(jnp.abs(den), jnp.exp(-(m_loc + delta)))
            m_new = jnp.maximum(b_last + m_st, g_max)
            decay = jnp.exp(b_last + m_st - m_new)
            gain = jnp.exp(g_max - m_new)
            c_s[idx] = decay * c_st + gain * _dot((vt.astype(F32) * w0).astype(BF16), k)
            w8 = jnp.broadcast_to(w0, (8, L)).astype(BF16)
            n_s[idx] = decay * n_st + gain * _dot(w8, k)
            m_s[idx] = m_new


def _mlstm_seq(k, qt, vt, gates, gcols, batch, seq, ctx_len):
    width = k.shape[1]
    L = ML_CHUNK
    n_lat, n_ctx = seq // L, ctx_len // L
    n_chunks = n_lat + n_ctx
    fwd = lambda t: (t + n_lat) % n_chunks
    bwd = lambda t: n_chunks - 1 - t
    rowblk = lambda b, c: jnp.where(c < n_lat, b * n_lat + c, batch * n_lat + b * n_ctx + (c - n_lat))

    def stream(chunk_of, d):
        feat = pl.BlockSpec((1, width, L), lambda b, t: (b, 0, chunk_of(t)))
        return [pl.BlockSpec((L, width), lambda b, t: (rowblk(b, chunk_of(t)), 0)), feat, feat,
                pl.BlockSpec((1, ML_GATE_ROWS, L), lambda b, t: (b, d, chunk_of(t))),
                pl.BlockSpec((L, LANES), lambda b, t: (rowblk(b, chunk_of(t)), 0))]

    out_f = pl.BlockSpec((1, width, L), lambda b, t: (b, 0, fwd(t)))
    out_b = pl.BlockSpec((1, width, L), lambda b, t: (b, 0, bwd(t)))
    n_state = 2 * ML_HEADS
    return pl.pallas_call(
        _mlstm_seq_body,
        out_shape=[jax.ShapeDtypeStruct(qt.shape, F32)] * 2,
        grid=(batch, n_chunks),
        in_specs=stream(fwd, 0) + stream(bwd, 1),
        out_specs=[out_f, out_b],
        scratch_shapes=[pltpu.VMEM((n_state, L, L), F32), pltpu.VMEM((n_state, 8, L), F32),
                        pltpu.VMEM((n_state, 1, L), F32)],
        compiler_params=_params(2), name="mlstm_recurrence",
    )(k, qt, vt, gates, gcols, k, qt, vt, gates, gcols)


def _mlstm_out_body(hf_ref, hb_ref, xc_ref, z_ref, gh_ref, sk_ref, o_ref):
    ht = hf_ref[0] + hb_ref[0]
    L = ML_CHUNK
    normed = []
    for hd in range(ML_HEADS):
        hh = ht[hd * L:(hd + 1) * L]
        mu = jnp.mean(hh, axis=0, keepdims=True)
        var = jnp.mean(jnp.square(hh - mu), axis=0, keepdims=True)
        normed.append((hh - mu) * lax.rsqrt(var + EPS))
    hn = jnp.concatenate(normed, axis=0).T * gh_ref[...]
    o = (hn + sk_ref[...] * xc_ref[...].astype(F32)) * _silu(z_ref[...].astype(F32))
    o_ref[...] = o.astype(o_ref.dtype)


def _mlstm_out(hf, hb, xc, z, g_head, skip, batch, seq):
    rows, width = xc.shape
    nj = seq // TOKEN_TILE
    tile = pl.BlockSpec((TOKEN_TILE, width), lambda b, j: (_tok_block(b, j, nj, batch), 0))
    feat = pl.BlockSpec((1, width, TOKEN_TILE), lambda b, j: (b, 0, j))
    return pl.pallas_call(
        _mlstm_out_body, out_shape=jax.ShapeDtypeStruct((rows, width), BF16), grid=(batch, nj + 1),
        in_specs=[feat, feat, tile, tile, _full((1, width)), _full((1, width))],
        out_specs=tile, compiler_params=_params(2), name="mlstm_out",
    )(hf, hb, xc, z, g_head.reshape(1, width), skip.reshape(1, width))


def kernel(x, c, ctx, c_ctx, w_mod, b_mod, g_norm, ab_w_in, ab_w_out, mla_g_q, mla_w_uq, mla_g_kv, mla_w_ukv,
           ml_conv_w, ml_conv_b, ml_w_q, ml_w_k, ml_w_v, ml_w_gate, ml_b_gate, ml_g_head, ml_skip,
           cd_w_in, cd_w_out, na_rpb, gqa_g_q, gqa_g_k, g_final):
    batch, seq, d = x.shape
    ctx_len = ctx.shape[1]
    assert ctx_len == TOKEN_TILE and seq % KEY_CHUNK == 0 and seq // GRID_W >= NA_KEY_ROWS
    dims = (batch, seq, ctx_len)

    mla_heads, mla_rope, mla_v = 8, 32, 64
    mla_nope = mla_w_uq.shape[2] // mla_heads - mla_rope
    q_lora, kv_lora = mla_g_q.shape[1], mla_g_kv.shape[1]
    ml_width = ml_conv_w.shape[2]
    mla_width = mla_heads * mla_v
    gqa_heads, gqa_dim = 8, gqa_g_q.shape[1]
    gqa_kv = (cd_w_in.shape[2] - 4 * 512 - 2 * gqa_heads * gqa_dim) // (2 * gqa_dim)
    na_width = na_rpb.shape[1] * 64

    mod_rows = -(-(batch + 1) // 8) * 8
    cvec = jnp.concatenate([c, c_ctx[None], jnp.zeros((mod_rows - batch - 1, d), F32)], axis=0)
    mod = _modulation(cvec, w_mod, b_mod)
    mod0 = mod[0].reshape(mod_rows, 1, 3 * d)
    mod1 = mod[1].reshape(mod_rows, 1, 3 * d)

    tok0 = _Tokens((x.reshape(batch * seq, d), ctx.reshape(batch * ctx_len, d)), *dims)
    w_in = ab_w_in[0]
    s1 = q_lora + kv_lora
    zcol = lambda n: jnp.zeros((d, n), w_in.dtype)
    w0 = jnp.concatenate([w_in[:, :s1], zcol(mla_nope), w_in[:, s1:s1 + mla_rope],
                          zcol(LANES - mla_nope - mla_rope), w_in[:, s1 + mla_rope:]], axis=1).astype(BF16)
    o_pa = s1 + LANES
    outs0 = ((0, o_pa, F32, 1.0, False), (o_pa, mla_width, BF16, 1.0, False),
             (o_pa + mla_width, ml_width, F32, 1.0, False), (o_pa + mla_width + ml_width, ml_width, BF16, 1.0, False))
    pa, gate_a, u, z = _in_proj(tok0, mod0, g_norm[0], w0, outs0, *dims)

    def mla_lanes(row, fill):
        n = row.shape[0]
        return jnp.concatenate([jnp.full((n, mla_nope), fill, F32), row,
                                jnp.full((n, LANES - mla_nope - mla_rope), fill, F32)], axis=-1)

    cos_a, sin_a = _rope_tables(seq, ctx_len, mla_rope, mla_lanes)
    q_a, k_a, vt_a = _mla_prep(pa, cos_a, sin_a, mla_g_q[0], mla_g_kv[0], mla_w_uq[0], mla_w_ukv[0],
                               *dims, mla_heads, mla_nope, mla_rope, mla_v)
    mix_a = _flash(q_a, k_a, vt_a, gate_a, *dims, k_heads_per_pair=2, v_rows_per_pair=2 * mla_v,
                   pairs_per_kv=1, ctx_queries=True)

    xc, k_m, qt_m, vt_m, gates, gcols = _mlstm_prep(u, ml_conv_w[0], ml_conv_b[0], ml_w_q[0], ml_w_k[0], ml_w_v[0],
                                                   ml_w_gate[0], ml_b_gate[0], *dims)
    h_f, h_b = _mlstm_seq(k_m, qt_m, vt_m, gates, gcols, *dims)
    mix_b = _mlstm_out(h_f, h_b, xc, z, ml_g_head[0], ml_skip[0], batch, seq)
    x1 = _out_proj(tok0, _Tokens(mix_a, *dims), mix_b, ab_w_out[0], mod0, batch, seq)

    tok1 = _Tokens((x1,), *dims)
    w1 = cd_w_in[0].astype(BF16)
    gq_w, gkv_w = gqa_heads * gqa_dim, gqa_kv * gqa_dim
    o_d = 4 * na_width
    outs1 = ((0, na_width, BF16, 64 ** -0.5 * LOG2E, False), (na_width, na_width, BF16, 1.0, False),
             (2 * na_width, na_width, BF16, 1.0, True), (3 * na_width, na_width, BF16, 1.0, False),
             (o_d, gq_w + gkv_w, F32, 1.0, False), (o_d + gq_w + gkv_w, gkv_w, BF16, 1.0, True),
             (o_d + gq_w + 2 * gkv_w, gq_w, BF16, 1.0, False))
    q_c, k_c, vt_c, gate_c, pd, vt_d, gate_d = _in_proj(tok1, mod1, g_norm[1], w1, outs1, *dims)

    mix_c = _na_attention(q_c, k_c, vt_c, _na_bias(na_rpb[0]), gate_c, *dims)

    gqa_lanes = lambda row, fill: jnp.tile(row, (1, LANES // gqa_dim))
    cos_d, sin_d = _rope_tables(seq, ctx_len, gqa_dim, gqa_lanes)
    q_d, k_d = _gqa_prep(pd, cos_d, sin_d, gqa_g_q[0], gqa_g_k[0], *dims, gqa_heads, gqa_kv, gqa_dim)
    mix_d = _flash(q_d, k_d, vt_d, gate_d, *dims, k_heads_per_pair=1, v_rows_per_pair=gqa_dim,
                   pairs_per_kv=gqa_heads // (2 * gqa_kv), ctx_queries=False)

    out = _out_proj(tok1, _Tokens((mix_c,), *dims), mix_d[0], cd_w_out[0], mod1, batch, seq, g_final=g_final)
    return out.reshape(batch, seq, d)
```

```python
import functools

import jax
import jax.numpy as jnp
from jax import lax
from jax.experimental import pallas as pl
from jax.experimental.pallas import tpu as pltpu

F32 = jnp.float32
BF16 = jnp.bfloat16

LANES = 128
TOKEN_TILE = 256
KEY_CHUNK = 512
GRID_W = 64
WIN_R = 8
WIN_C = 16
ML_CHUNK = 128
ML_HEADS = 4
EPS = 1e-6
ROPE_BASE = 10000.0
LOG2E = 1.4426950408889634
NEG = -1e30
VMEM_LIMIT = 56 * 1024 * 1024

_NT = (((1,), (1,)), ((), ()))


def _dot(a, b):
    return jnp.dot(a, b, preferred_element_type=F32)


def _dot_nt(a, b):
    return lax.dot_general(a, b, _NT, preferred_element_type=F32)


def _silu(v):
    return v * (1.0 / (1.0 + jnp.exp(-v)))


def _log_sigmoid(v):
    return -(jnp.maximum(-v, 0.0) + jnp.log1p(jnp.exp(-jnp.abs(v))))


def _params(n_axes):
    return pltpu.CompilerParams(dimension_semantics=("arbitrary",) * n_axes, vmem_limit_bytes=VMEM_LIMIT)


def _full(shape):
    nd = len(shape)
    return pl.BlockSpec(shape, lambda *_: (0,) * nd)


def _mod_body(c_ref, w_ref, b_ref, o_ref):
    s = _silu(c_ref[...])
    o_ref[0] = _dot(s.astype(BF16), w_ref[0].astype(BF16)) + b_ref[0]


def _modulation(cvec, w_mod, b_mod):
    depth, d, n = w_mod.shape
    rows = cvec.shape[0]
    tn = n // 4
    return pl.pallas_call(
        _mod_body,
        out_shape=jax.ShapeDtypeStruct((depth, rows, n), F32),
        grid=(depth, n // tn),
        in_specs=[_full((rows, d)),
                  pl.BlockSpec((1, d, tn), lambda l, j: (l, 0, j)),
                  pl.BlockSpec((1, 1, tn), lambda l, j: (l, 0, j))],
        out_specs=pl.BlockSpec((1, rows, tn), lambda l, j: (l, 0, j)),
        compiler_params=_params(2), name="modulation",
    )(cvec, w_mod, b_mod.reshape(depth, 1, n))


class _Tokens:
    def __init__(self, arrays, batch, seq, ctx_len):
        self.arrays = arrays
        self.split = len(arrays) == 2
        self.n_lat = batch * seq // TOKEN_TILE
        self.n_ctx = batch * ctx_len // TOKEN_TILE
        self.d = arrays[0].shape[-1]

    def specs(self):
        blk = (TOKEN_TILE, self.d)
        if not self.split:
            return [pl.BlockSpec(blk, lambda i: (i, 0))]
        n_lat = self.n_lat
        return [pl.BlockSpec(blk, lambda i: (jnp.minimum(i, n_lat - 1), 0)),
                pl.BlockSpec(blk, lambda i: (jnp.maximum(i - n_lat, 0), 0))]

    def load(self, refs, i):
        if not self.split:
            return refs[0][...]
        return jnp.where(i < self.n_lat, refs[0][...], refs[1][...])


def _mod_spec(n_lat, nj, batch, width):
    return pl.BlockSpec((1, 1, width), lambda i: (jnp.where(i < n_lat, i // nj, batch), 0, 0))


def _in_proj_body(*refs, tok, outs, d):
    n_tok = len(tok.arrays)
    mod_ref, g_ref, w_ref = refs[n_tok:n_tok + 3]
    o_refs = refs[n_tok + 3:]
    i = pl.program_id(0)
    x = tok.load(refs[:n_tok], i)
    y = x * lax.rsqrt(jnp.mean(x * x, axis=-1, keepdims=True) + EPS) * g_ref[...]
    mod = mod_ref[0]
    h = y * (1.0 + mod[:, d:2 * d]) + mod[:, :d]
    acc = _dot(h.astype(BF16), w_ref[...])
    for o_ref, (c0, width, _, scale, transposed) in zip(o_refs, outs):
        v = acc[:, c0:c0 + width]
        if scale != 1.0:
            v = v * scale
        if transposed:
            o_ref[0] = v.T.astype(o_ref.dtype)
        else:
            o_ref[...] = v.astype(o_ref.dtype)


def _in_proj(tok, mod_l, g, w, outs, batch, seq, ctx_len):
    d = tok.d
    nj = seq // TOKEN_TILE
    n_lat, n_all = tok.n_lat, tok.n_lat + tok.n_ctx
    rows = n_all * TOKEN_TILE
    t_len = seq + ctx_len
    out_shape, out_specs = [], []
    for (_, width, dtype, _, transposed) in outs:
        if transposed:
            out_shape.append(jax.ShapeDtypeStruct((batch, width, t_len), dtype))
            out_specs.append(pl.BlockSpec(
                (1, width, TOKEN_TILE),
                lambda i: (jnp.where(i < n_lat, i // nj, i - n_lat), 0, jnp.where(i < n_lat, i % nj, nj))))
        else:
            out_shape.append(jax.ShapeDtypeStruct((rows, width), dtype))
            out_specs.append(pl.BlockSpec((TOKEN_TILE, width), lambda i: (i, 0)))
    body = functools.partial(_in_proj_body, tok=tok, outs=outs, d=d)
    return pl.pallas_call(
        body, out_shape=out_shape, grid=(n_all,),
        in_specs=tok.specs() + [_mod_spec(n_lat, nj, batch, 3 * d), _full((1, d)), _full(w.shape)],
        out_specs=out_specs, compiler_params=_params(1), name="in_proj",
    )(*tok.arrays, mod_l, g.reshape(1, d), w)


def _out_proj_body(*refs, tok, mix_a, d, final):
    n_tok, n_a = len(tok.arrays), len(mix_a.arrays)
    mb_ref, wa_ref, wb_ref, mod_ref = refs[n_tok + n_a:n_tok + n_a + 4]
    rest = refs[n_tok + n_a + 4:]
    i = pl.program_id(0)
    x = tok.load(refs[:n_tok], i)
    acc = _dot(mix_a.load(refs[n_tok:n_tok + n_a], i), wa_ref[...]) + _dot(mb_ref[...], wb_ref[...])
    xn = x + mod_ref[0][:, 2 * d:] * acc
    if final:
        gf_ref, o_ref = rest
        xn = xn * lax.rsqrt(jnp.mean(xn * xn, axis=-1, keepdims=True) + EPS) * gf_ref[...]
    else:
        (o_ref,) = rest
    o_ref[...] = xn


def _out_proj(tok, mix_a, mix_b, w_out, mod_l, batch, seq, g_final=None):
    d = tok.d
    half = mix_a.d
    nj = seq // TOKEN_TILE
    n_lat = tok.n_lat
    final = g_final is not None
    n_tiles = n_lat if final else n_lat + tok.n_ctx
    wa, wb = w_out[:half].astype(BF16), w_out[half:].astype(BF16)
    tile = lambda width: pl.BlockSpec((TOKEN_TILE, width), lambda i: (i, 0))
    in_specs = tok.specs() + mix_a.specs() + [tile(half), _full(wa.shape), _full(wb.shape),
                                              _mod_spec(n_lat, nj, batch, 3 * d)]
    args = list(tok.arrays) + list(mix_a.arrays) + [mix_b, wa, wb, mod_l]
    if final:
        in_specs.append(_full((1, d)))
        args.append(g_final.reshape(1, d))
    body = functools.partial(_out_proj_body, tok=tok, mix_a=mix_a, d=d, final=final)
    return pl.pallas_call(
        body, out_shape=jax.ShapeDtypeStruct((n_tiles * TOKEN_TILE, d), F32), grid=(n_tiles,),
        in_specs=in_specs, out_specs=tile(d), compiler_params=_params(1), name="out_proj",
    )(*args)


def _rope_tables(seq, ctx_len, rot_dim, lane_pattern):
    t = jnp.arange(seq)
    pos = jnp.stack([t // GRID_W, t % GRID_W], axis=-1).astype(F32)
    n_freq = rot_dim // 4
    inv = ROPE_BASE ** (-jnp.arange(n_freq, dtype=F32) / n_freq)
    ang = pos[:, :, None] * inv
    cos, sin = jnp.cos(ang), jnp.sin(ang)
    cos_row = jnp.concatenate([cos[:, 0], cos[:, 0], cos[:, 1], cos[:, 1]], axis=-1)
    sin_row = jnp.concatenate([-sin[:, 0], sin[:, 0], -sin[:, 1], sin[:, 1]], axis=-1)
    cos_t, sin_t = lane_pattern(cos_row, 1.0), lane_pattern(sin_row, 0.0)
    pad = lambda a, v: jnp.concatenate([a, jnp.full((ctx_len, LANES), v, F32)], axis=0)
    return pad(cos_t, 1.0), pad(sin_t, 0.0)


def _rope(x, cos, sin, dist):
    lane = lax.broadcasted_iota(jnp.int32, x.shape, 1)
    first = (lane % (2 * dist)) < dist
    partner = jnp.where(first, pltpu.roll(x, LANES - dist, 1), pltpu.roll(x, dist, 1))
    return x * cos + partner * sin


def _tok_block(b, j, nj, batch):
    return jnp.where(j < nj, b * nj + j, batch * nj + b)


def _mla_prep_body(pa_ref, cos_ref, sin_ref, gq_ref, gkv_ref, wuq_ref, wuk_ref, wuv_ref,
                   q_ref, k_ref, vt_ref, *, heads, q_lora, kv_lora, qscale):
    pa = pa_ref[...]
    cos, sin = cos_ref[...], sin_ref[...]

    def norm(v, g_ref):
        return (v * lax.rsqrt(jnp.mean(v * v, axis=-1, keepdims=True) + EPS) * g_ref[...]).astype(BF16)

    cq = norm(pa[:, :q_lora], gq_ref)
    q_all = _dot(cq, wuq_ref[...])
    for h in range(heads):
        qh = _rope(q_all[:, h * LANES:(h + 1) * LANES], cos, sin, 8)
        q_ref[0, h] = (qh * qscale).astype(BF16)
    ckv = norm(pa[:, q_lora:q_lora + kv_lora], gkv_ref)
    k_nope = _dot(ckv, wuk_ref[...])
    k_rope = _rope(pa[:, q_lora + kv_lora:], cos, sin, 8)
    for h in range(heads):
        k_ref[0, h] = (k_nope[:, h * LANES:(h + 1) * LANES] + k_rope).astype(BF16)
    vt_ref[0] = _dot(ckv, wuv_ref[...]).T.astype(BF16)


def _mla_prep(pa, cos, sin, g_q, g_kv, w_uq, w_ukv, batch, seq, ctx_len, heads, nope, rope, v_dim):
    q_lora, kv_lora = g_q.shape[0], g_kv.shape[0]
    nj = seq // TOKEN_TILE
    t_len = seq + ctx_len
    pad = LANES - nope - rope
    wuq = jnp.pad(w_uq.reshape(q_lora, heads, nope + rope), ((0, 0), (0, 0), (0, pad)))
    wuq = wuq.reshape(q_lora, heads * LANES).astype(BF16)
    wkv = w_ukv.reshape(kv_lora, heads, nope + v_dim)
    wuk = jnp.pad(wkv[..., :nope], ((0, 0), (0, 0), (0, LANES - nope))).reshape(kv_lora, heads * LANES).astype(BF16)
    wuv = wkv[..., nope:].reshape(kv_lora, heads * v_dim).astype(BF16)
    body = functools.partial(_mla_prep_body, heads=heads, q_lora=q_lora, kv_lora=kv_lora,
                             qscale=(nope + rope) ** -0.5 * LOG2E)
    head_major = pl.BlockSpec((1, heads, TOKEN_TILE, LANES), lambda b, j: (b, 0, j, 0))
    return pl.pallas_call(
        body,
        out_shape=[jax.ShapeDtypeStruct((batch, heads, t_len, LANES), BF16),
                   jax.ShapeDtypeStruct((batch, heads, t_len, LANES), BF16),
                   jax.ShapeDtypeStruct((batch, heads * v_dim, t_len), BF16)],
        grid=(batch, nj + 1),
        in_specs=[pl.BlockSpec((TOKEN_TILE, pa.shape[1]), lambda b, j: (_tok_block(b, j, nj, batch), 0)),
                  pl.BlockSpec((TOKEN_TILE, LANES), lambda b, j: (j, 0)),
                  pl.BlockSpec((TOKEN_TILE, LANES), lambda b, j: (j, 0)),
                  _full((1, q_lora)), _full((1, kv_lora)), _full(wuq.shape), _full(wuk.shape), _full(wuv.shape)],
        out_specs=[head_major, head_major,
                   pl.BlockSpec((1, heads * v_dim, TOKEN_TILE), lambda b, j: (b, 0, j))],
        compiler_params=_params(2), name="mla_prep",
    )(pa, cos, sin, g_q.reshape(1, -1), g_kv.reshape(1, -1), wuq, wuk, wuv)


def _gqa_prep_body(pd_ref, cos_ref, sin_ref, gq_ref, gk_ref, bd_ref, q_ref, k_ref, *, q_width, qscale):
    pd = pd_ref[...]
    cos, sin = cos_ref[...], sin_ref[...]
    bd = bd_ref[...]
    lane = lax.broadcasted_iota(jnp.int32, (TOKEN_TILE, LANES), 1)
    upper = lane >= LANES // 2

    def head_norm(v, g_ref):
        ms = jnp.dot(v * v, bd, preferred_element_type=F32, precision=lax.Precision.HIGHEST)
        return v * lax.rsqrt(ms + EPS) * g_ref[...]

    n_blocks = q_width // LANES
    for i in range(n_blocks):
        xq = _rope(head_norm(pd[:, i * LANES:(i + 1) * LANES], gq_ref), cos, sin, 16) * qscale
        t = (2 * i) // n_blocks
        in_half = upper if t == 1 else jnp.logical_not(upper)
        swapped = pltpu.roll(xq, LANES // 2, 1)
        q_ref[0, 2 * i + t] = jnp.where(in_half, xq, 0.0).astype(BF16)
        q_ref[0, 2 * i + 1 - t] = jnp.where(in_half, swapped, 0.0).astype(BF16)
    xk = _rope(head_norm(pd[:, q_width:], gk_ref), cos, sin, 16)
    k_ref[0, 0] = xk.astype(BF16)


def _gqa_prep(pd, cos, sin, g_q, g_k, batch, seq, ctx_len, heads, kv_heads, head_dim):
    assert kv_heads * head_dim == LANES and heads % (2 * kv_heads) == 0
    nj = seq // TOKEN_TILE
    t_len = seq + ctx_len
    q_width = heads * head_dim
    grp = jnp.arange(LANES) // head_dim
    bd = (grp[:, None] == grp[None, :]).astype(F32) / head_dim
    rep = LANES // head_dim
    body = functools.partial(_gqa_prep_body, q_width=q_width, qscale=head_dim ** -0.5 * LOG2E)
    return pl.pallas_call(
        body,
        out_shape=[jax.ShapeDtypeStruct((batch, heads, t_len, LANES), BF16),
                   jax.ShapeDtypeStruct((batch, 1, t_len, LANES), BF16)],
        grid=(batch, nj + 1),
        in_specs=[pl.BlockSpec((TOKEN_TILE, pd.shape[1]), lambda b, j: (_tok_block(b, j, nj, batch), 0)),
                  pl.BlockSpec((TOKEN_TILE, LANES), lambda b, j: (j, 0)),
                  pl.BlockSpec((TOKEN_TILE, LANES), lambda b, j: (j, 0)),
                  _full((1, LANES)), _full((1, LANES)), _full((LANES, LANES))],
        out_specs=[pl.BlockSpec((1, heads, TOKEN_TILE, LANES), lambda b, j: (b, 0, j, 0)),
                   pl.BlockSpec((1, 1, TOKEN_TILE, LANES), lambda b, j: (b, 0, j, 0))],
        compiler_params=_params(2), name="gqa_prep",
    )(pd, cos, sin, jnp.tile(g_q, rep).reshape(1, LANES), jnp.tile(g_k, rep).reshape(1, LANES), bd)


def _score_pass(q, k_ref, k_head, chunks, s_ref):
    m = None
    for (st, sz) in chunks:
        s_t = _dot_nt(k_ref[0, k_head, st:st + sz, :], q)
        s_ref[st:st + sz, :] = s_t
        cm = jnp.max(s_t, axis=0, keepdims=True)
        m = cm if m is None else jnp.maximum(m, cm)
    return m


def _value_pass(s_ref, m, vt_ref, v_rows, chunks):
    l = jnp.zeros((1, TOKEN_TILE), F32)
    acc = jnp.zeros((v_rows.stop - v_rows.start, TOKEN_TILE), F32)
    for (st, sz) in chunks:
        p = jnp.exp2(s_ref[st:st + sz, :] - m)
        l = l + jnp.sum(p, axis=0, keepdims=True)
        acc = acc + _dot(vt_ref[0, v_rows, st:st + sz], p.astype(BF16))
    return acc / l


def _gated_store(outs, g_ref, o_ref):
    o2 = jnp.concatenate(outs, axis=0).T
    o_ref[...] = (o2 * _silu(g_ref[...].astype(F32))).astype(o_ref.dtype)


def _flash_body(q_ref, k_ref, vt_ref, g_ref, o_ref, s_scr, m_scr, *, k_sel, v_off, v_dim, chunks):
    j = pl.program_id(2)

    @pl.when((pl.program_id(0) == 0) & (pl.program_id(1) == 0) & (j == 0))
    def _():
        s_scr[...] = jnp.zeros(s_scr.shape, F32)
        m_scr[...] = jnp.zeros(m_scr.shape, F32)

    def step(slot, prev):
        heads = (0, 1)
        q = [q_ref[0, a] for a in heads]
        m_prev = [m_scr[prev, a] for a in heads]
        m = [None, None]
        l = [jnp.zeros((1, TOKEN_TILE), F32) for _ in heads]
        acc = [jnp.zeros((v_dim, TOKEN_TILE), F32) for _ in heads]
        for (st, sz) in chunks:
            for a in heads:
                s_t = _dot_nt(k_ref[0, k_sel[a], st:st + sz, :], q[a])
                s_scr[slot, a, st:st + sz, :] = s_t
                cm = jnp.max(s_t, axis=0, keepdims=True)
                m[a] = cm if m[a] is None else jnp.maximum(m[a], cm)
                p = jnp.exp2(s_scr[prev, a, st:st + sz, :] - m_prev[a])
                l[a] = l[a] + jnp.sum(p, axis=0, keepdims=True)
                acc[a] = acc[a] + _dot(vt_ref[0, v_off[a]:v_off[a] + v_dim, st:st + sz], p.astype(BF16))
        for a in heads:
            m_scr[slot, a] = m[a]
        _gated_store([acc[a] / l[a] for a in heads], g_ref, o_ref)

    for parity in (0, 1):
        pl.when(j % 2 == parity)(functools.partial(step, parity, 1 - parity))


def _flash_ctx_body(q_ref, k_ref, vt_ref, g_ref, o_ref, s_scr, *, k_sel, v_off, v_dim, chunks):
    outs = []
    for a in range(2):
        m = _score_pass(q_ref[0, a], k_ref, k_sel[a], chunks, s_scr.at[a])
        outs.append(_value_pass(s_scr.at[a], m, vt_ref, slice(v_off[a], v_off[a] + v_dim), chunks))
    _gated_store(outs, g_ref, o_ref)


def _flash(q, k, vt, gate, batch, seq, ctx_len, k_heads_per_pair, v_rows_per_pair, pairs_per_kv, ctx_queries):
    heads = q.shape[1]
    t_len = seq + ctx_len
    nj = seq // TOKEN_TILE
    v_dim = LANES // 2
    k_sel = (0, 1) if k_heads_per_pair == 2 else (0, 0)
    v_off = (0, v_dim) if v_rows_per_pair == 2 * v_dim else (0, 0)
    k_blocks = k.shape[1] // k_heads_per_pair
    k_block = lambda p: (p // pairs_per_kv) % k_blocks
    chunks = tuple((c * KEY_CHUNK, KEY_CHUNK) for c in range(seq // KEY_CHUNK)) + ((seq, ctx_len),)
    static = dict(k_sel=k_sel, v_off=v_off, v_dim=v_dim)

    width = gate.shape[1]
    done = lambda b, p, j: (b * nj + jnp.maximum(j - 1, 0), p)
    out = pl.pallas_call(
        functools.partial(_flash_body, chunks=chunks, **static),
        out_shape=jax.ShapeDtypeStruct((batch * seq, width), BF16),
        grid=(batch, heads // 2, nj + 1),
        in_specs=[pl.BlockSpec((1, 2, TOKEN_TILE, LANES), lambda b, p, j: (b, p, jnp.minimum(j, nj - 1), 0)),
                  pl.BlockSpec((1, k_heads_per_pair, t_len, LANES), lambda b, p, j: (b, k_block(p), 0, 0)),
                  pl.BlockSpec((1, v_rows_per_pair, t_len), lambda b, p, j: (b, p // pairs_per_kv, 0)),
                  pl.BlockSpec((TOKEN_TILE, LANES), done)],
        out_specs=pl.BlockSpec((TOKEN_TILE, LANES), done),
        scratch_shapes=[pltpu.VMEM((2, 2, t_len, TOKEN_TILE), F32), pltpu.VMEM((2, 2, 1, TOKEN_TILE), F32)],
        compiler_params=_params(3), name="flash_attention",
    )(q, k, vt, gate)
    if not ctx_queries:
        return (out,)

    out_ctx = pl.pallas_call(
        functools.partial(_flash_ctx_body, chunks=((0, ctx_len),), **static),
        out_shape=jax.ShapeDtypeStruct((batch * ctx_len, width), BF16),
        grid=(batch, heads // 2),
        in_specs=[pl.BlockSpec((1, 2, TOKEN_TILE, LANES), lambda b, p: (b, p, nj, 0)),
                  pl.BlockSpec((1, k_heads_per_pair, ctx_len, LANES), lambda b, p: (b, k_block(p), seq // ctx_len, 0)),
                  pl.BlockSpec((1, v_rows_per_pair, ctx_len), lambda b, p: (b, p // pairs_per_kv, seq // ctx_len)),
                  pl.BlockSpec((TOKEN_TILE, LANES), lambda b, p: (batch * nj + b, p))],
        out_specs=pl.BlockSpec((TOKEN_TILE, LANES), lambda b, p: (b, p)),
        scratch_shapes=[pltpu.VMEM((2, ctx_len, TOKEN_TILE), F32)],
        compiler_params=_params(2), name="flash_attention_ctx",
    )(q, k, vt, gate)
    return out, out_ctx


NA_KEY_ROWS = 12
NA_STEP_ROWS = TOKEN_TILE // GRID_W
NA_VARIANTS = 3


def _na_rel_row(variant, rr, a):
    if variant == 0:
        valid, dr = a < WIN_R, a - rr
    elif variant == 1:
        dr = a - WIN_R // 2 - rr
        valid = -(WIN_R // 2) <= dr < WIN_R // 2
    else:
        valid, dr = a >= NA_KEY_ROWS - WIN_R, a - (NA_KEY_ROWS - NA_STEP_ROWS) - rr
    return dr if valid else None


def _na_bias_body(rpb_ref, o_ref, blk_scr):
    p = pl.program_id(0)
    shape = (GRID_W, LANES)
    kc = lax.broadcasted_iota(jnp.int32, shape, 0)
    lane = lax.broadcasted_iota(jnp.int32, shape, 1)
    qc = lane % GRID_W
    upper = lane >= GRID_W
    rel = kc - qc + (WIN_C - 1)
    c0 = jnp.clip(qc - WIN_C // 2, 0, GRID_W - WIN_C)
    col_ok = (kc >= c0) & (kc < c0 + WIN_C)
    n_rel_r, n_rel_c = 2 * WIN_R - 1, 2 * WIN_C - 1

    def block(dd, carry):
        base0 = (2 * p) * (n_rel_r * n_rel_c) + dd * n_rel_c
        base1 = base0 + n_rel_r * n_rel_c
        val = jnp.zeros(shape, F32)
        for jj in range(n_rel_c):
            val = jnp.where(rel == jj, jnp.where(upper, rpb_ref[base1 + jj], rpb_ref[base0 + jj]), val)
        blk_scr[dd] = jnp.where(col_ok, val * LOG2E, NEG)
        return carry

    lax.fori_loop(0, n_rel_r, block, 0)
    outside = jnp.full(shape, NEG, F32)
    for variant in range(NA_VARIANTS):
        for rr in range(NA_STEP_ROWS):
            for a in range(NA_KEY_ROWS):
                dr = _na_rel_row(variant, rr, a)
                o_ref[0, variant, rr, a * GRID_W:(a + 1) * GRID_W, :] = (
                    outside if dr is None else blk_scr[dr + WIN_R - 1])


def _na_bias(rpb):
    heads = rpb.shape[0]
    tab = (NA_VARIANTS, NA_STEP_ROWS, NA_KEY_ROWS * GRID_W, LANES)
    return pl.pallas_call(
        _na_bias_body,
        out_shape=jax.ShapeDtypeStruct((heads // 2,) + tab, F32),
        grid=(heads // 2,),
        in_specs=[pl.BlockSpec(memory_space=pltpu.SMEM)],
        out_specs=pl.BlockSpec((1,) + tab, lambda p: (p, 0, 0, 0, 0)),
        scratch_shapes=[pltpu.VMEM((2 * WIN_R - 1, GRID_W, LANES), F32)],
        compiler_params=_params(1), name="na_bias",
    )(rpb.reshape(-1))


def _na_body(q_ref, kl_ref, kc_ref, vt_ref, bias_ref, g_ref, o_ref, sl_scr, sc_scr, m_scr, *, seq, ctx_len, n_steps):
    j = pl.program_id(2)
    n_rows = seq // GRID_W
    span = NA_KEY_ROWS * GRID_W
    lower = lax.broadcasted_iota(jnp.int32, (GRID_W, LANES), 1) < GRID_W

    @pl.when((pl.program_id(0) == 0) & (pl.program_id(1) == 0) & (j == 0))
    def _():
        sl_scr[...] = jnp.zeros(sl_scr.shape, F32)
        sc_scr[...] = jnp.zeros(sc_scr.shape, F32)
        m_scr[...] = jnp.zeros(m_scr.shape, F32)

    def span_start(step):
        first_row = jnp.clip(NA_STEP_ROWS * step - WIN_R // 2, 0, n_rows - NA_KEY_ROWS)
        return pl.multiple_of(first_row * GRID_W, TOKEN_TILE)

    def step(slot, prev):
        js = jnp.minimum(j, n_steps - 1)
        variant = jnp.where(js == 0, 0, jnp.where(js == n_steps - 1, 2, 1))
        q = q_ref[...]
        zero = jnp.zeros((GRID_W, LANES), q.dtype)
        parts = []
        for rr in range(NA_STEP_ROWS):
            q_r = q[rr * GRID_W:(rr + 1) * GRID_W]
            parts += [jnp.where(lower, q_r, zero), jnp.where(lower, zero, q_r)]
        q2 = jnp.concatenate(parts, axis=0)
        s_loc = _dot_nt(kl_ref[pl.ds(span_start(js), span), :], q2)
        s_loc = jnp.concatenate([s_loc[:, rr * LANES:(rr + 1) * LANES] + bias_ref[0, variant, rr]
                                 for rr in range(NA_STEP_ROWS)], axis=1)
        s_ctx = _dot_nt(kc_ref[...], q2)
        sl_scr[slot] = s_loc
        sc_scr[slot] = s_ctx
        m_scr[slot] = jnp.maximum(jnp.max(s_loc, axis=0, keepdims=True), jnp.max(s_ctx, axis=0, keepdims=True))

        m = m_scr[prev]
        p_loc = jnp.exp2(sl_scr[prev] - m)
        p_ctx = jnp.exp2(sc_scr[prev] - m)
        l = jnp.sum(p_loc, axis=0, keepdims=True) + jnp.sum(p_ctx, axis=0, keepdims=True)
        v_loc = vt_ref[0, :, pl.ds(span_start(jnp.maximum(j - 1, 0)), span)]
        r_t = _dot(v_loc, p_loc.astype(BF16)) + _dot(vt_ref[0, :, seq:seq + ctx_len], p_ctx.astype(BF16))
        r_n = (r_t / l).T
        outs = [jnp.where(lower, r_n[rr * LANES:rr * LANES + GRID_W], r_n[rr * LANES + GRID_W:(rr + 1) * LANES])
                for rr in range(NA_STEP_ROWS)]
        o = jnp.concatenate(outs, axis=0)
        o_ref[...] = (o * _silu(g_ref[...].astype(F32))).astype(o_ref.dtype)

    for parity in (0, 1):
        pl.when(j % 2 == parity)(functools.partial(step, parity, 1 - parity))


def _na_attention(q, k, vt, bias, gate, batch, seq, ctx_len):
    pairs = q.shape[1] // LANES
    t_len = seq + ctx_len
    n_steps = seq // TOKEN_TILE
    span = NA_KEY_ROWS * GRID_W
    cols = NA_STEP_ROWS * LANES
    done = lambda b, p, j: (b * n_steps + jnp.maximum(j - 1, 0), p)
    body = functools.partial(_na_body, seq=seq, ctx_len=ctx_len, n_steps=n_steps)
    return pl.pallas_call(
        body, out_shape=jax.ShapeDtypeStruct((batch * seq, q.shape[1]), BF16),
        grid=(batch, pairs, n_steps + 1),
        in_specs=[pl.BlockSpec((TOKEN_TILE, LANES), lambda b, p, j: (b * n_steps + jnp.minimum(j, n_steps - 1), p)),
                  pl.BlockSpec((seq, LANES), lambda b, p, j: (b, p)),
                  pl.BlockSpec((ctx_len, LANES), lambda b, p, j: (batch * seq // ctx_len + b, p)),
                  pl.BlockSpec((1, LANES, t_len), lambda b, p, j: (b, p, 0)),
                  pl.BlockSpec((1,) + bias.shape[1:], lambda b, p, j: (p, 0, 0, 0, 0)),
                  pl.BlockSpec((TOKEN_TILE, LANES), done)],
        out_specs=pl.BlockSpec((TOKEN_TILE, LANES), done),
        scratch_shapes=[pltpu.VMEM((2, span, cols), F32), pltpu.VMEM((2, ctx_len, cols), F32),
                        pltpu.VMEM((2, 1, cols), F32)],
        compiler_params=_params(3), name="neighborhood_attention",
    )(q, k, k, vt, bias, gate)


def _seg_scans(jobs, use_max=False):
    vals = [v for v, _ in jobs]
    n = vals[0].shape[1]
    lane = lax.broadcasted_iota(jnp.int32, vals[0].shape, 1) % ML_CHUNK
    k = 1
    while k < ML_CHUNK:
        for i, (_, reverse) in enumerate(jobs):
            v = vals[i]
            if reverse:
                ok, shifted = lane < ML_CHUNK - k, pltpu.roll(v, n - k, 1)
            else:
                ok, shifted = lane >= k, pltpu.roll(v, k, 1)
            vals[i] = jnp.maximum(v, jnp.where(ok, shifted, NEG)) if use_max else v + jnp.where(ok, shifted, 0.0)
        k *= 2
    return vals


ML_GATE_ROWS = 40


def _mlstm_prep_body(u_ref, up_ref, un_ref, cw_ref, cb_ref, wqk_ref, wv_ref, wg_ref, bg_ref,
                     xc_ref, k_ref, qt_ref, vt_ref, pre_ref, *, nj, width, kscale):
    j = pl.program_id(1)
    u = u_ref[...]
    row = lax.broadcasted_iota(jnp.int32, u.shape, 0)
    prev = jnp.where((j > 0) & (j < nj), up_ref[7:8, :], 0.0)
    nxt = jnp.where(j < nj - 1, un_ref[0:1, :], 0.0)
    u_m1 = jnp.where(row == 0, prev, pltpu.roll(u, 1, 0))
    u_p1 = jnp.where(row == TOKEN_TILE - 1, nxt, pltpu.roll(u, TOKEN_TILE - 1, 0))
    cw = cw_ref[...]
    xc = _silu(u_m1 * cw[0:1] + u * cw[1:2] + u_p1 * cw[2:3] + cb_ref[...])
    xcb = xc.astype(BF16)
    xc_ref[...] = xcb
    qk = _dot(xcb, wqk_ref[...])
    v = _dot(u.astype(BF16), wv_ref[...])
    qb, kb, vb = qk[:, :width].astype(BF16), qk[:, width:].astype(BF16), v.astype(BF16)
    k_ref[...] = (qk[:, width:] * kscale).astype(BF16)
    qt_ref[0] = qk[:, :width].T.astype(BF16)
    vt_ref[0] = v.T.astype(BF16)
    pre_ref[0] = _dot_nt(wg_ref[0], qb) + _dot_nt(wg_ref[1], kb) + _dot_nt(wg_ref[2], vb) + bg_ref[...]


def _mlstm_gates_body(pre_ref, g_ref, gc_ref):
    pre = pre_ref[0]
    i8 = [pre[16 * d:16 * d + 8] for d in range(2)]
    f8 = [_log_sigmoid(pre[16 * d + 8:16 * d + 16]) for d in range(2)]
    b0, b0_rev, b1, b1_rev = _seg_scans([(f8[0], False), (f8[0], True), (f8[1], True), (f8[1], False)])
    b8, b_last = [b0, b1], [b0 + b0_rev - f8[0], b1 + b1_rev - f8[1]]
    r8 = [i8[d] - b8[d] for d in range(2)]
    c0, c0_rev, c1, c1_rev = _seg_scans([(r8[0], False), (r8[0], True), (r8[1], True), (r8[1], False)], use_max=True)
    c8, r_max = [c0, c1], [jnp.maximum(c0, c0_rev), jnp.maximum(c1, c1_rev)]
    for d in range(2):
        g_ref[0, d * ML_GATE_ROWS:(d + 1) * ML_GATE_ROWS] = jnp.concatenate(
            [-c8[d], b8[d] + c8[d], jnp.exp(r8[d] - r_max[d]), b_last[d], b_last[d] + r_max[d]], axis=0)
    pad = jnp.zeros((LANES - 16, pre.shape[1]), F32)
    gc_ref[0] = jnp.concatenate(r8 + [pad], axis=0).T


def _mlstm_gates(pre):
    batch, _, t_len = pre.shape
    return pl.pallas_call(
        _mlstm_gates_body,
        out_shape=[jax.ShapeDtypeStruct((batch, 2 * ML_GATE_ROWS, t_len), F32),
                   jax.ShapeDtypeStruct((batch, t_len, LANES), F32)],
        grid=(batch,),
        in_specs=[pl.BlockSpec((1,) + pre.shape[1:], lambda b: (b, 0, 0))],
        out_specs=[pl.BlockSpec((1, 2 * ML_GATE_ROWS, t_len), lambda b: (b, 0, 0)),
                   pl.BlockSpec((1, t_len, LANES), lambda b: (b, 0, 0))],
        compiler_params=_params(1), name="mlstm_gates",
    )(pre)


def _mlstm_prep(u, conv_w, conv_b, w_q, w_k, w_v, w_gate, b_gate, batch, seq, ctx_len):
    heads, hd = w_q.shape[0], w_q.shape[1]
    assert heads == ML_HEADS and hd == ML_CHUNK
    width = heads * hd
    nj = seq // TOKEN_TILE
    t_len = seq + ctx_len
    rows = u.shape[0]

    def block_diag(w):
        eye = jnp.eye(heads, dtype=w.dtype)
        return (eye[:, None, :, None] * w[:, :, None, :]).reshape(width, width)

    wqk = jnp.concatenate([block_diag(w_q), block_diag(w_k)], axis=1).astype(BF16)
    wv = block_diag(w_v).astype(BF16)
    wg = w_gate.reshape(2, heads, 3, hd, 2, heads).transpose(2, 0, 4, 5, 1, 3).reshape(3, 2, 2, heads, width)
    wg = jnp.pad(wg, ((0, 0), (0, 0), (0, 0), (0, 8 - heads), (0, 0))).reshape(3, 32, width).astype(BF16)
    bg = jnp.pad(b_gate.reshape(2, 2, heads), ((0, 0), (0, 0), (0, 8 - heads))).reshape(32, 1)
    n_halo = rows // 8
    tokb = lambda b, j: _tok_block(b, j, nj, batch)
    tile = pl.BlockSpec((TOKEN_TILE, width), lambda b, j: (tokb(b, j), 0))
    feat = pl.BlockSpec((1, width, TOKEN_TILE), lambda b, j: (b, 0, j))
    body = functools.partial(_mlstm_prep_body, nj=nj, width=width, kscale=hd ** -0.5)
    per_tile = TOKEN_TILE // 8
    return pl.pallas_call(
        body,
        out_shape=[jax.ShapeDtypeStruct((rows, width), BF16)] * 2
        + [jax.ShapeDtypeStruct((batch, width, t_len), BF16)] * 2
        + [jax.ShapeDtypeStruct((batch, 32, t_len), F32)],
        grid=(batch, nj + 1),
        in_specs=[tile,
                  pl.BlockSpec((8, width), lambda b, j: (jnp.maximum(tokb(b, j) * per_tile - 1, 0), 0)),
                  pl.BlockSpec((8, width), lambda b, j: (jnp.minimum((tokb(b, j) + 1) * per_tile, n_halo - 1), 0)),
                  _full((3, width)), _full((1, width)), _full(wqk.shape), _full(wv.shape),
                  _full(wg.shape), _full((32, 1))],
        out_specs=[tile, tile, feat, feat, pl.BlockSpec((1, 32, TOKEN_TILE), lambda b, j: (b, 0, j))],
        compiler_params=_params(2), name="mlstm_prep",
    )(u, u, u, conv_w, conv_b.reshape(1, width), wqk, wv, wg, bg)


def _mlstm_seq_body(kf, qtf, vtf, gf, gcf, kb, qtb, vtb, gb, gcb, hf_ref, hb_ref, c_s, n_s, m_s):
    t = pl.program_id(1)
    L = ML_CHUNK

    @pl.when(t == 0)
    def _():
        c_s[...] = jnp.zeros(c_s.shape, F32)
        n_s[...] = jnp.zeros(n_s.shape, F32)
        m_s[...] = jnp.zeros(m_s.shape, F32)

    si = lax.broadcasted_iota(jnp.int32, (L, L), 0)
    li = lax.broadcasted_iota(jnp.int32, (L, L), 1)
    streams = ((kf, qtf, vtf, gf, gcf, hf_ref, si <= li), (kb, qtb, vtb, gb, gcb, hb_ref, si >= li))
    units = []
    for d, (k_ref, qt_ref, vt_ref, g_ref, gc_ref, h_ref, incl) in enumerate(streams):
        g = g_ref[0]
        gc = gc_ref[0]
        for h in range(ML_HEADS):
            cols = slice(h * L, (h + 1) * L)
            k, qt, vt = k_ref[:, cols], qt_ref[0, cols, :], vt_ref[0, cols, :]
            neg_c, m_loc, w0, b_last, g_max = (g[8 * i + h:8 * i + h + 1] for i in range(5))
            r_col = gc[:, 8 * d + h:8 * d + h + 1]
            p0 = jnp.where(incl, jnp.exp(r_col + neg_c), 0.0) * _dot(k, qt)
            s_sum = jnp.sum(p0, axis=0, keepdims=True)
            intra = _dot(vt, p0.astype(BF16))
            c_inc = _dot((vt.astype(F32) * w0).astype(BF16), k)
            n_inc = _dot(jnp.broadcast_to(w0, (8, L)).astype(BF16), k)
            units.append((d * ML_HEADS + h, h_ref, cols, qt, neg_c, m_loc, b_last, g_max, s_sum, intra, c_inc, n_inc))
    for (idx, h_ref, cols, qt, neg_c, m_loc, b_last, g_max, s_sum, intra, c_inc, n_inc) in units:
        c_st, n_st, m_st = c_s[idx], n_s[idx], m_s[idx]
        cn = _dot(jnp.concatenate([c_st, n_st], axis=0).astype(BF16), qt)
        delta = jnp.maximum(m_st + neg_c, 0.0)
        e_intra = jnp.exp(-delta)
        w_inter = jnp.exp(m_st + neg_c - delta)
        num = w_inter * cn[:L] + e_intra * intra
        den = w_inter * cn[L:L + 1] + e_intra * s_sum
        h_ref[0, cols, :] = num / jnp.maximum(jnp.abs(den), jnp.exp(-(m_loc + delta)))
        m_new = jnp.maximum(b_last + m_st, g_max)
        decay = jnp.exp(b_last + m_st - m_new)
        gain = jnp.exp(g_max - m_new)
        c_s[idx] = decay * c_st + gain * c_inc
        n_s[idx] = decay * n_st + gain * n_inc
        m_s[idx] = m_new


def _mlstm_seq(k, qt, vt, gates, gcols, batch, seq, ctx_len):
    width = k.shape[1]
    L = ML_CHUNK
    n_lat, n_ctx = seq // L, ctx_len // L
    n_chunks = n_lat + n_ctx
    fwd = lambda t: (t + n_lat) % n_chunks
    bwd = lambda t: n_chunks - 1 - t
    rowblk = lambda b, c: jnp.where(c < n_lat, b * n_lat + c, batch * n_lat + b * n_ctx + (c - n_lat))

    def stream(chunk_of, d):
        feat = pl.BlockSpec((1, width, L), lambda b, t: (b, 0, chunk_of(t)))
        return [pl.BlockSpec((L, width), lambda b, t: (rowblk(b, chunk_of(t)), 0)), feat, feat,
                pl.BlockSpec((1, ML_GATE_ROWS, L), lambda b, t: (b, d, chunk_of(t))),
                pl.BlockSpec((1, L, LANES), lambda b, t: (b, chunk_of(t), 0))]

    out_f = pl.BlockSpec((1, width, L), lambda b, t: (b, 0, fwd(t)))
    out_b = pl.BlockSpec((1, width, L), lambda b, t: (b, 0, bwd(t)))
    n_state = 2 * ML_HEADS
    return pl.pallas_call(
        _mlstm_seq_body,
        out_shape=[jax.ShapeDtypeStruct(qt.shape, F32)] * 2,
        grid=(batch, n_chunks),
        in_specs=stream(fwd, 0) + stream(bwd, 1),
        out_specs=[out_f, out_b],
        scratch_shapes=[pltpu.VMEM((n_state, L, L), F32), pltpu.VMEM((n_state, 8, L), F32),
                        pltpu.VMEM((n_state, 1, L), F32)],
        compiler_params=_params(2), name="mlstm_recurrence",
    )(k, qt, vt, gates, gcols, k, qt, vt, gates, gcols)


def _mlstm_out_body(hf_ref, hb_ref, xc_ref, z_ref, gh_ref, sk_ref, o_ref):
    ht = hf_ref[0] + hb_ref[0]
    L = ML_CHUNK
    normed = []
    for hd in range(ML_HEADS):
        hh = ht[hd * L:(hd + 1) * L]
        mu = jnp.mean(hh, axis=0, keepdims=True)
        var = jnp.mean(jnp.square(hh - mu), axis=0, keepdims=True)
        normed.append((hh - mu) * lax.rsqrt(var + EPS))
    hn = jnp.concatenate(normed, axis=0).T * gh_ref[...]
    o = (hn + sk_ref[...] * xc_ref[...].astype(F32)) * _silu(z_ref[...].astype(F32))
    o_ref[...] = o.astype(o_ref.dtype)


def _mlstm_out(hf, hb, xc, z, g_head, skip, batch, seq):
    rows, width = xc.shape
    nj = seq // TOKEN_TILE
    tile = pl.BlockSpec((TOKEN_TILE, width), lambda b, j: (_tok_block(b, j, nj, batch), 0))
    feat = pl.BlockSpec((1, width, TOKEN_TILE), lambda b, j: (b, 0, j))
    return pl.pallas_call(
        _mlstm_out_body, out_shape=jax.ShapeDtypeStruct((rows, width), BF16), grid=(batch, nj + 1),
        in_specs=[feat, feat, tile, tile, _full((1, width)), _full((1, width))],
        out_specs=tile, compiler_params=_params(2), name="mlstm_out",
    )(hf, hb, xc, z, g_head.reshape(1, width), skip.reshape(1, width))


def kernel(x, c, ctx, c_ctx, w_mod, b_mod, g_norm, ab_w_in, ab_w_out, mla_g_q, mla_w_uq, mla_g_kv, mla_w_ukv,
           ml_conv_w, ml_conv_b, ml_w_q, ml_w_k, ml_w_v, ml_w_gate, ml_b_gate, ml_g_head, ml_skip,
           cd_w_in, cd_w_out, na_rpb, gqa_g_q, gqa_g_k, g_final):
    batch, seq, d = x.shape
    ctx_len = ctx.shape[1]
    assert ctx_len == TOKEN_TILE and seq % KEY_CHUNK == 0 and seq // GRID_W >= NA_KEY_ROWS
    dims = (batch, seq, ctx_len)

    mla_heads, mla_rope, mla_v = 8, 32, 64
    mla_nope = mla_w_uq.shape[2] // mla_heads - mla_rope
    q_lora, kv_lora = mla_g_q.shape[1], mla_g_kv.shape[1]
    ml_width = ml_conv_w.shape[2]
    mla_width = mla_heads * mla_v
    gqa_heads, gqa_dim = 8, gqa_g_q.shape[1]
    gqa_kv = (cd_w_in.shape[2] - 4 * 512 - 2 * gqa_heads * gqa_dim) // (2 * gqa_dim)
    na_width = na_rpb.shape[1] * 64

    mod_rows = -(-(batch + 1) // 8) * 8
    cvec = jnp.concatenate([c, c_ctx[None], jnp.zeros((mod_rows - batch - 1, d), F32)], axis=0)
    mod = _modulation(cvec, w_mod, b_mod)
    mod0 = mod[0].reshape(mod_rows, 1, 3 * d)
    mod1 = mod[1].reshape(mod_rows, 1, 3 * d)

    tok0 = _Tokens((x.reshape(batch * seq, d), ctx.reshape(batch * ctx_len, d)), *dims)
    w_in = ab_w_in[0]
    s1 = q_lora + kv_lora
    zcol = lambda n: jnp.zeros((d, n), w_in.dtype)
    w0 = jnp.concatenate([w_in[:, :s1], zcol(mla_nope), w_in[:, s1:s1 + mla_rope],
                          zcol(LANES - mla_nope - mla_rope), w_in[:, s1 + mla_rope:]], axis=1).astype(BF16)
    o_pa = s1 + LANES
    outs0 = ((0, o_pa, F32, 1.0, False), (o_pa, mla_width, BF16, 1.0, False),
             (o_pa + mla_width, ml_width, F32, 1.0, False), (o_pa + mla_width + ml_width, ml_width, BF16, 1.0, False))
    pa, gate_a, u, z = _in_proj(tok0, mod0, g_norm[0], w0, outs0, *dims)

    def mla_lanes(row, fill):
        n = row.shape[0]
        return jnp.concatenate([jnp.full((n, mla_nope), fill, F32), row,
                                jnp.full((n, LANES - mla_nope - mla_rope), fill, F32)], axis=-1)

    cos_a, sin_a = _rope_tables(seq, ctx_len, mla_rope, mla_lanes)
    q_a, k_a, vt_a = _mla_prep(pa, cos_a, sin_a, mla_g_q[0], mla_g_kv[0], mla_w_uq[0], mla_w_ukv[0],
                               *dims, mla_heads, mla_nope, mla_rope, mla_v)
    mix_a = _flash(q_a, k_a, vt_a, gate_a, *dims, k_heads_per_pair=2, v_rows_per_pair=2 * mla_v,
                   pairs_per_kv=1, ctx_queries=True)

    xc, k_m, qt_m, vt_m, gate_pre = _mlstm_prep(u, ml_conv_w[0], ml_conv_b[0], ml_w_q[0], ml_w_k[0], ml_w_v[0],
                                                ml_w_gate[0], ml_b_gate[0], *dims)
    gates, gcols = _mlstm_gates(gate_pre)
    h_f, h_b = _mlstm_seq(k_m, qt_m, vt_m, gates, gcols, *dims)
    mix_b = _mlstm_out(h_f, h_b, xc, z, ml_g_head[0], ml_skip[0], batch, seq)
    x1 = _out_proj(tok0, _Tokens(mix_a, *dims), mix_b, ab_w_out[0], mod0, batch, seq)

    tok1 = _Tokens((x1,), *dims)
    w1 = cd_w_in[0].astype(BF16)
    gq_w, gkv_w = gqa_heads * gqa_dim, gqa_kv * gqa_dim
    o_d = 4 * na_width
    outs1 = ((0, na_width, BF16, 64 ** -0.5 * LOG2E, False), (na_width, na_width, BF16, 1.0, False),
             (2 * na_width, na_width, BF16, 1.0, True), (3 * na_width, na_width, BF16, 1.0, False),
             (o_d, gq_w + gkv_w, F32, 1.0, False), (o_d + gq_w + gkv_w, gkv_w, BF16, 1.0, True),
             (o_d + gq_w + 2 * gkv_w, gq_w, BF16, 1.0, False))
    q_c, k_c, vt_c, gate_c, pd, vt_d, gate_d = _in_proj(tok1, mod1, g_norm[1], w1, outs1, *dims)

    mix_c = _na_attention(q_c, k_c, vt_c, _na_bias(na_rpb[0]), gate_c, *dims)

    gqa_lanes = lambda row, fill: jnp.tile(row, (1, LANES // gqa_dim))
    cos_d, sin_d = _rope_tables(seq, ctx_len, gqa_dim, gqa_lanes)
    q_d, k_d = _gqa_prep(pd, cos_d, sin_d, gqa_g_q[0], gqa_g_k[0], *dims, gqa_heads, gqa_kv, gqa_dim)
    mix_d = _flash(q_d, k_d, vt_d, gate_d, *dims, k_heads_per_pair=1, v_rows_per_pair=gqa_dim,
                   pairs_per_kv=gqa_heads // (2 * gqa_kv), ctx_queries=False)

    out = _out_proj(tok1, _Tokens((mix_c,), *dims), mix_d[0], cd_w_out[0], mod1, batch, seq, g_final=g_final)
    return out.reshape(batch, seq, d)
```

```python
import functools

import jax
import jax.numpy as jnp
from jax import lax
from jax.experimental import pallas as pl
from jax.experimental.pallas import tpu as pltpu

F32 = jnp.float32
BF16 = jnp.bfloat16

LANES = 128
TOKEN_TILE = 256
KEY_CHUNK = 512
GRID_W = 64
WIN_R = 8
WIN_C = 16
ML_CHUNK = 128
ML_HEADS = 4
EPS = 1e-6
ROPE_BASE = 10000.0
LOG2E = 1.4426950408889634
NEG = -1e30
VMEM_LIMIT = 56 * 1024 * 1024

_NT = (((1,), (1,)), ((), ()))


def _dot(a, b):
    return jnp.dot(a, b, preferred_element_type=F32)


def _dot_nt(a, b):
    return lax.dot_general(a, b, _NT, preferred_element_type=F32)


def _silu(v):
    return v * (1.0 / (1.0 + jnp.exp(-v)))


def _log_sigmoid(v):
    return -(jnp.maximum(-v, 0.0) + jnp.log1p(jnp.exp(-jnp.abs(v))))


def _params(n_axes):
    return pltpu.CompilerParams(dimension_semantics=("arbitrary",) * n_axes, vmem_limit_bytes=VMEM_LIMIT)


def _full(shape):
    nd = len(shape)
    return pl.BlockSpec(shape, lambda *_: (0,) * nd)


def _mod_body(c_ref, w_ref, b_ref, o_ref):
    s = _silu(c_ref[...])
    o_ref[0] = _dot(s.astype(BF16), w_ref[0].astype(BF16)) + b_ref[0]


def _modulation(cvec, w_mod, b_mod):
    depth, d, n = w_mod.shape
    rows = cvec.shape[0]
    tn = n // 4
    return pl.pallas_call(
        _mod_body,
        out_shape=jax.ShapeDtypeStruct((depth, rows, n), F32),
        grid=(depth, n // tn),
        in_specs=[_full((rows, d)),
                  pl.BlockSpec((1, d, tn), lambda l, j: (l, 0, j)),
                  pl.BlockSpec((1, 1, tn), lambda l, j: (l, 0, j))],
        out_specs=pl.BlockSpec((1, rows, tn), lambda l, j: (l, 0, j)),
        compiler_params=_params(2), name="modulation",
    )(cvec, w_mod, b_mod.reshape(depth, 1, n))


class _Tokens:
    def __init__(self, arrays, batch, seq, ctx_len):
        self.arrays = arrays
        self.split = len(arrays) == 2
        self.n_lat = batch * seq // TOKEN_TILE
        self.n_ctx = batch * ctx_len // TOKEN_TILE
        self.d = arrays[0].shape[-1]

    def specs(self):
        blk = (TOKEN_TILE, self.d)
        if not self.split:
            return [pl.BlockSpec(blk, lambda i: (i, 0))]
        n_lat = self.n_lat
        return [pl.BlockSpec(blk, lambda i: (jnp.minimum(i, n_lat - 1), 0)),
                pl.BlockSpec(blk, lambda i: (jnp.maximum(i - n_lat, 0), 0))]

    def load(self, refs, i):
        if not self.split:
            return refs[0][...]
        return jnp.where(i < self.n_lat, refs[0][...], refs[1][...])


def _mod_spec(n_lat, nj, batch, width):
    return pl.BlockSpec((1, 1, width), lambda i: (jnp.where(i < n_lat, i // nj, batch), 0, 0))


def _in_proj_body(*refs, tok, outs, d):
    n_tok = len(tok.arrays)
    mod_ref, g_ref, w_ref = refs[n_tok:n_tok + 3]
    o_refs = refs[n_tok + 3:]
    i = pl.program_id(0)
    x = tok.load(refs[:n_tok], i)
    y = x * lax.rsqrt(jnp.mean(x * x, axis=-1, keepdims=True) + EPS) * g_ref[...]
    mod = mod_ref[0]
    h = y * (1.0 + mod[:, d:2 * d]) + mod[:, :d]
    acc = _dot(h.astype(BF16), w_ref[...])
    for o_ref, (c0, width, _, scale, transposed) in zip(o_refs, outs):
        v = acc[:, c0:c0 + width]
        if scale != 1.0:
            v = v * scale
        if transposed:
            o_ref[0] = v.T.astype(o_ref.dtype)
        else:
            o_ref[...] = v.astype(o_ref.dtype)


def _in_proj(tok, mod_l, g, w, outs, batch, seq, ctx_len):
    d = tok.d
    nj = seq // TOKEN_TILE
    n_lat, n_all = tok.n_lat, tok.n_lat + tok.n_ctx
    rows = n_all * TOKEN_TILE
    t_len = seq + ctx_len
    out_shape, out_specs = [], []
    for (_, width, dtype, _, transposed) in outs:
        if transposed:
            out_shape.append(jax.ShapeDtypeStruct((batch, width, t_len), dtype))
            out_specs.append(pl.BlockSpec(
                (1, width, TOKEN_TILE),
                lambda i: (jnp.where(i < n_lat, i // nj, i - n_lat), 0, jnp.where(i < n_lat, i % nj, nj))))
        else:
            out_shape.append(jax.ShapeDtypeStruct((rows, width), dtype))
            out_specs.append(pl.BlockSpec((TOKEN_TILE, width), lambda i: (i, 0)))
    body = functools.partial(_in_proj_body, tok=tok, outs=outs, d=d)
    return pl.pallas_call(
        body, out_shape=out_shape, grid=(n_all,),
        in_specs=tok.specs() + [_mod_spec(n_lat, nj, batch, 3 * d), _full((1, d)), _full(w.shape)],
        out_specs=out_specs, compiler_params=_params(1), name="in_proj",
    )(*tok.arrays, mod_l, g.reshape(1, d), w)


def _out_proj_body(*refs, tok, mix_a, d, final):
    n_tok, n_a = len(tok.arrays), len(mix_a.arrays)
    mb_ref, wa_ref, wb_ref, mod_ref = refs[n_tok + n_a:n_tok + n_a + 4]
    rest = refs[n_tok + n_a + 4:]
    i = pl.program_id(0)
    x = tok.load(refs[:n_tok], i)
    acc = _dot(mix_a.load(refs[n_tok:n_tok + n_a], i), wa_ref[...]) + _dot(mb_ref[...], wb_ref[...])
    xn = x + mod_ref[0][:, 2 * d:] * acc
    if final:
        gf_ref, o_ref = rest
        xn = xn * lax.rsqrt(jnp.mean(xn * xn, axis=-1, keepdims=True) + EPS) * gf_ref[...]
    else:
        (o_ref,) = rest
    o_ref[...] = xn


def _out_proj(tok, mix_a, mix_b, w_out, mod_l, batch, seq, g_final=None):
    d = tok.d
    half = mix_a.d
    nj = seq // TOKEN_TILE
    n_lat = tok.n_lat
    final = g_final is not None
    n_tiles = n_lat if final else n_lat + tok.n_ctx
    wa, wb = w_out[:half].astype(BF16), w_out[half:].astype(BF16)
    tile = lambda width: pl.BlockSpec((TOKEN_TILE, width), lambda i: (i, 0))
    in_specs = tok.specs() + mix_a.specs() + [tile(half), _full(wa.shape), _full(wb.shape),
                                              _mod_spec(n_lat, nj, batch, 3 * d)]
    args = list(tok.arrays) + list(mix_a.arrays) + [mix_b, wa, wb, mod_l]
    if final:
        in_specs.append(_full((1, d)))
        args.append(g_final.reshape(1, d))
    body = functools.partial(_out_proj_body, tok=tok, mix_a=mix_a, d=d, final=final)
    return pl.pallas_call(
        body, out_shape=jax.ShapeDtypeStruct((n_tiles * TOKEN_TILE, d), F32), grid=(n_tiles,),
        in_specs=in_specs, out_specs=tile(d), compiler_params=_params(1), name="out_proj",
    )(*args)


def _rope_tables(seq, ctx_len, rot_dim, lane_pattern):
    t = jnp.arange(seq)
    pos = jnp.stack([t // GRID_W, t % GRID_W], axis=-1).astype(F32)
    n_freq = rot_dim // 4
    inv = ROPE_BASE ** (-jnp.arange(n_freq, dtype=F32) / n_freq)
    ang = pos[:, :, None] * inv
    cos, sin = jnp.cos(ang), jnp.sin(ang)
    cos_row = jnp.concatenate([cos[:, 0], cos[:, 0], cos[:, 1], cos[:, 1]], axis=-1)
    sin_row = jnp.concatenate([-sin[:, 0], sin[:, 0], -sin[:, 1], sin[:, 1]], axis=-1)
    cos_t, sin_t = lane_pattern(cos_row, 1.0), lane_pattern(sin_row, 0.0)
    pad = lambda a, v: jnp.concatenate([a, jnp.full((ctx_len, LANES), v, F32)], axis=0)
    return pad(cos_t, 1.0), pad(sin_t, 0.0)


def _rope(x, cos, sin, dist):
    lane = lax.broadcasted_iota(jnp.int32, x.shape, 1)
    first = (lane % (2 * dist)) < dist
    partner = jnp.where(first, pltpu.roll(x, LANES - dist, 1), pltpu.roll(x, dist, 1))
    return x * cos + partner * sin


def _tok_block(b, j, nj, batch):
    return jnp.where(j < nj, b * nj + j, batch * nj + b)


def _mla_prep_body(pa_ref, cos_ref, sin_ref, gq_ref, gkv_ref, wuq_ref, wuk_ref, wuv_ref,
                   q_ref, k_ref, vt_ref, *, heads, q_lora, kv_lora, qscale):
    pa = pa_ref[...]
    cos, sin = cos_ref[...], sin_ref[...]

    def norm(v, g_ref):
        return (v * lax.rsqrt(jnp.mean(v * v, axis=-1, keepdims=True) + EPS) * g_ref[...]).astype(BF16)

    cq = norm(pa[:, :q_lora], gq_ref)
    q_all = _dot(cq, wuq_ref[...])
    for h in range(heads):
        qh = _rope(q_all[:, h * LANES:(h + 1) * LANES], cos, sin, 8)
        q_ref[0, h] = (qh * qscale).T.astype(BF16)
    ckv = norm(pa[:, q_lora:q_lora + kv_lora], gkv_ref)
    k_nope = _dot(ckv, wuk_ref[...])
    k_rope = _rope(pa[:, q_lora + kv_lora:], cos, sin, 8)
    for h in range(heads):
        k_ref[0, h] = (k_nope[:, h * LANES:(h + 1) * LANES] + k_rope).astype(BF16)
    vt_ref[0] = _dot(ckv, wuv_ref[...]).T.astype(BF16)


def _mla_prep(pa, cos, sin, g_q, g_kv, w_uq, w_ukv, batch, seq, ctx_len, heads, nope, rope, v_dim):
    q_lora, kv_lora = g_q.shape[0], g_kv.shape[0]
    nj = seq // TOKEN_TILE
    t_len = seq + ctx_len
    pad = LANES - nope - rope
    wuq = jnp.pad(w_uq.reshape(q_lora, heads, nope + rope), ((0, 0), (0, 0), (0, pad)))
    wuq = wuq.reshape(q_lora, heads * LANES).astype(BF16)
    wkv = w_ukv.reshape(kv_lora, heads, nope + v_dim)
    wuk = jnp.pad(wkv[..., :nope], ((0, 0), (0, 0), (0, LANES - nope))).reshape(kv_lora, heads * LANES).astype(BF16)
    wuv = wkv[..., nope:].reshape(kv_lora, heads * v_dim).astype(BF16)
    body = functools.partial(_mla_prep_body, heads=heads, q_lora=q_lora, kv_lora=kv_lora,
                             qscale=(nope + rope) ** -0.5 * LOG2E)
    head_major = pl.BlockSpec((1, heads, TOKEN_TILE, LANES), lambda b, j: (b, 0, j, 0))
    q_feature_major = pl.BlockSpec((1, heads, LANES, TOKEN_TILE), lambda b, j: (b, 0, 0, j))
    return pl.pallas_call(
        body,
        out_shape=[jax.ShapeDtypeStruct((batch, heads, LANES, t_len), BF16),
                   jax.ShapeDtypeStruct((batch, heads, t_len, LANES), BF16),
                   jax.ShapeDtypeStruct((batch, heads * v_dim, t_len), BF16)],
        grid=(batch, nj + 1),
        in_specs=[pl.BlockSpec((TOKEN_TILE, pa.shape[1]), lambda b, j: (_tok_block(b, j, nj, batch), 0)),
                  pl.BlockSpec((TOKEN_TILE, LANES), lambda b, j: (j, 0)),
                  pl.BlockSpec((TOKEN_TILE, LANES), lambda b, j: (j, 0)),
                  _full((1, q_lora)), _full((1, kv_lora)), _full(wuq.shape), _full(wuk.shape), _full(wuv.shape)],
        out_specs=[q_feature_major, head_major,
                   pl.BlockSpec((1, heads * v_dim, TOKEN_TILE), lambda b, j: (b, 0, j))],
        compiler_params=_params(2), name="mla_prep",
    )(pa, cos, sin, g_q.reshape(1, -1), g_kv.reshape(1, -1), wuq, wuk, wuv)


def _gqa_prep_body(pd_ref, cos_ref, sin_ref, gq_ref, gk_ref, bd_ref, q_ref, k_ref, *, q_width, qscale):
    pd = pd_ref[...]
    cos, sin = cos_ref[...], sin_ref[...]
    bd = bd_ref[...]
    lane = lax.broadcasted_iota(jnp.int32, (TOKEN_TILE, LANES), 1)
    upper = lane >= LANES // 2

    def head_norm(v, g_ref):
        ms = jnp.dot(v * v, bd, preferred_element_type=F32, precision=lax.Precision.HIGHEST)
        return v * lax.rsqrt(ms + EPS) * g_ref[...]

    n_blocks = q_width // LANES
    for i in range(n_blocks):
        xq = _rope(head_norm(pd[:, i * LANES:(i + 1) * LANES], gq_ref), cos, sin, 16) * qscale
        t = (2 * i) // n_blocks
        in_half = upper if t == 1 else jnp.logical_not(upper)
        swapped = pltpu.roll(xq, LANES // 2, 1)
        q_ref[0, 2 * i + t] = jnp.where(in_half, xq, 0.0).T.astype(BF16)
        q_ref[0, 2 * i + 1 - t] = jnp.where(in_half, swapped, 0.0).T.astype(BF16)
    xk = _rope(head_norm(pd[:, q_width:], gk_ref), cos, sin, 16)
    k_ref[0, 0] = xk.astype(BF16)


def _gqa_prep(pd, cos, sin, g_q, g_k, batch, seq, ctx_len, heads, kv_heads, head_dim):
    assert kv_heads * head_dim == LANES and heads % (2 * kv_heads) == 0
    nj = seq // TOKEN_TILE
    t_len = seq + ctx_len
    q_width = heads * head_dim
    grp = jnp.arange(LANES) // head_dim
    bd = (grp[:, None] == grp[None, :]).astype(F32) / head_dim
    rep = LANES // head_dim
    body = functools.partial(_gqa_prep_body, q_width=q_width, qscale=head_dim ** -0.5 * LOG2E)
    return pl.pallas_call(
        body,
        out_shape=[jax.ShapeDtypeStruct((batch, heads, LANES, t_len), BF16),
                   jax.ShapeDtypeStruct((batch, 1, t_len, LANES), BF16)],
        grid=(batch, nj + 1),
        in_specs=[pl.BlockSpec((TOKEN_TILE, pd.shape[1]), lambda b, j: (_tok_block(b, j, nj, batch), 0)),
                  pl.BlockSpec((TOKEN_TILE, LANES), lambda b, j: (j, 0)),
                  pl.BlockSpec((TOKEN_TILE, LANES), lambda b, j: (j, 0)),
                  _full((1, LANES)), _full((1, LANES)), _full((LANES, LANES))],
        out_specs=[pl.BlockSpec((1, heads, LANES, TOKEN_TILE), lambda b, j: (b, 0, 0, j)),
                   pl.BlockSpec((1, 1, TOKEN_TILE, LANES), lambda b, j: (b, 0, j, 0))],
        compiler_params=_params(2), name="gqa_prep",
    )(pd, cos, sin, jnp.tile(g_q, rep).reshape(1, LANES), jnp.tile(g_k, rep).reshape(1, LANES), bd)


def _score_pass(q, k_ref, k_head, chunks, s_ref):
    m = None
    for (st, sz) in chunks:
        s_t = _dot(k_ref[0, k_head, st:st + sz, :], q)
        s_ref[st:st + sz, :] = s_t
        cm = jnp.max(s_t, axis=0, keepdims=True)
        m = cm if m is None else jnp.maximum(m, cm)
    return m


def _value_pass(s_ref, m, vt_ref, v_rows, chunks):
    l = jnp.zeros((1, TOKEN_TILE), F32)
    acc = jnp.zeros((v_rows.stop - v_rows.start, TOKEN_TILE), F32)
    for (st, sz) in chunks:
        p = jnp.exp2(s_ref[st:st + sz, :] - m)
        l = l + jnp.sum(p, axis=0, keepdims=True)
        acc = acc + _dot(vt_ref[0, v_rows, st:st + sz], p.astype(BF16))
    return acc / l


def _gated_store(outs, g_ref, o_ref):
    o2 = jnp.concatenate(outs, axis=0).T
    o_ref[...] = (o2 * _silu(g_ref[...].astype(F32))).astype(o_ref.dtype)


def _flash_body(q_ref, k_ref, vt_ref, g_ref, o_ref, s_scr, m_scr, *, k_sel, v_off, v_dim, chunks):
    j = pl.program_id(2)

    @pl.when((pl.program_id(0) == 0) & (pl.program_id(1) == 0) & (j == 0))
    def _():
        s_scr[...] = jnp.zeros(s_scr.shape, F32)
        m_scr[...] = jnp.zeros(m_scr.shape, F32)

    def step(slot, prev):
        heads = (0, 1)
        q = [q_ref[0, a] for a in heads]
        m_prev = [m_scr[prev, a] for a in heads]
        m = [None, None]
        l = [jnp.zeros((1, TOKEN_TILE), F32) for _ in heads]
        acc = [jnp.zeros((v_dim, TOKEN_TILE), F32) for _ in heads]
        for (st, sz) in chunks:
            for a in heads:
                s_t = _dot(k_ref[0, k_sel[a], st:st + sz, :], q[a])
                s_scr[slot, a, st:st + sz, :] = s_t
                cm = jnp.max(s_t, axis=0, keepdims=True)
                m[a] = cm if m[a] is None else jnp.maximum(m[a], cm)
                p = jnp.exp2(s_scr[prev, a, st:st + sz, :] - m_prev[a])
                l[a] = l[a] + jnp.sum(p, axis=0, keepdims=True)
                acc[a] = acc[a] + _dot(vt_ref[0, v_off[a]:v_off[a] + v_dim, st:st + sz], p.astype(BF16))
        for a in heads:
            m_scr[slot, a] = m[a]
        _gated_store([acc[a] / l[a] for a in heads], g_ref, o_ref)

    for parity in (0, 1):
        pl.when(j % 2 == parity)(functools.partial(step, parity, 1 - parity))


def _flash_ctx_body(q_ref, k_ref, vt_ref, g_ref, o_ref, s_scr, *, k_sel, v_off, v_dim, chunks):
    outs = []
    for a in range(2):
        m = _score_pass(q_ref[0, a], k_ref, k_sel[a], chunks, s_scr.at[a])
        outs.append(_value_pass(s_scr.at[a], m, vt_ref, slice(v_off[a], v_off[a] + v_dim), chunks))
    _gated_store(outs, g_ref, o_ref)


def _flash(q, k, vt, gate, batch, seq, ctx_len, k_heads_per_pair, v_rows_per_pair, pairs_per_kv, ctx_queries):
    heads = q.shape[1]
    t_len = seq + ctx_len
    nj = seq // TOKEN_TILE
    v_dim = LANES // 2
    k_sel = (0, 1) if k_heads_per_pair == 2 else (0, 0)
    v_off = (0, v_dim) if v_rows_per_pair == 2 * v_dim else (0, 0)
    k_blocks = k.shape[1] // k_heads_per_pair
    k_block = lambda p: (p // pairs_per_kv) % k_blocks
    chunks = tuple((c * KEY_CHUNK, KEY_CHUNK) for c in range(seq // KEY_CHUNK)) + ((seq, ctx_len),)
    static = dict(k_sel=k_sel, v_off=v_off, v_dim=v_dim)

    width = gate.shape[1]
    done = lambda b, p, j: (b * nj + jnp.maximum(j - 1, 0), p)
    out = pl.pallas_call(
        functools.partial(_flash_body, chunks=chunks, **static),
        out_shape=jax.ShapeDtypeStruct((batch * seq, width), BF16),
        grid=(batch, heads // 2, nj + 1),
        in_specs=[pl.BlockSpec((1, 2, LANES, TOKEN_TILE), lambda b, p, j: (b, p, 0, jnp.minimum(j, nj - 1))),
                  pl.BlockSpec((1, k_heads_per_pair, t_len, LANES), lambda b, p, j: (b, k_block(p), 0, 0)),
                  pl.BlockSpec((1, v_rows_per_pair, t_len), lambda b, p, j: (b, p // pairs_per_kv, 0)),
                  pl.BlockSpec((TOKEN_TILE, LANES), done)],
        out_specs=pl.BlockSpec((TOKEN_TILE, LANES), done),
        scratch_shapes=[pltpu.VMEM((2, 2, t_len, TOKEN_TILE), F32), pltpu.VMEM((2, 2, 1, TOKEN_TILE), F32)],
        compiler_params=_params(3), name="flash_attention",
    )(q, k, vt, gate)
    if not ctx_queries:
        return (out,)

    out_ctx = pl.pallas_call(
        functools.partial(_flash_ctx_body, chunks=((0, ctx_len),), **static),
        out_shape=jax.ShapeDtypeStruct((batch * ctx_len, width), BF16),
        grid=(batch, heads // 2),
        in_specs=[pl.BlockSpec((1, 2, LANES, TOKEN_TILE), lambda b, p: (b, p, 0, nj)),
                  pl.BlockSpec((1, k_heads_per_pair, ctx_len, LANES), lambda b, p: (b, k_block(p), seq // ctx_len, 0)),
                  pl.BlockSpec((1, v_rows_per_pair, ctx_len), lambda b, p: (b, p // pairs_per_kv, seq // ctx_len)),
                  pl.BlockSpec((TOKEN_TILE, LANES), lambda b, p: (batch * nj + b, p))],
        out_specs=pl.BlockSpec((TOKEN_TILE, LANES), lambda b, p: (b, p)),
        scratch_shapes=[pltpu.VMEM((2, ctx_len, TOKEN_TILE), F32)],
        compiler_params=_params(2), name="flash_attention_ctx",
    )(q, k, vt, gate)
    return out, out_ctx


NA_KEY_ROWS = 12
NA_STEP_ROWS = TOKEN_TILE // GRID_W
NA_VARIANTS = 3


def _na_rel_row(variant, rr, a):
    if variant == 0:
        valid, dr = a < WIN_R, a - rr
    elif variant == 1:
        dr = a - WIN_R // 2 - rr
        valid = -(WIN_R // 2) <= dr < WIN_R // 2
    else:
        valid, dr = a >= NA_KEY_ROWS - WIN_R, a - (NA_KEY_ROWS - NA_STEP_ROWS) - rr
    return dr if valid else None


def _na_bias_body(rpb_ref, o_ref, blk_scr):
    p = pl.program_id(0)
    shape = (GRID_W, LANES)
    kc = lax.broadcasted_iota(jnp.int32, shape, 0)
    lane = lax.broadcasted_iota(jnp.int32, shape, 1)
    qc = lane % GRID_W
    upper = lane >= GRID_W
    rel = kc - qc + (WIN_C - 1)
    c0 = jnp.clip(qc - WIN_C // 2, 0, GRID_W - WIN_C)
    col_ok = (kc >= c0) & (kc < c0 + WIN_C)
    n_rel_r, n_rel_c = 2 * WIN_R - 1, 2 * WIN_C - 1

    def block(dd, carry):
        base0 = (2 * p) * (n_rel_r * n_rel_c) + dd * n_rel_c
        base1 = base0 + n_rel_r * n_rel_c
        val = jnp.zeros(shape, F32)
        for jj in range(n_rel_c):
            val = jnp.where(rel == jj, jnp.where(upper, rpb_ref[base1 + jj], rpb_ref[base0 + jj]), val)
        blk_scr[dd] = jnp.where(col_ok, val * LOG2E, NEG)
        return carry

    lax.fori_loop(0, n_rel_r, block, 0)
    outside = jnp.full(shape, NEG, F32)
    for variant in range(NA_VARIANTS):
        for rr in range(NA_STEP_ROWS):
            for a in range(NA_KEY_ROWS):
                dr = _na_rel_row(variant, rr, a)
                o_ref[0, variant, rr, a * GRID_W:(a + 1) * GRID_W, :] = (
                    outside if dr is None else blk_scr[dr + WIN_R - 1])


def _na_bias(rpb):
    heads = rpb.shape[0]
    tab = (NA_VARIANTS, NA_STEP_ROWS, NA_KEY_ROWS * GRID_W, LANES)
    return pl.pallas_call(
        _na_bias_body,
        out_shape=jax.ShapeDtypeStruct((heads // 2,) + tab, F32),
        grid=(heads // 2,),
        in_specs=[pl.BlockSpec(memory_space=pltpu.SMEM)],
        out_specs=pl.BlockSpec((1,) + tab, lambda p: (p, 0, 0, 0, 0)),
        scratch_shapes=[pltpu.VMEM((2 * WIN_R - 1, GRID_W, LANES), F32)],
        compiler_params=_params(1), name="na_bias",
    )(rpb.reshape(-1))


def _na_body(q_ref, kl_ref, kc_ref, vt_ref, bias_ref, g_ref, o_ref, sl_scr, sc_scr, m_scr, *, seq, ctx_len, n_steps):
    j = pl.program_id(2)
    n_rows = seq // GRID_W
    span = NA_KEY_ROWS * GRID_W
    lower = lax.broadcasted_iota(jnp.int32, (GRID_W, LANES), 1) < GRID_W

    @pl.when((pl.program_id(0) == 0) & (pl.program_id(1) == 0) & (j == 0))
    def _():
        sl_scr[...] = jnp.zeros(sl_scr.shape, F32)
        sc_scr[...] = jnp.zeros(sc_scr.shape, F32)
        m_scr[...] = jnp.zeros(m_scr.shape, F32)

    def span_start(step):
        first_row = jnp.clip(NA_STEP_ROWS * step - WIN_R // 2, 0, n_rows - NA_KEY_ROWS)
        return pl.multiple_of(first_row * GRID_W, TOKEN_TILE)

    def step(slot, prev):
        js = jnp.minimum(j, n_steps - 1)
        variant = jnp.where(js == 0, 0, jnp.where(js == n_steps - 1, 2, 1))
        q = q_ref[...]
        zero = jnp.zeros((GRID_W, LANES), q.dtype)
        parts = []
        for rr in range(NA_STEP_ROWS):
            q_r = q[rr * GRID_W:(rr + 1) * GRID_W]
            parts += [jnp.where(lower, q_r, zero), jnp.where(lower, zero, q_r)]
        q2 = jnp.concatenate(parts, axis=0)
        s_loc = _dot_nt(kl_ref[pl.ds(span_start(js), span), :], q2)
        s_loc = jnp.concatenate([s_loc[:, rr * LANES:(rr + 1) * LANES] + bias_ref[0, variant, rr]
                                 for rr in range(NA_STEP_ROWS)], axis=1)
        s_ctx = _dot_nt(kc_ref[...], q2)
        sl_scr[slot] = s_loc
        sc_scr[slot] = s_ctx
        m_scr[slot] = jnp.maximum(jnp.max(s_loc, axis=0, keepdims=True), jnp.max(s_ctx, axis=0, keepdims=True))

        m = m_scr[prev]
        p_loc = jnp.exp2(sl_scr[prev] - m)
        p_ctx = jnp.exp2(sc_scr[prev] - m)
        l = jnp.sum(p_loc, axis=0, keepdims=True) + jnp.sum(p_ctx, axis=0, keepdims=True)
        v_loc = vt_ref[0, :, pl.ds(span_start(jnp.maximum(j - 1, 0)), span)]
        r_t = _dot(v_loc, p_loc.astype(BF16)) + _dot(vt_ref[0, :, seq:seq + ctx_len], p_ctx.astype(BF16))
        r_n = (r_t / l).T
        outs = [jnp.where(lower, r_n[rr * LANES:rr * LANES + GRID_W], r_n[rr * LANES + GRID_W:(rr + 1) * LANES])
                for rr in range(NA_STEP_ROWS)]
        o = jnp.concatenate(outs, axis=0)
        o_ref[...] = (o * _silu(g_ref[...].astype(F32))).astype(o_ref.dtype)

    for parity in (0, 1):
        pl.when(j % 2 == parity)(functools.partial(step, parity, 1 - parity))


def _na_attention(q, k, vt, bias, gate, batch, seq, ctx_len):
    pairs = q.shape[1] // LANES
    t_len = seq + ctx_len
    n_steps = seq // TOKEN_TILE
    span = NA_KEY_ROWS * GRID_W
    cols = NA_STEP_ROWS * LANES
    done = lambda b, p, j: (b * n_steps + jnp.maximum(j - 1, 0), p)
    body = functools.partial(_na_body, seq=seq, ctx_len=ctx_len, n_steps=n_steps)
    return pl.pallas_call(
        body, out_shape=jax.ShapeDtypeStruct((batch * seq, q.shape[1]), BF16),
        grid=(batch, pairs, n_steps + 1),
        in_specs=[pl.BlockSpec((TOKEN_TILE, LANES), lambda b, p, j: (b * n_steps + jnp.minimum(j, n_steps - 1), p)),
                  pl.BlockSpec((seq, LANES), lambda b, p, j: (b, p)),
                  pl.BlockSpec((ctx_len, LANES), lambda b, p, j: (batch * seq // ctx_len + b, p)),
                  pl.BlockSpec((1, LANES, t_len), lambda b, p, j: (b, p, 0)),
                  pl.BlockSpec((1,) + bias.shape[1:], lambda b, p, j: (p, 0, 0, 0, 0)),
                  pl.BlockSpec((TOKEN_TILE, LANES), done)],
        out_specs=pl.BlockSpec((TOKEN_TILE, LANES), done),
        scratch_shapes=[pltpu.VMEM((2, span, cols), F32), pltpu.VMEM((2, ctx_len, cols), F32),
                        pltpu.VMEM((2, 1, cols), F32)],
        compiler_params=_params(3), name="neighborhood_attention",
    )(q, k, k, vt, bias, gate)


def _seg_scans(jobs, use_max=False):
    vals = [v for v, _ in jobs]
    n = vals[0].shape[1]
    lane = lax.broadcasted_iota(jnp.int32, vals[0].shape, 1) % ML_CHUNK
    k = 1
    while k < ML_CHUNK:
        for i, (_, reverse) in enumerate(jobs):
            v = vals[i]
            if reverse:
                ok, shifted = lane < ML_CHUNK - k, pltpu.roll(v, n - k, 1)
            else:
                ok, shifted = lane >= k, pltpu.roll(v, k, 1)
            vals[i] = jnp.maximum(v, jnp.where(ok, shifted, NEG)) if use_max else v + jnp.where(ok, shifted, 0.0)
        k *= 2
    return vals


ML_GATE_ROWS = 40


def _mlstm_prep_body(u_ref, up_ref, un_ref, cw_ref, cb_ref, wqk_ref, wv_ref, wg_ref, bg_ref,
                     xc_ref, k_ref, qt_ref, vt_ref, pre_ref, *, nj, width, kscale):
    j = pl.program_id(1)
    u = u_ref[...]
    row = lax.broadcasted_iota(jnp.int32, u.shape, 0)
    prev = jnp.where((j > 0) & (j < nj), up_ref[7:8, :], 0.0)
    nxt = jnp.where(j < nj - 1, un_ref[0:1, :], 0.0)
    u_m1 = jnp.where(row == 0, prev, pltpu.roll(u, 1, 0))
    u_p1 = jnp.where(row == TOKEN_TILE - 1, nxt, pltpu.roll(u, TOKEN_TILE - 1, 0))
    cw = cw_ref[...]
    xc = _silu(u_m1 * cw[0:1] + u * cw[1:2] + u_p1 * cw[2:3] + cb_ref[...])
    xcb = xc.astype(BF16)
    xc_ref[...] = xcb
    qk = _dot(xcb, wqk_ref[...])
    v = _dot(u.astype(BF16), wv_ref[...])
    qb, kb, vb = qk[:, :width].astype(BF16), qk[:, width:].astype(BF16), v.astype(BF16)
    k_ref[...] = (qk[:, width:] * kscale).astype(BF16)
    qt_ref[0] = qk[:, :width].T.astype(BF16)
    vt_ref[0] = v.T.astype(BF16)
    pre_ref[0] = _dot_nt(wg_ref[0], qb) + _dot_nt(wg_ref[1], kb) + _dot_nt(wg_ref[2], vb) + bg_ref[...]


def _mlstm_gates_body(pre_ref, g_ref, gc_ref):
    pre = pre_ref[0]
    i8 = [pre[16 * d:16 * d + 8] for d in range(2)]
    f8 = [_log_sigmoid(pre[16 * d + 8:16 * d + 16]) for d in range(2)]
    b0, b0_rev, b1, b1_rev = _seg_scans([(f8[0], False), (f8[0], True), (f8[1], True), (f8[1], False)])
    b8, b_last = [b0, b1], [b0 + b0_rev - f8[0], b1 + b1_rev - f8[1]]
    r8 = [i8[d] - b8[d] for d in range(2)]
    c0, c0_rev, c1, c1_rev = _seg_scans([(r8[0], False), (r8[0], True), (r8[1], True), (r8[1], False)], use_max=True)
    c8, r_max = [c0, c1], [jnp.maximum(c0, c0_rev), jnp.maximum(c1, c1_rev)]
    for d in range(2):
        g_ref[0, d * ML_GATE_ROWS:(d + 1) * ML_GATE_ROWS] = jnp.concatenate(
            [-c8[d], b8[d] + c8[d], jnp.exp(r8[d] - r_max[d]), b_last[d], b_last[d] + r_max[d]], axis=0)
    pad = jnp.zeros((LANES - 16, pre.shape[1]), F32)
    gc_ref[0] = jnp.concatenate(r8 + [pad], axis=0).T


def _mlstm_gates(pre):
    batch, _, t_len = pre.shape
    return pl.pallas_call(
        _mlstm_gates_body,
        out_shape=[jax.ShapeDtypeStruct((batch, 2 * ML_GATE_ROWS, t_len), F32),
                   jax.ShapeDtypeStruct((batch, t_len, LANES), F32)],
        grid=(batch,),
        in_specs=[pl.BlockSpec((1,) + pre.shape[1:], lambda b: (b, 0, 0))],
        out_specs=[pl.BlockSpec((1, 2 * ML_GATE_ROWS, t_len), lambda b: (b, 0, 0)),
                   pl.BlockSpec((1, t_len, LANES), lambda b: (b, 0, 0))],
        compiler_params=_params(1), name="mlstm_gates",
    )(pre)


def _mlstm_prep(u, conv_w, conv_b, w_q, w_k, w_v, w_gate, b_gate, batch, seq, ctx_len):
    heads, hd = w_q.shape[0], w_q.shape[1]
    assert heads == ML_HEADS and hd == ML_CHUNK
    width = heads * hd
    nj = seq // TOKEN_TILE
    t_len = seq + ctx_len
    rows = u.shape[0]

    def block_diag(w):
        eye = jnp.eye(heads, dtype=w.dtype)
        return (eye[:, None, :, None] * w[:, :, None, :]).reshape(width, width)

    wqk = jnp.concatenate([block_diag(w_q), block_diag(w_k)], axis=1).astype(BF16)
    wv = block_diag(w_v).astype(BF16)
    wg = w_gate.reshape(2, heads, 3, hd, 2, heads).transpose(2, 0, 4, 5, 1, 3).reshape(3, 2, 2, heads, width)
    wg = jnp.pad(wg, ((0, 0), (0, 0), (0, 0), (0, 8 - heads), (0, 0))).reshape(3, 32, width).astype(BF16)
    bg = jnp.pad(b_gate.reshape(2, 2, heads), ((0, 0), (0, 0), (0, 8 - heads))).reshape(32, 1)
    n_halo = rows // 8
    tokb = lambda b, j: _tok_block(b, j, nj, batch)
    tile = pl.BlockSpec((TOKEN_TILE, width), lambda b, j: (tokb(b, j), 0))
    feat = pl.BlockSpec((1, width, TOKEN_TILE), lambda b, j: (b, 0, j))
    body = functools.partial(_mlstm_prep_body, nj=nj, width=width, kscale=hd ** -0.5)
    per_tile = TOKEN_TILE // 8
    return pl.pallas_call(
        body,
        out_shape=[jax.ShapeDtypeStruct((rows, width), BF16)] * 2
        + [jax.ShapeDtypeStruct((batch, width, t_len), BF16)] * 2
        + [jax.ShapeDtypeStruct((batch, 32, t_len), F32)],
        grid=(batch, nj + 1),
        in_specs=[tile,
                  pl.BlockSpec((8, width), lambda b, j: (jnp.maximum(tokb(b, j) * per_tile - 1, 0), 0)),
                  pl.BlockSpec((8, width), lambda b, j: (jnp.minimum((tokb(b, j) + 1) * per_tile, n_halo - 1), 0)),
                  _full((3, width)), _full((1, width)), _full(wqk.shape), _full(wv.shape),
                  _full(wg.shape), _full((32, 1))],
        out_specs=[tile, tile, feat, feat, pl.BlockSpec((1, 32, TOKEN_TILE), lambda b, j: (b, 0, j))],
        compiler_params=_params(2), name="mlstm_prep",
    )(u, u, u, conv_w, conv_b.reshape(1, width), wqk, wv, wg, bg)


def _mlstm_seq_body(kf, qtf, vtf, gf, gcf, kb, qtb, vtb, gb, gcb, hf_ref, hb_ref, c_s, n_s, m_s):
    t = pl.program_id(1)
    L = ML_CHUNK

    @pl.when(t == 0)
    def _():
        c_s[...] = jnp.zeros(c_s.shape, F32)
        n_s[...] = jnp.zeros(n_s.shape, F32)
        m_s[...] = jnp.zeros(m_s.shape, F32)

    si = lax.broadcasted_iota(jnp.int32, (L, L), 0)
    li = lax.broadcasted_iota(jnp.int32, (L, L), 1)
    streams = ((kf, qtf, vtf, gf, gcf, hf_ref, si <= li), (kb, qtb, vtb, gb, gcb, hb_ref, si >= li))
    units = []
    for d, (k_ref, qt_ref, vt_ref, g_ref, gc_ref, h_ref, incl) in enumerate(streams):
        g = g_ref[0]
        gc = gc_ref[0]
        for h in range(ML_HEADS):
            cols = slice(h * L, (h + 1) * L)
            k, qt, vt = k_ref[:, cols], qt_ref[0, cols, :], vt_ref[0, cols, :]
            neg_c, m_loc, w0, b_last, g_max = (g[8 * i + h:8 * i + h + 1] for i in range(5))
            r_col = gc[:, 8 * d + h:8 * d + h + 1]
            p0 = jnp.where(incl, jnp.exp(r_col + neg_c), 0.0) * _dot(k, qt)
            s_sum = jnp.sum(p0, axis=0, keepdims=True)
            intra = _dot(vt, p0.astype(BF16))
            c_inc = _dot((vt.astype(F32) * w0).astype(BF16), k)
            n_inc = _dot(jnp.broadcast_to(w0, (8, L)).astype(BF16), k)
            units.append((d * ML_HEADS + h, h_ref, cols, qt, neg_c, m_loc, b_last, g_max, s_sum, intra, c_inc, n_inc))
    for (idx, h_ref, cols, qt, neg_c, m_loc, b_last, g_max, s_sum, intra, c_inc, n_inc) in units:
        c_st, n_st, m_st = c_s[idx], n_s[idx], m_s[idx]
        cn = _dot(jnp.concatenate([c_st, n_st], axis=0).astype(BF16), qt)
        delta = jnp.maximum(m_st + neg_c, 0.0)
        e_intra = jnp.exp(-delta)
        w_inter = jnp.exp(m_st + neg_c - delta)
        num = w_inter * cn[:L] + e_intra * intra
        den = w_inter * cn[L:L + 1] + e_intra * s_sum
        h_ref[0, cols, :] = num / jnp.maximum(jnp.abs(den), jnp.exp(-(m_loc + delta)))
        m_new = jnp.maximum(b_last + m_st, g_max)
        decay = jnp.exp(b_last + m_st - m_new)
        gain = jnp.exp(g_max - m_new)
        c_s[idx] = decay * c_st + gain * c_inc
        n_s[idx] = decay * n_st + gain * n_inc
        m_s[idx] = m_new


def _mlstm_seq(k, qt, vt, gates, gcols, batch, seq, ctx_len):
    width = k.shape[1]
    L = ML_CHUNK
    n_lat, n_ctx = seq // L, ctx_len // L
    n_chunks = n_lat + n_ctx
    fwd = lambda t: (t + n_lat) % n_chunks
    bwd = lambda t: n_chunks - 1 - t
    rowblk = lambda b, c: jnp.where(c < n_lat, b * n_lat + c, batch * n_lat + b * n_ctx + (c - n_lat))

    def stream(chunk_of, d):
        feat = pl.BlockSpec((1, width, L), lambda b, t: (b, 0, chunk_of(t)))
        return [pl.BlockSpec((L, width), lambda b, t: (rowblk(b, chunk_of(t)), 0)), feat, feat,
                pl.BlockSpec((1, ML_GATE_ROWS, L), lambda b, t: (b, d, chunk_of(t))),
                pl.BlockSpec((1, L, LANES), lambda b, t: (b, chunk_of(t), 0))]

    out_f = pl.BlockSpec((1, width, L), lambda b, t: (b, 0, fwd(t)))
    out_b = pl.BlockSpec((1, width, L), lambda b, t: (b, 0, bwd(t)))
    n_state = 2 * ML_HEADS
    return pl.pallas_call(
        _mlstm_seq_body,
        out_shape=[jax.ShapeDtypeStruct(qt.shape, F32)] * 2,
        grid=(batch, n_chunks),
        in_specs=stream(fwd, 0) + stream(bwd, 1),
        out_specs=[out_f, out_b],
        scratch_shapes=[pltpu.VMEM((n_state, L, L), F32), pltpu.VMEM((n_state, 8, L), F32),
                        pltpu.VMEM((n_state, 1, L), F32)],
        compiler_params=_params(2), name="mlstm_recurrence",
    )(k, qt, vt, gates, gcols, k, qt, vt, gates, gcols)


def _mlstm_out_body(hf_ref, hb_ref, xc_ref, z_ref, gh_ref, sk_ref, o_ref):
    ht = hf_ref[0] + hb_ref[0]
    L = ML_CHUNK
    normed = []
    for hd in range(ML_HEADS):
        hh = ht[hd * L:(hd + 1) * L]
        mu = jnp.mean(hh, axis=0, keepdims=True)
        var = jnp.mean(jnp.square(hh - mu), axis=0, keepdims=True)
        normed.append((hh - mu) * lax.rsqrt(var + EPS))
    hn = jnp.concatenate(normed, axis=0).T * gh_ref[...]
    o = (hn + sk_ref[...] * xc_ref[...].astype(F32)) * _silu(z_ref[...].astype(F32))
    o_ref[...] = o.astype(o_ref.dtype)


def _mlstm_out(hf, hb, xc, z, g_head, skip, batch, seq):
    rows, width = xc.shape
    nj = seq // TOKEN_TILE
    tile = pl.BlockSpec((TOKEN_TILE, width), lambda b, j: (_tok_block(b, j, nj, batch), 0))
    feat = pl.BlockSpec((1, width, TOKEN_TILE), lambda b, j: (b, 0, j))
    return pl.pallas_call(
        _mlstm_out_body, out_shape=jax.ShapeDtypeStruct((rows, width), BF16), grid=(batch, nj + 1),
        in_specs=[feat, feat, tile, tile, _full((1, width)), _full((1, width))],
        out_specs=tile, compiler_params=_params(2), name="mlstm_out",
    )(hf, hb, xc, z, g_head.reshape(1, width), skip.reshape(1, width))


def kernel(x, c, ctx, c_ctx, w_mod, b_mod, g_norm, ab_w_in, ab_w_out, mla_g_q, mla_w_uq, mla_g_kv, mla_w_ukv,
           ml_conv_w, ml_conv_b, ml_w_q, ml_w_k, ml_w_v, ml_w_gate, ml_b_gate, ml_g_head, ml_skip,
           cd_w_in, cd_w_out, na_rpb, gqa_g_q, gqa_g_k, g_final):
    batch, seq, d = x.shape
    ctx_len = ctx.shape[1]
    assert ctx_len == TOKEN_TILE and seq % KEY_CHUNK == 0 and seq // GRID_W >= NA_KEY_ROWS
    dims = (batch, seq, ctx_len)

    mla_heads, mla_rope, mla_v = 8, 32, 64
    mla_nope = mla_w_uq.shape[2] // mla_heads - mla_rope
    q_lora, kv_lora = mla_g_q.shape[1], mla_g_kv.shape[1]
    ml_width = ml_conv_w.shape[2]
    mla_width = mla_heads * mla_v
    gqa_heads, gqa_dim = 8, gqa_g_q.shape[1]
    gqa_kv = (cd_w_in.shape[2] - 4 * 512 - 2 * gqa_heads * gqa_dim) // (2 * gqa_dim)
    na_width = na_rpb.shape[1] * 64

    mod_rows = -(-(batch + 1) // 8) * 8
    cvec = jnp.concatenate([c, c_ctx[None], jnp.zeros((mod_rows - batch - 1, d), F32)], axis=0)
    mod = _modulation(cvec, w_mod, b_mod)
    mod0 = mod[0].reshape(mod_rows, 1, 3 * d)
    mod1 = mod[1].reshape(mod_rows, 1, 3 * d)

    tok0 = _Tokens((x.reshape(batch * seq, d), ctx.reshape(batch * ctx_len, d)), *dims)
    w_in = ab_w_in[0]
    s1 = q_lora + kv_lora
    zcol = lambda n: jnp.zeros((d, n), w_in.dtype)
    w0 = jnp.concatenate([w_in[:, :s1], zcol(mla_nope), w_in[:, s1:s1 + mla_rope],
                          zcol(LANES - mla_nope - mla_rope), w_in[:, s1 + mla_rope:]], axis=1).astype(BF16)
    o_pa = s1 + LANES
    outs0 = ((0, o_pa, F32, 1.0, False), (o_pa, mla_width, BF16, 1.0, False),
             (o_pa + mla_width, ml_width, F32, 1.0, False), (o_pa + mla_width + ml_width, ml_width, BF16, 1.0, False))
    pa, gate_a, u, z = _in_proj(tok0, mod0, g_norm[0], w0, outs0, *dims)

    def mla_lanes(row, fill):
        n = row.shape[0]
        return jnp.concatenate([jnp.full((n, mla_nope), fill, F32), row,
                                jnp.full((n, LANES - mla_nope - mla_rope), fill, F32)], axis=-1)

    cos_a, sin_a = _rope_tables(seq, ctx_len, mla_rope, mla_lanes)
    q_a, k_a, vt_a = _mla_prep(pa, cos_a, sin_a, mla_g_q[0], mla_g_kv[0], mla_w_uq[0], mla_w_ukv[0],
                               *dims, mla_heads, mla_nope, mla_rope, mla_v)
    mix_a = _flash(q_a, k_a, vt_a, gate_a, *dims, k_heads_per_pair=2, v_rows_per_pair=2 * mla_v,
                   pairs_per_kv=1, ctx_queries=True)

    xc, k_m, qt_m, vt_m, gate_pre = _mlstm_prep(u, ml_conv_w[0], ml_conv_b[0], ml_w_q[0], ml_w_k[0], ml_w_v[0],
                                                ml_w_gate[0], ml_b_gate[0], *dims)
    gates, gcols = _mlstm_gates(gate_pre)
    h_f, h_b = _mlstm_seq(k_m, qt_m, vt_m, gates, gcols, *dims)
    mix_b = _mlstm_out(h_f, h_b, xc, z, ml_g_head[0], ml_skip[0], batch, seq)
    x1 = _out_proj(tok0, _Tokens(mix_a, *dims), mix_b, ab_w_out[0], mod0, batch, seq)

    tok1 = _Tokens((x1,), *dims)
    w1 = cd_w_in[0].astype(BF16)
    gq_w, gkv_w = gqa_heads * gqa_dim, gqa_kv * gqa_dim
    o_d = 4 * na_width
    outs1 = ((0, na_width, BF16, 64 ** -0.5 * LOG2E, False), (na_width, na_width, BF16, 1.0, False),
             (2 * na_width, na_width, BF16, 1.0, True), (3 * na_width, na_width, BF16, 1.0, False),
             (o_d, gq_w + gkv_w, F32, 1.0, False), (o_d + gq_w + gkv_w, gkv_w, BF16, 1.0, True),
             (o_d + gq_w + 2 * gkv_w, gq_w, BF16, 1.0, False))
    q_c, k_c, vt_c, gate_c, pd, vt_d, gate_d = _in_proj(tok1, mod1, g_norm[1], w1, outs1, *dims)

    mix_c = _na_attention(q_c, k_c, vt_c, _na_bias(na_rpb[0]), gate_c, *dims)

    gqa_lanes = lambda row, fill: jnp.tile(row, (1, LANES // gqa_dim))
    cos_d, sin_d = _rope_tables(seq, ctx_len, gqa_dim, gqa_lanes)
    q_d, k_d = _gqa_prep(pd, cos_d, sin_d, gqa_g_q[0], gqa_g_k[0], *dims, gqa_heads, gqa_kv, gqa_dim)
    mix_d = _flash(q_d, k_d, vt_d, gate_d, *dims, k_heads_per_pair=1, v_rows_per_pair=gqa_dim,
                   pairs_per_kv=gqa_heads // (2 * gqa_kv), ctx_queries=False)

    out = _out_proj(tok1, _Tokens((mix_c,), *dims), mix_d[0], cd_w_out[0], mod1, batch, seq, g_final=g_final)
    return out.reshape(batch, seq, d)
```

```python
import functools

import jax
import jax.numpy as jnp
from jax import lax
from jax.experimental import pallas as pl
from jax.experimental.pallas import tpu as pltpu

F32 = jnp.float32
BF16 = jnp.bfloat16

LANES = 128
TOKEN_TILE = 256
KEY_CHUNK = 512
GRID_W = 64
WIN_R = 8
WIN_C = 16
ML_CHUNK = 128
ML_HEADS = 4
EPS = 1e-6
ROPE_BASE = 10000.0
LOG2E = 1.4426950408889634
NEG = -1e30
VMEM_LIMIT = 56 * 1024 * 1024

_NT = (((1,), (1,)), ((), ()))


def _dot(a, b):
    return jnp.dot(a, b, preferred_element_type=F32)


def _dot_nt(a, b):
    return lax.dot_general(a, b, _NT, preferred_element_type=F32)


def _silu(v):
    return v * (1.0 / (1.0 + jnp.exp(-v)))


def _log_sigmoid(v):
    return -(jnp.maximum(-v, 0.0) + jnp.log1p(jnp.exp(-jnp.abs(v))))


def _params(n_axes):
    return pltpu.CompilerParams(dimension_semantics=("arbitrary",) * n_axes, vmem_limit_bytes=VMEM_LIMIT)


def _full(shape):
    nd = len(shape)
    return pl.BlockSpec(shape, lambda *_: (0,) * nd)


def _mod_body(c_ref, w_ref, b_ref, o_ref):
    s = _silu(c_ref[...])
    o_ref[0] = _dot(s.astype(BF16), w_ref[0].astype(BF16)) + b_ref[0]


def _modulation(cvec, w_mod, b_mod):
    depth, d, n = w_mod.shape
    rows = cvec.shape[0]
    tn = n // 4
    return pl.pallas_call(
        _mod_body,
        out_shape=jax.ShapeDtypeStruct((depth, rows, n), F32),
        grid=(depth, n // tn),
        in_specs=[_full((rows, d)),
                  pl.BlockSpec((1, d, tn), lambda l, j: (l, 0, j)),
                  pl.BlockSpec((1, 1, tn), lambda l, j: (l, 0, j))],
        out_specs=pl.BlockSpec((1, rows, tn), lambda l, j: (l, 0, j)),
        compiler_params=_params(2), name="modulation",
    )(cvec, w_mod, b_mod.reshape(depth, 1, n))


class _Tokens:
    def __init__(self, arrays, batch, seq, ctx_len):
        self.arrays = arrays
        self.split = len(arrays) == 2
        self.n_lat = batch * seq // TOKEN_TILE
        self.n_ctx = batch * ctx_len // TOKEN_TILE
        self.d = arrays[0].shape[-1]

    def specs(self):
        blk = (TOKEN_TILE, self.d)
        if not self.split:
            return [pl.BlockSpec(blk, lambda i: (i, 0))]
        n_lat = self.n_lat
        return [pl.BlockSpec(blk, lambda i: (jnp.minimum(i, n_lat - 1), 0)),
                pl.BlockSpec(blk, lambda i: (jnp.maximum(i - n_lat, 0), 0))]

    def load(self, refs, i):
        if not self.split:
            return refs[0][...]
        return jnp.where(i < self.n_lat, refs[0][...], refs[1][...])


def _mod_spec(n_lat, nj, batch, width):
    return pl.BlockSpec((1, 1, width), lambda i: (jnp.where(i < n_lat, i // nj, batch), 0, 0))


def _in_proj_body(*refs, tok, outs, d):
    n_tok = len(tok.arrays)
    mod_ref, g_ref, w_ref = refs[n_tok:n_tok + 3]
    o_refs = refs[n_tok + 3:]
    i = pl.program_id(0)
    x = tok.load(refs[:n_tok], i)
    y = x * lax.rsqrt(jnp.mean(x * x, axis=-1, keepdims=True) + EPS) * g_ref[...]
    mod = mod_ref[0]
    h = y * (1.0 + mod[:, d:2 * d]) + mod[:, :d]
    acc = _dot(h.astype(BF16), w_ref[...])
    for o_ref, (c0, width, _, scale, transposed) in zip(o_refs, outs):
        v = acc[:, c0:c0 + width]
        if scale != 1.0:
            v = v * scale
        if transposed:
            o_ref[0] = v.T.astype(o_ref.dtype)
        else:
            o_ref[...] = v.astype(o_ref.dtype)


def _in_proj(tok, mod_l, g, w, outs, batch, seq, ctx_len):
    d = tok.d
    nj = seq // TOKEN_TILE
    n_lat, n_all = tok.n_lat, tok.n_lat + tok.n_ctx
    rows = n_all * TOKEN_TILE
    t_len = seq + ctx_len
    out_shape, out_specs = [], []
    for (_, width, dtype, _, transposed) in outs:
        if transposed:
            out_shape.append(jax.ShapeDtypeStruct((batch, width, t_len), dtype))
            out_specs.append(pl.BlockSpec(
                (1, width, TOKEN_TILE),
                lambda i: (jnp.where(i < n_lat, i // nj, i - n_lat), 0, jnp.where(i < n_lat, i % nj, nj))))
        else:
            out_shape.append(jax.ShapeDtypeStruct((rows, width), dtype))
            out_specs.append(pl.BlockSpec((TOKEN_TILE, width), lambda i: (i, 0)))
    body = functools.partial(_in_proj_body, tok=tok, outs=outs, d=d)
    return pl.pallas_call(
        body, out_shape=out_shape, grid=(n_all,),
        in_specs=tok.specs() + [_mod_spec(n_lat, nj, batch, 3 * d), _full((1, d)), _full(w.shape)],
        out_specs=out_specs, compiler_params=_params(1), name="in_proj",
    )(*tok.arrays, mod_l, g.reshape(1, d), w)


def _out_proj_body(*refs, tok, mix_a, d, final):
    n_tok, n_a = len(tok.arrays), len(mix_a.arrays)
    mb_ref, wa_ref, wb_ref, mod_ref = refs[n_tok + n_a:n_tok + n_a + 4]
    rest = refs[n_tok + n_a + 4:]
    i = pl.program_id(0)
    x = tok.load(refs[:n_tok], i)
    acc = _dot(mix_a.load(refs[n_tok:n_tok + n_a], i), wa_ref[...]) + _dot(mb_ref[...], wb_ref[...])
    xn = x + mod_ref[0][:, 2 * d:] * acc
    if final:
        gf_ref, o_ref = rest
        xn = xn * lax.rsqrt(jnp.mean(xn * xn, axis=-1, keepdims=True) + EPS) * gf_ref[...]
    else:
        (o_ref,) = rest
    o_ref[...] = xn


def _out_proj(tok, mix_a, mix_b, w_out, mod_l, batch, seq, g_final=None):
    d = tok.d
    half = mix_a.d
    nj = seq // TOKEN_TILE
    n_lat = tok.n_lat
    final = g_final is not None
    n_tiles = n_lat if final else n_lat + tok.n_ctx
    wa, wb = w_out[:half].astype(BF16), w_out[half:].astype(BF16)
    tile = lambda width: pl.BlockSpec((TOKEN_TILE, width), lambda i: (i, 0))
    in_specs = tok.specs() + mix_a.specs() + [tile(half), _full(wa.shape), _full(wb.shape),
                                              _mod_spec(n_lat, nj, batch, 3 * d)]
    args = list(tok.arrays) + list(mix_a.arrays) + [mix_b, wa, wb, mod_l]
    if final:
        in_specs.append(_full((1, d)))
        args.append(g_final.reshape(1, d))
    body = functools.partial(_out_proj_body, tok=tok, mix_a=mix_a, d=d, final=final)
    return pl.pallas_call(
        body, out_shape=jax.ShapeDtypeStruct((n_tiles * TOKEN_TILE, d), F32), grid=(n_tiles,),
        in_specs=in_specs, out_specs=tile(d), compiler_params=_params(1), name="out_proj",
    )(*args)


def _rope_tables(seq, ctx_len, rot_dim):
    t = jnp.arange(seq)
    pos = jnp.stack([t // GRID_W, t % GRID_W], axis=-1).astype(F32)
    n_freq = rot_dim // 4
    inv = ROPE_BASE ** (-jnp.arange(n_freq, dtype=F32) / n_freq)
    ang = pos[:, :, None] * inv
    cos, sin = jnp.cos(ang), jnp.sin(ang)
    cos_t = jnp.concatenate([cos[:, 0], cos[:, 0], cos[:, 1], cos[:, 1]], axis=-1)
    sin_t = jnp.concatenate([-sin[:, 0], sin[:, 0], -sin[:, 1], sin[:, 1]], axis=-1)
    cos_t = jnp.concatenate([cos_t, jnp.ones((ctx_len, rot_dim), F32)], axis=0)
    sin_t = jnp.concatenate([sin_t, jnp.zeros((ctx_len, rot_dim), F32)], axis=0)
    return cos_t, sin_t


def _rope(x, cos, sin, dist):
    lane = lax.broadcasted_iota(jnp.int32, x.shape, 1)
    first = (lane % (2 * dist)) < dist
    partner = jnp.where(first, pltpu.roll(x, LANES - dist, 1), pltpu.roll(x, dist, 1))
    return x * cos + partner * sin


def _rope_rows(x, cos, sin, dist):
    n = x.shape[0] // dist
    partner = jnp.concatenate([x[(i ^ 1) * dist:((i ^ 1) + 1) * dist] for i in range(n)], axis=0)
    return x * cos + partner * sin


def _tok_block(b, j, nj, batch):
    return jnp.where(j < nj, b * nj + j, batch * nj + b)


def _mla_prep_body(pa_ref, cos_ref, sin_ref, cost_ref, sint_ref, gq_ref, gkv_ref, wuqt_ref, wuk_ref, wuvt_ref,
                   q_ref, k_ref, vt_ref, *, heads, q_lora, kv_lora, qscale):
    pa = pa_ref[...]

    def norm(v, g_ref):
        return v * lax.rsqrt(jnp.mean(v * v, axis=-1, keepdims=True) + EPS) * g_ref[...]

    cq_t = norm(pa[:, :q_lora], gq_ref).T.astype(BF16)
    q_all = _dot(wuqt_ref[...], cq_t)
    cos_t, sin_t = cost_ref[...], sint_ref[...]
    for h in range(heads):
        qh = _rope_rows(q_all[h * LANES:(h + 1) * LANES], cos_t, sin_t, 8)
        q_ref[0, h] = (qh * qscale).astype(BF16)
    ckv = norm(pa[:, q_lora:q_lora + kv_lora], gkv_ref)
    k_nope = _dot(ckv.astype(BF16), wuk_ref[...])
    k_rope = _rope(pa[:, q_lora + kv_lora:], cos_ref[...], sin_ref[...], 8)
    for h in range(heads):
        k_ref[0, h] = (k_nope[:, h * LANES:(h + 1) * LANES] + k_rope).astype(BF16)
    vt_ref[0] = _dot(wuvt_ref[...], ckv.T.astype(BF16)).astype(BF16)


def _mla_prep(pa, cos, sin, g_q, g_kv, w_uq, w_ukv, batch, seq, ctx_len, heads, nope, rope, v_dim):
    q_lora, kv_lora = g_q.shape[0], g_kv.shape[0]
    nj = seq // TOKEN_TILE
    t_len = seq + ctx_len
    pad = LANES - nope - rope
    wuq = jnp.pad(w_uq.reshape(q_lora, heads, nope + rope), ((0, 0), (0, 0), (0, pad)))
    wuq_t = wuq.reshape(q_lora, heads * LANES).T.astype(BF16)
    wkv = w_ukv.reshape(kv_lora, heads, nope + v_dim)
    wuk = jnp.pad(wkv[..., :nope], ((0, 0), (0, 0), (0, LANES - nope))).reshape(kv_lora, heads * LANES).astype(BF16)
    wuv_t = wkv[..., nope:].reshape(kv_lora, heads * v_dim).T.astype(BF16)
    body = functools.partial(_mla_prep_body, heads=heads, q_lora=q_lora, kv_lora=kv_lora,
                             qscale=(nope + rope) ** -0.5 * LOG2E)
    head_major = pl.BlockSpec((1, heads, TOKEN_TILE, LANES), lambda b, j: (b, 0, j, 0))
    q_feature_major = pl.BlockSpec((1, heads, LANES, TOKEN_TILE), lambda b, j: (b, 0, 0, j))
    tok_table = pl.BlockSpec((TOKEN_TILE, LANES), lambda b, j: (j, 0))
    feat_table = pl.BlockSpec((LANES, TOKEN_TILE), lambda b, j: (0, j))
    return pl.pallas_call(
        body,
        out_shape=[jax.ShapeDtypeStruct((batch, heads, LANES, t_len), BF16),
                   jax.ShapeDtypeStruct((batch, heads, t_len, LANES), BF16),
                   jax.ShapeDtypeStruct((batch, heads * v_dim, t_len), BF16)],
        grid=(batch, nj + 1),
        in_specs=[pl.BlockSpec((TOKEN_TILE, pa.shape[1]), lambda b, j: (_tok_block(b, j, nj, batch), 0)),
                  tok_table, tok_table, feat_table, feat_table,
                  _full((1, q_lora)), _full((1, kv_lora)), _full(wuq_t.shape), _full(wuk.shape), _full(wuv_t.shape)],
        out_specs=[q_feature_major, head_major,
                   pl.BlockSpec((1, heads * v_dim, TOKEN_TILE), lambda b, j: (b, 0, j))],
        compiler_params=_params(2), name="mla_prep",
    )(pa, cos, sin, cos.T, sin.T, g_q.reshape(1, -1), g_kv.reshape(1, -1), wuq_t, wuk, wuv_t)


def _gqa_prep_body(pd_ref, cos_ref, sin_ref, gq_ref, gk_ref, q_ref, k_ref, *, heads, kv_heads, dim, qscale):
    cos, sin = cos_ref[...], sin_ref[...]

    def head(first_row, g_ref):
        x = pd_ref[0, first_row:first_row + dim, :]
        y = x * lax.rsqrt(jnp.mean(x * x, axis=0, keepdims=True) + EPS) * g_ref[...]
        return _rope_rows(y, cos, sin, dim // 4)

    zeros = jnp.zeros((dim, TOKEN_TILE), F32)
    for h in range(heads):
        parts = [zeros] * kv_heads
        parts[h // (heads // kv_heads)] = head(h * dim, gq_ref) * qscale
        q_ref[0, h] = jnp.concatenate(parts, axis=0).astype(BF16)
    keys = [head((heads + kv) * dim, gk_ref) for kv in range(kv_heads)]
    k_ref[0, 0] = jnp.concatenate(keys, axis=0).T.astype(BF16)


def _gqa_prep(pd_t, cos, sin, g_q, g_k, batch, seq, ctx_len, heads, kv_heads, head_dim):
    assert kv_heads * head_dim == LANES
    nj = seq // TOKEN_TILE
    t_len = seq + ctx_len
    body = functools.partial(_gqa_prep_body, heads=heads, kv_heads=kv_heads, dim=head_dim,
                             qscale=head_dim ** -0.5 * LOG2E)
    table = pl.BlockSpec((head_dim, TOKEN_TILE), lambda b, j: (0, j))
    gain = lambda g: jnp.broadcast_to(g[:, None], (head_dim, TOKEN_TILE))
    return pl.pallas_call(
        body,
        out_shape=[jax.ShapeDtypeStruct((batch, heads, LANES, t_len), BF16),
                   jax.ShapeDtypeStruct((batch, 1, t_len, LANES), BF16)],
        grid=(batch, nj + 1),
        in_specs=[pl.BlockSpec((1, pd_t.shape[1], TOKEN_TILE), lambda b, j: (b, 0, j)), table, table,
                  _full((head_dim, TOKEN_TILE)), _full((head_dim, TOKEN_TILE))],
        out_specs=[pl.BlockSpec((1, heads, LANES, TOKEN_TILE), lambda b, j: (b, 0, 0, j)),
                   pl.BlockSpec((1, 1, TOKEN_TILE, LANES), lambda b, j: (b, 0, j, 0))],
        compiler_params=_params(2), name="gqa_prep",
    )(pd_t, cos.T, sin.T, gain(g_q), gain(g_k))


def _score_pass(q, k_ref, k_head, chunks, s_ref):
    m = None
    for (st, sz) in chunks:
        s_t = _dot(k_ref[0, k_head, st:st + sz, :], q)
        s_ref[st:st + sz, :] = s_t
        cm = jnp.max(s_t, axis=0, keepdims=True)
        m = cm if m is None else jnp.maximum(m, cm)
    return m


def _value_pass(s_ref, m, vt_ref, v_rows, chunks):
    l = jnp.zeros((1, TOKEN_TILE), F32)
    acc = jnp.zeros((v_rows.stop - v_rows.start, TOKEN_TILE), F32)
    for (st, sz) in chunks:
        p = jnp.exp2(s_ref[st:st + sz, :] - m)
        l = l + jnp.sum(p, axis=0, keepdims=True)
        acc = acc + _dot(vt_ref[0, v_rows, st:st + sz], p.astype(BF16))
    return acc / l


def _gated_store(outs, g_ref, o_ref):
    o2 = jnp.concatenate(outs, axis=0).T
    o_ref[...] = (o2 * _silu(g_ref[...].astype(F32))).astype(o_ref.dtype)


def _flash_body(q_ref, k_ref, vt_ref, g_ref, o_ref, s_scr, m_scr, *, k_sel, v_off, v_dim, chunks):
    j = pl.program_id(2)

    @pl.when((pl.program_id(0) == 0) & (pl.program_id(1) == 0) & (j == 0))
    def _():
        s_scr[...] = jnp.zeros(s_scr.shape, F32)
        m_scr[...] = jnp.zeros(m_scr.shape, F32)

    def step(slot, prev):
        heads = (0, 1)
        q = [q_ref[0, a] for a in heads]
        m_prev = [m_scr[prev, a] for a in heads]
        m = [None, None]
        l = [jnp.zeros((1, TOKEN_TILE), F32) for _ in heads]
        acc = [jnp.zeros((v_dim, TOKEN_TILE), F32) for _ in heads]
        for (st, sz) in chunks:
            for a in heads:
                s_t = _dot(k_ref[0, k_sel[a], st:st + sz, :], q[a])
                s_scr[slot, a, st:st + sz, :] = s_t
                cm = jnp.max(s_t, axis=0, keepdims=True)
                m[a] = cm if m[a] is None else jnp.maximum(m[a], cm)
                p = jnp.exp2(s_scr[prev, a, st:st + sz, :] - m_prev[a])
                l[a] = l[a] + jnp.sum(p, axis=0, keepdims=True)
                acc[a] = acc[a] + _dot(vt_ref[0, v_off[a]:v_off[a] + v_dim, st:st + sz], p.astype(BF16))
        for a in heads:
            m_scr[slot, a] = m[a]
        _gated_store([acc[a] / l[a] for a in heads], g_ref, o_ref)

    for parity in (0, 1):
        pl.when(j % 2 == parity)(functools.partial(step, parity, 1 - parity))


def _flash_ctx_body(q_ref, k_ref, vt_ref, g_ref, o_ref, s_scr, *, k_sel, v_off, v_dim, chunks):
    outs = []
    for a in range(2):
        m = _score_pass(q_ref[0, a], k_ref, k_sel[a], chunks, s_scr.at[a])
        outs.append(_value_pass(s_scr.at[a], m, vt_ref, slice(v_off[a], v_off[a] + v_dim), chunks))
    _gated_store(outs, g_ref, o_ref)


def _flash(q, k, vt, gate, batch, seq, ctx_len, k_heads_per_pair, v_rows_per_pair, pairs_per_kv, ctx_queries):
    heads = q.shape[1]
    t_len = seq + ctx_len
    nj = seq // TOKEN_TILE
    v_dim = LANES // 2
    k_sel = (0, 1) if k_heads_per_pair == 2 else (0, 0)
    v_off = (0, v_dim) if v_rows_per_pair == 2 * v_dim else (0, 0)
    k_blocks = k.shape[1] // k_heads_per_pair
    k_block = lambda p: (p // pairs_per_kv) % k_blocks
    chunks = tuple((c * KEY_CHUNK, KEY_CHUNK) for c in range(seq // KEY_CHUNK)) + ((seq, ctx_len),)
    static = dict(k_sel=k_sel, v_off=v_off, v_dim=v_dim)

    width = gate.shape[1]
    done = lambda b, p, j: (b * nj + jnp.maximum(j - 1, 0), p)
    out = pl.pallas_call(
        functools.partial(_flash_body, chunks=chunks, **static),
        out_shape=jax.ShapeDtypeStruct((batch * seq, width), BF16),
        grid=(batch, heads // 2, nj + 1),
        in_specs=[pl.BlockSpec((1, 2, LANES, TOKEN_TILE), lambda b, p, j: (b, p, 0, jnp.minimum(j, nj - 1))),
                  pl.BlockSpec((1, k_heads_per_pair, t_len, LANES), lambda b, p, j: (b, k_block(p), 0, 0)),
                  pl.BlockSpec((1, v_rows_per_pair, t_len), lambda b, p, j: (b, p // pairs_per_kv, 0)),
                  pl.BlockSpec((TOKEN_TILE, LANES), done)],
        out_specs=pl.BlockSpec((TOKEN_TILE, LANES), done),
        scratch_shapes=[pltpu.VMEM((2, 2, t_len, TOKEN_TILE), F32), pltpu.VMEM((2, 2, 1, TOKEN_TILE), F32)],
        compiler_params=_params(3), name="flash_attention",
    )(q, k, vt, gate)
    if not ctx_queries:
        return (out,)

    out_ctx = pl.pallas_call(
        functools.partial(_flash_ctx_body, chunks=((0, ctx_len),), **static),
        out_shape=jax.ShapeDtypeStruct((batch * ctx_len, width), BF16),
        grid=(batch, heads // 2),
        in_specs=[pl.BlockSpec((1, 2, LANES, TOKEN_TILE), lambda b, p: (b, p, 0, nj)),
                  pl.BlockSpec((1, k_heads_per_pair, ctx_len, LANES), lambda b, p: (b, k_block(p), seq // ctx_len, 0)),
                  pl.BlockSpec((1, v_rows_per_pair, ctx_len), lambda b, p: (b, p // pairs_per_kv, seq // ctx_len)),
                  pl.BlockSpec((TOKEN_TILE, LANES), lambda b, p: (batch * nj + b, p))],
        out_specs=pl.BlockSpec((TOKEN_TILE, LANES), lambda b, p: (b, p)),
        scratch_shapes=[pltpu.VMEM((2, ctx_len, TOKEN_TILE), F32)],
        compiler_params=_params(2), name="flash_attention_ctx",
    )(q, k, vt, gate)
    return out, out_ctx


NA_KEY_ROWS = 12
NA_STEP_ROWS = TOKEN_TILE // GRID_W
NA_VARIANTS = 3


def _na_rel_row(variant, rr, a):
    if variant == 0:
        valid, dr = a < WIN_R, a - rr
    elif variant == 1:
        dr = a - WIN_R // 2 - rr
        valid = -(WIN_R // 2) <= dr < WIN_R // 2
    else:
        valid, dr = a >= NA_KEY_ROWS - WIN_R, a - (NA_KEY_ROWS - NA_STEP_ROWS) - rr
    return dr if valid else None


def _na_bias_body(rpb_ref, o_ref, blk_scr):
    p = pl.program_id(0)
    shape = (GRID_W, LANES)
    kc = lax.broadcasted_iota(jnp.int32, shape, 0)
    lane = lax.broadcasted_iota(jnp.int32, shape, 1)
    qc = lane % GRID_W
    upper = lane >= GRID_W
    rel = kc - qc + (WIN_C - 1)
    c0 = jnp.clip(qc - WIN_C // 2, 0, GRID_W - WIN_C)
    col_ok = (kc >= c0) & (kc < c0 + WIN_C)
    n_rel_r, n_rel_c = 2 * WIN_R - 1, 2 * WIN_C - 1

    def block(dd, carry):
        base0 = (2 * p) * (n_rel_r * n_rel_c) + dd * n_rel_c
        base1 = base0 + n_rel_r * n_rel_c
        val = jnp.zeros(shape, F32)
        for jj in range(n_rel_c):
            val = jnp.where(rel == jj, jnp.where(upper, rpb_ref[base1 + jj], rpb_ref[base0 + jj]), val)
        blk_scr[dd] = jnp.where(col_ok, val * LOG2E, NEG)
        return carry

    lax.fori_loop(0, n_rel_r, block, 0)
    outside = jnp.full(shape, NEG, F32)
    for variant in range(NA_VARIANTS):
        for rr in range(NA_STEP_ROWS):
            for a in range(NA_KEY_ROWS):
                dr = _na_rel_row(variant, rr, a)
                o_ref[0, variant, rr, a * GRID_W:(a + 1) * GRID_W, :] = (
                    outside if dr is None else blk_scr[dr + WIN_R - 1])


def _na_bias(rpb):
    heads = rpb.shape[0]
    tab = (NA_VARIANTS, NA_STEP_ROWS, NA_KEY_ROWS * GRID_W, LANES)
    return pl.pallas_call(
        _na_bias_body,
        out_shape=jax.ShapeDtypeStruct((heads // 2,) + tab, F32),
        grid=(heads // 2,),
        in_specs=[pl.BlockSpec(memory_space=pltpu.SMEM)],
        out_specs=pl.BlockSpec((1,) + tab, lambda p: (p, 0, 0, 0, 0)),
        scratch_shapes=[pltpu.VMEM((2 * WIN_R - 1, GRID_W, LANES), F32)],
        compiler_params=_params(1), name="na_bias",
    )(rpb.reshape(-1))


def _na_body(q_ref, kl_ref, kc_ref, vt_ref, bias_ref, g_ref, o_ref, sl_scr, sc_scr, m_scr, *, seq, ctx_len, n_steps):
    j = pl.program_id(2)
    n_rows = seq // GRID_W
    span = NA_KEY_ROWS * GRID_W
    lower = lax.broadcasted_iota(jnp.int32, (GRID_W, LANES), 1) < GRID_W
    lower_lanes = lax.broadcasted_iota(jnp.int32, (LANES, LANES), 1) < GRID_W
    lower_rows = lax.broadcasted_iota(jnp.int32, (LANES, LANES), 0) < GRID_W

    @pl.when((pl.program_id(0) == 0) & (pl.program_id(1) == 0) & (j == 0))
    def _():
        sl_scr[...] = jnp.zeros(sl_scr.shape, F32)
        sc_scr[...] = jnp.zeros(sc_scr.shape, F32)
        m_scr[...] = jnp.zeros(m_scr.shape, F32)

    def span_start(step):
        first_row = jnp.clip(NA_STEP_ROWS * step - WIN_R // 2, 0, n_rows - NA_KEY_ROWS)
        return pl.multiple_of(first_row * GRID_W, TOKEN_TILE)

    def step(slot, prev):
        js = jnp.minimum(j, n_steps - 1)
        variant = jnp.where(js == 0, 0, jnp.where(js == n_steps - 1, 2, 1))
        qt = q_ref[0].astype(F32)
        same_half = lower_rows == lower_lanes
        blocks = []
        for t in range(TOKEN_TILE // LANES):
            y = qt[:, t * LANES:(t + 1) * LANES]
            y_swapped = pltpu.roll(y, GRID_W, 1)
            for dup in (jnp.where(lower_lanes, y, y_swapped), jnp.where(lower_lanes, y_swapped, y)):
                blocks.append(jnp.where(same_half, dup, 0.0))
        q2t = jnp.concatenate(blocks, axis=1).astype(BF16)
        s_loc = _dot(kl_ref[pl.ds(span_start(js), span), :], q2t)
        s_loc = jnp.concatenate([s_loc[:, rr * LANES:(rr + 1) * LANES] + bias_ref[0, variant, rr]
                                 for rr in range(NA_STEP_ROWS)], axis=1)
        s_ctx = _dot(kc_ref[...], q2t)
        sl_scr[slot] = s_loc
        sc_scr[slot] = s_ctx
        m_scr[slot] = jnp.maximum(jnp.max(s_loc, axis=0, keepdims=True), jnp.max(s_ctx, axis=0, keepdims=True))

        m = m_scr[prev]
        p_loc = jnp.exp2(sl_scr[prev] - m)
        p_ctx = jnp.exp2(sc_scr[prev] - m)
        l = jnp.sum(p_loc, axis=0, keepdims=True) + jnp.sum(p_ctx, axis=0, keepdims=True)
        v_loc = vt_ref[0, :, pl.ds(span_start(jnp.maximum(j - 1, 0)), span)]
        r_t = _dot(v_loc, p_loc.astype(BF16)) + _dot(vt_ref[0, :, seq:seq + ctx_len], p_ctx.astype(BF16))
        r_n = (r_t / l).T
        outs = [jnp.where(lower, r_n[rr * LANES:rr * LANES + GRID_W], r_n[rr * LANES + GRID_W:(rr + 1) * LANES])
                for rr in range(NA_STEP_ROWS)]
        o = jnp.concatenate(outs, axis=0)
        o_ref[...] = (o * _silu(g_ref[...].astype(F32))).astype(o_ref.dtype)

    for parity in (0, 1):
        pl.when(j % 2 == parity)(functools.partial(step, parity, 1 - parity))


def _na_attention(qt, k, vt, bias, gate, batch, seq, ctx_len):
    pairs = qt.shape[1] // LANES
    t_len = seq + ctx_len
    n_steps = seq // TOKEN_TILE
    span = NA_KEY_ROWS * GRID_W
    cols = NA_STEP_ROWS * LANES
    done = lambda b, p, j: (b * n_steps + jnp.maximum(j - 1, 0), p)
    body = functools.partial(_na_body, seq=seq, ctx_len=ctx_len, n_steps=n_steps)
    return pl.pallas_call(
        body, out_shape=jax.ShapeDtypeStruct((batch * seq, qt.shape[1]), BF16),
        grid=(batch, pairs, n_steps + 1),
        in_specs=[pl.BlockSpec((1, LANES, TOKEN_TILE), lambda b, p, j: (b, p, jnp.minimum(j, n_steps - 1))),
                  pl.BlockSpec((seq, LANES), lambda b, p, j: (b, p)),
                  pl.BlockSpec((ctx_len, LANES), lambda b, p, j: (batch * seq // ctx_len + b, p)),
                  pl.BlockSpec((1, LANES, t_len), lambda b, p, j: (b, p, 0)),
                  pl.BlockSpec((1,) + bias.shape[1:], lambda b, p, j: (p, 0, 0, 0, 0)),
                  pl.BlockSpec((TOKEN_TILE, LANES), done)],
        out_specs=pl.BlockSpec((TOKEN_TILE, LANES), done),
        scratch_shapes=[pltpu.VMEM((2, span, cols), F32), pltpu.VMEM((2, ctx_len, cols), F32),
                        pltpu.VMEM((2, 1, cols), F32)],
        compiler_params=_params(3), name="neighborhood_attention",
    )(qt, k, k, vt, bias, gate)


def _seg_scans(jobs, use_max=False):
    vals = [v for v, _ in jobs]
    n = vals[0].shape[1]
    lane = lax.broadcasted_iota(jnp.int32, vals[0].shape, 1) % ML_CHUNK
    k = 1
    while k < ML_CHUNK:
        for i, (_, reverse) in enumerate(jobs):
            v = vals[i]
            if reverse:
                ok, shifted = lane < ML_CHUNK - k, pltpu.roll(v, n - k, 1)
            else:
                ok, shifted = lane >= k, pltpu.roll(v, k, 1)
            vals[i] = jnp.maximum(v, jnp.where(ok, shifted, NEG)) if use_max else v + jnp.where(ok, shifted, 0.0)
        k *= 2
    return vals


ML_GATE_ROWS = 40


def _mlstm_prep_body(u_ref, up_ref, un_ref, cw_ref, cb_ref, wqk_ref, wv_ref, wg_ref, bg_ref,
                     xc_ref, k_ref, qt_ref, vt_ref, pre_ref, *, nj, width, kscale):
    j = pl.program_id(1)
    u = u_ref[...]
    row = lax.broadcasted_iota(jnp.int32, u.shape, 0)
    prev = jnp.where((j > 0) & (j < nj), up_ref[7:8, :], 0.0)
    nxt = jnp.where(j < nj - 1, un_ref[0:1, :], 0.0)
    u_m1 = jnp.where(row == 0, prev, pltpu.roll(u, 1, 0))
    u_p1 = jnp.where(row == TOKEN_TILE - 1, nxt, pltpu.roll(u, TOKEN_TILE - 1, 0))
    cw = cw_ref[...]
    xc = _silu(u_m1 * cw[0:1] + u * cw[1:2] + u_p1 * cw[2:3] + cb_ref[...])
    xcb = xc.astype(BF16)
    xc_ref[...] = xcb
    qk = _dot(xcb, wqk_ref[...])
    v = _dot(u.astype(BF16), wv_ref[...])
    qb, kb, vb = qk[:, :width].astype(BF16), qk[:, width:].astype(BF16), v.astype(BF16)
    k_ref[...] = (qk[:, width:] * kscale).astype(BF16)
    qt_ref[0] = qk[:, :width].T.astype(BF16)
    vt_ref[0] = v.T.astype(BF16)
    pre_ref[0] = _dot_nt(wg_ref[0], qb) + _dot_nt(wg_ref[1], kb) + _dot_nt(wg_ref[2], vb) + bg_ref[...]


def _mlstm_gates_body(pre_ref, g_ref, gc_ref):
    pre = pre_ref[0]
    i8 = [pre[16 * d:16 * d + 8] for d in range(2)]
    f8 = [_log_sigmoid(pre[16 * d + 8:16 * d + 16]) for d in range(2)]
    b0, b0_rev, b1, b1_rev = _seg_scans([(f8[0], False), (f8[0], True), (f8[1], True), (f8[1], False)])
    b8, b_last = [b0, b1], [b0 + b0_rev - f8[0], b1 + b1_rev - f8[1]]
    r8 = [i8[d] - b8[d] for d in range(2)]
    c0, c0_rev, c1, c1_rev = _seg_scans([(r8[0], False), (r8[0], True), (r8[1], True), (r8[1], False)], use_max=True)
    c8, r_max = [c0, c1], [jnp.maximum(c0, c0_rev), jnp.maximum(c1, c1_rev)]
    for d in range(2):
        g_ref[0, d * ML_GATE_ROWS:(d + 1) * ML_GATE_ROWS] = jnp.concatenate(
            [-c8[d], b8[d] + c8[d], jnp.exp(r8[d] - r_max[d]), b_last[d], b_last[d] + r_max[d]], axis=0)
    pad = jnp.zeros((LANES - 16, pre.shape[1]), F32)
    gc_ref[0] = jnp.concatenate(r8 + [pad], axis=0).T


def _mlstm_gates(pre):
    batch, _, t_len = pre.shape
    return pl.pallas_call(
        _mlstm_gates_body,
        out_shape=[jax.ShapeDtypeStruct((batch, 2 * ML_GATE_ROWS, t_len), F32),
                   jax.ShapeDtypeStruct((batch, t_len, LANES), F32)],
        grid=(batch,),
        in_specs=[pl.BlockSpec((1,) + pre.shape[1:], lambda b: (b, 0, 0))],
        out_specs=[pl.BlockSpec((1, 2 * ML_GATE_ROWS, t_len), lambda b: (b, 0, 0)),
                   pl.BlockSpec((1, t_len, LANES), lambda b: (b, 0, 0))],
        compiler_params=_params(1), name="mlstm_gates",
    )(pre)


def _mlstm_prep(u, conv_w, conv_b, w_q, w_k, w_v, w_gate, b_gate, batch, seq, ctx_len):
    heads, hd = w_q.shape[0], w_q.shape[1]
    assert heads == ML_HEADS and hd == ML_CHUNK
    width = heads * hd
    nj = seq // TOKEN_TILE
    t_len = seq + ctx_len
    rows = u.shape[0]

    def block_diag(w):
        eye = jnp.eye(heads, dtype=w.dtype)
        return (eye[:, None, :, None] * w[:, :, None, :]).reshape(width, width)

    wqk = jnp.concatenate([block_diag(w_q), block_diag(w_k)], axis=1).astype(BF16)
    wv = block_diag(w_v).astype(BF16)
    wg = w_gate.reshape(2, heads, 3, hd, 2, heads).transpose(2, 0, 4, 5, 1, 3).reshape(3, 2, 2, heads, width)
    wg = jnp.pad(wg, ((0, 0), (0, 0), (0, 0), (0, 8 - heads), (0, 0))).reshape(3, 32, width).astype(BF16)
    bg = jnp.pad(b_gate.reshape(2, 2, heads), ((0, 0), (0, 0), (0, 8 - heads))).reshape(32, 1)
    n_halo = rows // 8
    tokb = lambda b, j: _tok_block(b, j, nj, batch)
    tile = pl.BlockSpec((TOKEN_TILE, width), lambda b, j: (tokb(b, j), 0))
    feat = pl.BlockSpec((1, width, TOKEN_TILE), lambda b, j: (b, 0, j))
    body = functools.partial(_mlstm_prep_body, nj=nj, width=width, kscale=hd ** -0.5)
    per_tile = TOKEN_TILE // 8
    return pl.pallas_call(
        body,
        out_shape=[jax.ShapeDtypeStruct((rows, width), BF16)] * 2
        + [jax.ShapeDtypeStruct((batch, width, t_len), BF16)] * 2
        + [jax.ShapeDtypeStruct((batch, 32, t_len), F32)],
        grid=(batch, nj + 1),
        in_specs=[tile,
                  pl.BlockSpec((8, width), lambda b, j: (jnp.maximum(tokb(b, j) * per_tile - 1, 0), 0)),
                  pl.BlockSpec((8, width), lambda b, j: (jnp.minimum((tokb(b, j) + 1) * per_tile, n_halo - 1), 0)),
                  _full((3, width)), _full((1, width)), _full(wqk.shape), _full(wv.shape),
                  _full(wg.shape), _full((32, 1))],
        out_specs=[tile, tile, feat, feat, pl.BlockSpec((1, 32, TOKEN_TILE), lambda b, j: (b, 0, j))],
        compiler_params=_params(2), name="mlstm_prep",
    )(u, u, u, conv_w, conv_b.reshape(1, width), wqk, wv, wg, bg)


def _mlstm_seq_body(kf, qtf, vtf, gf, gcf, kb, qtb, vtb, gb, gcb, hf_ref, hb_ref, c_s, n_s, m_s):
    t = pl.program_id(1)
    L = ML_CHUNK

    @pl.when(t == 0)
    def _():
        c_s[...] = jnp.zeros(c_s.shape, F32)
        n_s[...] = jnp.zeros(n_s.shape, F32)
        m_s[...] = jnp.zeros(m_s.shape, F32)

    si = lax.broadcasted_iota(jnp.int32, (L, L), 0)
    li = lax.broadcasted_iota(jnp.int32, (L, L), 1)
    streams = ((kf, qtf, vtf, gf, gcf, hf_ref, si <= li), (kb, qtb, vtb, gb, gcb, hb_ref, si >= li))
    units = []
    for d, (k_ref, qt_ref, vt_ref, g_ref, gc_ref, h_ref, incl) in enumerate(streams):
        g = g_ref[0]
        gc = gc_ref[0]
        for h in range(ML_HEADS):
            cols = slice(h * L, (h + 1) * L)
            k, qt, vt = k_ref[:, cols], qt_ref[0, cols, :], vt_ref[0, cols, :]
            neg_c, m_loc, w0, b_last, g_max = (g[8 * i + h:8 * i + h + 1] for i in range(5))
            r_col = gc[:, 8 * d + h:8 * d + h + 1]
            p0 = jnp.where(incl, jnp.exp(r_col + neg_c), 0.0) * _dot(k, qt)
            s_sum = jnp.sum(p0, axis=0, keepdims=True)
            intra = _dot(vt, p0.astype(BF16))
            c_inc = _dot((vt.astype(F32) * w0).astype(BF16), k)
            n_inc = _dot(jnp.broadcast_to(w0, (8, L)).astype(BF16), k)
            units.append((d * ML_HEADS + h, h_ref, cols, qt, neg_c, m_loc, b_last, g_max, s_sum, intra, c_inc, n_inc))
    for (idx, h_ref, cols, qt, neg_c, m_loc, b_last, g_max, s_sum, intra, c_inc, n_inc) in units:
        c_st, n_st, m_st = c_s[idx], n_s[idx], m_s[idx]
        cn = _dot(jnp.concatenate([c_st, n_st], axis=0).astype(BF16), qt)
        delta = jnp.maximum(m_st + neg_c, 0.0)
        e_intra = jnp.exp(-delta)
        w_inter = jnp.exp(m_st + neg_c - delta)
        num = w_inter * cn[:L] + e_intra * intra
        den = w_inter * cn[L:L + 1] + e_intra * s_sum
        h_ref[0, cols, :] = num / jnp.maximum(jnp.abs(den), jnp.exp(-(m_loc + delta)))
        m_new = jnp.maximum(b_last + m_st, g_max)
        decay = jnp.exp(b_last + m_st - m_new)
        gain = jnp.exp(g_max - m_new)
        c_s[idx] = decay * c_st + gain * c_inc
        n_s[idx] = decay * n_st + gain * n_inc
        m_s[idx] = m_new


def _mlstm_seq(k, qt, vt, gates, gcols, batch, seq, ctx_len):
    width = k.shape[1]
    L = ML_CHUNK
    n_lat, n_ctx = seq // L, ctx_len // L
    n_chunks = n_lat + n_ctx
    fwd = lambda t: (t + n_lat) % n_chunks
    bwd = lambda t: n_chunks - 1 - t
    rowblk = lambda b, c: jnp.where(c < n_lat, b * n_lat + c, batch * n_lat + b * n_ctx + (c - n_lat))

    def stream(chunk_of, d):
        feat = pl.BlockSpec((1, width, L), lambda b, t: (b, 0, chunk_of(t)))
        return [pl.BlockSpec((L, width), lambda b, t: (rowblk(b, chunk_of(t)), 0)), feat, feat,
                pl.BlockSpec((1, ML_GATE_ROWS, L), lambda b, t: (b, d, chunk_of(t))),
                pl.BlockSpec((1, L, LANES), lambda b, t: (b, chunk_of(t), 0))]

    out_f = pl.BlockSpec((1, width, L), lambda b, t: (b, 0, fwd(t)))
    out_b = pl.BlockSpec((1, width, L), lambda b, t: (b, 0, bwd(t)))
    n_state = 2 * ML_HEADS
    return pl.pallas_call(
        _mlstm_seq_body,
        out_shape=[jax.ShapeDtypeStruct(qt.shape, F32)] * 2,
        grid=(batch, n_chunks),
        in_specs=stream(fwd, 0) + stream(bwd, 1),
        out_specs=[out_f, out_b],
        scratch_shapes=[pltpu.VMEM((n_state, L, L), F32), pltpu.VMEM((n_state, 8, L), F32),
                        pltpu.VMEM((n_state, 1, L), F32)],
        compiler_params=_params(2), name="mlstm_recurrence",
    )(k, qt, vt, gates, gcols, k, qt, vt, gates, gcols)


def _mlstm_out_body(hf_ref, hb_ref, xc_ref, z_ref, gh_ref, sk_ref, o_ref):
    ht = hf_ref[0] + hb_ref[0]
    L = ML_CHUNK
    normed = []
    for hd in range(ML_HEADS):
        hh = ht[hd * L:(hd + 1) * L]
        mu = jnp.mean(hh, axis=0, keepdims=True)
        var = jnp.mean(jnp.square(hh - mu), axis=0, keepdims=True)
        normed.append((hh - mu) * lax.rsqrt(var + EPS))
    hn = jnp.concatenate(normed, axis=0).T * gh_ref[...]
    o = (hn + sk_ref[...] * xc_ref[...].astype(F32)) * _silu(z_ref[...].astype(F32))
    o_ref[...] = o.astype(o_ref.dtype)


def _mlstm_out(hf, hb, xc, z, g_head, skip, batch, seq):
    rows, width = xc.shape
    nj = seq // TOKEN_TILE
    tile = pl.BlockSpec((TOKEN_TILE, width), lambda b, j: (_tok_block(b, j, nj, batch), 0))
    feat = pl.BlockSpec((1, width, TOKEN_TILE), lambda b, j: (b, 0, j))
    return pl.pallas_call(
        _mlstm_out_body, out_shape=jax.ShapeDtypeStruct((rows, width), BF16), grid=(batch, nj + 1),
        in_specs=[feat, feat, tile, tile, _full((1, width)), _full((1, width))],
        out_specs=tile, compiler_params=_params(2), name="mlstm_out",
    )(hf, hb, xc, z, g_head.reshape(1, width), skip.reshape(1, width))


def kernel(x, c, ctx, c_ctx, w_mod, b_mod, g_norm, ab_w_in, ab_w_out, mla_g_q, mla_w_uq, mla_g_kv, mla_w_ukv,
           ml_conv_w, ml_conv_b, ml_w_q, ml_w_k, ml_w_v, ml_w_gate, ml_b_gate, ml_g_head, ml_skip,
           cd_w_in, cd_w_out, na_rpb, gqa_g_q, gqa_g_k, g_final):
    batch, seq, d = x.shape
    ctx_len = ctx.shape[1]
    assert ctx_len == TOKEN_TILE and seq % KEY_CHUNK == 0 and seq // GRID_W >= NA_KEY_ROWS
    dims = (batch, seq, ctx_len)

    mla_heads, mla_rope, mla_v = 8, 32, 64
    mla_nope = mla_w_uq.shape[2] // mla_heads - mla_rope
    q_lora, kv_lora = mla_g_q.shape[1], mla_g_kv.shape[1]
    ml_width = ml_conv_w.shape[2]
    mla_width = mla_heads * mla_v
    gqa_heads, gqa_dim = 8, gqa_g_q.shape[1]
    gqa_kv = (cd_w_in.shape[2] - 4 * 512 - 2 * gqa_heads * gqa_dim) // (2 * gqa_dim)
    na_width = na_rpb.shape[1] * 64

    mod_rows = -(-(batch + 1) // 8) * 8
    cvec = jnp.concatenate([c, c_ctx[None], jnp.zeros((mod_rows - batch - 1, d), F32)], axis=0)
    mod = _modulation(cvec, w_mod, b_mod)
    mod0 = mod[0].reshape(mod_rows, 1, 3 * d)
    mod1 = mod[1].reshape(mod_rows, 1, 3 * d)

    tok0 = _Tokens((x.reshape(batch * seq, d), ctx.reshape(batch * ctx_len, d)), *dims)
    w_in = ab_w_in[0]
    s1 = q_lora + kv_lora
    zcol = lambda n: jnp.zeros((d, n), w_in.dtype)
    w0 = jnp.concatenate([w_in[:, :s1], zcol(mla_nope), w_in[:, s1:s1 + mla_rope],
                          zcol(LANES - mla_nope - mla_rope), w_in[:, s1 + mla_rope:]], axis=1).astype(BF16)
    o_pa = s1 + LANES
    outs0 = ((0, o_pa, F32, 1.0, False), (o_pa, mla_width, BF16, 1.0, False),
             (o_pa + mla_width, ml_width, F32, 1.0, False), (o_pa + mla_width + ml_width, ml_width, BF16, 1.0, False))
    pa, gate_a, u, z = _in_proj(tok0, mod0, g_norm[0], w0, outs0, *dims)

    def mla_lanes(table, fill):
        n = table.shape[0]
        return jnp.concatenate([jnp.full((n, mla_nope), fill, F32), table,
                                jnp.full((n, LANES - mla_nope - mla_rope), fill, F32)], axis=-1)

    cos_a, sin_a = _rope_tables(seq, ctx_len, mla_rope)
    cos_a, sin_a = mla_lanes(cos_a, 1.0), mla_lanes(sin_a, 0.0)
    q_a, k_a, vt_a = _mla_prep(pa, cos_a, sin_a, mla_g_q[0], mla_g_kv[0], mla_w_uq[0], mla_w_ukv[0],
                               *dims, mla_heads, mla_nope, mla_rope, mla_v)
    mix_a = _flash(q_a, k_a, vt_a, gate_a, *dims, k_heads_per_pair=2, v_rows_per_pair=2 * mla_v,
                   pairs_per_kv=1, ctx_queries=True)

    xc, k_m, qt_m, vt_m, gate_pre = _mlstm_prep(u, ml_conv_w[0], ml_conv_b[0], ml_w_q[0], ml_w_k[0], ml_w_v[0],
                                                ml_w_gate[0], ml_b_gate[0], *dims)
    gates, gcols = _mlstm_gates(gate_pre)
    h_f, h_b = _mlstm_seq(k_m, qt_m, vt_m, gates, gcols, *dims)
    mix_b = _mlstm_out(h_f, h_b, xc, z, ml_g_head[0], ml_skip[0], batch, seq)
    x1 = _out_proj(tok0, _Tokens(mix_a, *dims), mix_b, ab_w_out[0], mod0, batch, seq)

    tok1 = _Tokens((x1,), *dims)
    w1 = cd_w_in[0].astype(BF16)
    gq_w, gkv_w = gqa_heads * gqa_dim, gqa_kv * gqa_dim
    o_d = 4 * na_width
    outs1 = ((0, na_width, BF16, 64 ** -0.5 * LOG2E, True), (na_width, na_width, BF16, 1.0, False),
             (2 * na_width, na_width, BF16, 1.0, True), (3 * na_width, na_width, BF16, 1.0, False),
             (o_d, gq_w + gkv_w, F32, 1.0, True), (o_d + gq_w + gkv_w, gkv_w, BF16, 1.0, True),
             (o_d + gq_w + 2 * gkv_w, gq_w, BF16, 1.0, False))
    q_c, k_c, vt_c, gate_c, pd_t, vt_d, gate_d = _in_proj(tok1, mod1, g_norm[1], w1, outs1, *dims)

    mix_c = _na_attention(q_c, k_c, vt_c, _na_bias(na_rpb[0]), gate_c, *dims)

    cos_d, sin_d = _rope_tables(seq, ctx_len, gqa_dim)
    q_d, k_d = _gqa_prep(pd_t, cos_d, sin_d, gqa_g_q[0], gqa_g_k[0], *dims, gqa_heads, gqa_kv, gqa_dim)
    mix_d = _flash(q_d, k_d, vt_d, gate_d, *dims, k_heads_per_pair=1, v_rows_per_pair=gqa_dim,
                   pairs_per_kv=gqa_heads // (2 * gqa_kv), ctx_queries=False)

    out = _out_proj(tok1, _Tokens((mix_c,), *dims), mix_d[0], cd_w_out[0], mod1, batch, seq, g_final=g_final)
    return out.reshape(batch, seq, d)
```

```python
import functools

import jax
import jax.numpy as jnp
from jax import lax
from jax.experimental import pallas as pl
from jax.experimental.pallas import tpu as pltpu

F32 = jnp.float32
BF16 = jnp.bfloat16

LANES = 128
TOKEN_TILE = 256
KEY_CHUNK = 512
FLASH_TILE = 512
GRID_W = 64
WIN_R = 8
WIN_C = 16
ML_CHUNK = 128
ML_HEADS = 4
EPS = 1e-6
ROPE_BASE = 10000.0
LOG2E = 1.4426950408889634
NEG = -1e30
VMEM_LIMIT = 56 * 1024 * 1024

_NT = (((1,), (1,)), ((), ()))


def _dot(a, b):
    return jnp.dot(a, b, preferred_element_type=F32)


def _dot_nt(a, b):
    return lax.dot_general(a, b, _NT, preferred_element_type=F32)


def _silu(v):
    return v * (1.0 / (1.0 + jnp.exp(-v)))


def _log_sigmoid(v):
    return -(jnp.maximum(-v, 0.0) + jnp.log1p(jnp.exp(-jnp.abs(v))))


def _params(n_axes):
    return pltpu.CompilerParams(dimension_semantics=("arbitrary",) * n_axes, vmem_limit_bytes=VMEM_LIMIT)


def _full(shape):
    nd = len(shape)
    return pl.BlockSpec(shape, lambda *_: (0,) * nd)


def _mod_body(c_ref, w_ref, b_ref, o_ref):
    s = _silu(c_ref[...])
    o_ref[0] = _dot(s.astype(BF16), w_ref[0].astype(BF16)) + b_ref[0]


def _modulation(cvec, w_mod, b_mod):
    depth, d, n = w_mod.shape
    rows = cvec.shape[0]
    tn = n // 4
    return pl.pallas_call(
        _mod_body,
        out_shape=jax.ShapeDtypeStruct((depth, rows, n), F32),
        grid=(depth, n // tn),
        in_specs=[_full((rows, d)),
                  pl.BlockSpec((1, d, tn), lambda l, j: (l, 0, j)),
                  pl.BlockSpec((1, 1, tn), lambda l, j: (l, 0, j))],
        out_specs=pl.BlockSpec((1, rows, tn), lambda l, j: (l, 0, j)),
        compiler_params=_params(2), name="modulation",
    )(cvec, w_mod, b_mod.reshape(depth, 1, n))


class _Tokens:
    def __init__(self, arrays, batch, seq, ctx_len):
        self.arrays = arrays
        self.split = len(arrays) == 2
        self.n_lat = batch * seq // TOKEN_TILE
        self.n_ctx = batch * ctx_len // TOKEN_TILE
        self.d = arrays[0].shape[-1]

    def specs(self):
        blk = (TOKEN_TILE, self.d)
        if not self.split:
            return [pl.BlockSpec(blk, lambda i: (i, 0))]
        n_lat = self.n_lat
        return [pl.BlockSpec(blk, lambda i: (jnp.minimum(i, n_lat - 1), 0)),
                pl.BlockSpec(blk, lambda i: (jnp.maximum(i - n_lat, 0), 0))]

    def load(self, refs, i):
        if not self.split:
            return refs[0][...]
        return jnp.where(i < self.n_lat, refs[0][...], refs[1][...])


def _mod_spec(n_lat, nj, batch, width):
    return pl.BlockSpec((1, 1, width), lambda i: (jnp.where(i < n_lat, i // nj, batch), 0, 0))


def _in_proj_body(*refs, tok, outs, d):
    n_tok = len(tok.arrays)
    mod_ref, g_ref, w_ref = refs[n_tok:n_tok + 3]
    o_refs = refs[n_tok + 3:]
    i = pl.program_id(0)
    x = tok.load(refs[:n_tok], i)
    y = x * lax.rsqrt(jnp.mean(x * x, axis=-1, keepdims=True) + EPS) * g_ref[...]
    mod = mod_ref[0]
    h = y * (1.0 + mod[:, d:2 * d]) + mod[:, :d]
    acc = _dot(h.astype(BF16), w_ref[...])
    for o_ref, (c0, width, _, scale, transposed) in zip(o_refs, outs):
        v = acc[:, c0:c0 + width]
        if scale != 1.0:
            v = v * scale
        if transposed:
            o_ref[0] = v.T.astype(o_ref.dtype)
        else:
            o_ref[...] = v.astype(o_ref.dtype)


def _in_proj(tok, mod_l, g, w, outs, batch, seq, ctx_len):
    d = tok.d
    nj = seq // TOKEN_TILE
    n_lat, n_all = tok.n_lat, tok.n_lat + tok.n_ctx
    rows = n_all * TOKEN_TILE
    t_len = seq + ctx_len
    out_shape, out_specs = [], []
    for (_, width, dtype, _, transposed) in outs:
        if transposed:
            out_shape.append(jax.ShapeDtypeStruct((batch, width, t_len), dtype))
            out_specs.append(pl.BlockSpec(
                (1, width, TOKEN_TILE),
                lambda i: (jnp.where(i < n_lat, i // nj, i - n_lat), 0, jnp.where(i < n_lat, i % nj, nj))))
        else:
            out_shape.append(jax.ShapeDtypeStruct((rows, width), dtype))
            out_specs.append(pl.BlockSpec((TOKEN_TILE, width), lambda i: (i, 0)))
    body = functools.partial(_in_proj_body, tok=tok, outs=outs, d=d)
    return pl.pallas_call(
        body, out_shape=out_shape, grid=(n_all,),
        in_specs=tok.specs() + [_mod_spec(n_lat, nj, batch, 3 * d), _full((1, d)), _full(w.shape)],
        out_specs=out_specs, compiler_params=_params(1), name="in_proj",
    )(*tok.arrays, mod_l, g.reshape(1, d), w)


def _mlstm_mix(hf_ref, hb_ref, xc_ref, z_ref, gh_ref, sk_ref):
    ht = hf_ref[0] + hb_ref[0]
    L = ML_CHUNK
    normed = []
    for hd in range(ML_HEADS):
        hh = ht[hd * L:(hd + 1) * L]
        mu = jnp.mean(hh, axis=0, keepdims=True)
        var = jnp.mean(jnp.square(hh - mu), axis=0, keepdims=True)
        normed.append((hh - mu) * lax.rsqrt(var + EPS))
    hn = jnp.concatenate(normed, axis=0).T * gh_ref[...]
    return ((hn + sk_ref[...] * xc_ref[...].astype(F32)) * _silu(z_ref[...].astype(F32))).astype(BF16)


def _out_proj_body(*refs, tok, mix_a, n_b, d, final):
    n_tok, n_a = len(tok.arrays), len(mix_a.arrays)
    b_refs = refs[n_tok + n_a:n_tok + n_a + n_b]
    wa_ref, wb_ref, mod_ref = refs[n_tok + n_a + n_b:n_tok + n_a + n_b + 3]
    rest = refs[n_tok + n_a + n_b + 3:]
    i = pl.program_id(0)
    x = tok.load(refs[:n_tok], i)
    mb = b_refs[0][...] if n_b == 1 else _mlstm_mix(*b_refs)
    acc = _dot(mix_a.load(refs[n_tok:n_tok + n_a], i), wa_ref[...]) + _dot(mb, wb_ref[...])
    xn = x + mod_ref[0][:, 2 * d:] * acc
    if final:
        gf_ref, o_ref = rest
        xn = xn * lax.rsqrt(jnp.mean(xn * xn, axis=-1, keepdims=True) + EPS) * gf_ref[...]
    else:
        (o_ref,) = rest
    o_ref[...] = xn


def _out_proj(tok, mix_a, mix_b, w_out, mod_l, batch, seq, g_final=None):
    d = tok.d
    half = mix_a.d
    nj = seq // TOKEN_TILE
    n_lat = tok.n_lat
    final = g_final is not None
    n_tiles = n_lat if final else n_lat + tok.n_ctx
    wa, wb = w_out[:half].astype(BF16), w_out[half:].astype(BF16)
    tile = lambda width: pl.BlockSpec((TOKEN_TILE, width), lambda i: (i, 0))
    if isinstance(mix_b, tuple):
        hf, hb, xc, z, g_head, skip = mix_b
        width = xc.shape[1]
        feat = pl.BlockSpec((1, width, TOKEN_TILE),
                            lambda i: (jnp.where(i < n_lat, i // nj, i - n_lat), 0, jnp.where(i < n_lat, i % nj, nj)))
        b_specs = [feat, feat, tile(width), tile(width), _full((1, width)), _full((1, width))]
        b_args = [hf, hb, xc, z, g_head.reshape(1, width), skip.reshape(1, width)]
    else:
        b_specs, b_args = [tile(half)], [mix_b]
    in_specs = tok.specs() + mix_a.specs() + b_specs + [_full(wa.shape), _full(wb.shape),
                                                        _mod_spec(n_lat, nj, batch, 3 * d)]
    args = list(tok.arrays) + list(mix_a.arrays) + b_args + [wa, wb, mod_l]
    if final:
        in_specs.append(_full((1, d)))
        args.append(g_final.reshape(1, d))
    body = functools.partial(_out_proj_body, tok=tok, mix_a=mix_a, n_b=len(b_args), d=d, final=final)
    return pl.pallas_call(
        body, out_shape=jax.ShapeDtypeStruct((n_tiles * TOKEN_TILE, d), F32), grid=(n_tiles,),
        in_specs=in_specs, out_specs=tile(d), compiler_params=_params(1), name="out_proj",
    )(*args)


def _rope_tables(seq, ctx_len, rot_dim):
    t = jnp.arange(seq)
    pos = jnp.stack([t // GRID_W, t % GRID_W], axis=-1).astype(F32)
    n_freq = rot_dim // 4
    inv = ROPE_BASE ** (-jnp.arange(n_freq, dtype=F32) / n_freq)
    ang = pos[:, :, None] * inv
    cos, sin = jnp.cos(ang), jnp.sin(ang)
    cos_t = jnp.concatenate([cos[:, 0], cos[:, 0], cos[:, 1], cos[:, 1]], axis=-1)
    sin_t = jnp.concatenate([-sin[:, 0], sin[:, 0], -sin[:, 1], sin[:, 1]], axis=-1)
    cos_t = jnp.concatenate([cos_t, jnp.ones((ctx_len, rot_dim), F32)], axis=0)
    sin_t = jnp.concatenate([sin_t, jnp.zeros((ctx_len, rot_dim), F32)], axis=0)
    return cos_t, sin_t


def _rope(x, cos, sin, dist):
    lane = lax.broadcasted_iota(jnp.int32, x.shape, 1)
    first = (lane % (2 * dist)) < dist
    partner = jnp.where(first, pltpu.roll(x, LANES - dist, 1), pltpu.roll(x, dist, 1))
    return x * cos + partner * sin


def _rope_rows(x, cos, sin, dist):
    n = x.shape[0] // dist
    partner = jnp.concatenate([x[(i ^ 1) * dist:((i ^ 1) + 1) * dist] for i in range(n)], axis=0)
    return x * cos + partner * sin


def _tok_block(b, j, nj, batch):
    return jnp.where(j < nj, b * nj + j, batch * nj + b)


def _mla_prep_body(pa_ref, cos_ref, sin_ref, cost_ref, sint_ref, gq_ref, gkv_ref, wuqt_ref, wuk_ref, wuvt_ref,
                   q_ref, k_ref, vt_ref, *, heads, q_lora, kv_lora, qscale):
    pa = pa_ref[...]

    def norm(v, g_ref):
        return v * lax.rsqrt(jnp.mean(v * v, axis=-1, keepdims=True) + EPS) * g_ref[...]

    cq_t = norm(pa[:, :q_lora], gq_ref).T.astype(BF16)
    q_all = _dot(wuqt_ref[...], cq_t)
    cos_t, sin_t = cost_ref[...], sint_ref[...]
    for h in range(heads):
        qh = _rope_rows(q_all[h * LANES:(h + 1) * LANES], cos_t, sin_t, 8)
        q_ref[0, h] = (qh * qscale).astype(BF16)
    ckv = norm(pa[:, q_lora:q_lora + kv_lora], gkv_ref)
    k_nope = _dot(ckv.astype(BF16), wuk_ref[...])
    k_rope = _rope(pa[:, q_lora + kv_lora:], cos_ref[...], sin_ref[...], 8)
    for h in range(heads):
        k_ref[0, h] = (k_nope[:, h * LANES:(h + 1) * LANES] + k_rope).astype(BF16)
    vt_ref[0] = _dot(wuvt_ref[...], ckv.T.astype(BF16)).astype(BF16)


def _mla_prep(pa, cos, sin, g_q, g_kv, w_uq, w_ukv, batch, seq, ctx_len, heads, nope, rope, v_dim):
    q_lora, kv_lora = g_q.shape[0], g_kv.shape[0]
    nj = seq // TOKEN_TILE
    t_len = seq + ctx_len
    pad = LANES - nope - rope
    wuq = jnp.pad(w_uq.reshape(q_lora, heads, nope + rope), ((0, 0), (0, 0), (0, pad)))
    wuq_t = wuq.reshape(q_lora, heads * LANES).T.astype(BF16)
    wkv = w_ukv.reshape(kv_lora, heads, nope + v_dim)
    wuk = jnp.pad(wkv[..., :nope], ((0, 0), (0, 0), (0, LANES - nope))).reshape(kv_lora, heads * LANES).astype(BF16)
    wuv_t = wkv[..., nope:].reshape(kv_lora, heads * v_dim).T.astype(BF16)
    body = functools.partial(_mla_prep_body, heads=heads, q_lora=q_lora, kv_lora=kv_lora,
                             qscale=(nope + rope) ** -0.5 * LOG2E)
    head_major = pl.BlockSpec((1, heads, TOKEN_TILE, LANES), lambda b, j: (b, 0, j, 0))
    q_feature_major = pl.BlockSpec((1, heads, LANES, TOKEN_TILE), lambda b, j: (b, 0, 0, j))
    tok_table = pl.BlockSpec((TOKEN_TILE, LANES), lambda b, j: (j, 0))
    feat_table = pl.BlockSpec((LANES, TOKEN_TILE), lambda b, j: (0, j))
    return pl.pallas_call(
        body,
        out_shape=[jax.ShapeDtypeStruct((batch, heads, LANES, t_len), BF16),
                   jax.ShapeDtypeStruct((batch, heads, t_len, LANES), BF16),
                   jax.ShapeDtypeStruct((batch, heads * v_dim, t_len), BF16)],
        grid=(batch, nj + 1),
        in_specs=[pl.BlockSpec((TOKEN_TILE, pa.shape[1]), lambda b, j: (_tok_block(b, j, nj, batch), 0)),
                  tok_table, tok_table, feat_table, feat_table,
                  _full((1, q_lora)), _full((1, kv_lora)), _full(wuq_t.shape), _full(wuk.shape), _full(wuv_t.shape)],
        out_specs=[q_feature_major, head_major,
                   pl.BlockSpec((1, heads * v_dim, TOKEN_TILE), lambda b, j: (b, 0, j))],
        compiler_params=_params(2), name="mla_prep",
    )(pa, cos, sin, cos.T, sin.T, g_q.reshape(1, -1), g_kv.reshape(1, -1), wuq_t, wuk, wuv_t)


def _gqa_prep_body(pd_ref, cos_ref, sin_ref, gq_ref, gk_ref, q_ref, k_ref, *, heads, kv_heads, dim, qscale):
    cos, sin = cos_ref[...], sin_ref[...]

    def head(first_row, g_ref):
        x = pd_ref[0, first_row:first_row + dim, :]
        y = x * lax.rsqrt(jnp.mean(x * x, axis=0, keepdims=True) + EPS) * g_ref[...]
        return _rope_rows(y, cos, sin, dim // 4)

    zeros = jnp.zeros((dim, TOKEN_TILE), F32)
    for h in range(heads):
        parts = [zeros] * kv_heads
        parts[h // (heads // kv_heads)] = head(h * dim, gq_ref) * qscale
        q_ref[0, h] = jnp.concatenate(parts, axis=0).astype(BF16)
    keys = [head((heads + kv) * dim, gk_ref) for kv in range(kv_heads)]
    k_ref[0, 0] = jnp.concatenate(keys, axis=0).T.astype(BF16)


def _gqa_prep(pd_t, cos, sin, g_q, g_k, batch, seq, ctx_len, heads, kv_heads, head_dim):
    assert kv_heads * head_dim == LANES
    nj = seq // TOKEN_TILE
    t_len = seq + ctx_len
    body = functools.partial(_gqa_prep_body, heads=heads, kv_heads=kv_heads, dim=head_dim,
                             qscale=head_dim ** -0.5 * LOG2E)
    table = pl.BlockSpec((head_dim, TOKEN_TILE), lambda b, j: (0, j))
    gain = lambda g: jnp.broadcast_to(g[:, None], (head_dim, TOKEN_TILE))
    return pl.pallas_call(
        body,
        out_shape=[jax.ShapeDtypeStruct((batch, heads, LANES, t_len), BF16),
                   jax.ShapeDtypeStruct((batch, 1, t_len, LANES), BF16)],
        grid=(batch, nj + 1),
        in_specs=[pl.BlockSpec((1, pd_t.shape[1], TOKEN_TILE), lambda b, j: (b, 0, j)), table, table,
                  _full((head_dim, TOKEN_TILE)), _full((head_dim, TOKEN_TILE))],
        out_specs=[pl.BlockSpec((1, heads, LANES, TOKEN_TILE), lambda b, j: (b, 0, 0, j)),
                   pl.BlockSpec((1, 1, TOKEN_TILE, LANES), lambda b, j: (b, 0, j, 0))],
        compiler_params=_params(2), name="gqa_prep",
    )(pd_t, cos.T, sin.T, gain(g_q), gain(g_k))


def _score_pass(q, k_ref, k_head, chunks, s_ref):
    m = None
    for (st, sz) in chunks:
        s_t = _dot(k_ref[0, k_head, st:st + sz, :], q)
        s_ref[st:st + sz, :] = s_t
        cm = jnp.max(s_t, axis=0, keepdims=True)
        m = cm if m is None else jnp.maximum(m, cm)
    return m


def _value_pass(s_ref, m, vt_ref, v_rows, chunks):
    l = jnp.zeros(m.shape, F32)
    acc = jnp.zeros((v_rows.stop - v_rows.start, m.shape[1]), F32)
    for (st, sz) in chunks:
        p = jnp.exp2(s_ref[st:st + sz, :] - m)
        l = l + jnp.sum(p, axis=0, keepdims=True)
        acc = acc + _dot(vt_ref[0, v_rows, st:st + sz], p.astype(BF16))
    return acc / l


def _gated_store(outs, g_ref, o_ref):
    o2 = jnp.concatenate(outs, axis=0).T
    o_ref[...] = (o2 * _silu(g_ref[...].astype(F32))).astype(o_ref.dtype)


def _flash_body(q_ref, k_ref, vt_ref, g_ref, o_ref, s_scr, m_scr, *, k_sel, v_off, v_dim, chunks):
    j = pl.program_id(2)

    @pl.when((pl.program_id(0) == 0) & (pl.program_id(1) == 0) & (j == 0))
    def _():
        s_scr[...] = jnp.zeros(s_scr.shape, F32)
        m_scr[...] = jnp.zeros(m_scr.shape, F32)

    def step(slot, prev):
        heads = (0, 1)
        q = [q_ref[0, a] for a in heads]
        m_prev = [m_scr[prev, a] for a in heads]
        m = [None, None]
        l = [jnp.zeros(m_prev[a].shape, F32) for a in heads]
        acc = [jnp.zeros((v_dim, m_prev[a].shape[1]), F32) for a in heads]
        for (st, sz) in chunks:
            for a in heads:
                s_t = _dot(k_ref[0, k_sel[a], st:st + sz, :], q[a])
                s_scr[slot, a, st:st + sz, :] = s_t
                cm = jnp.max(s_t, axis=0, keepdims=True)
                m[a] = cm if m[a] is None else jnp.maximum(m[a], cm)
                p = jnp.exp2(s_scr[prev, a, st:st + sz, :] - m_prev[a])
                l[a] = l[a] + jnp.sum(p, axis=0, keepdims=True)
                acc[a] = acc[a] + _dot(vt_ref[0, v_off[a]:v_off[a] + v_dim, st:st + sz], p.astype(BF16))
        for a in heads:
            m_scr[slot, a] = m[a]
        _gated_store([acc[a] / l[a] for a in heads], g_ref, o_ref)

    for parity in (0, 1):
        pl.when(j % 2 == parity)(functools.partial(step, parity, 1 - parity))


def _flash_ctx_body(q_ref, k_ref, vt_ref, g_ref, o_ref, s_scr, *, k_sel, v_off, v_dim, chunks):
    outs = []
    for a in range(2):
        m = _score_pass(q_ref[0, a], k_ref, k_sel[a], chunks, s_scr.at[a])
        outs.append(_value_pass(s_scr.at[a], m, vt_ref, slice(v_off[a], v_off[a] + v_dim), chunks))
    _gated_store(outs, g_ref, o_ref)


def _flash(q, k, vt, gate, batch, seq, ctx_len, k_heads_per_pair, v_rows_per_pair, pairs_per_kv, ctx_queries):
    heads = q.shape[1]
    t_len = seq + ctx_len
    nj = seq // TOKEN_TILE
    v_dim = LANES // 2
    k_sel = (0, 1) if k_heads_per_pair == 2 else (0, 0)
    v_off = (0, v_dim) if v_rows_per_pair == 2 * v_dim else (0, 0)
    k_blocks = k.shape[1] // k_heads_per_pair
    k_block = lambda p: (p // pairs_per_kv) % k_blocks
    chunks = tuple((c * KEY_CHUNK, KEY_CHUNK) for c in range(seq // KEY_CHUNK)) + ((seq, ctx_len),)
    static = dict(k_sel=k_sel, v_off=v_off, v_dim=v_dim)

    width = gate.shape[1]
    tq = FLASH_TILE
    n_q = seq // tq
    done = lambda b, p, j: (b * n_q + jnp.maximum(j - 1, 0), p)
    out = pl.pallas_call(
        functools.partial(_flash_body, chunks=chunks, **static),
        out_shape=jax.ShapeDtypeStruct((batch * seq, width), BF16),
        grid=(batch, heads // 2, n_q + 1),
        in_specs=[pl.BlockSpec((1, 2, LANES, tq), lambda b, p, j: (b, p, 0, jnp.minimum(j, n_q - 1))),
                  pl.BlockSpec((1, k_heads_per_pair, t_len, LANES), lambda b, p, j: (b, k_block(p), 0, 0)),
                  pl.BlockSpec((1, v_rows_per_pair, t_len), lambda b, p, j: (b, p // pairs_per_kv, 0)),
                  pl.BlockSpec((tq, LANES), done)],
        out_specs=pl.BlockSpec((tq, LANES), done),
        scratch_shapes=[pltpu.VMEM((2, 2, t_len, tq), F32), pltpu.VMEM((2, 2, 1, tq), F32)],
        compiler_params=_params(3), name="flash_attention",
    )(q, k, vt, gate)
    if not ctx_queries:
        return (out,)

    out_ctx = pl.pallas_call(
        functools.partial(_flash_ctx_body, chunks=((0, ctx_len),), **static),
        out_shape=jax.ShapeDtypeStruct((batch * ctx_len, width), BF16),
        grid=(batch, heads // 2),
        in_specs=[pl.BlockSpec((1, 2, LANES, TOKEN_TILE), lambda b, p: (b, p, 0, nj)),
                  pl.BlockSpec((1, k_heads_per_pair, ctx_len, LANES), lambda b, p: (b, k_block(p), seq // ctx_len, 0)),
                  pl.BlockSpec((1, v_rows_per_pair, ctx_len), lambda b, p: (b, p // pairs_per_kv, seq // ctx_len)),
                  pl.BlockSpec((TOKEN_TILE, LANES), lambda b, p: (batch * nj + b, p))],
        out_specs=pl.BlockSpec((TOKEN_TILE, LANES), lambda b, p: (b, p)),
        scratch_shapes=[pltpu.VMEM((2, ctx_len, TOKEN_TILE), F32)],
        compiler_params=_params(2), name="flash_attention_ctx",
    )(q, k, vt, gate)
    return out, out_ctx


NA_KEY_ROWS = 12
NA_STEP_ROWS = TOKEN_TILE // GRID_W
NA_VARIANTS = 3


def _na_rel_row(variant, rr, a):
    if variant == 0:
        valid, dr = a < WIN_R, a - rr
    elif variant == 1:
        dr = a - WIN_R // 2 - rr
        valid = -(WIN_R // 2) <= dr < WIN_R // 2
    else:
        valid, dr = a >= NA_KEY_ROWS - WIN_R, a - (NA_KEY_ROWS - NA_STEP_ROWS) - rr
    return dr if valid else None


def _na_bias_body(rpb_ref, o_ref, blk_scr):
    p = pl.program_id(0)
    shape = (GRID_W, LANES)
    kc = lax.broadcasted_iota(jnp.int32, shape, 0)
    lane = lax.broadcasted_iota(jnp.int32, shape, 1)
    qc = lane % GRID_W
    upper = lane >= GRID_W
    rel = kc - qc + (WIN_C - 1)
    c0 = jnp.clip(qc - WIN_C // 2, 0, GRID_W - WIN_C)
    col_ok = (kc >= c0) & (kc < c0 + WIN_C)
    n_rel_r, n_rel_c = 2 * WIN_R - 1, 2 * WIN_C - 1

    def block(dd, carry):
        base0 = (2 * p) * (n_rel_r * n_rel_c) + dd * n_rel_c
        base1 = base0 + n_rel_r * n_rel_c
        val = jnp.zeros(shape, F32)
        for jj in range(n_rel_c):
            val = jnp.where(rel == jj, jnp.where(upper, rpb_ref[base1 + jj], rpb_ref[base0 + jj]), val)
        blk_scr[dd] = jnp.where(col_ok, val * LOG2E, NEG)
        return carry

    lax.fori_loop(0, n_rel_r, block, 0)
    outside = jnp.full(shape, NEG, F32)
    for variant in range(NA_VARIANTS):
        for rr in range(NA_STEP_ROWS):
            for a in range(NA_KEY_ROWS):
                dr = _na_rel_row(variant, rr, a)
                o_ref[0, variant, rr, a * GRID_W:(a + 1) * GRID_W, :] = (
                    outside if dr is None else blk_scr[dr + WIN_R - 1])


def _na_bias(rpb):
    heads = rpb.shape[0]
    tab = (NA_VARIANTS, NA_STEP_ROWS, NA_KEY_ROWS * GRID_W, LANES)
    return pl.pallas_call(
        _na_bias_body,
        out_shape=jax.ShapeDtypeStruct((heads // 2,) + tab, F32),
        grid=(heads // 2,),
        in_specs=[pl.BlockSpec(memory_space=pltpu.SMEM)],
        out_specs=pl.BlockSpec((1,) + tab, lambda p: (p, 0, 0, 0, 0)),
        scratch_shapes=[pltpu.VMEM((2 * WIN_R - 1, GRID_W, LANES), F32)],
        compiler_params=_params(1), name="na_bias",
    )(rpb.reshape(-1))


def _na_body(q_ref, kl_ref, kc_ref, vt_ref, bias_ref, g_ref, o_ref, *, seq, ctx_len, n_steps):
    j = pl.program_id(2)
    n_rows = seq // GRID_W
    span = NA_KEY_ROWS * GRID_W
    lower = lax.broadcasted_iota(jnp.int32, (GRID_W, LANES), 1) < GRID_W
    first_row = jnp.clip(NA_STEP_ROWS * j - WIN_R // 2, 0, n_rows - NA_KEY_ROWS)
    k_off = pl.multiple_of(first_row * GRID_W, TOKEN_TILE)
    variant = jnp.where(j == 0, 0, jnp.where(j == n_steps - 1, 2, 1))
    q = q_ref[...]
    zero = jnp.zeros((GRID_W, LANES), q.dtype)
    parts = []
    for rr in range(NA_STEP_ROWS):
        q_r = q[rr * GRID_W:(rr + 1) * GRID_W]
        parts += [jnp.where(lower, q_r, zero), jnp.where(lower, zero, q_r)]
    q2 = jnp.concatenate(parts, axis=0)
    s_loc = _dot_nt(kl_ref[pl.ds(k_off, span), :], q2)
    s_loc = jnp.concatenate([s_loc[:, rr * LANES:(rr + 1) * LANES] + bias_ref[0, variant, rr]
                             for rr in range(NA_STEP_ROWS)], axis=1)
    s_ctx = _dot_nt(kc_ref[...], q2)
    m = jnp.maximum(jnp.max(s_loc, axis=0, keepdims=True), jnp.max(s_ctx, axis=0, keepdims=True))
    p_loc = jnp.exp2(s_loc - m)
    p_ctx = jnp.exp2(s_ctx - m)
    l = jnp.sum(p_loc, axis=0, keepdims=True) + jnp.sum(p_ctx, axis=0, keepdims=True)
    r_t = (_dot(vt_ref[0, :, pl.ds(k_off, span)], p_loc.astype(BF16))
           + _dot(vt_ref[0, :, seq:seq + ctx_len], p_ctx.astype(BF16)))
    r_n = (r_t / l).T
    outs = [jnp.where(lower, r_n[rr * LANES:rr * LANES + GRID_W], r_n[rr * LANES + GRID_W:(rr + 1) * LANES])
            for rr in range(NA_STEP_ROWS)]
    o = jnp.concatenate(outs, axis=0)
    o_ref[...] = (o * _silu(g_ref[...].astype(F32))).astype(o_ref.dtype)


def _na_attention(q, k, vt, bias, gate, batch, seq, ctx_len):
    pairs = q.shape[1] // LANES
    t_len = seq + ctx_len
    n_steps = seq // TOKEN_TILE
    tile = pl.BlockSpec((TOKEN_TILE, LANES), lambda b, p, j: (b * n_steps + j, p))
    body = functools.partial(_na_body, seq=seq, ctx_len=ctx_len, n_steps=n_steps)
    return pl.pallas_call(
        body, out_shape=jax.ShapeDtypeStruct((batch * seq, q.shape[1]), BF16),
        grid=(batch, pairs, n_steps),
        in_specs=[tile,
                  pl.BlockSpec((seq, LANES), lambda b, p, j: (b, p)),
                  pl.BlockSpec((ctx_len, LANES), lambda b, p, j: (batch * seq // ctx_len + b, p)),
                  pl.BlockSpec((1, LANES, t_len), lambda b, p, j: (b, p, 0)),
                  pl.BlockSpec((1,) + bias.shape[1:], lambda b, p, j: (p, 0, 0, 0, 0)),
                  tile],
        out_specs=tile,
        compiler_params=_params(3), name="neighborhood_attention",
    )(q, k, k, vt, bias, gate)


def _seg_scans(jobs, use_max=False):
    vals = [v for v, _ in jobs]
    n = vals[0].shape[1]
    lane = lax.broadcasted_iota(jnp.int32, vals[0].shape, 1) % ML_CHUNK
    k = 1
    while k < ML_CHUNK:
        for i, (_, reverse) in enumerate(jobs):
            v = vals[i]
            if reverse:
                ok, shifted = lane < ML_CHUNK - k, pltpu.roll(v, n - k, 1)
            else:
                ok, shifted = lane >= k, pltpu.roll(v, k, 1)
            vals[i] = jnp.maximum(v, jnp.where(ok, shifted, NEG)) if use_max else v + jnp.where(ok, shifted, 0.0)
        k *= 2
    return vals


ML_GATE_ROWS = 40


def _mlstm_prep_body(u_ref, up_ref, un_ref, cw_ref, cb_ref, wqk_ref, wv_ref, wg_ref, bg_ref,
                     xc_ref, k_ref, qt_ref, vt_ref, pre_ref, *, nj, width, kscale):
    j = pl.program_id(1)
    u = u_ref[...]
    row = lax.broadcasted_iota(jnp.int32, u.shape, 0)
    prev = jnp.where((j > 0) & (j < nj), up_ref[7:8, :], 0.0)
    nxt = jnp.where(j < nj - 1, un_ref[0:1, :], 0.0)
    u_m1 = jnp.where(row == 0, prev, pltpu.roll(u, 1, 0))
    u_p1 = jnp.where(row == TOKEN_TILE - 1, nxt, pltpu.roll(u, TOKEN_TILE - 1, 0))
    cw = cw_ref[...]
    xc = _silu(u_m1 * cw[0:1] + u * cw[1:2] + u_p1 * cw[2:3] + cb_ref[...])
    xcb = xc.astype(BF16)
    xc_ref[...] = xcb
    qk = _dot(xcb, wqk_ref[...])
    v = _dot(u.astype(BF16), wv_ref[...])
    qb, kb, vb = qk[:, :width].astype(BF16), qk[:, width:].astype(BF16), v.astype(BF16)
    k_ref[...] = (qk[:, width:] * kscale).astype(BF16)
    qt_ref[0] = qk[:, :width].T.astype(BF16)
    vt_ref[0] = v.T.astype(BF16)
    pre_ref[0] = _dot_nt(wg_ref[0], qb) + _dot_nt(wg_ref[1], kb) + _dot_nt(wg_ref[2], vb) + bg_ref[...]


def _mlstm_gates_body(pre_ref, g_ref, gc_ref):
    pre = pre_ref[0]
    i8 = [pre[16 * d:16 * d + 8] for d in range(2)]
    f8 = [_log_sigmoid(pre[16 * d + 8:16 * d + 16]) for d in range(2)]
    b0, b0_rev, b1, b1_rev = _seg_scans([(f8[0], False), (f8[0], True), (f8[1], True), (f8[1], False)])
    b8, b_last = [b0, b1], [b0 + b0_rev - f8[0], b1 + b1_rev - f8[1]]
    r8 = [i8[d] - b8[d] for d in range(2)]
    c0, c0_rev, c1, c1_rev = _seg_scans([(r8[0], False), (r8[0], True), (r8[1], True), (r8[1], False)], use_max=True)
    c8, r_max = [c0, c1], [jnp.maximum(c0, c0_rev), jnp.maximum(c1, c1_rev)]
    for d in range(2):
        g_ref[0, d * ML_GATE_ROWS:(d + 1) * ML_GATE_ROWS] = jnp.concatenate(
            [-c8[d], b8[d] + c8[d], jnp.exp(r8[d] - r_max[d]), b_last[d], b_last[d] + r_max[d]], axis=0)
    pad = jnp.zeros((LANES - 16, pre.shape[1]), F32)
    gc_ref[0] = jnp.concatenate(r8 + [pad], axis=0).T


def _mlstm_gates(pre):
    batch, _, t_len = pre.shape
    return pl.pallas_call(
        _mlstm_gates_body,
        out_shape=[jax.ShapeDtypeStruct((batch, 2 * ML_GATE_ROWS, t_len), F32),
                   jax.ShapeDtypeStruct((batch, t_len, LANES), F32)],
        grid=(batch,),
        in_specs=[pl.BlockSpec((1,) + pre.shape[1:], lambda b: (b, 0, 0))],
        out_specs=[pl.BlockSpec((1, 2 * ML_GATE_ROWS, t_len), lambda b: (b, 0, 0)),
                   pl.BlockSpec((1, t_len, LANES), lambda b: (b, 0, 0))],
        compiler_params=_params(1), name="mlstm_gates",
    )(pre)


def _mlstm_prep(u, conv_w, conv_b, w_q, w_k, w_v, w_gate, b_gate, batch, seq, ctx_len):
    heads, hd = w_q.shape[0], w_q.shape[1]
    assert heads == ML_HEADS and hd == ML_CHUNK
    width = heads * hd
    nj = seq // TOKEN_TILE
    t_len = seq + ctx_len
    rows = u.shape[0]

    def block_diag(w):
        eye = jnp.eye(heads, dtype=w.dtype)
        return (eye[:, None, :, None] * w[:, :, None, :]).reshape(width, width)

    wqk = jnp.concatenate([block_diag(w_q), block_diag(w_k)], axis=1).astype(BF16)
    wv = block_diag(w_v).astype(BF16)
    wg = w_gate.reshape(2, heads, 3, hd, 2, heads).transpose(2, 0, 4, 5, 1, 3).reshape(3, 2, 2, heads, width)
    wg = jnp.pad(wg, ((0, 0), (0, 0), (0, 0), (0, 8 - heads), (0, 0))).reshape(3, 32, width).astype(BF16)
    bg = jnp.pad(b_gate.reshape(2, 2, heads), ((0, 0), (0, 0), (0, 8 - heads))).reshape(32, 1)
    n_halo = rows // 8
    tokb = lambda b, j: _tok_block(b, j, nj, batch)
    tile = pl.BlockSpec((TOKEN_TILE, width), lambda b, j: (tokb(b, j), 0))
    feat = pl.BlockSpec((1, width, TOKEN_TILE), lambda b, j: (b, 0, j))
    body = functools.partial(_mlstm_prep_body, nj=nj, width=width, kscale=hd ** -0.5)
    per_tile = TOKEN_TILE // 8
    return pl.pallas_call(
        body,
        out_shape=[jax.ShapeDtypeStruct((rows, width), BF16)] * 2
        + [jax.ShapeDtypeStruct((batch, width, t_len), BF16)] * 2
        + [jax.ShapeDtypeStruct((batch, 32, t_len), F32)],
        grid=(batch, nj + 1),
        in_specs=[tile,
                  pl.BlockSpec((8, width), lambda b, j: (jnp.maximum(tokb(b, j) * per_tile - 1, 0), 0)),
                  pl.BlockSpec((8, width), lambda b, j: (jnp.minimum((tokb(b, j) + 1) * per_tile, n_halo - 1), 0)),
                  _full((3, width)), _full((1, width)), _full(wqk.shape), _full(wv.shape),
                  _full(wg.shape), _full((32, 1))],
        out_specs=[tile, tile, feat, feat, pl.BlockSpec((1, 32, TOKEN_TILE), lambda b, j: (b, 0, j))],
        compiler_params=_params(2), name="mlstm_prep",
    )(u, u, u, conv_w, conv_b.reshape(1, width), wqk, wv, wg, bg)


def _mlstm_seq_body(kf, qtf, vtf, gf, gcf, kb, qtb, vtb, gb, gcb, hf_ref, hb_ref, c_s, n_s, m_s):
    t = pl.program_id(1)
    L = ML_CHUNK

    @pl.when(t == 0)
    def _():
        c_s[...] = jnp.zeros(c_s.shape, F32)
        n_s[...] = jnp.zeros(n_s.shape, F32)
        m_s[...] = jnp.zeros(m_s.shape, F32)

    si = lax.broadcasted_iota(jnp.int32, (L, L), 0)
    li = lax.broadcasted_iota(jnp.int32, (L, L), 1)
    streams = ((kf, qtf, vtf, gf, gcf, hf_ref, si <= li), (kb, qtb, vtb, gb, gcb, hb_ref, si >= li))
    units = []
    for d, (k_ref, qt_ref, vt_ref, g_ref, gc_ref, h_ref, incl) in enumerate(streams):
        g = g_ref[0]
        gc = gc_ref[0]
        for h in range(ML_HEADS):
            cols = slice(h * L, (h + 1) * L)
            k, qt, vt = k_ref[:, cols], qt_ref[0, cols, :], vt_ref[0, cols, :]
            neg_c, m_loc, w0, b_last, g_max = (g[8 * i + h:8 * i + h + 1] for i in range(5))
            r_col = gc[:, 8 * d + h:8 * d + h + 1]
            p0 = jnp.where(incl, jnp.exp(r_col + neg_c), 0.0) * _dot(k, qt)
            s_sum = jnp.sum(p0, axis=0, keepdims=True)
            intra = _dot(vt, p0.astype(BF16))
            c_inc = _dot((vt.astype(F32) * w0).astype(BF16), k)
            n_inc = _dot(jnp.broadcast_to(w0, (8, L)).astype(BF16), k)
            units.append((d * ML_HEADS + h, h_ref, cols, qt, neg_c, m_loc, b_last, g_max, s_sum, intra, c_inc, n_inc))
    for (idx, h_ref, cols, qt, neg_c, m_loc, b_last, g_max, s_sum, intra, c_inc, n_inc) in units:
        c_st, n_st, m_st = c_s[idx], n_s[idx], m_s[idx]
        cn = _dot(jnp.concatenate([c_st, n_st], axis=0).astype(BF16), qt)
        delta = jnp.maximum(m_st + neg_c, 0.0)
        e_intra = jnp.exp(-delta)
        w_inter = jnp.exp(m_st + neg_c - delta)
        num = w_inter * cn[:L] + e_intra * intra
        den = w_inter * cn[L:L + 1] + e_intra * s_sum
        h_ref[0, cols, :] = num / jnp.maximum(jnp.abs(den), jnp.exp(-(m_loc + delta)))
        m_new = jnp.maximum(b_last + m_st, g_max)
        decay = jnp.exp(b_last + m_st - m_new)
        gain = jnp.exp(g_max - m_new)
        c_s[idx] = decay * c_st + gain * c_inc
        n_s[idx] = decay * n_st + gain * n_inc
        m_s[idx] = m_new


def _mlstm_seq(k, qt, vt, gates, gcols, batch, seq, ctx_len):
    width = k.shape[1]
    L = ML_CHUNK
    n_lat, n_ctx = seq // L, ctx_len // L
    n_chunks = n_lat + n_ctx
    fwd = lambda t: (t + n_lat) % n_chunks
    bwd = lambda t: n_chunks - 1 - t
    rowblk = lambda b, c: jnp.where(c < n_lat, b * n_lat + c, batch * n_lat + b * n_ctx + (c - n_lat))

    def stream(chunk_of, d):
        feat = pl.BlockSpec((1, width, L), lambda b, t: (b, 0, chunk_of(t)))
        return [pl.BlockSpec((L, width), lambda b, t: (rowblk(b, chunk_of(t)), 0)), feat, feat,
                pl.BlockSpec((1, ML_GATE_ROWS, L), lambda b, t: (b, d, chunk_of(t))),
                pl.BlockSpec((1, L, LANES), lambda b, t: (b, chunk_of(t), 0))]

    out_f = pl.BlockSpec((1, width, L), lambda b, t: (b, 0, fwd(t)))
    out_b = pl.BlockSpec((1, width, L), lambda b, t: (b, 0, bwd(t)))
    n_state = 2 * ML_HEADS
    return pl.pallas_call(
        _mlstm_seq_body,
        out_shape=[jax.ShapeDtypeStruct(qt.shape, F32)] * 2,
        grid=(batch, n_chunks),
        in_specs=stream(fwd, 0) + stream(bwd, 1),
        out_specs=[out_f, out_b],
        scratch_shapes=[pltpu.VMEM((n_state, L, L), F32), pltpu.VMEM((n_state, 8, L), F32),
                        pltpu.VMEM((n_state, 1, L), F32)],
        compiler_params=_params(2), name="mlstm_recurrence",
    )(k, qt, vt, gates, gcols, k, qt, vt, gates, gcols)


def kernel(x, c, ctx, c_ctx, w_mod, b_mod, g_norm, ab_w_in, ab_w_out, mla_g_q, mla_w_uq, mla_g_kv, mla_w_ukv,
           ml_conv_w, ml_conv_b, ml_w_q, ml_w_k, ml_w_v, ml_w_gate, ml_b_gate, ml_g_head, ml_skip,
           cd_w_in, cd_w_out, na_rpb, gqa_g_q, gqa_g_k, g_final):
    batch, seq, d = x.shape
    ctx_len = ctx.shape[1]
    assert ctx_len == TOKEN_TILE and seq % KEY_CHUNK == 0 and seq // GRID_W >= NA_KEY_ROWS
    dims = (batch, seq, ctx_len)

    mla_heads, mla_rope, mla_v = 8, 32, 64
    mla_nope = mla_w_uq.shape[2] // mla_heads - mla_rope
    q_lora, kv_lora = mla_g_q.shape[1], mla_g_kv.shape[1]
    ml_width = ml_conv_w.shape[2]
    mla_width = mla_heads * mla_v
    gqa_heads, gqa_dim = 8, gqa_g_q.shape[1]
    gqa_kv = (cd_w_in.shape[2] - 4 * 512 - 2 * gqa_heads * gqa_dim) // (2 * gqa_dim)
    na_width = na_rpb.shape[1] * 64

    mod_rows = -(-(batch + 1) // 8) * 8
    cvec = jnp.concatenate([c, c_ctx[None], jnp.zeros((mod_rows - batch - 1, d), F32)], axis=0)
    mod = _modulation(cvec, w_mod, b_mod)
    mod0 = mod[0].reshape(mod_rows, 1, 3 * d)
    mod1 = mod[1].reshape(mod_rows, 1, 3 * d)

    tok0 = _Tokens((x.reshape(batch * seq, d), ctx.reshape(batch * ctx_len, d)), *dims)
    w_in = ab_w_in[0]
    s1 = q_lora + kv_lora
    zcol = lambda n: jnp.zeros((d, n), w_in.dtype)
    w0 = jnp.concatenate([w_in[:, :s1], zcol(mla_nope), w_in[:, s1:s1 + mla_rope],
                          zcol(LANES - mla_nope - mla_rope), w_in[:, s1 + mla_rope:]], axis=1).astype(BF16)
    o_pa = s1 + LANES
    outs0 = ((0, o_pa, F32, 1.0, False), (o_pa, mla_width, BF16, 1.0, False),
             (o_pa + mla_width, ml_width, F32, 1.0, False), (o_pa + mla_width + ml_width, ml_width, BF16, 1.0, False))
    pa, gate_a, u, z = _in_proj(tok0, mod0, g_norm[0], w0, outs0, *dims)

    def mla_lanes(table, fill):
        n = table.shape[0]
        return jnp.concatenate([jnp.full((n, mla_nope), fill, F32), table,
                                jnp.full((n, LANES - mla_nope - mla_rope), fill, F32)], axis=-1)

    cos_a, sin_a = _rope_tables(seq, ctx_len, mla_rope)
    cos_a, sin_a = mla_lanes(cos_a, 1.0), mla_lanes(sin_a, 0.0)
    q_a, k_a, vt_a = _mla_prep(pa, cos_a, sin_a, mla_g_q[0], mla_g_kv[0], mla_w_uq[0], mla_w_ukv[0],
                               *dims, mla_heads, mla_nope, mla_rope, mla_v)
    mix_a = _flash(q_a, k_a, vt_a, gate_a, *dims, k_heads_per_pair=2, v_rows_per_pair=2 * mla_v,
                   pairs_per_kv=1, ctx_queries=True)

    xc, k_m, qt_m, vt_m, gate_pre = _mlstm_prep(u, ml_conv_w[0], ml_conv_b[0], ml_w_q[0], ml_w_k[0], ml_w_v[0],
                                                ml_w_gate[0], ml_b_gate[0], *dims)
    gates, gcols = _mlstm_gates(gate_pre)
    h_f, h_b = _mlstm_seq(k_m, qt_m, vt_m, gates, gcols, *dims)
    x1 = _out_proj(tok0, _Tokens(mix_a, *dims), (h_f, h_b, xc, z, ml_g_head[0], ml_skip[0]),
                   ab_w_out[0], mod0, batch, seq)

    tok1 = _Tokens((x1,), *dims)
    w1 = cd_w_in[0].astype(BF16)
    gq_w, gkv_w = gqa_heads * gqa_dim, gqa_kv * gqa_dim
    o_d = 4 * na_width
    outs1 = ((0, na_width, BF16, 64 ** -0.5 * LOG2E, False), (na_width, na_width, BF16, 1.0, False),
             (2 * na_width, na_width, BF16, 1.0, True), (3 * na_width, na_width, BF16, 1.0, False),
             (o_d, gq_w + gkv_w, F32, 1.0, True), (o_d + gq_w + gkv_w, gkv_w, BF16, 1.0, True),
             (o_d + gq_w + 2 * gkv_w, gq_w, BF16, 1.0, False))
    q_c, k_c, vt_c, gate_c, pd_t, vt_d, gate_d = _in_proj(tok1, mod1, g_norm[1], w1, outs1, *dims)

    mix_c = _na_attention(q_c, k_c, vt_c, _na_bias(na_rpb[0]), gate_c, *dims)

    cos_d, sin_d = _rope_tables(seq, ctx_len, gqa_dim)
    q_d, k_d = _gqa_prep(pd_t, cos_d, sin_d, gqa_g_q[0], gqa_g_k[0], *dims, gqa_heads, gqa_kv, gqa_dim)
    mix_d = _flash(q_d, k_d, vt_d, gate_d, *dims, k_heads_per_pair=1, v_rows_per_pair=gqa_dim,
                   pairs_per_kv=gqa_heads // (2 * gqa_kv), ctx_queries=False)

    out = _out_proj(tok1, _Tokens((mix_c,), *dims), mix_d[0], cd_w_out[0], mod1, batch, seq, g_final=g_final)
    return out.reshape(batch, seq, d)
```

```python
import functools

import jax
import jax.numpy as jnp
from jax import lax
from jax.experimental import pallas as pl
from jax.experimental.pallas import tpu as pltpu

F32 = jnp.float32
BF16 = jnp.bfloat16

LANES = 128
TOKEN_TILE = 256
KEY_CHUNK = 256
FLASH_TILE = 256
GRID_W = 64
WIN_R = 8
WIN_C = 16
ML_CHUNK = 128
ML_HEADS = 4
EPS = 1e-6
ROPE_BASE = 10000.0
LOG2E = 1.4426950408889634
NEG = -1e30
VMEM_LIMIT = 56 * 1024 * 1024

_NT = (((1,), (1,)), ((), ()))


def _dot(a, b):
    return jnp.dot(a, b, preferred_element_type=F32)


def _dot_nt(a, b):
    return lax.dot_general(a, b, _NT, preferred_element_type=F32)


def _silu(v):
    return v * (1.0 / (1.0 + jnp.exp(-v)))


def _log_sigmoid(v):
    return -(jnp.maximum(-v, 0.0) + jnp.log1p(jnp.exp(-jnp.abs(v))))


def _params(n_axes):
    return pltpu.CompilerParams(dimension_semantics=("arbitrary",) * n_axes, vmem_limit_bytes=VMEM_LIMIT)


def _full(shape):
    nd = len(shape)
    return pl.BlockSpec(shape, lambda *_: (0,) * nd)


def _mod_body(c_ref, w_ref, b_ref, o_ref):
    s = _silu(c_ref[...])
    o_ref[0] = _dot(s.astype(BF16), w_ref[0].astype(BF16)) + b_ref[0]


def _modulation(cvec, w_mod, b_mod):
    depth, d, n = w_mod.shape
    rows = cvec.shape[0]
    tn = n // 4
    return pl.pallas_call(
        _mod_body,
        out_shape=jax.ShapeDtypeStruct((depth, rows, n), F32),
        grid=(depth, n // tn),
        in_specs=[_full((rows, d)),
                  pl.BlockSpec((1, d, tn), lambda l, j: (l, 0, j)),
                  pl.BlockSpec((1, 1, tn), lambda l, j: (l, 0, j))],
        out_specs=pl.BlockSpec((1, rows, tn), lambda l, j: (l, 0, j)),
        compiler_params=_params(2), name="modulation",
    )(cvec, w_mod, b_mod.reshape(depth, 1, n))


class _Tokens:
    def __init__(self, arrays, batch, seq, ctx_len):
        self.arrays = arrays
        self.split = len(arrays) == 2
        self.n_lat = batch * seq // TOKEN_TILE
        self.n_ctx = batch * ctx_len // TOKEN_TILE
        self.d = arrays[0].shape[-1]

    def specs(self):
        blk = (TOKEN_TILE, self.d)
        if not self.split:
            return [pl.BlockSpec(blk, lambda i: (i, 0))]
        n_lat = self.n_lat
        return [pl.BlockSpec(blk, lambda i: (jnp.minimum(i, n_lat - 1), 0)),
                pl.BlockSpec(blk, lambda i: (jnp.maximum(i - n_lat, 0), 0))]

    def load(self, refs, i):
        if not self.split:
            return refs[0][...]
        return jnp.where(i < self.n_lat, refs[0][...], refs[1][...])


def _mod_spec(n_lat, nj, batch, width):
    return pl.BlockSpec((1, 1, width), lambda i: (jnp.where(i < n_lat, i // nj, batch), 0, 0))


def _in_proj_body(*refs, tok, outs, d):
    n_tok = len(tok.arrays)
    mod_ref, g_ref, w_ref = refs[n_tok:n_tok + 3]
    o_refs = refs[n_tok + 3:]
    i = pl.program_id(0)
    x = tok.load(refs[:n_tok], i)
    y = x * lax.rsqrt(jnp.mean(x * x, axis=-1, keepdims=True) + EPS) * g_ref[...]
    mod = mod_ref[0]
    h = y * (1.0 + mod[:, d:2 * d]) + mod[:, :d]
    acc = _dot(h.astype(BF16), w_ref[...])
    for o_ref, (c0, width, _, scale, transposed) in zip(o_refs, outs):
        v = acc[:, c0:c0 + width]
        if scale != 1.0:
            v = v * scale
        if transposed:
            o_ref[0] = v.T.astype(o_ref.dtype)
        else:
            o_ref[...] = v.astype(o_ref.dtype)


def _in_proj(tok, mod_l, g, w, outs, batch, seq, ctx_len):
    d = tok.d
    nj = seq // TOKEN_TILE
    n_lat, n_all = tok.n_lat, tok.n_lat + tok.n_ctx
    rows = n_all * TOKEN_TILE
    t_len = seq + ctx_len
    out_shape, out_specs = [], []
    for (_, width, dtype, _, transposed) in outs:
        if transposed:
            out_shape.append(jax.ShapeDtypeStruct((batch, width, t_len), dtype))
            out_specs.append(pl.BlockSpec(
                (1, width, TOKEN_TILE),
                lambda i: (jnp.where(i < n_lat, i // nj, i - n_lat), 0, jnp.where(i < n_lat, i % nj, nj))))
        else:
            out_shape.append(jax.ShapeDtypeStruct((rows, width), dtype))
            out_specs.append(pl.BlockSpec((TOKEN_TILE, width), lambda i: (i, 0)))
    body = functools.partial(_in_proj_body, tok=tok, outs=outs, d=d)
    return pl.pallas_call(
        body, out_shape=out_shape, grid=(n_all,),
        in_specs=tok.specs() + [_mod_spec(n_lat, nj, batch, 3 * d), _full((1, d)), _full(w.shape)],
        out_specs=out_specs, compiler_params=_params(1), name="in_proj",
    )(*tok.arrays, mod_l, g.reshape(1, d), w)


def _mlstm_mix(hf_ref, hb_ref, xc_ref, z_ref, gh_ref, sk_ref):
    ht = hf_ref[0] + hb_ref[0]
    L = ML_CHUNK
    normed = []
    for hd in range(ML_HEADS):
        hh = ht[hd * L:(hd + 1) * L]
        mu = jnp.mean(hh, axis=0, keepdims=True)
        var = jnp.mean(jnp.square(hh - mu), axis=0, keepdims=True)
        normed.append((hh - mu) * lax.rsqrt(var + EPS))
    hn = jnp.concatenate(normed, axis=0).T * gh_ref[...]
    return ((hn + sk_ref[...] * xc_ref[...].astype(F32)) * _silu(z_ref[...].astype(F32))).astype(BF16)


def _out_proj_body(*refs, tok, mix_a, n_b, d, final):
    n_tok, n_a = len(tok.arrays), len(mix_a.arrays)
    b_refs = refs[n_tok + n_a:n_tok + n_a + n_b]
    wa_ref, wb_ref, mod_ref = refs[n_tok + n_a + n_b:n_tok + n_a + n_b + 3]
    rest = refs[n_tok + n_a + n_b + 3:]
    i = pl.program_id(0)
    x = tok.load(refs[:n_tok], i)
    mb = b_refs[0][...] if n_b == 1 else _mlstm_mix(*b_refs)
    acc = _dot(mix_a.load(refs[n_tok:n_tok + n_a], i), wa_ref[...]) + _dot(mb, wb_ref[...])
    xn = x + mod_ref[0][:, 2 * d:] * acc
    if final:
        gf_ref, o_ref = rest
        xn = xn * lax.rsqrt(jnp.mean(xn * xn, axis=-1, keepdims=True) + EPS) * gf_ref[...]
    else:
        (o_ref,) = rest
    o_ref[...] = xn


def _out_proj(tok, mix_a, mix_b, w_out, mod_l, batch, seq, g_final=None):
    d = tok.d
    half = mix_a.d
    nj = seq // TOKEN_TILE
    n_lat = tok.n_lat
    final = g_final is not None
    n_tiles = n_lat if final else n_lat + tok.n_ctx
    wa, wb = w_out[:half].astype(BF16), w_out[half:].astype(BF16)
    tile = lambda width: pl.BlockSpec((TOKEN_TILE, width), lambda i: (i, 0))
    if isinstance(mix_b, tuple):
        hf, hb, xc, z, g_head, skip = mix_b
        width = xc.shape[1]
        feat = pl.BlockSpec((1, width, TOKEN_TILE),
                            lambda i: (jnp.where(i < n_lat, i // nj, i - n_lat), 0, jnp.where(i < n_lat, i % nj, nj)))
        b_specs = [feat, feat, tile(width), tile(width), _full((1, width)), _full((1, width))]
        b_args = [hf, hb, xc, z, g_head.reshape(1, width), skip.reshape(1, width)]
    else:
        b_specs, b_args = [tile(half)], [mix_b]
    in_specs = tok.specs() + mix_a.specs() + b_specs + [_full(wa.shape), _full(wb.shape),
                                                        _mod_spec(n_lat, nj, batch, 3 * d)]
    args = list(tok.arrays) + list(mix_a.arrays) + b_args + [wa, wb, mod_l]
    if final:
        in_specs.append(_full((1, d)))
        args.append(g_final.reshape(1, d))
    body = functools.partial(_out_proj_body, tok=tok, mix_a=mix_a, n_b=len(b_args), d=d, final=final)
    return pl.pallas_call(
        body, out_shape=jax.ShapeDtypeStruct((n_tiles * TOKEN_TILE, d), F32), grid=(n_tiles,),
        in_specs=in_specs, out_specs=tile(d), compiler_params=_params(1), name="out_proj",
    )(*args)


def _rope_tables(seq, ctx_len, rot_dim):
    t = jnp.arange(seq)
    pos = jnp.stack([t // GRID_W, t % GRID_W], axis=-1).astype(F32)
    n_freq = rot_dim // 4
    inv = ROPE_BASE ** (-jnp.arange(n_freq, dtype=F32) / n_freq)
    ang = pos[:, :, None] * inv
    cos, sin = jnp.cos(ang), jnp.sin(ang)
    cos_t = jnp.concatenate([cos[:, 0], cos[:, 0], cos[:, 1], cos[:, 1]], axis=-1)
    sin_t = jnp.concatenate([-sin[:, 0], sin[:, 0], -sin[:, 1], sin[:, 1]], axis=-1)
    cos_t = jnp.concatenate([cos_t, jnp.ones((ctx_len, rot_dim), F32)], axis=0)
    sin_t = jnp.concatenate([sin_t, jnp.zeros((ctx_len, rot_dim), F32)], axis=0)
    return cos_t, sin_t


def _rope(x, cos, sin, dist):
    lane = lax.broadcasted_iota(jnp.int32, x.shape, 1)
    first = (lane % (2 * dist)) < dist
    partner = jnp.where(first, pltpu.roll(x, LANES - dist, 1), pltpu.roll(x, dist, 1))
    return x * cos + partner * sin


def _rope_rows(x, cos, sin, dist):
    n = x.shape[0] // dist
    partner = jnp.concatenate([x[(i ^ 1) * dist:((i ^ 1) + 1) * dist] for i in range(n)], axis=0)
    return x * cos + partner * sin


def _tok_block(b, j, nj, batch):
    return jnp.where(j < nj, b * nj + j, batch * nj + b)


def _mla_prep_body(pa_ref, cos_ref, sin_ref, cost_ref, sint_ref, gq_ref, gkv_ref, wuqt_ref, wuk_ref, wuvt_ref,
                   q_ref, k_ref, vt_ref, *, heads, q_lora, kv_lora, qscale):
    pa = pa_ref[...]

    def norm(v, g_ref):
        return v * lax.rsqrt(jnp.mean(v * v, axis=-1, keepdims=True) + EPS) * g_ref[...]

    cq_t = norm(pa[:, :q_lora], gq_ref).T.astype(BF16)
    q_all = _dot(wuqt_ref[...], cq_t)
    cos_t, sin_t = cost_ref[...], sint_ref[...]
    for h in range(heads):
        qh = _rope_rows(q_all[h * LANES:(h + 1) * LANES], cos_t, sin_t, 8)
        q_ref[0, h] = (qh * qscale).astype(BF16)
    ckv = norm(pa[:, q_lora:q_lora + kv_lora], gkv_ref)
    k_nope = _dot(ckv.astype(BF16), wuk_ref[...])
    k_rope = _rope(pa[:, q_lora + kv_lora:], cos_ref[...], sin_ref[...], 8)
    for h in range(heads):
        k_ref[0, h] = (k_nope[:, h * LANES:(h + 1) * LANES] + k_rope).astype(BF16)
    vt_ref[0] = _dot(wuvt_ref[...], ckv.T.astype(BF16)).astype(BF16)


def _mla_prep(pa, cos, sin, g_q, g_kv, w_uq, w_ukv, batch, seq, ctx_len, heads, nope, rope, v_dim):
    q_lora, kv_lora = g_q.shape[0], g_kv.shape[0]
    nj = seq // TOKEN_TILE
    t_len = seq + ctx_len
    pad = LANES - nope - rope
    wuq = jnp.pad(w_uq.reshape(q_lora, heads, nope + rope), ((0, 0), (0, 0), (0, pad)))
    wuq_t = wuq.reshape(q_lora, heads * LANES).T.astype(BF16)
    wkv = w_ukv.reshape(kv_lora, heads, nope + v_dim)
    wuk = jnp.pad(wkv[..., :nope], ((0, 0), (0, 0), (0, LANES - nope))).reshape(kv_lora, heads * LANES).astype(BF16)
    wuv_t = wkv[..., nope:].reshape(kv_lora, heads * v_dim).T.astype(BF16)
    body = functools.partial(_mla_prep_body, heads=heads, q_lora=q_lora, kv_lora=kv_lora,
                             qscale=(nope + rope) ** -0.5 * LOG2E)
    head_major = pl.BlockSpec((1, heads, TOKEN_TILE, LANES), lambda b, j: (b, 0, j, 0))
    q_feature_major = pl.BlockSpec((1, heads, LANES, TOKEN_TILE), lambda b, j: (b, 0, 0, j))
    tok_table = pl.BlockSpec((TOKEN_TILE, LANES), lambda b, j: (j, 0))
    feat_table = pl.BlockSpec((LANES, TOKEN_TILE), lambda b, j: (0, j))
    return pl.pallas_call(
        body,
        out_shape=[jax.ShapeDtypeStruct((batch, heads, LANES, t_len), BF16),
                   jax.ShapeDtypeStruct((batch, heads, t_len, LANES), BF16),
                   jax.ShapeDtypeStruct((batch, heads * v_dim, t_len), BF16)],
        grid=(batch, nj + 1),
        in_specs=[pl.BlockSpec((TOKEN_TILE, pa.shape[1]), lambda b, j: (_tok_block(b, j, nj, batch), 0)),
                  tok_table, tok_table, feat_table, feat_table,
                  _full((1, q_lora)), _full((1, kv_lora)), _full(wuq_t.shape), _full(wuk.shape), _full(wuv_t.shape)],
        out_specs=[q_feature_major, head_major,
                   pl.BlockSpec((1, heads * v_dim, TOKEN_TILE), lambda b, j: (b, 0, j))],
        compiler_params=_params(2), name="mla_prep",
    )(pa, cos, sin, cos.T, sin.T, g_q.reshape(1, -1), g_kv.reshape(1, -1), wuq_t, wuk, wuv_t)


def _gqa_prep_body(pd_ref, cos_ref, sin_ref, gq_ref, gk_ref, q_ref, k_ref, *, heads, kv_heads, dim, qscale):
    cos, sin = cos_ref[...], sin_ref[...]

    def head(first_row, g_ref):
        x = pd_ref[0, first_row:first_row + dim, :]
        y = x * lax.rsqrt(jnp.mean(x * x, axis=0, keepdims=True) + EPS) * g_ref[...]
        return _rope_rows(y, cos, sin, dim // 4)

    zeros = jnp.zeros((dim, TOKEN_TILE), F32)
    for h in range(heads):
        parts = [zeros] * kv_heads
        parts[h // (heads // kv_heads)] = head(h * dim, gq_ref) * qscale
        q_ref[0, h] = jnp.concatenate(parts, axis=0).astype(BF16)
    keys = [head((heads + kv) * dim, gk_ref) for kv in range(kv_heads)]
    k_ref[0, 0] = jnp.concatenate(keys, axis=0).T.astype(BF16)


def _gqa_prep(pd_t, cos, sin, g_q, g_k, batch, seq, ctx_len, heads, kv_heads, head_dim):
    assert kv_heads * head_dim == LANES
    nj = seq // TOKEN_TILE
    t_len = seq + ctx_len
    body = functools.partial(_gqa_prep_body, heads=heads, kv_heads=kv_heads, dim=head_dim,
                             qscale=head_dim ** -0.5 * LOG2E)
    table = pl.BlockSpec((head_dim, TOKEN_TILE), lambda b, j: (0, j))
    gain = lambda g: jnp.broadcast_to(g[:, None], (head_dim, TOKEN_TILE))
    return pl.pallas_call(
        body,
        out_shape=[jax.ShapeDtypeStruct((batch, heads, LANES, t_len), BF16),
                   jax.ShapeDtypeStruct((batch, 1, t_len, LANES), BF16)],
        grid=(batch, nj + 1),
        in_specs=[pl.BlockSpec((1, pd_t.shape[1], TOKEN_TILE), lambda b, j: (b, 0, j)), table, table,
                  _full((head_dim, TOKEN_TILE)), _full((head_dim, TOKEN_TILE))],
        out_specs=[pl.BlockSpec((1, heads, LANES, TOKEN_TILE), lambda b, j: (b, 0, 0, j)),
                   pl.BlockSpec((1, 1, TOKEN_TILE, LANES), lambda b, j: (b, 0, j, 0))],
        compiler_params=_params(2), name="gqa_prep",
    )(pd_t, cos.T, sin.T, gain(g_q), gain(g_k))


def _score_pass(q, k_ref, k_head, chunks, s_ref):
    m = None
    for (st, sz) in chunks:
        s_t = _dot(k_ref[0, k_head, st:st + sz, :], q)
        s_ref[st:st + sz, :] = s_t
        cm = jnp.max(s_t, axis=0, keepdims=True)
        m = cm if m is None else jnp.maximum(m, cm)
    return m


def _value_pass(s_ref, m, vt_ref, v_rows, chunks):
    l = jnp.zeros(m.shape, F32)
    acc = jnp.zeros((v_rows.stop - v_rows.start, m.shape[1]), F32)
    for (st, sz) in chunks:
        p = jnp.exp2(s_ref[st:st + sz, :] - m)
        l = l + jnp.sum(p, axis=0, keepdims=True)
        acc = acc + _dot(vt_ref[0, v_rows, st:st + sz], p.astype(BF16))
    return acc / l


def _gated_store(outs, g_ref, o_ref):
    o2 = jnp.concatenate(outs, axis=0).T
    o_ref[...] = (o2 * _silu(g_ref[...].astype(F32))).astype(o_ref.dtype)


def _flash_body(q_ref, k_ref, vt_ref, g_ref, o_ref, s_scr, m_scr, *, k_sel, v_off, v_dim, chunks):
    j = pl.program_id(2)

    @pl.when((pl.program_id(0) == 0) & (pl.program_id(1) == 0) & (j == 0))
    def _():
        s_scr[...] = jnp.zeros(s_scr.shape, F32)
        m_scr[...] = jnp.zeros(m_scr.shape, F32)

    def step(slot, prev):
        heads = (0, 1)
        q = [q_ref[0, a] for a in heads]
        m_prev = [m_scr[prev, a] for a in heads]
        m = [None, None]
        l = [jnp.zeros(m_prev[a].shape, F32) for a in heads]
        acc = [jnp.zeros((v_dim, m_prev[a].shape[1]), F32) for a in heads]
        for (st, sz) in chunks:
            for a in heads:
                s_t = _dot(k_ref[0, k_sel[a], st:st + sz, :], q[a])
                s_scr[slot, a, st:st + sz, :] = s_t
                cm = jnp.max(s_t, axis=0, keepdims=True)
                m[a] = cm if m[a] is None else jnp.maximum(m[a], cm)
                p = jnp.exp2(s_scr[prev, a, st:st + sz, :] - m_prev[a])
                l[a] = l[a] + jnp.sum(p, axis=0, keepdims=True)
                acc[a] = acc[a] + _dot(vt_ref[0, v_off[a]:v_off[a] + v_dim, st:st + sz], p.astype(BF16))
        for a in heads:
            m_scr[slot, a] = m[a]
        _gated_store([acc[a] / l[a] for a in heads], g_ref, o_ref)

    for parity in (0, 1):
        pl.when(j % 2 == parity)(functools.partial(step, parity, 1 - parity))


def _flash_ctx_body(q_ref, k_ref, vt_ref, g_ref, o_ref, s_scr, *, k_sel, v_off, v_dim, chunks):
    outs = []
    for a in range(2):
        m = _score_pass(q_ref[0, a], k_ref, k_sel[a], chunks, s_scr.at[a])
        outs.append(_value_pass(s_scr.at[a], m, vt_ref, slice(v_off[a], v_off[a] + v_dim), chunks))
    _gated_store(outs, g_ref, o_ref)


def _flash(q, k, vt, gate, batch, seq, ctx_len, k_heads_per_pair, v_rows_per_pair, pairs_per_kv, ctx_queries):
    heads = q.shape[1]
    t_len = seq + ctx_len
    nj = seq // TOKEN_TILE
    v_dim = LANES // 2
    k_sel = (0, 1) if k_heads_per_pair == 2 else (0, 0)
    v_off = (0, v_dim) if v_rows_per_pair == 2 * v_dim else (0, 0)
    k_blocks = k.shape[1] // k_heads_per_pair
    k_block = lambda p: (p // pairs_per_kv) % k_blocks
    chunks = tuple((c * KEY_CHUNK, KEY_CHUNK) for c in range(seq // KEY_CHUNK)) + ((seq, ctx_len),)
    static = dict(k_sel=k_sel, v_off=v_off, v_dim=v_dim)

    width = gate.shape[1]
    tq = FLASH_TILE
    n_q = seq // tq
    done = lambda b, p, j: (b * n_q + jnp.maximum(j - 1, 0), p)
    out = pl.pallas_call(
        functools.partial(_flash_body, chunks=chunks, **static),
        out_shape=jax.ShapeDtypeStruct((batch * seq, width), BF16),
        grid=(batch, heads // 2, n_q + 1),
        in_specs=[pl.BlockSpec((1, 2, LANES, tq), lambda b, p, j: (b, p, 0, jnp.minimum(j, n_q - 1))),
                  pl.BlockSpec((1, k_heads_per_pair, t_len, LANES), lambda b, p, j: (b, k_block(p), 0, 0)),
                  pl.BlockSpec((1, v_rows_per_pair, t_len), lambda b, p, j: (b, p // pairs_per_kv, 0)),
                  pl.BlockSpec((tq, LANES), done)],
        out_specs=pl.BlockSpec((tq, LANES), done),
        scratch_shapes=[pltpu.VMEM((2, 2, t_len, tq), F32), pltpu.VMEM((2, 2, 1, tq), F32)],
        compiler_params=_params(3), name="flash_attention",
    )(q, k, vt, gate)
    if not ctx_queries:
        return (out,)

    out_ctx = pl.pallas_call(
        functools.partial(_flash_ctx_body, chunks=((0, ctx_len),), **static),
        out_shape=jax.ShapeDtypeStruct((batch * ctx_len, width), BF16),
        grid=(batch, heads // 2),
        in_specs=[pl.BlockSpec((1, 2, LANES, TOKEN_TILE), lambda b, p: (b, p, 0, nj)),
                  pl.BlockSpec((1, k_heads_per_pair, ctx_len, LANES), lambda b, p: (b, k_block(p), seq // ctx_len, 0)),
                  pl.BlockSpec((1, v_rows_per_pair, ctx_len), lambda b, p: (b, p // pairs_per_kv, seq // ctx_len)),
                  pl.BlockSpec((TOKEN_TILE, LANES), lambda b, p: (batch * nj + b, p))],
        out_specs=pl.BlockSpec((TOKEN_TILE, LANES), lambda b, p: (b, p)),
        scratch_shapes=[pltpu.VMEM((2, ctx_len, TOKEN_TILE), F32)],
        compiler_params=_params(2), name="flash_attention_ctx",
    )(q, k, vt, gate)
    return out, out_ctx


NA_KEY_ROWS = 12
NA_STEP_ROWS = TOKEN_TILE // GRID_W
NA_VARIANTS = 3


def _na_rel_row(variant, rr, a):
    if variant == 0:
        valid, dr = a < WIN_R, a - rr
    elif variant == 1:
        dr = a - WIN_R // 2 - rr
        valid = -(WIN_R // 2) <= dr < WIN_R // 2
    else:
        valid, dr = a >= NA_KEY_ROWS - WIN_R, a - (NA_KEY_ROWS - NA_STEP_ROWS) - rr
    return dr if valid else None


def _na_bias_body(rpb_ref, o_ref, blk_scr):
    p = pl.program_id(0)
    shape = (GRID_W, LANES)
    kc = lax.broadcasted_iota(jnp.int32, shape, 0)
    lane = lax.broadcasted_iota(jnp.int32, shape, 1)
    qc = lane % GRID_W
    upper = lane >= GRID_W
    rel = kc - qc + (WIN_C - 1)
    c0 = jnp.clip(qc - WIN_C // 2, 0, GRID_W - WIN_C)
    col_ok = (kc >= c0) & (kc < c0 + WIN_C)
    n_rel_r, n_rel_c = 2 * WIN_R - 1, 2 * WIN_C - 1

    def block(dd, carry):
        base0 = (2 * p) * (n_rel_r * n_rel_c) + dd * n_rel_c
        base1 = base0 + n_rel_r * n_rel_c
        val = jnp.zeros(shape, F32)
        for jj in range(n_rel_c):
            val = jnp.where(rel == jj, jnp.where(upper, rpb_ref[base1 + jj], rpb_ref[base0 + jj]), val)
        blk_scr[dd] = jnp.where(col_ok, val * LOG2E, NEG)
        return carry

    lax.fori_loop(0, n_rel_r, block, 0)
    outside = jnp.full(shape, NEG, F32)
    for variant in range(NA_VARIANTS):
        for rr in range(NA_STEP_ROWS):
            for a in range(NA_KEY_ROWS):
                dr = _na_rel_row(variant, rr, a)
                o_ref[0, variant, rr, a * GRID_W:(a + 1) * GRID_W, :] = (
                    outside if dr is None else blk_scr[dr + WIN_R - 1])


def _na_bias(rpb):
    heads = rpb.shape[0]
    tab = (NA_VARIANTS, NA_STEP_ROWS, NA_KEY_ROWS * GRID_W, LANES)
    return pl.pallas_call(
        _na_bias_body,
        out_shape=jax.ShapeDtypeStruct((heads // 2,) + tab, F32),
        grid=(heads // 2,),
        in_specs=[pl.BlockSpec(memory_space=pltpu.SMEM)],
        out_specs=pl.BlockSpec((1,) + tab, lambda p: (p, 0, 0, 0, 0)),
        scratch_shapes=[pltpu.VMEM((2 * WIN_R - 1, GRID_W, LANES), F32)],
        compiler_params=_params(1), name="na_bias",
    )(rpb.reshape(-1))


NA_CHUNK = 256


def _na_body(q_ref, kl_ref, kc_ref, vt_ref, bias_ref, g_ref, o_ref, s_scr, m_scr, *, seq, ctx_len, n_steps):
    j = pl.program_id(2)
    n_rows = seq // GRID_W
    n_loc = NA_KEY_ROWS * GRID_W // NA_CHUNK
    cols = NA_STEP_ROWS * LANES
    lower = lax.broadcasted_iota(jnp.int32, (GRID_W, LANES), 1) < GRID_W

    @pl.when((pl.program_id(0) == 0) & (pl.program_id(1) == 0) & (j == 0))
    def _():
        s_scr[...] = jnp.zeros(s_scr.shape, F32)
        m_scr[...] = jnp.zeros(m_scr.shape, F32)

    def span_start(step):
        first_row = jnp.clip(NA_STEP_ROWS * step - WIN_R // 2, 0, n_rows - NA_KEY_ROWS)
        return first_row * GRID_W

    def step(slot, prev):
        js = jnp.minimum(j, n_steps - 1)
        variant = jnp.where(js == 0, 0, jnp.where(js == n_steps - 1, 2, 1))
        k0, v0 = span_start(js), span_start(jnp.maximum(j - 1, 0))
        q = q_ref[...]
        zero = jnp.zeros((GRID_W, LANES), q.dtype)
        parts = []
        for rr in range(NA_STEP_ROWS):
            q_r = q[rr * GRID_W:(rr + 1) * GRID_W]
            parts += [jnp.where(lower, q_r, zero), jnp.where(lower, zero, q_r)]
        q2 = jnp.concatenate(parts, axis=0)
        m_prev = m_scr[prev]
        m = None
        l = jnp.zeros((1, cols), F32)
        acc = jnp.zeros((LANES, cols), F32)
        for c in range(n_loc + 1):
            if c < n_loc:
                rows = slice(c * NA_CHUNK, (c + 1) * NA_CHUNK)
                s = _dot_nt(kl_ref[pl.ds(pl.multiple_of(k0 + c * NA_CHUNK, NA_CHUNK), NA_CHUNK), :], q2)
                s = jnp.concatenate([s[:, rr * LANES:(rr + 1) * LANES] + bias_ref[0, variant, rr, rows, :]
                                     for rr in range(NA_STEP_ROWS)], axis=1)
                v_chunk = vt_ref[0, :, pl.ds(pl.multiple_of(v0 + c * NA_CHUNK, NA_CHUNK), NA_CHUNK)]
            else:
                s = _dot_nt(kc_ref[...], q2)
                v_chunk = vt_ref[0, :, seq:seq + ctx_len]
            s_scr[slot, c] = s
            cm = jnp.max(s, axis=0, keepdims=True)
            m = cm if m is None else jnp.maximum(m, cm)
            p = jnp.exp2(s_scr[prev, c] - m_prev)
            l = l + jnp.sum(p, axis=0, keepdims=True)
            acc = acc + _dot(v_chunk, p.astype(BF16))
        m_scr[slot] = m
        r_n = (acc / l).T
        outs = [jnp.where(lower, r_n[rr * LANES:rr * LANES + GRID_W], r_n[rr * LANES + GRID_W:(rr + 1) * LANES])
                for rr in range(NA_STEP_ROWS)]
        o = jnp.concatenate(outs, axis=0)
        o_ref[...] = (o * _silu(g_ref[...].astype(F32))).astype(o_ref.dtype)

    for parity in (0, 1):
        pl.when(j % 2 == parity)(functools.partial(step, parity, 1 - parity))


def _na_attention(q, k, vt, bias, gate, batch, seq, ctx_len):
    assert ctx_len == NA_CHUNK
    pairs = q.shape[1] // LANES
    t_len = seq + ctx_len
    n_steps = seq // TOKEN_TILE
    n_chunks = NA_KEY_ROWS * GRID_W // NA_CHUNK + 1
    cols = NA_STEP_ROWS * LANES
    done = lambda b, p, j: (b * n_steps + jnp.maximum(j - 1, 0), p)
    body = functools.partial(_na_body, seq=seq, ctx_len=ctx_len, n_steps=n_steps)
    return pl.pallas_call(
        body, out_shape=jax.ShapeDtypeStruct((batch * seq, q.shape[1]), BF16),
        grid=(batch, pairs, n_steps + 1),
        in_specs=[pl.BlockSpec((TOKEN_TILE, LANES), lambda b, p, j: (b * n_steps + jnp.minimum(j, n_steps - 1), p)),
                  pl.BlockSpec((seq, LANES), lambda b, p, j: (b, p)),
                  pl.BlockSpec((ctx_len, LANES), lambda b, p, j: (batch * seq // ctx_len + b, p)),
                  pl.BlockSpec((1, LANES, t_len), lambda b, p, j: (b, p, 0)),
                  pl.BlockSpec((1,) + bias.shape[1:], lambda b, p, j: (p, 0, 0, 0, 0)),
                  pl.BlockSpec((TOKEN_TILE, LANES), done)],
        out_specs=pl.BlockSpec((TOKEN_TILE, LANES), done),
        scratch_shapes=[pltpu.VMEM((2, n_chunks, NA_CHUNK, cols), F32), pltpu.VMEM((2, 1, cols), F32)],
        compiler_params=_params(3), name="neighborhood_attention",
    )(q, k, k, vt, bias, gate)


def _seg_scans(jobs, use_max=False):
    vals = [v for v, _ in jobs]
    n = vals[0].shape[1]
    lane = lax.broadcasted_iota(jnp.int32, vals[0].shape, 1) % ML_CHUNK
    k = 1
    while k < ML_CHUNK:
        for i, (_, reverse) in enumerate(jobs):
            v = vals[i]
            if reverse:
                ok, shifted = lane < ML_CHUNK - k, pltpu.roll(v, n - k, 1)
            else:
                ok, shifted = lane >= k, pltpu.roll(v, k, 1)
            vals[i] = jnp.maximum(v, jnp.where(ok, shifted, NEG)) if use_max else v + jnp.where(ok, shifted, 0.0)
        k *= 2
    return vals


ML_GATE_ROWS = 40


def _mlstm_prep_body(u_ref, up_ref, un_ref, cw_ref, cb_ref, wqk_ref, wv_ref, wg_ref, bg_ref,
                     xc_ref, k_ref, qt_ref, vt_ref, pre_ref, *, nj, width, kscale):
    j = pl.program_id(1)
    u = u_ref[...]
    row = lax.broadcasted_iota(jnp.int32, u.shape, 0)
    prev = jnp.where((j > 0) & (j < nj), up_ref[7:8, :], 0.0)
    nxt = jnp.where(j < nj - 1, un_ref[0:1, :], 0.0)
    u_m1 = jnp.where(row == 0, prev, pltpu.roll(u, 1, 0))
    u_p1 = jnp.where(row == TOKEN_TILE - 1, nxt, pltpu.roll(u, TOKEN_TILE - 1, 0))
    cw = cw_ref[...]
    xc = _silu(u_m1 * cw[0:1] + u * cw[1:2] + u_p1 * cw[2:3] + cb_ref[...])
    xcb = xc.astype(BF16)
    xc_ref[...] = xcb
    qk = _dot(xcb, wqk_ref[...])
    v = _dot(u.astype(BF16), wv_ref[...])
    qb, kb, vb = qk[:, :width].astype(BF16), qk[:, width:].astype(BF16), v.astype(BF16)
    k_ref[...] = (qk[:, width:] * kscale).astype(BF16)
    qt_ref[0] = qk[:, :width].T.astype(BF16)
    vt_ref[0] = v.T.astype(BF16)
    pre_ref[0] = _dot_nt(wg_ref[0], qb) + _dot_nt(wg_ref[1], kb) + _dot_nt(wg_ref[2], vb) + bg_ref[...]


def _mlstm_gates_body(pre_ref, g_ref, gc_ref):
    pre = pre_ref[0]
    i8 = [pre[16 * d:16 * d + 8] for d in range(2)]
    f8 = [_log_sigmoid(pre[16 * d + 8:16 * d + 16]) for d in range(2)]
    b0, b0_rev, b1, b1_rev = _seg_scans([(f8[0], False), (f8[0], True), (f8[1], True), (f8[1], False)])
    b8, b_last = [b0, b1], [b0 + b0_rev - f8[0], b1 + b1_rev - f8[1]]
    r8 = [i8[d] - b8[d] for d in range(2)]
    c0, c0_rev, c1, c1_rev = _seg_scans([(r8[0], False), (r8[0], True), (r8[1], True), (r8[1], False)], use_max=True)
    c8, r_max = [c0, c1], [jnp.maximum(c0, c0_rev), jnp.maximum(c1, c1_rev)]
    for d in range(2):
        g_ref[0, d * ML_GATE_ROWS:(d + 1) * ML_GATE_ROWS] = jnp.concatenate(
            [-c8[d], b8[d] + c8[d], jnp.exp(r8[d] - r_max[d]), b_last[d], b_last[d] + r_max[d]], axis=0)
    pad = jnp.zeros((LANES - 16, pre.shape[1]), F32)
    gc_ref[0] = jnp.concatenate(r8 + [pad], axis=0).T


def _mlstm_gates(pre):
    batch, _, t_len = pre.shape
    return pl.pallas_call(
        _mlstm_gates_body,
        out_shape=[jax.ShapeDtypeStruct((batch, 2 * ML_GATE_ROWS, t_len), F32),
                   jax.ShapeDtypeStruct((batch, t_len, LANES), F32)],
        grid=(batch,),
        in_specs=[pl.BlockSpec((1,) + pre.shape[1:], lambda b: (b, 0, 0))],
        out_specs=[pl.BlockSpec((1, 2 * ML_GATE_ROWS, t_len), lambda b: (b, 0, 0)),
                   pl.BlockSpec((1, t_len, LANES), lambda b: (b, 0, 0))],
        compiler_params=_params(1), name="mlstm_gates",
    )(pre)


def _mlstm_prep(u, conv_w, conv_b, w_q, w_k, w_v, w_gate, b_gate, batch, seq, ctx_len):
    heads, hd = w_q.shape[0], w_q.shape[1]
    assert heads == ML_HEADS and hd == ML_CHUNK
    width = heads * hd
    nj = seq // TOKEN_TILE
    t_len = seq + ctx_len
    rows = u.shape[0]

    def block_diag(w):
        eye = jnp.eye(heads, dtype=w.dtype)
        return (eye[:, None, :, None] * w[:, :, None, :]).reshape(width, width)

    wqk = jnp.concatenate([block_diag(w_q), block_diag(w_k)], axis=1).astype(BF16)
    wv = block_diag(w_v).astype(BF16)
    wg = w_gate.reshape(2, heads, 3, hd, 2, heads).transpose(2, 0, 4, 5, 1, 3).reshape(3, 2, 2, heads, width)
    wg = jnp.pad(wg, ((0, 0), (0, 0), (0, 0), (0, 8 - heads), (0, 0))).reshape(3, 32, width).astype(BF16)
    bg = jnp.pad(b_gate.reshape(2, 2, heads), ((0, 0), (0, 0), (0, 8 - heads))).reshape(32, 1)
    n_halo = rows // 8
    tokb = lambda b, j: _tok_block(b, j, nj, batch)
    tile = pl.BlockSpec((TOKEN_TILE, width), lambda b, j: (tokb(b, j), 0))
    feat = pl.BlockSpec((1, width, TOKEN_TILE), lambda b, j: (b, 0, j))
    body = functools.partial(_mlstm_prep_body, nj=nj, width=width, kscale=hd ** -0.5)
    per_tile = TOKEN_TILE // 8
    return pl.pallas_call(
        body,
        out_shape=[jax.ShapeDtypeStruct((rows, width), BF16)] * 2
        + [jax.ShapeDtypeStruct((batch, width, t_len), BF16)] * 2
        + [jax.ShapeDtypeStruct((batch, 32, t_len), F32)],
        grid=(batch, nj + 1),
        in_specs=[tile,
                  pl.BlockSpec((8, width), lambda b, j: (jnp.maximum(tokb(b, j) * per_tile - 1, 0), 0)),
                  pl.BlockSpec((8, width), lambda b, j: (jnp.minimum((tokb(b, j) + 1) * per_tile, n_halo - 1), 0)),
                  _full((3, width)), _full((1, width)), _full(wqk.shape), _full(wv.shape),
                  _full(wg.shape), _full((32, 1))],
        out_specs=[tile, tile, feat, feat, pl.BlockSpec((1, 32, TOKEN_TILE), lambda b, j: (b, 0, j))],
        compiler_params=_params(2), name="mlstm_prep",
    )(u, u, u, conv_w, conv_b.reshape(1, width), wqk, wv, wg, bg)


def _mlstm_seq_body(kf, qtf, vtf, gf, gcf, kb, qtb, vtb, gb, gcb, hf_ref, hb_ref, c_s, n_s, m_s):
    t = pl.program_id(1)
    L = ML_CHUNK

    @pl.when(t == 0)
    def _():
        c_s[...] = jnp.zeros(c_s.shape, F32)
        n_s[...] = jnp.zeros(n_s.shape, F32)
        m_s[...] = jnp.zeros(m_s.shape, F32)

    si = lax.broadcasted_iota(jnp.int32, (L, L), 0)
    li = lax.broadcasted_iota(jnp.int32, (L, L), 1)
    streams = ((kf, qtf, vtf, gf, gcf, hf_ref, si <= li), (kb, qtb, vtb, gb, gcb, hb_ref, si >= li))
    units = []
    for d, (k_ref, qt_ref, vt_ref, g_ref, gc_ref, h_ref, incl) in enumerate(streams):
        g = g_ref[0]
        gc = gc_ref[0]
        for h in range(ML_HEADS):
            cols = slice(h * L, (h + 1) * L)
            k, qt, vt = k_ref[:, cols], qt_ref[0, cols, :], vt_ref[0, cols, :]
            neg_c, m_loc, w0, b_last, g_max = (g[8 * i + h:8 * i + h + 1] for i in range(5))
            r_col = gc[:, 8 * d + h:8 * d + h + 1]
            p0 = jnp.where(incl, jnp.exp(r_col + neg_c), 0.0) * _dot(k, qt)
            s_sum = jnp.sum(p0, axis=0, keepdims=True)
            intra = _dot(vt, p0.astype(BF16))
            c_inc = _dot((vt.astype(F32) * w0).astype(BF16), k)
            n_inc = _dot(jnp.broadcast_to(w0, (8, L)).astype(BF16), k)
            units.append((d * ML_HEADS + h, h_ref, cols, qt, neg_c, m_loc, b_last, g_max, s_sum, intra, c_inc, n_inc))
    for (idx, h_ref, cols, qt, neg_c, m_loc, b_last, g_max, s_sum, intra, c_inc, n_inc) in units:
        c_st, n_st, m_st = c_s[idx], n_s[idx], m_s[idx]
        cn = _dot(jnp.concatenate([c_st, n_st], axis=0).astype(BF16), qt)
        delta = jnp.maximum(m_st + neg_c, 0.0)
        e_intra = jnp.exp(-delta)
        w_inter = jnp.exp(m_st + neg_c - delta)
        num = w_inter * cn[:L] + e_intra * intra
        den = w_inter * cn[L:L + 1] + e_intra * s_sum
        h_ref[0, cols, :] = num / jnp.maximum(jnp.abs(den), jnp.exp(-(m_loc + delta)))
        m_new = jnp.maximum(b_last + m_st, g_max)
        decay = jnp.exp(b_last + m_st - m_new)
        gain = jnp.exp(g_max - m_new)
        c_s[idx] = decay * c_st + gain * c_inc
        n_s[idx] = decay * n_st + gain * n_inc
        m_s[idx] = m_new


def _mlstm_seq(k, qt, vt, gates, gcols, batch, seq, ctx_len):
    width = k.shape[1]
    L = ML_CHUNK
    n_lat, n_ctx = seq // L, ctx_len // L
    n_chunks = n_lat + n_ctx
    fwd = lambda t: (t + n_lat) % n_chunks
    bwd = lambda t: n_chunks - 1 - t
    rowblk = lambda b, c: jnp.where(c < n_lat, b * n_lat + c, batch * n_lat + b * n_ctx + (c - n_lat))

    def stream(chunk_of, d):
        feat = pl.BlockSpec((1, width, L), lambda b, t: (b, 0, chunk_of(t)))
        return [pl.BlockSpec((L, width), lambda b, t: (rowblk(b, chunk_of(t)), 0)), feat, feat,
                pl.BlockSpec((1, ML_GATE_ROWS, L), lambda b, t: (b, d, chunk_of(t))),
                pl.BlockSpec((1, L, LANES), lambda b, t: (b, chunk_of(t), 0))]

    out_f = pl.BlockSpec((1, width, L), lambda b, t: (b, 0, fwd(t)))
    out_b = pl.BlockSpec((1, width, L), lambda b, t: (b, 0, bwd(t)))
    n_state = 2 * ML_HEADS
    return pl.pallas_call(
        _mlstm_seq_body,
        out_shape=[jax.ShapeDtypeStruct(qt.shape, F32)] * 2,
        grid=(batch, n_chunks),
        in_specs=stream(fwd, 0) + stream(bwd, 1),
        out_specs=[out_f, out_b],
        scratch_shapes=[pltpu.VMEM((n_state, L, L), F32), pltpu.VMEM((n_state, 8, L), F32),
                        pltpu.VMEM((n_state, 1, L), F32)],
        compiler_params=_params(2), name="mlstm_recurrence",
    )(k, qt, vt, gates, gcols, k, qt, vt, gates, gcols)


def kernel(x, c, ctx, c_ctx, w_mod, b_mod, g_norm, ab_w_in, ab_w_out, mla_g_q, mla_w_uq, mla_g_kv, mla_w_ukv,
           ml_conv_w, ml_conv_b, ml_w_q, ml_w_k, ml_w_v, ml_w_gate, ml_b_gate, ml_g_head, ml_skip,
           cd_w_in, cd_w_out, na_rpb, gqa_g_q, gqa_g_k, g_final):
    batch, seq, d = x.shape
    ctx_len = ctx.shape[1]
    assert ctx_len == TOKEN_TILE and seq % KEY_CHUNK == 0 and seq // GRID_W >= NA_KEY_ROWS
    dims = (batch, seq, ctx_len)

    mla_heads, mla_rope, mla_v = 8, 32, 64
    mla_nope = mla_w_uq.shape[2] // mla_heads - mla_rope
    q_lora, kv_lora = mla_g_q.shape[1], mla_g_kv.shape[1]
    ml_width = ml_conv_w.shape[2]
    mla_width = mla_heads * mla_v
    gqa_heads, gqa_dim = 8, gqa_g_q.shape[1]
    gqa_kv = (cd_w_in.shape[2] - 4 * 512 - 2 * gqa_heads * gqa_dim) // (2 * gqa_dim)
    na_width = na_rpb.shape[1] * 64

    mod_rows = -(-(batch + 1) // 8) * 8
    cvec = jnp.concatenate([c, c_ctx[None], jnp.zeros((mod_rows - batch - 1, d), F32)], axis=0)
    mod = _modulation(cvec, w_mod, b_mod)
    mod0 = mod[0].reshape(mod_rows, 1, 3 * d)
    mod1 = mod[1].reshape(mod_rows, 1, 3 * d)

    tok0 = _Tokens((x.reshape(batch * seq, d), ctx.reshape(batch * ctx_len, d)), *dims)
    w_in = ab_w_in[0]
    s1 = q_lora + kv_lora
    zcol = lambda n: jnp.zeros((d, n), w_in.dtype)
    w0 = jnp.concatenate([w_in[:, :s1], zcol(mla_nope), w_in[:, s1:s1 + mla_rope],
                          zcol(LANES - mla_nope - mla_rope), w_in[:, s1 + mla_rope:]], axis=1).astype(BF16)
    o_pa = s1 + LANES
    outs0 = ((0, o_pa, F32, 1.0, False), (o_pa, mla_width, BF16, 1.0, False),
             (o_pa + mla_width, ml_width, F32, 1.0, False), (o_pa + mla_width + ml_width, ml_width, BF16, 1.0, False))
    pa, gate_a, u, z = _in_proj(tok0, mod0, g_norm[0], w0, outs0, *dims)

    def mla_lanes(table, fill):
        n = table.shape[0]
        return jnp.concatenate([jnp.full((n, mla_nope), fill, F32), table,
                                jnp.full((n, LANES - mla_nope - mla_rope), fill, F32)], axis=-1)

    cos_a, sin_a = _rope_tables(seq, ctx_len, mla_rope)
    cos_a, sin_a = mla_lanes(cos_a, 1.0), mla_lanes(sin_a, 0.0)
    q_a, k_a, vt_a = _mla_prep(pa, cos_a, sin_a, mla_g_q[0], mla_g_kv[0], mla_w_uq[0], mla_w_ukv[0],
                               *dims, mla_heads, mla_nope, mla_rope, mla_v)
    mix_a = _flash(q_a, k_a, vt_a, gate_a, *dims, k_heads_per_pair=2, v_rows_per_pair=2 * mla_v,
                   pairs_per_kv=1, ctx_queries=True)

    xc, k_m, qt_m, vt_m, gate_pre = _mlstm_prep(u, ml_conv_w[0], ml_conv_b[0], ml_w_q[0], ml_w_k[0], ml_w_v[0],
                                                ml_w_gate[0], ml_b_gate[0], *dims)
    gates, gcols = _mlstm_gates(gate_pre)
    h_f, h_b = _mlstm_seq(k_m, qt_m, vt_m, gates, gcols, *dims)
    x1 = _out_proj(tok0, _Tokens(mix_a, *dims), (h_f, h_b, xc, z, ml_g_head[0], ml_skip[0]),
                   ab_w_out[0], mod0, batch, seq)

    tok1 = _Tokens((x1,), *dims)
    w1 = cd_w_in[0].astype(BF16)
    gq_w, gkv_w = gqa_heads * gqa_dim, gqa_kv * gqa_dim
    o_d = 4 * na_width
    outs1 = ((0, na_width, BF16, 64 ** -0.5 * LOG2E, False), (na_width, na_width, BF16, 1.0, False),
             (2 * na_width, na_width, BF16, 1.0, True), (3 * na_width, na_width, BF16, 1.0, False),
             (o_d, gq_w + gkv_w, F32, 1.0, True), (o_d + gq_w + gkv_w, gkv_w, BF16, 1.0, True),
             (o_d + gq_w + 2 * gkv_w, gq_w, BF16, 1.0, False))
    q_c, k_c, vt_c, gate_c, pd_t, vt_d, gate_d = _in_proj(tok1, mod1, g_norm[1], w1, outs1, *dims)

    mix_c = _na_attention(q_c, k_c, vt_c, _na_bias(na_rpb[0]), gate_c, *dims)

    cos_d, sin_d = _rope_tables(seq, ctx_len, gqa_dim)
    q_d, k_d = _gqa_prep(pd_t, cos_d, sin_d, gqa_g_q[0], gqa_g_k[0], *dims, gqa_heads, gqa_kv, gqa_dim)
    mix_d = _flash(q_d, k_d, vt_d, gate_d, *dims, k_heads_per_pair=1, v_rows_per_pair=gqa_dim,
                   pairs_per_kv=gqa_heads // (2 * gqa_kv), ctx_queries=False)

    out = _out_proj(tok1, _Tokens((mix_c,), *dims), mix_d[0], cd_w_out[0], mod1, batch, seq, g_final=g_final)
    return out.reshape(batch, seq, d)
```

```python
import functools

import jax
import jax.numpy as jnp
from jax import lax
from jax.experimental import pallas as pl
from jax.experimental.pallas import tpu as pltpu

F32 = jnp.float32
BF16 = jnp.bfloat16

LANES = 128
TOKEN_TILE = 256
KEY_CHUNK = 256
FLASH_TILE = 256
GRID_W = 64
WIN_R = 8
WIN_C = 16
ML_CHUNK = 128
ML_HEADS = 4
EPS = 1e-6
ROPE_BASE = 10000.0
LOG2E = 1.4426950408889634
NEG = -1e30
VMEM_LIMIT = 56 * 1024 * 1024

_NT = (((1,), (1,)), ((), ()))


def _dot(a, b):
    return jnp.dot(a, b, preferred_element_type=F32)


def _dot_nt(a, b):
    return lax.dot_general(a, b, _NT, preferred_element_type=F32)


def _silu(v):
    return v * (1.0 / (1.0 + jnp.exp(-v)))


def _log_sigmoid(v):
    return -(jnp.maximum(-v, 0.0) + jnp.log1p(jnp.exp(-jnp.abs(v))))


def _params(n_axes):
    return pltpu.CompilerParams(dimension_semantics=("arbitrary",) * n_axes, vmem_limit_bytes=VMEM_LIMIT)


def _full(shape):
    nd = len(shape)
    return pl.BlockSpec(shape, lambda *_: (0,) * nd)


def _mod_body(c_ref, w_ref, b_ref, o_ref):
    s = _silu(c_ref[...])
    o_ref[0] = _dot(s.astype(BF16), w_ref[0].astype(BF16)) + b_ref[0]


def _modulation(cvec, w_mod, b_mod):
    depth, d, n = w_mod.shape
    rows = cvec.shape[0]
    tn = n // 4
    return pl.pallas_call(
        _mod_body,
        out_shape=jax.ShapeDtypeStruct((depth, rows, n), F32),
        grid=(depth, n // tn),
        in_specs=[_full((rows, d)),
                  pl.BlockSpec((1, d, tn), lambda l, j: (l, 0, j)),
                  pl.BlockSpec((1, 1, tn), lambda l, j: (l, 0, j))],
        out_specs=pl.BlockSpec((1, rows, tn), lambda l, j: (l, 0, j)),
        compiler_params=_params(2), name="modulation",
    )(cvec, w_mod, b_mod.reshape(depth, 1, n))


class _Tokens:
    def __init__(self, arrays, batch, seq, ctx_len):
        self.arrays = arrays
        self.split = len(arrays) == 2
        self.n_lat = batch * seq // TOKEN_TILE
        self.n_ctx = batch * ctx_len // TOKEN_TILE
        self.d = arrays[0].shape[-1]

    def specs(self):
        blk = (TOKEN_TILE, self.d)
        if not self.split:
            return [pl.BlockSpec(blk, lambda i: (i, 0))]
        n_lat = self.n_lat
        return [pl.BlockSpec(blk, lambda i: (jnp.minimum(i, n_lat - 1), 0)),
                pl.BlockSpec(blk, lambda i: (jnp.maximum(i - n_lat, 0), 0))]

    def load(self, refs, i):
        if not self.split:
            return refs[0][...]
        return jnp.where(i < self.n_lat, refs[0][...], refs[1][...])


def _mod_spec(n_lat, nj, batch, width):
    return pl.BlockSpec((1, 1, width), lambda i: (jnp.where(i < n_lat, i // nj, batch), 0, 0))


def _in_proj_body(*refs, tok, outs, d):
    n_tok = len(tok.arrays)
    mod_ref, g_ref, w_ref = refs[n_tok:n_tok + 3]
    o_refs = refs[n_tok + 3:]
    i = pl.program_id(0)
    x = tok.load(refs[:n_tok], i)
    y = x * lax.rsqrt(jnp.mean(x * x, axis=-1, keepdims=True) + EPS) * g_ref[...]
    mod = mod_ref[0]
    h = y * (1.0 + mod[:, d:2 * d]) + mod[:, :d]
    acc = _dot(h.astype(BF16), w_ref[...])
    for o_ref, (c0, width, _, scale, transposed) in zip(o_refs, outs):
        v = acc[:, c0:c0 + width]
        if scale != 1.0:
            v = v * scale
        if transposed:
            o_ref[0] = v.T.astype(o_ref.dtype)
        else:
            o_ref[...] = v.astype(o_ref.dtype)


def _in_proj(tok, mod_l, g, w, outs, batch, seq, ctx_len):
    d = tok.d
    nj = seq // TOKEN_TILE
    n_lat, n_all = tok.n_lat, tok.n_lat + tok.n_ctx
    rows = n_all * TOKEN_TILE
    t_len = seq + ctx_len
    out_shape, out_specs = [], []
    for (_, width, dtype, _, transposed) in outs:
        if transposed:
            out_shape.append(jax.ShapeDtypeStruct((batch, width, t_len), dtype))
            out_specs.append(pl.BlockSpec(
                (1, width, TOKEN_TILE),
                lambda i: (jnp.where(i < n_lat, i // nj, i - n_lat), 0, jnp.where(i < n_lat, i % nj, nj))))
        else:
            out_shape.append(jax.ShapeDtypeStruct((rows, width), dtype))
            out_specs.append(pl.BlockSpec((TOKEN_TILE, width), lambda i: (i, 0)))
    body = functools.partial(_in_proj_body, tok=tok, outs=outs, d=d)
    return pl.pallas_call(
        body, out_shape=out_shape, grid=(n_all,),
        in_specs=tok.specs() + [_mod_spec(n_lat, nj, batch, 3 * d), _full((1, d)), _full(w.shape)],
        out_specs=out_specs, compiler_params=_params(1), name="in_proj",
    )(*tok.arrays, mod_l, g.reshape(1, d), w)


def _mlstm_mix(hf_ref, hb_ref, xc_ref, z_ref, gh_ref, sk_ref):
    ht = hf_ref[0].astype(F32) + hb_ref[0].astype(F32)
    L = ML_CHUNK
    normed = []
    for hd in range(ML_HEADS):
        hh = ht[hd * L:(hd + 1) * L]
        mu = jnp.mean(hh, axis=0, keepdims=True)
        var = jnp.mean(jnp.square(hh - mu), axis=0, keepdims=True)
        normed.append((hh - mu) * lax.rsqrt(var + EPS))
    hn = jnp.concatenate(normed, axis=0).T * gh_ref[...]
    return ((hn + sk_ref[...] * xc_ref[...].astype(F32)) * _silu(z_ref[...].astype(F32))).astype(BF16)


def _out_proj_body(*refs, tok, mix_a, n_b, d, final):
    n_tok, n_a = len(tok.arrays), len(mix_a.arrays)
    b_refs = refs[n_tok + n_a:n_tok + n_a + n_b]
    wa_ref, wb_ref, mod_ref = refs[n_tok + n_a + n_b:n_tok + n_a + n_b + 3]
    rest = refs[n_tok + n_a + n_b + 3:]
    i = pl.program_id(0)
    x = tok.load(refs[:n_tok], i)
    mb = b_refs[0][...] if n_b == 1 else _mlstm_mix(*b_refs)
    acc = _dot(mix_a.load(refs[n_tok:n_tok + n_a], i), wa_ref[...]) + _dot(mb, wb_ref[...])
    xn = x + mod_ref[0][:, 2 * d:] * acc
    if final:
        gf_ref, o_ref = rest
        xn = xn * lax.rsqrt(jnp.mean(xn * xn, axis=-1, keepdims=True) + EPS) * gf_ref[...]
    else:
        (o_ref,) = rest
    o_ref[...] = xn


def _out_proj(tok, mix_a, mix_b, w_out, mod_l, batch, seq, g_final=None):
    d = tok.d
    half = mix_a.d
    nj = seq // TOKEN_TILE
    n_lat = tok.n_lat
    final = g_final is not None
    n_tiles = n_lat if final else n_lat + tok.n_ctx
    wa, wb = w_out[:half].astype(BF16), w_out[half:].astype(BF16)
    tile = lambda width: pl.BlockSpec((TOKEN_TILE, width), lambda i: (i, 0))
    if isinstance(mix_b, tuple):
        hf, hb, xc, z, g_head, skip = mix_b
        width = xc.shape[1]
        feat = pl.BlockSpec((1, width, TOKEN_TILE),
                            lambda i: (jnp.where(i < n_lat, i // nj, i - n_lat), 0, jnp.where(i < n_lat, i % nj, nj)))
        b_specs = [feat, feat, tile(width), tile(width), _full((1, width)), _full((1, width))]
        b_args = [hf, hb, xc, z, g_head.reshape(1, width), skip.reshape(1, width)]
    else:
        b_specs, b_args = [tile(half)], [mix_b]
    in_specs = tok.specs() + mix_a.specs() + b_specs + [_full(wa.shape), _full(wb.shape),
                                                        _mod_spec(n_lat, nj, batch, 3 * d)]
    args = list(tok.arrays) + list(mix_a.arrays) + b_args + [wa, wb, mod_l]
    if final:
        in_specs.append(_full((1, d)))
        args.append(g_final.reshape(1, d))
    body = functools.partial(_out_proj_body, tok=tok, mix_a=mix_a, n_b=len(b_args), d=d, final=final)
    return pl.pallas_call(
        body, out_shape=jax.ShapeDtypeStruct((n_tiles * TOKEN_TILE, d), F32), grid=(n_tiles,),
        in_specs=in_specs, out_specs=tile(d), compiler_params=_params(1), name="out_proj",
    )(*args)


def _rope_tables(seq, ctx_len, rot_dim):
    t = jnp.arange(seq)
    pos = jnp.stack([t // GRID_W, t % GRID_W], axis=-1).astype(F32)
    n_freq = rot_dim // 4
    inv = ROPE_BASE ** (-jnp.arange(n_freq, dtype=F32) / n_freq)
    ang = pos[:, :, None] * inv
    cos, sin = jnp.cos(ang), jnp.sin(ang)
    cos_t = jnp.concatenate([cos[:, 0], cos[:, 0], cos[:, 1], cos[:, 1]], axis=-1)
    sin_t = jnp.concatenate([-sin[:, 0], sin[:, 0], -sin[:, 1], sin[:, 1]], axis=-1)
    cos_t = jnp.concatenate([cos_t, jnp.ones((ctx_len, rot_dim), F32)], axis=0)
    sin_t = jnp.concatenate([sin_t, jnp.zeros((ctx_len, rot_dim), F32)], axis=0)
    return cos_t, sin_t


def _rope(x, cos, sin, dist):
    lane = lax.broadcasted_iota(jnp.int32, x.shape, 1)
    first = (lane % (2 * dist)) < dist
    partner = jnp.where(first, pltpu.roll(x, LANES - dist, 1), pltpu.roll(x, dist, 1))
    return x * cos + partner * sin


def _rope_rows(x, cos, sin, dist):
    n = x.shape[0] // dist
    partner = jnp.concatenate([x[(i ^ 1) * dist:((i ^ 1) + 1) * dist] for i in range(n)], axis=0)
    return x * cos + partner * sin


def _tok_block(b, j, nj, batch):
    return jnp.where(j < nj, b * nj + j, batch * nj + b)


def _mla_prep_body(pa_ref, cos_ref, sin_ref, cost_ref, sint_ref, gq_ref, gkv_ref, wuqt_ref, wuk_ref, wuvt_ref,
                   q_ref, k_ref, vt_ref, *, heads, q_lora, kv_lora, qscale):
    pa = pa_ref[...]

    def norm(v, g_ref):
        return v * lax.rsqrt(jnp.mean(v * v, axis=-1, keepdims=True) + EPS) * g_ref[...]

    cq_t = norm(pa[:, :q_lora], gq_ref).T.astype(BF16)
    q_all = _dot(wuqt_ref[...], cq_t)
    cos_t, sin_t = cost_ref[...], sint_ref[...]
    for h in range(heads):
        qh = _rope_rows(q_all[h * LANES:(h + 1) * LANES], cos_t, sin_t, 8)
        q_ref[0, h] = (qh * qscale).astype(BF16)
    ckv = norm(pa[:, q_lora:q_lora + kv_lora], gkv_ref)
    k_nope = _dot(ckv.astype(BF16), wuk_ref[...])
    k_rope = _rope(pa[:, q_lora + kv_lora:], cos_ref[...], sin_ref[...], 8)
    for h in range(heads):
        k_ref[0, h] = (k_nope[:, h * LANES:(h + 1) * LANES] + k_rope).astype(BF16)
    vt_ref[0] = _dot(wuvt_ref[...], ckv.T.astype(BF16)).astype(BF16)


def _mla_prep(pa, cos, sin, g_q, g_kv, w_uq, w_ukv, batch, seq, ctx_len, heads, nope, rope, v_dim):
    q_lora, kv_lora = g_q.shape[0], g_kv.shape[0]
    nj = seq // TOKEN_TILE
    t_len = seq + ctx_len
    pad = LANES - nope - rope
    wuq = jnp.pad(w_uq.reshape(q_lora, heads, nope + rope), ((0, 0), (0, 0), (0, pad)))
    wuq_t = wuq.reshape(q_lora, heads * LANES).T.astype(BF16)
    wkv = w_ukv.reshape(kv_lora, heads, nope + v_dim)
    wuk = jnp.pad(wkv[..., :nope], ((0, 0), (0, 0), (0, LANES - nope))).reshape(kv_lora, heads * LANES).astype(BF16)
    wuv_t = wkv[..., nope:].reshape(kv_lora, heads * v_dim).T.astype(BF16)
    body = functools.partial(_mla_prep_body, heads=heads, q_lora=q_lora, kv_lora=kv_lora,
                             qscale=(nope + rope) ** -0.5 * LOG2E)
    head_major = pl.BlockSpec((1, heads, TOKEN_TILE, LANES), lambda j, b: (b, 0, j, 0))
    q_feature_major = pl.BlockSpec((1, heads, LANES, TOKEN_TILE), lambda j, b: (b, 0, 0, j))
    tok_table = pl.BlockSpec((TOKEN_TILE, LANES), lambda j, b: (j, 0))
    feat_table = pl.BlockSpec((LANES, TOKEN_TILE), lambda j, b: (0, j))
    return pl.pallas_call(
        body,
        out_shape=[jax.ShapeDtypeStruct((batch, heads, LANES, t_len), BF16),
                   jax.ShapeDtypeStruct((batch, heads, t_len, LANES), BF16),
                   jax.ShapeDtypeStruct((batch, heads * v_dim, t_len), BF16)],
        grid=(nj + 1, batch),
        in_specs=[pl.BlockSpec((TOKEN_TILE, pa.shape[1]), lambda j, b: (_tok_block(b, j, nj, batch), 0)),
                  tok_table, tok_table, feat_table, feat_table,
                  _full((1, q_lora)), _full((1, kv_lora)), _full(wuq_t.shape), _full(wuk.shape), _full(wuv_t.shape)],
        out_specs=[q_feature_major, head_major,
                   pl.BlockSpec((1, heads * v_dim, TOKEN_TILE), lambda j, b: (b, 0, j))],
        compiler_params=_params(2), name="mla_prep",
    )(pa, cos, sin, cos.T, sin.T, g_q.reshape(1, -1), g_kv.reshape(1, -1), wuq_t, wuk, wuv_t)


def _gqa_prep_body(pd_ref, cos_ref, sin_ref, gq_ref, gk_ref, q_ref, k_ref, *, heads, kv_heads, dim, qscale):
    cos, sin = cos_ref[...], sin_ref[...]

    def head(first_row, g_ref):
        x = pd_ref[0, first_row:first_row + dim, :]
        y = x * lax.rsqrt(jnp.mean(x * x, axis=0, keepdims=True) + EPS) * g_ref[...]
        return _rope_rows(y, cos, sin, dim // 4)

    zeros = jnp.zeros((dim, TOKEN_TILE), F32)
    for h in range(heads):
        parts = [zeros] * kv_heads
        parts[h // (heads // kv_heads)] = head(h * dim, gq_ref) * qscale
        q_ref[0, h] = jnp.concatenate(parts, axis=0).astype(BF16)
    keys = [head((heads + kv) * dim, gk_ref) for kv in range(kv_heads)]
    k_ref[0, 0] = jnp.concatenate(keys, axis=0).T.astype(BF16)


def _gqa_prep(pd_t, cos, sin, g_q, g_k, batch, seq, ctx_len, heads, kv_heads, head_dim):
    assert kv_heads * head_dim == LANES
    nj = seq // TOKEN_TILE
    t_len = seq + ctx_len
    body = functools.partial(_gqa_prep_body, heads=heads, kv_heads=kv_heads, dim=head_dim,
                             qscale=head_dim ** -0.5 * LOG2E)
    table = pl.BlockSpec((head_dim, TOKEN_TILE), lambda j, b: (0, j))
    gain = lambda g: jnp.broadcast_to(g[:, None], (head_dim, TOKEN_TILE))
    return pl.pallas_call(
        body,
        out_shape=[jax.ShapeDtypeStruct((batch, heads, LANES, t_len), BF16),
                   jax.ShapeDtypeStruct((batch, 1, t_len, LANES), BF16)],
        grid=(nj + 1, batch),
        in_specs=[pl.BlockSpec((1, pd_t.shape[1], TOKEN_TILE), lambda j, b: (b, 0, j)), table, table,
                  _full((head_dim, TOKEN_TILE)), _full((head_dim, TOKEN_TILE))],
        out_specs=[pl.BlockSpec((1, heads, LANES, TOKEN_TILE), lambda j, b: (b, 0, 0, j)),
                   pl.BlockSpec((1, 1, TOKEN_TILE, LANES), lambda j, b: (b, 0, j, 0))],
        compiler_params=_params(2), name="gqa_prep",
    )(pd_t, cos.T, sin.T, gain(g_q), gain(g_k))


def _score_pass(q, k_ref, k_head, chunks, s_ref):
    m = None
    for (st, sz) in chunks:
        s_t = _dot(k_ref[0, k_head, st:st + sz, :], q)
        s_ref[st:st + sz, :] = s_t
        cm = jnp.max(s_t, axis=0, keepdims=True)
        m = cm if m is None else jnp.maximum(m, cm)
    return m


def _value_pass(s_ref, m, vt_ref, v_rows, chunks):
    l = jnp.zeros(m.shape, F32)
    acc = jnp.zeros((v_rows.stop - v_rows.start, m.shape[1]), F32)
    for (st, sz) in chunks:
        p = jnp.exp2(s_ref[st:st + sz, :] - m)
        l = l + jnp.sum(p, axis=0, keepdims=True)
        acc = acc + _dot(vt_ref[0, v_rows, st:st + sz], p.astype(BF16))
    return acc / l


def _gated_store(outs, g_ref, o_ref):
    o2 = jnp.concatenate(outs, axis=0).T
    o_ref[...] = (o2 * _silu(g_ref[...].astype(F32))).astype(o_ref.dtype)


def _flash_body(q_ref, k_ref, vt_ref, g_ref, o_ref, s_scr, m_scr, *, k_sel, v_off, v_dim, chunks):
    j = pl.program_id(2)

    @pl.when((pl.program_id(0) == 0) & (pl.program_id(1) == 0) & (j == 0))
    def _():
        s_scr[...] = jnp.zeros(s_scr.shape, F32)
        m_scr[...] = jnp.zeros(m_scr.shape, F32)

    def step(slot, prev):
        heads = (0, 1)
        q = [q_ref[0, a] for a in heads]
        m_prev = [m_scr[prev, a] for a in heads]
        m = [None, None]
        l = [jnp.zeros(m_prev[a].shape, F32) for a in heads]
        acc = [jnp.zeros((v_dim, m_prev[a].shape[1]), F32) for a in heads]
        for (st, sz) in chunks:
            for a in heads:
                s_t = _dot(k_ref[0, k_sel[a], st:st + sz, :], q[a])
                s_scr[slot, a, st:st + sz, :] = s_t
                cm = jnp.max(s_t, axis=0, keepdims=True)
                m[a] = cm if m[a] is None else jnp.maximum(m[a], cm)
                p = jnp.exp2(s_scr[prev, a, st:st + sz, :] - m_prev[a])
                l[a] = l[a] + jnp.sum(p, axis=0, keepdims=True)
                acc[a] = acc[a] + _dot(vt_ref[0, v_off[a]:v_off[a] + v_dim, st:st + sz], p.astype(BF16))
        for a in heads:
            m_scr[slot, a] = m[a]
        _gated_store([acc[a] / l[a] for a in heads], g_ref, o_ref)

    for parity in (0, 1):
        pl.when(j % 2 == parity)(functools.partial(step, parity, 1 - parity))


def _flash_ctx_body(q_ref, k_ref, vt_ref, g_ref, o_ref, s_scr, *, k_sel, v_off, v_dim, chunks):
    outs = []
    for a in range(2):
        m = _score_pass(q_ref[0, a], k_ref, k_sel[a], chunks, s_scr.at[a])
        outs.append(_value_pass(s_scr.at[a], m, vt_ref, slice(v_off[a], v_off[a] + v_dim), chunks))
    _gated_store(outs, g_ref, o_ref)


def _flash(q, k, vt, gate, batch, seq, ctx_len, k_heads_per_pair, v_rows_per_pair, pairs_per_kv, ctx_queries):
    heads = q.shape[1]
    t_len = seq + ctx_len
    nj = seq // TOKEN_TILE
    v_dim = LANES // 2
    k_sel = (0, 1) if k_heads_per_pair == 2 else (0, 0)
    v_off = (0, v_dim) if v_rows_per_pair == 2 * v_dim else (0, 0)
    k_blocks = k.shape[1] // k_heads_per_pair
    k_block = lambda p: (p // pairs_per_kv) % k_blocks
    chunks = tuple((c * KEY_CHUNK, KEY_CHUNK) for c in range(t_len // KEY_CHUNK))
    static = dict(k_sel=k_sel, v_off=v_off, v_dim=v_dim)

    width = gate.shape[1]
    tq = FLASH_TILE
    n_q = seq // tq
    done = lambda b, p, j: (b * n_q + jnp.maximum(j - 1, 0), p)
    out = pl.pallas_call(
        functools.partial(_flash_body, chunks=chunks, **static),
        out_shape=jax.ShapeDtypeStruct((batch * seq, width), BF16),
        grid=(batch, heads // 2, n_q + 1),
        in_specs=[pl.BlockSpec((1, 2, LANES, tq), lambda b, p, j: (b, p, 0, jnp.minimum(j, n_q - 1))),
                  pl.BlockSpec((1, k_heads_per_pair, t_len, LANES), lambda b, p, j: (b, k_block(p), 0, 0)),
                  pl.BlockSpec((1, v_rows_per_pair, t_len), lambda b, p, j: (b, p // pairs_per_kv, 0)),
                  pl.BlockSpec((tq, LANES), done)],
        out_specs=pl.BlockSpec((tq, LANES), done),
        scratch_shapes=[pltpu.VMEM((2, 2, t_len, tq), F32), pltpu.VMEM((2, 2, 1, tq), F32)],
        compiler_params=_params(3), name="flash_attention",
    )(q, k, vt, gate)
    if not ctx_queries:
        return (out,)

    out_ctx = pl.pallas_call(
        functools.partial(_flash_ctx_body, chunks=((0, ctx_len),), **static),
        out_shape=jax.ShapeDtypeStruct((batch * ctx_len, width), BF16),
        grid=(batch, heads // 2),
        in_specs=[pl.BlockSpec((1, 2, LANES, TOKEN_TILE), lambda b, p: (b, p, 0, nj)),
                  pl.BlockSpec((1, k_heads_per_pair, ctx_len, LANES), lambda b, p: (b, k_block(p), seq // ctx_len, 0)),
                  pl.BlockSpec((1, v_rows_per_pair, ctx_len), lambda b, p: (b, p // pairs_per_kv, seq // ctx_len)),
                  pl.BlockSpec((TOKEN_TILE, LANES), lambda b, p: (batch * nj + b, p))],
        out_specs=pl.BlockSpec((TOKEN_TILE, LANES), lambda b, p: (b, p)),
        scratch_shapes=[pltpu.VMEM((2, ctx_len, TOKEN_TILE), F32)],
        compiler_params=_params(2), name="flash_attention_ctx",
    )(q, k, vt, gate)
    return out, out_ctx


NA_KEY_ROWS = 12
NA_STEP_ROWS = TOKEN_TILE // GRID_W
NA_VARIANTS = 3


def _na_rel_row(variant, rr, a):
    if variant == 0:
        valid, dr = a < WIN_R, a - rr
    elif variant == 1:
        dr = a - WIN_R // 2 - rr
        valid = -(WIN_R // 2) <= dr < WIN_R // 2
    else:
        valid, dr = a >= NA_KEY_ROWS - WIN_R, a - (NA_KEY_ROWS - NA_STEP_ROWS) - rr
    return dr if valid else None


def _na_bias_body(rpb_ref, o_ref, blk_scr):
    p = pl.program_id(0)
    shape = (GRID_W, LANES)
    kc = lax.broadcasted_iota(jnp.int32, shape, 0)
    lane = lax.broadcasted_iota(jnp.int32, shape, 1)
    qc = lane % GRID_W
    upper = lane >= GRID_W
    rel = kc - qc + (WIN_C - 1)
    c0 = jnp.clip(qc - WIN_C // 2, 0, GRID_W - WIN_C)
    col_ok = (kc >= c0) & (kc < c0 + WIN_C)
    n_rel_r, n_rel_c = 2 * WIN_R - 1, 2 * WIN_C - 1

    def block(dd, carry):
        base0 = (2 * p) * (n_rel_r * n_rel_c) + dd * n_rel_c
        base1 = base0 + n_rel_r * n_rel_c
        val = jnp.zeros(shape, F32)
        for jj in range(n_rel_c):
            val = jnp.where(rel == jj, jnp.where(upper, rpb_ref[base1 + jj], rpb_ref[base0 + jj]), val)
        blk_scr[dd] = jnp.where(col_ok, val * LOG2E, NEG)
        return carry

    lax.fori_loop(0, n_rel_r, block, 0)
    outside = jnp.full(shape, NEG, F32)
    for variant in range(NA_VARIANTS):
        for rr in range(NA_STEP_ROWS):
            for a in range(NA_KEY_ROWS):
                dr = _na_rel_row(variant, rr, a)
                o_ref[0, variant, rr, a * GRID_W:(a + 1) * GRID_W, :] = (
                    outside if dr is None else blk_scr[dr + WIN_R - 1])


def _na_bias(rpb):
    heads = rpb.shape[0]
    tab = (NA_VARIANTS, NA_STEP_ROWS, NA_KEY_ROWS * GRID_W, LANES)
    return pl.pallas_call(
        _na_bias_body,
        out_shape=jax.ShapeDtypeStruct((heads // 2,) + tab, F32),
        grid=(heads // 2,),
        in_specs=[pl.BlockSpec(memory_space=pltpu.SMEM)],
        out_specs=pl.BlockSpec((1,) + tab, lambda p: (p, 0, 0, 0, 0)),
        scratch_shapes=[pltpu.VMEM((2 * WIN_R - 1, GRID_W, LANES), F32)],
        compiler_params=_params(1), name="na_bias",
    )(rpb.reshape(-1))


NA_CHUNK = 256


def _na_body(q_ref, kl_ref, kc_ref, vt_ref, bias_ref, g_ref, o_ref, s_scr, m_scr, *, seq, ctx_len, n_steps):
    j = pl.program_id(2)
    n_rows = seq // GRID_W
    n_loc = NA_KEY_ROWS * GRID_W // NA_CHUNK
    cols = NA_STEP_ROWS * LANES
    lower = lax.broadcasted_iota(jnp.int32, (GRID_W, LANES), 1) < GRID_W

    @pl.when((pl.program_id(0) == 0) & (pl.program_id(1) == 0) & (j == 0))
    def _():
        s_scr[...] = jnp.zeros(s_scr.shape, F32)
        m_scr[...] = jnp.zeros(m_scr.shape, F32)

    def span_start(step):
        first_row = jnp.clip(NA_STEP_ROWS * step - WIN_R // 2, 0, n_rows - NA_KEY_ROWS)
        return first_row * GRID_W

    def step(slot, prev):
        js = jnp.minimum(j, n_steps - 1)
        variant = jnp.where(js == 0, 0, jnp.where(js == n_steps - 1, 2, 1))
        k0, v0 = span_start(js), span_start(jnp.maximum(j - 1, 0))
        q = q_ref[...]
        zero = jnp.zeros((GRID_W, LANES), q.dtype)
        parts = []
        for rr in range(NA_STEP_ROWS):
            q_r = q[rr * GRID_W:(rr + 1) * GRID_W]
            parts += [jnp.where(lower, q_r, zero), jnp.where(lower, zero, q_r)]
        q2 = jnp.concatenate(parts, axis=0)
        m_prev = m_scr[prev]
        m = None
        l = jnp.zeros((1, cols), F32)
        acc = jnp.zeros((LANES, cols), F32)
        for c in range(n_loc + ctx_len // NA_CHUNK):
            rows = slice((c % n_loc) * NA_CHUNK, (c % n_loc + 1) * NA_CHUNK)
            if c < n_loc:
                s = _dot_nt(kl_ref[pl.ds(pl.multiple_of(k0 + c * NA_CHUNK, NA_CHUNK), NA_CHUNK), :], q2)
                s = jnp.concatenate([s[:, rr * LANES:(rr + 1) * LANES] + bias_ref[0, variant, rr, rows, :]
                                     for rr in range(NA_STEP_ROWS)], axis=1)
                v_chunk = vt_ref[0, :, pl.ds(pl.multiple_of(v0 + c * NA_CHUNK, NA_CHUNK), NA_CHUNK)]
            else:
                s = _dot_nt(kc_ref[rows, :], q2)
                v_chunk = vt_ref[0, :, seq + rows.start:seq + rows.stop]
            s_scr[slot, c] = s
            cm = jnp.max(s, axis=0, keepdims=True)
            m = cm if m is None else jnp.maximum(m, cm)
            p = jnp.exp2(s_scr[prev, c] - m_prev)
            l = l + jnp.sum(p, axis=0, keepdims=True)
            acc = acc + _dot(v_chunk, p.astype(BF16))
        m_scr[slot] = m
        r_n = (acc / l).T
        outs = [jnp.where(lower, r_n[rr * LANES:rr * LANES + GRID_W], r_n[rr * LANES + GRID_W:(rr + 1) * LANES])
                for rr in range(NA_STEP_ROWS)]
        o = jnp.concatenate(outs, axis=0)
        o_ref[...] = (o * _silu(g_ref[...].astype(F32))).astype(o_ref.dtype)

    for parity in (0, 1):
        pl.when(j % 2 == parity)(functools.partial(step, parity, 1 - parity))


def _na_attention(q, k, vt, bias, gate, batch, seq, ctx_len):
    pairs = q.shape[1] // LANES
    t_len = seq + ctx_len
    n_steps = seq // TOKEN_TILE
    n_chunks = (NA_KEY_ROWS * GRID_W + ctx_len) // NA_CHUNK
    cols = NA_STEP_ROWS * LANES
    done = lambda b, p, j: (b * n_steps + jnp.maximum(j - 1, 0), p)
    body = functools.partial(_na_body, seq=seq, ctx_len=ctx_len, n_steps=n_steps)
    return pl.pallas_call(
        body, out_shape=jax.ShapeDtypeStruct((batch * seq, q.shape[1]), BF16),
        grid=(batch, pairs, n_steps + 1),
        in_specs=[pl.BlockSpec((TOKEN_TILE, LANES), lambda b, p, j: (b * n_steps + jnp.minimum(j, n_steps - 1), p)),
                  pl.BlockSpec((seq, LANES), lambda b, p, j: (b, p)),
                  pl.BlockSpec((ctx_len, LANES), lambda b, p, j: (batch * seq // ctx_len + b, p)),
                  pl.BlockSpec((1, LANES, t_len), lambda b, p, j: (b, p, 0)),
                  pl.BlockSpec((1,) + bias.shape[1:], lambda b, p, j: (p, 0, 0, 0, 0)),
                  pl.BlockSpec((TOKEN_TILE, LANES), done)],
        out_specs=pl.BlockSpec((TOKEN_TILE, LANES), done),
        scratch_shapes=[pltpu.VMEM((2, n_chunks, NA_CHUNK, cols), F32), pltpu.VMEM((2, 1, cols), F32)],
        compiler_params=_params(3), name="neighborhood_attention",
    )(q, k, k, vt, bias, gate)


def _seg_scans(jobs, use_max=False):
    vals = [v for v, _ in jobs]
    n = vals[0].shape[1]
    lane = lax.broadcasted_iota(jnp.int32, vals[0].shape, 1) % ML_CHUNK
    k = 1
    while k < ML_CHUNK:
        for i, (_, reverse) in enumerate(jobs):
            v = vals[i]
            if reverse:
                ok, shifted = lane < ML_CHUNK - k, pltpu.roll(v, n - k, 1)
            else:
                ok, shifted = lane >= k, pltpu.roll(v, k, 1)
            vals[i] = jnp.maximum(v, jnp.where(ok, shifted, NEG)) if use_max else v + jnp.where(ok, shifted, 0.0)
        k *= 2
    return vals


ML_GATE_ROWS = 40


def _mlstm_prep_body(u_ref, up_ref, un_ref, cw_ref, cb_ref, wqk_ref, wv_ref, wg_ref, bg_ref,
                     xc_ref, k_ref, qt_ref, vt_ref, pre_ref, *, nj, width, kscale):
    j = pl.program_id(1)
    u = u_ref[...]
    row = lax.broadcasted_iota(jnp.int32, u.shape, 0)
    prev = jnp.where((j > 0) & (j < nj), up_ref[7:8, :], 0.0)
    nxt = jnp.where(j < nj - 1, un_ref[0:1, :], 0.0)
    u_m1 = jnp.where(row == 0, prev, pltpu.roll(u, 1, 0))
    u_p1 = jnp.where(row == TOKEN_TILE - 1, nxt, pltpu.roll(u, TOKEN_TILE - 1, 0))
    cw = cw_ref[...]
    xc = _silu(u_m1 * cw[0:1] + u * cw[1:2] + u_p1 * cw[2:3] + cb_ref[...])
    xcb = xc.astype(BF16)
    xc_ref[...] = xcb
    qk = _dot(xcb, wqk_ref[...])
    v = _dot(u.astype(BF16), wv_ref[...])
    qb, kb, vb = qk[:, :width].astype(BF16), qk[:, width:].astype(BF16), v.astype(BF16)
    k_ref[...] = (qk[:, width:] * kscale).astype(BF16)
    qt_ref[0] = qk[:, :width].T.astype(BF16)
    vt_ref[0] = v.T.astype(BF16)
    pre_ref[0] = _dot_nt(wg_ref[0], qb) + _dot_nt(wg_ref[1], kb) + _dot_nt(wg_ref[2], vb) + bg_ref[...]


def _mlstm_gates_body(pre_ref, g_ref, gc_ref):
    pre = pre_ref[0]
    i8 = [pre[16 * d:16 * d + 8] for d in range(2)]
    f8 = [_log_sigmoid(pre[16 * d + 8:16 * d + 16]) for d in range(2)]
    b0, b0_rev, b1, b1_rev = _seg_scans([(f8[0], False), (f8[0], True), (f8[1], True), (f8[1], False)])
    b8, b_last = [b0, b1], [b0 + b0_rev - f8[0], b1 + b1_rev - f8[1]]
    r8 = [i8[d] - b8[d] for d in range(2)]
    c0, c0_rev, c1, c1_rev = _seg_scans([(r8[0], False), (r8[0], True), (r8[1], True), (r8[1], False)], use_max=True)
    c8, r_max = [c0, c1], [jnp.maximum(c0, c0_rev), jnp.maximum(c1, c1_rev)]
    for d in range(2):
        g_ref[0, d * ML_GATE_ROWS:(d + 1) * ML_GATE_ROWS] = jnp.concatenate(
            [-c8[d], b8[d] + c8[d], jnp.exp(r8[d] - r_max[d]), b_last[d], b_last[d] + r_max[d]], axis=0)
    pad = jnp.zeros((LANES - 16, pre.shape[1]), F32)
    gc_ref[0] = jnp.concatenate(r8 + [pad], axis=0).T


def _mlstm_gates(pre):
    batch, _, t_len = pre.shape
    return pl.pallas_call(
        _mlstm_gates_body,
        out_shape=[jax.ShapeDtypeStruct((batch, 2 * ML_GATE_ROWS, t_len), F32),
                   jax.ShapeDtypeStruct((batch, t_len, LANES), F32)],
        grid=(batch,),
        in_specs=[pl.BlockSpec((1,) + pre.shape[1:], lambda b: (b, 0, 0))],
        out_specs=[pl.BlockSpec((1, 2 * ML_GATE_ROWS, t_len), lambda b: (b, 0, 0)),
                   pl.BlockSpec((1, t_len, LANES), lambda b: (b, 0, 0))],
        compiler_params=_params(1), name="mlstm_gates",
    )(pre)


def _mlstm_prep(u, conv_w, conv_b, w_q, w_k, w_v, w_gate, b_gate, batch, seq, ctx_len):
    heads, hd = w_q.shape[0], w_q.shape[1]
    assert heads == ML_HEADS and hd == ML_CHUNK
    width = heads * hd
    nj = seq // TOKEN_TILE
    t_len = seq + ctx_len
    rows = u.shape[0]

    def block_diag(w):
        eye = jnp.eye(heads, dtype=w.dtype)
        return (eye[:, None, :, None] * w[:, :, None, :]).reshape(width, width)

    wqk = jnp.concatenate([block_diag(w_q), block_diag(w_k)], axis=1).astype(BF16)
    wv = block_diag(w_v).astype(BF16)
    wg = w_gate.reshape(2, heads, 3, hd, 2, heads).transpose(2, 0, 4, 5, 1, 3).reshape(3, 2, 2, heads, width)
    wg = jnp.pad(wg, ((0, 0), (0, 0), (0, 0), (0, 8 - heads), (0, 0))).reshape(3, 32, width).astype(BF16)
    bg = jnp.pad(b_gate.reshape(2, 2, heads), ((0, 0), (0, 0), (0, 8 - heads))).reshape(32, 1)
    n_halo = rows // 8
    tokb = lambda b, j: _tok_block(b, j, nj, batch)
    tile = pl.BlockSpec((TOKEN_TILE, width), lambda b, j: (tokb(b, j), 0))
    feat = pl.BlockSpec((1, width, TOKEN_TILE), lambda b, j: (b, 0, j))
    body = functools.partial(_mlstm_prep_body, nj=nj, width=width, kscale=hd ** -0.5)
    per_tile = TOKEN_TILE // 8
    return pl.pallas_call(
        body,
        out_shape=[jax.ShapeDtypeStruct((rows, width), BF16)] * 2
        + [jax.ShapeDtypeStruct((batch, width, t_len), BF16)] * 2
        + [jax.ShapeDtypeStruct((batch, 32, t_len), F32)],
        grid=(batch, nj + 1),
        in_specs=[tile,
                  pl.BlockSpec((8, width), lambda b, j: (jnp.maximum(tokb(b, j) * per_tile - 1, 0), 0)),
                  pl.BlockSpec((8, width), lambda b, j: (jnp.minimum((tokb(b, j) + 1) * per_tile, n_halo - 1), 0)),
                  _full((3, width)), _full((1, width)), _full(wqk.shape), _full(wv.shape),
                  _full(wg.shape), _full((32, 1))],
        out_specs=[tile, tile, feat, feat, pl.BlockSpec((1, 32, TOKEN_TILE), lambda b, j: (b, 0, j))],
        compiler_params=_params(2), name="mlstm_prep",
    )(u, u, u, conv_w, conv_b.reshape(1, width), wqk, wv, wg, bg)


def _mlstm_seq_body(kf, qtf, vtf, gf, gcf, kb, qtb, vtb, gb, gcb, hf_ref, hb_ref, c_s, n_s, m_s):
    t = pl.program_id(1)
    L = ML_CHUNK
    subs = TOKEN_TILE // L

    @pl.when(t == 0)
    def _():
        c_s[...] = jnp.zeros(c_s.shape, F32)
        n_s[...] = jnp.zeros(n_s.shape, F32)
        m_s[...] = jnp.zeros(m_s.shape, F32)

    si = lax.broadcasted_iota(jnp.int32, (L, L), 0)
    li = lax.broadcasted_iota(jnp.int32, (L, L), 1)
    streams = ((kf, qtf, vtf, gf, gcf, hf_ref, si <= li), (kb, qtb, vtb, gb, gcb, hb_ref, si >= li))
    units = {}
    for d, (k_ref, qt_ref, vt_ref, g_ref, gc_ref, h_ref, incl) in enumerate(streams):
        for sub in range(subs):
            toks = slice(sub * L, (sub + 1) * L)
            g = g_ref[0, :, toks]
            gc = gc_ref[0, toks, :]
            for h in range(ML_HEADS):
                cols = slice(h * L, (h + 1) * L)
                k, qt, vt = k_ref[toks, cols], qt_ref[0, cols, toks], vt_ref[0, cols, toks]
                neg_c, m_loc, w0, b_last, g_max = (g[8 * i + h:8 * i + h + 1] for i in range(5))
                r_col = gc[:, 8 * d + h:8 * d + h + 1]
                p0 = jnp.where(incl, jnp.exp(r_col + neg_c), 0.0) * _dot(k, qt)
                s_sum = jnp.sum(p0, axis=0, keepdims=True)
                intra = _dot(vt, p0.astype(BF16))
                c_inc = _dot((vt.astype(F32) * w0).astype(BF16), k)
                n_inc = _dot(jnp.broadcast_to(w0, (8, L)).astype(BF16), k)
                units[d, sub, h] = (h_ref, cols, toks, qt, neg_c, m_loc, b_last, g_max, s_sum, intra, c_inc, n_inc)
    for stage in range(subs):
        for d in range(2):
            sub = stage if d == 0 else subs - 1 - stage
            for h in range(ML_HEADS):
                idx = d * ML_HEADS + h
                h_ref, cols, toks, qt, neg_c, m_loc, b_last, g_max, s_sum, intra, c_inc, n_inc = units[d, sub, h]
                c_st, n_st, m_st = c_s[idx], n_s[idx], m_s[idx]
                cn = _dot(jnp.concatenate([c_st, n_st], axis=0).astype(BF16), qt)
                delta = jnp.maximum(m_st + neg_c, 0.0)
                e_intra = jnp.exp(-delta)
                w_inter = jnp.exp(m_st + neg_c - delta)
                num = w_inter * cn[:L] + e_intra * intra
                den = w_inter * cn[L:L + 1] + e_intra * s_sum
                h_out = num / jnp.maximum(jnp.abs(den), jnp.exp(-(m_loc + delta)))
                h_ref[0, cols, toks] = h_out.astype(h_ref.dtype)
                m_new = jnp.maximum(b_last + m_st, g_max)
                decay = jnp.exp(b_last + m_st - m_new)
                gain = jnp.exp(g_max - m_new)
                c_s[idx] = decay * c_st + gain * c_inc
                n_s[idx] = decay * n_st + gain * n_inc
                m_s[idx] = m_new


def _mlstm_seq(k, qt, vt, gates, gcols, batch, seq, ctx_len):
    width = k.shape[1]
    L = ML_CHUNK
    assert ctx_len == TOKEN_TILE
    nj = seq // TOKEN_TILE
    fwd = lambda t: (t + nj) % (nj + 1)
    bwd = lambda t: nj - t

    def stream(tile_of, d):
        feat = pl.BlockSpec((1, width, TOKEN_TILE), lambda b, t: (b, 0, tile_of(t)))
        return [pl.BlockSpec((TOKEN_TILE, width), lambda b, t: (_tok_block(b, tile_of(t), nj, batch), 0)), feat, feat,
                pl.BlockSpec((1, ML_GATE_ROWS, TOKEN_TILE), lambda b, t: (b, d, tile_of(t))),
                pl.BlockSpec((1, TOKEN_TILE, LANES), lambda b, t: (b, tile_of(t), 0))]

    out_f = pl.BlockSpec((1, width, TOKEN_TILE), lambda b, t: (b, 0, fwd(t)))
    out_b = pl.BlockSpec((1, width, TOKEN_TILE), lambda b, t: (b, 0, bwd(t)))
    n_state = 2 * ML_HEADS
    return pl.pallas_call(
        _mlstm_seq_body,
        out_shape=[jax.ShapeDtypeStruct(qt.shape, BF16)] * 2,
        grid=(batch, nj + 1),
        in_specs=stream(fwd, 0) + stream(bwd, 1),
        out_specs=[out_f, out_b],
        scratch_shapes=[pltpu.VMEM((n_state, L, L), F32), pltpu.VMEM((n_state, 8, L), F32),
                        pltpu.VMEM((n_state, 1, L), F32)],
        compiler_params=_params(2), name="mlstm_recurrence",
    )(k, qt, vt, gates, gcols, k, qt, vt, gates, gcols)


def kernel(x, c, ctx, c_ctx, w_mod, b_mod, g_norm, ab_w_in, ab_w_out, mla_g_q, mla_w_uq, mla_g_kv, mla_w_ukv,
           ml_conv_w, ml_conv_b, ml_w_q, ml_w_k, ml_w_v, ml_w_gate, ml_b_gate, ml_g_head, ml_skip,
           cd_w_in, cd_w_out, na_rpb, gqa_g_q, gqa_g_k, g_final):
    batch, seq, d = x.shape
    ctx_len = ctx.shape[1]
    assert ctx_len == TOKEN_TILE and seq % KEY_CHUNK == 0 and seq // GRID_W >= NA_KEY_ROWS
    dims = (batch, seq, ctx_len)

    mla_heads, mla_rope, mla_v = 8, 32, 64
    mla_nope = mla_w_uq.shape[2] // mla_heads - mla_rope
    q_lora, kv_lora = mla_g_q.shape[1], mla_g_kv.shape[1]
    ml_width = ml_conv_w.shape[2]
    mla_width = mla_heads * mla_v
    gqa_heads, gqa_dim = 8, gqa_g_q.shape[1]
    gqa_kv = (cd_w_in.shape[2] - 4 * 512 - 2 * gqa_heads * gqa_dim) // (2 * gqa_dim)
    na_width = na_rpb.shape[1] * 64

    mod_rows = -(-(batch + 1) // 8) * 8
    cvec = jnp.concatenate([c, c_ctx[None], jnp.zeros((mod_rows - batch - 1, d), F32)], axis=0)
    mod = _modulation(cvec, w_mod, b_mod)
    mod0 = mod[0].reshape(mod_rows, 1, 3 * d)
    mod1 = mod[1].reshape(mod_rows, 1, 3 * d)

    tok0 = _Tokens((x.reshape(batch * seq, d), ctx.reshape(batch * ctx_len, d)), *dims)
    w_in = ab_w_in[0]
    s1 = q_lora + kv_lora
    zcol = lambda n: jnp.zeros((d, n), w_in.dtype)
    w0 = jnp.concatenate([w_in[:, :s1], zcol(mla_nope), w_in[:, s1:s1 + mla_rope],
                          zcol(LANES - mla_nope - mla_rope), w_in[:, s1 + mla_rope:]], axis=1).astype(BF16)
    o_pa = s1 + LANES
    outs0 = ((0, o_pa, F32, 1.0, False), (o_pa, mla_width, BF16, 1.0, False),
             (o_pa + mla_width, ml_width, F32, 1.0, False), (o_pa + mla_width + ml_width, ml_width, BF16, 1.0, False))
    pa, gate_a, u, z = _in_proj(tok0, mod0, g_norm[0], w0, outs0, *dims)

    def mla_lanes(table, fill):
        n = table.shape[0]
        return jnp.concatenate([jnp.full((n, mla_nope), fill, F32), table,
                                jnp.full((n, LANES - mla_nope - mla_rope), fill, F32)], axis=-1)

    cos_a, sin_a = _rope_tables(seq, ctx_len, mla_rope)
    cos_a, sin_a = mla_lanes(cos_a, 1.0), mla_lanes(sin_a, 0.0)
    q_a, k_a, vt_a = _mla_prep(pa, cos_a, sin_a, mla_g_q[0], mla_g_kv[0], mla_w_uq[0], mla_w_ukv[0],
                               *dims, mla_heads, mla_nope, mla_rope, mla_v)
    mix_a = _flash(q_a, k_a, vt_a, gate_a, *dims, k_heads_per_pair=2, v_rows_per_pair=2 * mla_v,
                   pairs_per_kv=1, ctx_queries=True)

    xc, k_m, qt_m, vt_m, gate_pre = _mlstm_prep(u, ml_conv_w[0], ml_conv_b[0], ml_w_q[0], ml_w_k[0], ml_w_v[0],
                                                ml_w_gate[0], ml_b_gate[0], *dims)
    gates, gcols = _mlstm_gates(gate_pre)
    h_f, h_b = _mlstm_seq(k_m, qt_m, vt_m, gates, gcols, *dims)
    x1 = _out_proj(tok0, _Tokens(mix_a, *dims), (h_f, h_b, xc, z, ml_g_head[0], ml_skip[0]),
                   ab_w_out[0], mod0, batch, seq)

    tok1 = _Tokens((x1,), *dims)
    w1 = cd_w_in[0].astype(BF16)
    gq_w, gkv_w = gqa_heads * gqa_dim, gqa_kv * gqa_dim
    o_d = 4 * na_width
    outs1 = ((0, na_width, BF16, 64 ** -0.5 * LOG2E, False), (na_width, na_width, BF16, 1.0, False),
             (2 * na_width, na_width, BF16, 1.0, True), (3 * na_width, na_width, BF16, 1.0, False),
             (o_d, gq_w + gkv_w, F32, 1.0, True), (o_d + gq_w + gkv_w, gkv_w, BF16, 1.0, True),
             (o_d + gq_w + 2 * gkv_w, gq_w, BF16, 1.0, False))
    q_c, k_c, vt_c, gate_c, pd_t, vt_d, gate_d = _in_proj(tok1, mod1, g_norm[1], w1, outs1, *dims)

    mix_c = _na_attention(q_c, k_c, vt_c, _na_bias(na_rpb[0]), gate_c, *dims)

    cos_d, sin_d = _rope_tables(seq, ctx_len, gqa_dim)
    q_d, k_d = _gqa_prep(pd_t, cos_d, sin_d, gqa_g_q[0], gqa_g_k[0], *dims, gqa_heads, gqa_kv, gqa_dim)
    mix_d = _flash(q_d, k_d, vt_d, gate_d, *dims, k_heads_per_pair=1, v_rows_per_pair=gqa_dim,
                   pairs_per_kv=gqa_heads // (2 * gqa_kv), ctx_queries=False)

    out = _out_proj(tok1, _Tokens((mix_c,), *dims), mix_d[0], cd_w_out[0], mod1, batch, seq, g_final=g_final)
    return out.reshape(batch, seq, d)
```

```python
import functools

import jax
import jax.numpy as jnp
from jax import lax
from jax.experimental import pallas as pl
from jax.experimental.pallas import tpu as pltpu

F32 = jnp.float32
BF16 = jnp.bfloat16

LANES = 128
TOKEN_TILE = 256
KEY_CHUNK = 256
FLASH_TILE = 256
FLASH_STEP_TILES = 2
GRID_W = 64
WIN_R = 8
WIN_C = 16
ML_CHUNK = 128
ML_HEADS = 4
EPS = 1e-6
ROPE_BASE = 10000.0
LOG2E = 1.4426950408889634
NEG = -1e30
VMEM_LIMIT = 56 * 1024 * 1024

_NT = (((1,), (1,)), ((), ()))


def _dot(a, b):
    return jnp.dot(a, b, preferred_element_type=F32)


def _dot_nt(a, b):
    return lax.dot_general(a, b, _NT, preferred_element_type=F32)


def _silu(v):
    return v * (1.0 / (1.0 + jnp.exp(-v)))


def _log_sigmoid(v):
    return -(jnp.maximum(-v, 0.0) + jnp.log1p(jnp.exp(-jnp.abs(v))))


def _params(n_axes):
    return pltpu.CompilerParams(dimension_semantics=("arbitrary",) * n_axes, vmem_limit_bytes=VMEM_LIMIT)


def _full(shape):
    nd = len(shape)
    return pl.BlockSpec(shape, lambda *_: (0,) * nd)


def _mod_body(c_ref, w_ref, b_ref, o_ref):
    s = _silu(c_ref[...])
    o_ref[0] = _dot(s.astype(BF16), w_ref[0].astype(BF16)) + b_ref[0]


def _modulation(cvec, w_mod, b_mod):
    depth, d, n = w_mod.shape
    rows = cvec.shape[0]
    tn = n // 4
    return pl.pallas_call(
        _mod_body,
        out_shape=jax.ShapeDtypeStruct((depth, rows, n), F32),
        grid=(depth, n // tn),
        in_specs=[_full((rows, d)),
                  pl.BlockSpec((1, d, tn), lambda l, j: (l, 0, j)),
                  pl.BlockSpec((1, 1, tn), lambda l, j: (l, 0, j))],
        out_specs=pl.BlockSpec((1, rows, tn), lambda l, j: (l, 0, j)),
        compiler_params=_params(2), name="modulation",
    )(cvec, w_mod, b_mod.reshape(depth, 1, n))


class _Tokens:
    def __init__(self, arrays, batch, seq, ctx_len):
        self.arrays = arrays
        self.split = len(arrays) == 2
        self.n_lat = batch * seq // TOKEN_TILE
        self.n_ctx = batch * ctx_len // TOKEN_TILE
        self.d = arrays[0].shape[-1]

    def specs(self):
        blk = (TOKEN_TILE, self.d)
        if not self.split:
            return [pl.BlockSpec(blk, lambda i: (i, 0))]
        n_lat = self.n_lat
        return [pl.BlockSpec(blk, lambda i: (jnp.minimum(i, n_lat - 1), 0)),
                pl.BlockSpec(blk, lambda i: (jnp.maximum(i - n_lat, 0), 0))]

    def load(self, refs, i):
        if not self.split:
            return refs[0][...]
        return jnp.where(i < self.n_lat, refs[0][...], refs[1][...])


def _mod_spec(n_lat, nj, batch, width):
    return pl.BlockSpec((1, 1, width), lambda i: (jnp.where(i < n_lat, i // nj, batch), 0, 0))


def _in_proj_body(*refs, tok, outs, d):
    n_tok = len(tok.arrays)
    mod_ref, g_ref, w_ref = refs[n_tok:n_tok + 3]
    o_refs = refs[n_tok + 3:]
    i = pl.program_id(0)
    x = tok.load(refs[:n_tok], i)
    y = x * lax.rsqrt(jnp.mean(x * x, axis=-1, keepdims=True) + EPS) * g_ref[...]
    mod = mod_ref[0]
    h = y * (1.0 + mod[:, d:2 * d]) + mod[:, :d]
    acc = _dot(h.astype(BF16), w_ref[...])
    for o_ref, (c0, width, _, scale, transposed) in zip(o_refs, outs):
        v = acc[:, c0:c0 + width]
        if scale != 1.0:
            v = v * scale
        if transposed:
            o_ref[0] = v.T.astype(o_ref.dtype)
        else:
            o_ref[...] = v.astype(o_ref.dtype)


def _in_proj(tok, mod_l, g, w, outs, batch, seq, ctx_len):
    d = tok.d
    nj = seq // TOKEN_TILE
    n_lat, n_all = tok.n_lat, tok.n_lat + tok.n_ctx
    rows = n_all * TOKEN_TILE
    t_len = seq + ctx_len
    out_shape, out_specs = [], []
    for (_, width, dtype, _, transposed) in outs:
        if transposed:
            out_shape.append(jax.ShapeDtypeStruct((batch, width, t_len), dtype))
            out_specs.append(pl.BlockSpec(
                (1, width, TOKEN_TILE),
                lambda i: (jnp.where(i < n_lat, i // nj, i - n_lat), 0, jnp.where(i < n_lat, i % nj, nj))))
        else:
            out_shape.append(jax.ShapeDtypeStruct((rows, width), dtype))
            out_specs.append(pl.BlockSpec((TOKEN_TILE, width), lambda i: (i, 0)))
    body = functools.partial(_in_proj_body, tok=tok, outs=outs, d=d)
    return pl.pallas_call(
        body, out_shape=out_shape, grid=(n_all,),
        in_specs=tok.specs() + [_mod_spec(n_lat, nj, batch, 3 * d), _full((1, d)), _full(w.shape)],
        out_specs=out_specs, compiler_params=_params(1), name="in_proj",
    )(*tok.arrays, mod_l, g.reshape(1, d), w)


def _mlstm_mix(hf_ref, hb_ref, xc_ref, z_ref, gh_ref, sk_ref):
    ht = hf_ref[0].astype(F32) + hb_ref[0].astype(F32)
    L = ML_CHUNK
    normed = []
    for hd in range(ML_HEADS):
        hh = ht[hd * L:(hd + 1) * L]
        mu = jnp.mean(hh, axis=0, keepdims=True)
        var = jnp.mean(jnp.square(hh - mu), axis=0, keepdims=True)
        normed.append((hh - mu) * lax.rsqrt(var + EPS))
    hn = jnp.concatenate(normed, axis=0).T * gh_ref[...]
    return ((hn + sk_ref[...] * xc_ref[...].astype(F32)) * _silu(z_ref[...].astype(F32))).astype(BF16)


def _out_proj_body(*refs, tok, mix_a, n_b, d, final):
    n_tok, n_a = len(tok.arrays), len(mix_a.arrays)
    b_refs = refs[n_tok + n_a:n_tok + n_a + n_b]
    wa_ref, wb_ref, mod_ref = refs[n_tok + n_a + n_b:n_tok + n_a + n_b + 3]
    rest = refs[n_tok + n_a + n_b + 3:]
    i = pl.program_id(0)
    x = tok.load(refs[:n_tok], i)
    mb = b_refs[0][...] if n_b == 1 else _mlstm_mix(*b_refs)
    acc = _dot(mix_a.load(refs[n_tok:n_tok + n_a], i), wa_ref[...]) + _dot(mb, wb_ref[...])
    xn = x + mod_ref[0][:, 2 * d:] * acc
    if final:
        gf_ref, o_ref = rest
        xn = xn * lax.rsqrt(jnp.mean(xn * xn, axis=-1, keepdims=True) + EPS) * gf_ref[...]
    else:
        (o_ref,) = rest
    o_ref[...] = xn


def _out_proj(tok, mix_a, mix_b, w_out, mod_l, batch, seq, g_final=None):
    d = tok.d
    half = mix_a.d
    nj = seq // TOKEN_TILE
    n_lat = tok.n_lat
    final = g_final is not None
    n_tiles = n_lat if final else n_lat + tok.n_ctx
    wa, wb = w_out[:half].astype(BF16), w_out[half:].astype(BF16)
    tile = lambda width: pl.BlockSpec((TOKEN_TILE, width), lambda i: (i, 0))
    if isinstance(mix_b, tuple):
        hf, hb, xc, z, g_head, skip = mix_b
        width = xc.shape[1]
        feat = pl.BlockSpec((1, width, TOKEN_TILE),
                            lambda i: (jnp.where(i < n_lat, i // nj, i - n_lat), 0, jnp.where(i < n_lat, i % nj, nj)))
        b_specs = [feat, feat, tile(width), tile(width), _full((1, width)), _full((1, width))]
        b_args = [hf, hb, xc, z, g_head.reshape(1, width), skip.reshape(1, width)]
    else:
        b_specs, b_args = [tile(half)], [mix_b]
    in_specs = tok.specs() + mix_a.specs() + b_specs + [_full(wa.shape), _full(wb.shape),
                                                        _mod_spec(n_lat, nj, batch, 3 * d)]
    args = list(tok.arrays) + list(mix_a.arrays) + b_args + [wa, wb, mod_l]
    if final:
        in_specs.append(_full((1, d)))
        args.append(g_final.reshape(1, d))
    body = functools.partial(_out_proj_body, tok=tok, mix_a=mix_a, n_b=len(b_args), d=d, final=final)
    return pl.pallas_call(
        body, out_shape=jax.ShapeDtypeStruct((n_tiles * TOKEN_TILE, d), F32), grid=(n_tiles,),
        in_specs=in_specs, out_specs=tile(d), compiler_params=_params(1), name="out_proj",
    )(*args)


def _rope_tables(seq, ctx_len, rot_dim):
    t = jnp.arange(seq)
    pos = jnp.stack([t // GRID_W, t % GRID_W], axis=-1).astype(F32)
    n_freq = rot_dim // 4
    inv = ROPE_BASE ** (-jnp.arange(n_freq, dtype=F32) / n_freq)
    ang = pos[:, :, None] * inv
    cos, sin = jnp.cos(ang), jnp.sin(ang)
    cos_t = jnp.concatenate([cos[:, 0], cos[:, 0], cos[:, 1], cos[:, 1]], axis=-1)
    sin_t = jnp.concatenate([-sin[:, 0], sin[:, 0], -sin[:, 1], sin[:, 1]], axis=-1)
    cos_t = jnp.concatenate([cos_t, jnp.ones((ctx_len, rot_dim), F32)], axis=0)
    sin_t = jnp.concatenate([sin_t, jnp.zeros((ctx_len, rot_dim), F32)], axis=0)
    return cos_t, sin_t


def _rope(x, cos, sin, dist):
    lane = lax.broadcasted_iota(jnp.int32, x.shape, 1)
    first = (lane % (2 * dist)) < dist
    partner = jnp.where(first, pltpu.roll(x, LANES - dist, 1), pltpu.roll(x, dist, 1))
    return x * cos + partner * sin


def _rope_rows(x, cos, sin, dist):
    n = x.shape[0] // dist
    partner = jnp.concatenate([x[(i ^ 1) * dist:((i ^ 1) + 1) * dist] for i in range(n)], axis=0)
    return x * cos + partner * sin


def _tok_block(b, j, nj, batch):
    return jnp.where(j < nj, b * nj + j, batch * nj + b)


def _mla_prep_body(pa_ref, cos_ref, sin_ref, cost_ref, sint_ref, gq_ref, gkv_ref, wuqt_ref, wuk_ref, wuvt_ref,
                   q_ref, k_ref, vt_ref, *, heads, q_lora, kv_lora, qscale):
    pa = pa_ref[...]

    def norm(v, g_ref):
        return v * lax.rsqrt(jnp.mean(v * v, axis=-1, keepdims=True) + EPS) * g_ref[...]

    cq_t = norm(pa[:, :q_lora], gq_ref).T.astype(BF16)
    q_all = _dot(wuqt_ref[...], cq_t)
    cos_t, sin_t = cost_ref[...], sint_ref[...]
    for h in range(heads):
        qh = _rope_rows(q_all[h * LANES:(h + 1) * LANES], cos_t, sin_t, 8)
        q_ref[0, h] = (qh * qscale).astype(BF16)
    ckv = norm(pa[:, q_lora:q_lora + kv_lora], gkv_ref)
    k_nope = _dot(ckv.astype(BF16), wuk_ref[...])
    k_rope = _rope(pa[:, q_lora + kv_lora:], cos_ref[...], sin_ref[...], 8)
    for h in range(heads):
        k_ref[0, h] = (k_nope[:, h * LANES:(h + 1) * LANES] + k_rope).astype(BF16)
    vt_ref[0] = _dot(wuvt_ref[...], ckv.T.astype(BF16)).astype(BF16)


def _mla_prep(pa, cos, sin, g_q, g_kv, w_uq, w_ukv, batch, seq, ctx_len, heads, nope, rope, v_dim):
    q_lora, kv_lora = g_q.shape[0], g_kv.shape[0]
    nj = seq // TOKEN_TILE
    t_len = seq + ctx_len
    pad = LANES - nope - rope
    wuq = jnp.pad(w_uq.reshape(q_lora, heads, nope + rope), ((0, 0), (0, 0), (0, pad)))
    wuq_t = wuq.reshape(q_lora, heads * LANES).T.astype(BF16)
    wkv = w_ukv.reshape(kv_lora, heads, nope + v_dim)
    wuk = jnp.pad(wkv[..., :nope], ((0, 0), (0, 0), (0, LANES - nope))).reshape(kv_lora, heads * LANES).astype(BF16)
    wuv_t = wkv[..., nope:].reshape(kv_lora, heads * v_dim).T.astype(BF16)
    body = functools.partial(_mla_prep_body, heads=heads, q_lora=q_lora, kv_lora=kv_lora,
                             qscale=(nope + rope) ** -0.5 * LOG2E)
    head_major = pl.BlockSpec((1, heads, TOKEN_TILE, LANES), lambda j, b: (b, 0, j, 0))
    q_feature_major = pl.BlockSpec((1, heads, LANES, TOKEN_TILE), lambda j, b: (b, 0, 0, j))
    tok_table = pl.BlockSpec((TOKEN_TILE, LANES), lambda j, b: (j, 0))
    feat_table = pl.BlockSpec((LANES, TOKEN_TILE), lambda j, b: (0, j))
    return pl.pallas_call(
        body,
        out_shape=[jax.ShapeDtypeStruct((batch, heads, LANES, t_len), BF16),
                   jax.ShapeDtypeStruct((batch, heads, t_len, LANES), BF16),
                   jax.ShapeDtypeStruct((batch, heads * v_dim, t_len), BF16)],
        grid=(nj + 1, batch),
        in_specs=[pl.BlockSpec((TOKEN_TILE, pa.shape[1]), lambda j, b: (_tok_block(b, j, nj, batch), 0)),
                  tok_table, tok_table, feat_table, feat_table,
                  _full((1, q_lora)), _full((1, kv_lora)), _full(wuq_t.shape), _full(wuk.shape), _full(wuv_t.shape)],
        out_specs=[q_feature_major, head_major,
                   pl.BlockSpec((1, heads * v_dim, TOKEN_TILE), lambda j, b: (b, 0, j))],
        compiler_params=_params(2), name="mla_prep",
    )(pa, cos, sin, cos.T, sin.T, g_q.reshape(1, -1), g_kv.reshape(1, -1), wuq_t, wuk, wuv_t)


def _gqa_prep_body(pd_ref, cos_ref, sin_ref, gq_ref, gk_ref, q_ref, k_ref, *, heads, kv_heads, dim, qscale):
    cos, sin = cos_ref[...], sin_ref[...]

    def head(first_row, g_ref):
        x = pd_ref[0, first_row:first_row + dim, :]
        y = x * lax.rsqrt(jnp.mean(x * x, axis=0, keepdims=True) + EPS) * g_ref[...]
        return _rope_rows(y, cos, sin, dim // 4)

    zeros = jnp.zeros((dim, TOKEN_TILE), F32)
    for h in range(heads):
        parts = [zeros] * kv_heads
        parts[h // (heads // kv_heads)] = head(h * dim, gq_ref) * qscale
        q_ref[0, h] = jnp.concatenate(parts, axis=0).astype(BF16)
    keys = [head((heads + kv) * dim, gk_ref) for kv in range(kv_heads)]
    k_ref[0, 0] = jnp.concatenate(keys, axis=0).T.astype(BF16)


def _gqa_prep(pd_t, cos, sin, g_q, g_k, batch, seq, ctx_len, heads, kv_heads, head_dim):
    assert kv_heads * head_dim == LANES
    nj = seq // TOKEN_TILE
    t_len = seq + ctx_len
    body = functools.partial(_gqa_prep_body, heads=heads, kv_heads=kv_heads, dim=head_dim,
                             qscale=head_dim ** -0.5 * LOG2E)
    table = pl.BlockSpec((head_dim, TOKEN_TILE), lambda j, b: (0, j))
    gain = lambda g: jnp.broadcast_to(g[:, None], (head_dim, TOKEN_TILE))
    return pl.pallas_call(
        body,
        out_shape=[jax.ShapeDtypeStruct((batch, heads, LANES, t_len), BF16),
                   jax.ShapeDtypeStruct((batch, 1, t_len, LANES), BF16)],
        grid=(nj + 1, batch),
        in_specs=[pl.BlockSpec((1, pd_t.shape[1], TOKEN_TILE), lambda j, b: (b, 0, j)), table, table,
                  _full((head_dim, TOKEN_TILE)), _full((head_dim, TOKEN_TILE))],
        out_specs=[pl.BlockSpec((1, heads, LANES, TOKEN_TILE), lambda j, b: (b, 0, 0, j)),
                   pl.BlockSpec((1, 1, TOKEN_TILE, LANES), lambda j, b: (b, 0, j, 0))],
        compiler_params=_params(2), name="gqa_prep",
    )(pd_t, cos.T, sin.T, gain(g_q), gain(g_k))


def _score_pass(q, k_ref, k_head, chunks, s_ref):
    m = None
    for (st, sz) in chunks:
        s_t = _dot(k_ref[0, k_head, st:st + sz, :], q)
        s_ref[st:st + sz, :] = s_t
        cm = jnp.max(s_t, axis=0, keepdims=True)
        m = cm if m is None else jnp.maximum(m, cm)
    return m


def _value_pass(s_ref, m, vt_ref, v_rows, chunks):
    l = jnp.zeros(m.shape, F32)
    acc = jnp.zeros((v_rows.stop - v_rows.start, m.shape[1]), F32)
    for (st, sz) in chunks:
        p = jnp.exp2(s_ref[st:st + sz, :] - m)
        l = l + jnp.sum(p, axis=0, keepdims=True)
        acc = acc + _dot(vt_ref[0, v_rows, st:st + sz], p.astype(BF16))
    return acc / l


def _gated_store(outs, g_ref, o_ref):
    o2 = jnp.concatenate(outs, axis=0).T
    o_ref[...] = (o2 * _silu(g_ref[...].astype(F32))).astype(o_ref.dtype)


def _flash_body(q_ref, k_ref, vt_ref, g_ref, o_ref, s_scr, m_scr, *, k_sel, v_off, v_dim, chunks):
    j = pl.program_id(2)

    @pl.when((pl.program_id(0) == 0) & (pl.program_id(1) == 0) & (j == 0))
    def _():
        s_scr[...] = jnp.zeros(s_scr.shape, F32)
        m_scr[...] = jnp.zeros(m_scr.shape, F32)

    def step(slot, prev):
        n_sub = q_ref.shape[3] // FLASH_TILE
        units = [(sub, a) for sub in range(n_sub) for a in (0, 1)]
        q = [q_ref[0, a, :, sub * FLASH_TILE:(sub + 1) * FLASH_TILE] for sub, a in units]
        m_prev = [m_scr[prev, u] for u in range(len(units))]
        m = [None] * len(units)
        l = [jnp.zeros((1, FLASH_TILE), F32) for _ in units]
        acc = [jnp.zeros((v_dim, FLASH_TILE), F32) for _ in units]
        for (st, sz) in chunks:
            for u, (sub, a) in enumerate(units):
                s_t = _dot(k_ref[0, k_sel[a], st:st + sz, :], q[u])
                s_scr[slot, u, st:st + sz, :] = s_t
                cm = jnp.max(s_t, axis=0, keepdims=True)
                m[u] = cm if m[u] is None else jnp.maximum(m[u], cm)
                p = jnp.exp2(s_scr[prev, u, st:st + sz, :] - m_prev[u])
                l[u] = l[u] + jnp.sum(p, axis=0, keepdims=True)
                acc[u] = acc[u] + _dot(vt_ref[0, v_off[a]:v_off[a] + v_dim, st:st + sz], p.astype(BF16))
        for u in range(len(units)):
            m_scr[slot, u] = m[u]
        for sub in range(n_sub):
            toks = slice(sub * FLASH_TILE, (sub + 1) * FLASH_TILE)
            _gated_store([acc[2 * sub + a] / l[2 * sub + a] for a in (0, 1)], g_ref.at[toks, :], o_ref.at[toks, :])

    for parity in (0, 1):
        pl.when(j % 2 == parity)(functools.partial(step, parity, 1 - parity))


def _flash_ctx_body(q_ref, k_ref, vt_ref, g_ref, o_ref, s_scr, *, k_sel, v_off, v_dim, chunks):
    outs = []
    for a in range(2):
        m = _score_pass(q_ref[0, a], k_ref, k_sel[a], chunks, s_scr.at[a])
        outs.append(_value_pass(s_scr.at[a], m, vt_ref, slice(v_off[a], v_off[a] + v_dim), chunks))
    _gated_store(outs, g_ref, o_ref)


def _flash(q, k, vt, gate, batch, seq, ctx_len, k_heads_per_pair, v_rows_per_pair, pairs_per_kv, ctx_queries):
    heads = q.shape[1]
    t_len = seq + ctx_len
    nj = seq // TOKEN_TILE
    v_dim = LANES // 2
    k_sel = (0, 1) if k_heads_per_pair == 2 else (0, 0)
    v_off = (0, v_dim) if v_rows_per_pair == 2 * v_dim else (0, 0)
    k_blocks = k.shape[1] // k_heads_per_pair
    k_block = lambda p: (p // pairs_per_kv) % k_blocks
    chunks = tuple((c * KEY_CHUNK, KEY_CHUNK) for c in range(t_len // KEY_CHUNK))
    static = dict(k_sel=k_sel, v_off=v_off, v_dim=v_dim)

    width = gate.shape[1]
    tq = FLASH_TILE * FLASH_STEP_TILES
    n_q = seq // tq
    n_units = 2 * FLASH_STEP_TILES
    done = lambda b, p, j: (b * n_q + jnp.maximum(j - 1, 0), p)
    out = pl.pallas_call(
        functools.partial(_flash_body, chunks=chunks, **static),
        out_shape=jax.ShapeDtypeStruct((batch * seq, width), BF16),
        grid=(batch, heads // 2, n_q + 1),
        in_specs=[pl.BlockSpec((1, 2, LANES, tq), lambda b, p, j: (b, p, 0, jnp.minimum(j, n_q - 1))),
                  pl.BlockSpec((1, k_heads_per_pair, t_len, LANES), lambda b, p, j: (b, k_block(p), 0, 0)),
                  pl.BlockSpec((1, v_rows_per_pair, t_len), lambda b, p, j: (b, p // pairs_per_kv, 0)),
                  pl.BlockSpec((tq, LANES), done)],
        out_specs=pl.BlockSpec((tq, LANES), done),
        scratch_shapes=[pltpu.VMEM((2, n_units, t_len, FLASH_TILE), F32),
                        pltpu.VMEM((2, n_units, 1, FLASH_TILE), F32)],
        compiler_params=_params(3), name="flash_attention",
    )(q, k, vt, gate)
    if not ctx_queries:
        return (out,)

    out_ctx = pl.pallas_call(
        functools.partial(_flash_ctx_body, chunks=((0, ctx_len),), **static),
        out_shape=jax.ShapeDtypeStruct((batch * ctx_len, width), BF16),
        grid=(batch, heads // 2),
        in_specs=[pl.BlockSpec((1, 2, LANES, TOKEN_TILE), lambda b, p: (b, p, 0, nj)),
                  pl.BlockSpec((1, k_heads_per_pair, ctx_len, LANES), lambda b, p: (b, k_block(p), seq // ctx_len, 0)),
                  pl.BlockSpec((1, v_rows_per_pair, ctx_len), lambda b, p: (b, p // pairs_per_kv, seq // ctx_len)),
                  pl.BlockSpec((TOKEN_TILE, LANES), lambda b, p: (batch * nj + b, p))],
        out_specs=pl.BlockSpec((TOKEN_TILE, LANES), lambda b, p: (b, p)),
        scratch_shapes=[pltpu.VMEM((2, ctx_len, TOKEN_TILE), F32)],
        compiler_params=_params(2), name="flash_attention_ctx",
    )(q, k, vt, gate)
    return out, out_ctx


NA_KEY_ROWS = 12
NA_STEP_ROWS = TOKEN_TILE // GRID_W
NA_VARIANTS = 3


def _na_rel_row(variant, rr, a):
    if variant == 0:
        valid, dr = a < WIN_R, a - rr
    elif variant == 1:
        dr = a - WIN_R // 2 - rr
        valid = -(WIN_R // 2) <= dr < WIN_R // 2
    else:
        valid, dr = a >= NA_KEY_ROWS - WIN_R, a - (NA_KEY_ROWS - NA_STEP_ROWS) - rr
    return dr if valid else None


def _na_bias_body(rpb_ref, o_ref, blk_scr):
    p = pl.program_id(0)
    shape = (GRID_W, LANES)
    kc = lax.broadcasted_iota(jnp.int32, shape, 0)
    lane = lax.broadcasted_iota(jnp.int32, shape, 1)
    qc = lane % GRID_W
    upper = lane >= GRID_W
    rel = kc - qc + (WIN_C - 1)
    c0 = jnp.clip(qc - WIN_C // 2, 0, GRID_W - WIN_C)
    col_ok = (kc >= c0) & (kc < c0 + WIN_C)
    n_rel_r, n_rel_c = 2 * WIN_R - 1, 2 * WIN_C - 1

    def block(dd, carry):
        base0 = (2 * p) * (n_rel_r * n_rel_c) + dd * n_rel_c
        base1 = base0 + n_rel_r * n_rel_c
        val = jnp.zeros(shape, F32)
        for jj in range(n_rel_c):
            val = jnp.where(rel == jj, jnp.where(upper, rpb_ref[base1 + jj], rpb_ref[base0 + jj]), val)
        blk_scr[dd] = jnp.where(col_ok, val * LOG2E, NEG)
        return carry

    lax.fori_loop(0, n_rel_r, block, 0)
    outside = jnp.full(shape, NEG, F32)
    for variant in range(NA_VARIANTS):
        for rr in range(NA_STEP_ROWS):
            for a in range(NA_KEY_ROWS):
                dr = _na_rel_row(variant, rr, a)
                o_ref[0, variant, rr, a * GRID_W:(a + 1) * GRID_W, :] = (
                    outside if dr is None else blk_scr[dr + WIN_R - 1])


def _na_bias(rpb):
    heads = rpb.shape[0]
    tab = (NA_VARIANTS, NA_STEP_ROWS, NA_KEY_ROWS * GRID_W, LANES)
    return pl.pallas_call(
        _na_bias_body,
        out_shape=jax.ShapeDtypeStruct((heads // 2,) + tab, F32),
        grid=(heads // 2,),
        in_specs=[pl.BlockSpec(memory_space=pltpu.SMEM)],
        out_specs=pl.BlockSpec((1,) + tab, lambda p: (p, 0, 0, 0, 0)),
        scratch_shapes=[pltpu.VMEM((2 * WIN_R - 1, GRID_W, LANES), F32)],
        compiler_params=_params(1), name="na_bias",
    )(rpb.reshape(-1))


NA_CHUNK = 256
NA_STEP_BLOCKS = 2


def _na_body(q_ref, kl_ref, kc_ref, vt_ref, bias_ref, g_ref, o_ref, s_scr, m_scr, *, seq, ctx_len, n_steps):
    j = pl.program_id(2)
    n_rows = seq // GRID_W
    n_loc = NA_KEY_ROWS * GRID_W // NA_CHUNK
    cols = NA_STEP_ROWS * LANES
    lower = lax.broadcasted_iota(jnp.int32, (GRID_W, LANES), 1) < GRID_W

    @pl.when((pl.program_id(0) == 0) & (pl.program_id(1) == 0) & (j == 0))
    def _():
        s_scr[...] = jnp.zeros(s_scr.shape, F32)
        m_scr[...] = jnp.zeros(m_scr.shape, F32)

    def span_start(step):
        first_row = jnp.clip(NA_STEP_ROWS * step - WIN_R // 2, 0, n_rows - NA_KEY_ROWS)
        return first_row * GRID_W

    def step(slot, prev):
        for sub in range(NA_STEP_BLOCKS):
            toks = slice(sub * TOKEN_TILE, (sub + 1) * TOKEN_TILE)
            blk = NA_STEP_BLOCKS * jnp.minimum(j, n_steps - 1) + sub
            blk_prev = NA_STEP_BLOCKS * jnp.maximum(j - 1, 0) + sub
            variant = jnp.where(blk == 0, 0, jnp.where(blk == NA_STEP_BLOCKS * n_steps - 1, 2, 1))
            k0, v0 = span_start(blk), span_start(blk_prev)
            q = q_ref[toks, :]
            zero = jnp.zeros((GRID_W, LANES), q.dtype)
            parts = []
            for rr in range(NA_STEP_ROWS):
                q_r = q[rr * GRID_W:(rr + 1) * GRID_W]
                parts += [jnp.where(lower, q_r, zero), jnp.where(lower, zero, q_r)]
            q2 = jnp.concatenate(parts, axis=0)
            m_prev = m_scr[prev, sub]
            m = None
            l = jnp.zeros((1, cols), F32)
            acc = jnp.zeros((LANES, cols), F32)
            for c in range(n_loc + ctx_len // NA_CHUNK):
                rows = slice((c % n_loc) * NA_CHUNK, (c % n_loc + 1) * NA_CHUNK)
                if c < n_loc:
                    s = _dot_nt(kl_ref[pl.ds(pl.multiple_of(k0 + c * NA_CHUNK, NA_CHUNK), NA_CHUNK), :], q2)
                    s = jnp.concatenate([s[:, rr * LANES:(rr + 1) * LANES] + bias_ref[0, variant, rr, rows, :]
                                         for rr in range(NA_STEP_ROWS)], axis=1)
                    v_chunk = vt_ref[0, :, pl.ds(pl.multiple_of(v0 + c * NA_CHUNK, NA_CHUNK), NA_CHUNK)]
                else:
                    s = _dot_nt(kc_ref[rows, :], q2)
                    v_chunk = vt_ref[0, :, seq + rows.start:seq + rows.stop]
                s_scr[slot, sub, c] = s
                cm = jnp.max(s, axis=0, keepdims=True)
                m = cm if m is None else jnp.maximum(m, cm)
                p = jnp.exp2(s_scr[prev, sub, c] - m_prev)
                l = l + jnp.sum(p, axis=0, keepdims=True)
                acc = acc + _dot(v_chunk, p.astype(BF16))
            m_scr[slot, sub] = m
            r_n = (acc / l).T
            outs = [jnp.where(lower, r_n[rr * LANES:rr * LANES + GRID_W], r_n[rr * LANES + GRID_W:(rr + 1) * LANES])
                    for rr in range(NA_STEP_ROWS)]
            o = jnp.concatenate(outs, axis=0)
            o_ref[toks, :] = (o * _silu(g_ref[toks, :].astype(F32))).astype(o_ref.dtype)

    for parity in (0, 1):
        pl.when(j % 2 == parity)(functools.partial(step, parity, 1 - parity))


def _na_attention(q, k, vt, bias, gate, batch, seq, ctx_len):
    pairs = q.shape[1] // LANES
    t_len = seq + ctx_len
    step_tokens = NA_STEP_BLOCKS * TOKEN_TILE
    n_steps = seq // step_tokens
    n_chunks = (NA_KEY_ROWS * GRID_W + ctx_len) // NA_CHUNK
    cols = NA_STEP_ROWS * LANES
    done = lambda b, p, j: (b * n_steps + jnp.maximum(j - 1, 0), p)
    body = functools.partial(_na_body, seq=seq, ctx_len=ctx_len, n_steps=n_steps)
    return pl.pallas_call(
        body, out_shape=jax.ShapeDtypeStruct((batch * seq, q.shape[1]), BF16),
        grid=(batch, pairs, n_steps + 1),
        in_specs=[pl.BlockSpec((step_tokens, LANES), lambda b, p, j: (b * n_steps + jnp.minimum(j, n_steps - 1), p)),
                  pl.BlockSpec((seq, LANES), lambda b, p, j: (b, p)),
                  pl.BlockSpec((ctx_len, LANES), lambda b, p, j: (batch * seq // ctx_len + b, p)),
                  pl.BlockSpec((1, LANES, t_len), lambda b, p, j: (b, p, 0)),
                  pl.BlockSpec((1,) + bias.shape[1:], lambda b, p, j: (p, 0, 0, 0, 0)),
                  pl.BlockSpec((step_tokens, LANES), done)],
        out_specs=pl.BlockSpec((step_tokens, LANES), done),
        scratch_shapes=[pltpu.VMEM((2, NA_STEP_BLOCKS, n_chunks, NA_CHUNK, cols), F32),
                        pltpu.VMEM((2, NA_STEP_BLOCKS, 1, cols), F32)],
        compiler_params=_params(3), name="neighborhood_attention",
    )(q, k, k, vt, bias, gate)


def _seg_scans(jobs, use_max=False):
    vals = [v for v, _ in jobs]
    n = vals[0].shape[1]
    lane = lax.broadcasted_iota(jnp.int32, vals[0].shape, 1) % ML_CHUNK
    k = 1
    while k < ML_CHUNK:
        for i, (_, reverse) in enumerate(jobs):
            v = vals[i]
            if reverse:
                ok, shifted = lane < ML_CHUNK - k, pltpu.roll(v, n - k, 1)
            else:
                ok, shifted = lane >= k, pltpu.roll(v, k, 1)
            vals[i] = jnp.maximum(v, jnp.where(ok, shifted, NEG)) if use_max else v + jnp.where(ok, shifted, 0.0)
        k *= 2
    return vals


ML_GATE_ROWS = 40


def _mlstm_prep_body(u_ref, up_ref, un_ref, cw_ref, cb_ref, wqk_ref, wv_ref, wg_ref, bg_ref,
                     xc_ref, k_ref, qt_ref, vt_ref, pre_ref, *, nj, width, kscale):
    j = pl.program_id(1)
    u = u_ref[...]
    row = lax.broadcasted_iota(jnp.int32, u.shape, 0)
    prev = jnp.where((j > 0) & (j < nj), up_ref[7:8, :], 0.0)
    nxt = jnp.where(j < nj - 1, un_ref[0:1, :], 0.0)
    u_m1 = jnp.where(row == 0, prev, pltpu.roll(u, 1, 0))
    u_p1 = jnp.where(row == TOKEN_TILE - 1, nxt, pltpu.roll(u, TOKEN_TILE - 1, 0))
    cw = cw_ref[...]
    xc = _silu(u_m1 * cw[0:1] + u * cw[1:2] + u_p1 * cw[2:3] + cb_ref[...])
    xcb = xc.astype(BF16)
    xc_ref[...] = xcb
    qk = _dot(xcb, wqk_ref[...])
    v = _dot(u.astype(BF16), wv_ref[...])
    qb, kb, vb = qk[:, :width].astype(BF16), qk[:, width:].astype(BF16), v.astype(BF16)
    k_ref[...] = (qk[:, width:] * kscale).astype(BF16)
    qt_ref[0] = qk[:, :width].T.astype(BF16)
    vt_ref[0] = v.T.astype(BF16)
    pre_ref[0] = _dot_nt(wg_ref[0], qb) + _dot_nt(wg_ref[1], kb) + _dot_nt(wg_ref[2], vb) + bg_ref[...]


def _mlstm_gates_body(pre_ref, g_ref, gc_ref):
    pre = pre_ref[0]
    i8 = [pre[16 * d:16 * d + 8] for d in range(2)]
    f8 = [_log_sigmoid(pre[16 * d + 8:16 * d + 16]) for d in range(2)]
    b0, b0_rev, b1, b1_rev = _seg_scans([(f8[0], False), (f8[0], True), (f8[1], True), (f8[1], False)])
    b8, b_last = [b0, b1], [b0 + b0_rev - f8[0], b1 + b1_rev - f8[1]]
    r8 = [i8[d] - b8[d] for d in range(2)]
    c0, c0_rev, c1, c1_rev = _seg_scans([(r8[0], False), (r8[0], True), (r8[1], True), (r8[1], False)], use_max=True)
    c8, r_max = [c0, c1], [jnp.maximum(c0, c0_rev), jnp.maximum(c1, c1_rev)]
    for d in range(2):
        g_ref[0, d * ML_GATE_ROWS:(d + 1) * ML_GATE_ROWS] = jnp.concatenate(
            [-c8[d], b8[d] + c8[d], jnp.exp(r8[d] - r_max[d]), b_last[d], b_last[d] + r_max[d]], axis=0)
    pad = jnp.zeros((LANES - 16, pre.shape[1]), F32)
    gc_ref[0] = jnp.concatenate(r8 + [pad], axis=0).T


def _mlstm_gates(pre):
    batch, _, t_len = pre.shape
    return pl.pallas_call(
        _mlstm_gates_body,
        out_shape=[jax.ShapeDtypeStruct((batch, 2 * ML_GATE_ROWS, t_len), F32),
                   jax.ShapeDtypeStruct((batch, t_len, LANES), F32)],
        grid=(batch,),
        in_specs=[pl.BlockSpec((1,) + pre.shape[1:], lambda b: (b, 0, 0))],
        out_specs=[pl.BlockSpec((1, 2 * ML_GATE_ROWS, t_len), lambda b: (b, 0, 0)),
                   pl.BlockSpec((1, t_len, LANES), lambda b: (b, 0, 0))],
        compiler_params=_params(1), name="mlstm_gates",
    )(pre)


def _mlstm_prep(u, conv_w, conv_b, w_q, w_k, w_v, w_gate, b_gate, batch, seq, ctx_len):
    heads, hd = w_q.shape[0], w_q.shape[1]
    assert heads == ML_HEADS and hd == ML_CHUNK
    width = heads * hd
    nj = seq // TOKEN_TILE
    t_len = seq + ctx_len
    rows = u.shape[0]

    def block_diag(w):
        eye = jnp.eye(heads, dtype=w.dtype)
        return (eye[:, None, :, None] * w[:, :, None, :]).reshape(width, width)

    wqk = jnp.concatenate([block_diag(w_q), block_diag(w_k)], axis=1).astype(BF16)
    wv = block_diag(w_v).astype(BF16)
    wg = w_gate.reshape(2, heads, 3, hd, 2, heads).transpose(2, 0, 4, 5, 1, 3).reshape(3, 2, 2, heads, width)
    wg = jnp.pad(wg, ((0, 0), (0, 0), (0, 0), (0, 8 - heads), (0, 0))).reshape(3, 32, width).astype(BF16)
    bg = jnp.pad(b_gate.reshape(2, 2, heads), ((0, 0), (0, 0), (0, 8 - heads))).reshape(32, 1)
    n_halo = rows // 8
    tokb = lambda b, j: _tok_block(b, j, nj, batch)
    tile = pl.BlockSpec((TOKEN_TILE, width), lambda b, j: (tokb(b, j), 0))
    feat = pl.BlockSpec((1, width, TOKEN_TILE), lambda b, j: (b, 0, j))
    body = functools.partial(_mlstm_prep_body, nj=nj, width=width, kscale=hd ** -0.5)
    per_tile = TOKEN_TILE // 8
    return pl.pallas_call(
        body,
        out_shape=[jax.ShapeDtypeStruct((rows, width), BF16)] * 2
        + [jax.ShapeDtypeStruct((batch, width, t_len), BF16)] * 2
        + [jax.ShapeDtypeStruct((batch, 32, t_len), F32)],
        grid=(batch, nj + 1),
        in_specs=[tile,
                  pl.BlockSpec((8, width), lambda b, j: (jnp.maximum(tokb(b, j) * per_tile - 1, 0), 0)),
                  pl.BlockSpec((8, width), lambda b, j: (jnp.minimum((tokb(b, j) + 1) * per_tile, n_halo - 1), 0)),
                  _full((3, width)), _full((1, width)), _full(wqk.shape), _full(wv.shape),
                  _full(wg.shape), _full((32, 1))],
        out_specs=[tile, tile, feat, feat, pl.BlockSpec((1, 32, TOKEN_TILE), lambda b, j: (b, 0, j))],
        compiler_params=_params(2), name="mlstm_prep",
    )(u, u, u, conv_w, conv_b.reshape(1, width), wqk, wv, wg, bg)


def _mlstm_seq_body(kf, qtf, vtf, gf, gcf, kb, qtb, vtb, gb, gcb, hf_ref, hb_ref, c_s, n_s, m_s):
    t = pl.program_id(1)
    L = ML_CHUNK
    subs = TOKEN_TILE // L

    @pl.when(t == 0)
    def _():
        c_s[...] = jnp.zeros(c_s.shape, F32)
        n_s[...] = jnp.zeros(n_s.shape, F32)
        m_s[...] = jnp.zeros(m_s.shape, F32)

    si = lax.broadcasted_iota(jnp.int32, (L, L), 0)
    li = lax.broadcasted_iota(jnp.int32, (L, L), 1)
    streams = ((kf, qtf, vtf, gf, gcf, hf_ref, si <= li), (kb, qtb, vtb, gb, gcb, hb_ref, si >= li))
    units = {}
    for d, (k_ref, qt_ref, vt_ref, g_ref, gc_ref, h_ref, incl) in enumerate(streams):
        for sub in range(subs):
            toks = slice(sub * L, (sub + 1) * L)
            g = g_ref[0, :, toks]
            gc = gc_ref[0, toks, :]
            for h in range(ML_HEADS):
                cols = slice(h * L, (h + 1) * L)
                k, qt, vt = k_ref[toks, cols], qt_ref[0, cols, toks], vt_ref[0, cols, toks]
                neg_c, m_loc, w0, b_last, g_max = (g[8 * i + h:8 * i + h + 1] for i in range(5))
                r_col = gc[:, 8 * d + h:8 * d + h + 1]
                p0 = jnp.where(incl, jnp.exp(r_col + neg_c), 0.0) * _dot(k, qt)
                s_sum = jnp.sum(p0, axis=0, keepdims=True)
                intra = _dot(vt, p0.astype(BF16))
                c_inc = _dot((vt.astype(F32) * w0).astype(BF16), k)
                n_inc = _dot(jnp.broadcast_to(w0, (8, L)).astype(BF16), k)
                units[d, sub, h] = (h_ref, cols, toks, qt, neg_c, m_loc, b_last, g_max, s_sum, intra, c_inc, n_inc)
    for stage in range(subs):
        for d in range(2):
            sub = stage if d == 0 else subs - 1 - stage
            for h in range(ML_HEADS):
                idx = d * ML_HEADS + h
                h_ref, cols, toks, qt, neg_c, m_loc, b_last, g_max, s_sum, intra, c_inc, n_inc = units[d, sub, h]
                c_st, n_st, m_st = c_s[idx], n_s[idx], m_s[idx]
                cn = _dot(jnp.concatenate([c_st, n_st], axis=0).astype(BF16), qt)
                delta = jnp.maximum(m_st + neg_c, 0.0)
                e_intra = jnp.exp(-delta)
                w_inter = jnp.exp(m_st + neg_c - delta)
                num = w_inter * cn[:L] + e_intra * intra
                den = w_inter * cn[L:L + 1] + e_intra * s_sum
                h_out = num / jnp.maximum(jnp.abs(den), jnp.exp(-(m_loc + delta)))
                h_ref[0, cols, toks] = h_out.astype(h_ref.dtype)
                m_new = jnp.maximum(b_last + m_st, g_max)
                decay = jnp.exp(b_last + m_st - m_new)
                gain = jnp.exp(g_max - m_new)
                c_s[idx] = decay * c_st + gain * c_inc
                n_s[idx] = decay * n_st + gain * n_inc
                m_s[idx] = m_new


def _mlstm_seq(k, qt, vt, gates, gcols, batch, seq, ctx_len):
    width = k.shape[1]
    L = ML_CHUNK
    assert ctx_len == TOKEN_TILE
    nj = seq // TOKEN_TILE
    fwd = lambda t: (t + nj) % (nj + 1)
    bwd = lambda t: nj - t

    def stream(tile_of, d):
        feat = pl.BlockSpec((1, width, TOKEN_TILE), lambda b, t: (b, 0, tile_of(t)))
        return [pl.BlockSpec((TOKEN_TILE, width), lambda b, t: (_tok_block(b, tile_of(t), nj, batch), 0)), feat, feat,
                pl.BlockSpec((1, ML_GATE_ROWS, TOKEN_TILE), lambda b, t: (b, d, tile_of(t))),
                pl.BlockSpec((1, TOKEN_TILE, LANES), lambda b, t: (b, tile_of(t), 0))]

    out_f = pl.BlockSpec((1, width, TOKEN_TILE), lambda b, t: (b, 0, fwd(t)))
    out_b = pl.BlockSpec((1, width, TOKEN_TILE), lambda b, t: (b, 0, bwd(t)))
    n_state = 2 * ML_HEADS
    return pl.pallas_call(
        _mlstm_seq_body,
        out_shape=[jax.ShapeDtypeStruct(qt.shape, BF16)] * 2,
        grid=(batch, nj + 1),
        in_specs=stream(fwd, 0) + stream(bwd, 1),
        out_specs=[out_f, out_b],
        scratch_shapes=[pltpu.VMEM((n_state, L, L), F32), pltpu.VMEM((n_state, 8, L), F32),
                        pltpu.VMEM((n_state, 1, L), F32)],
        compiler_params=_params(2), name="mlstm_recurrence",
    )(k, qt, vt, gates, gcols, k, qt, vt, gates, gcols)


def kernel(x, c, ctx, c_ctx, w_mod, b_mod, g_norm, ab_w_in, ab_w_out, mla_g_q, mla_w_uq, mla_g_kv, mla_w_ukv,
           ml_conv_w, ml_conv_b, ml_w_q, ml_w_k, ml_w_v, ml_w_gate, ml_b_gate, ml_g_head, ml_skip,
           cd_w_in, cd_w_out, na_rpb, gqa_g_q, gqa_g_k, g_final):
    batch, seq, d = x.shape
    ctx_len = ctx.shape[1]
    assert ctx_len == TOKEN_TILE and seq % KEY_CHUNK == 0 and seq // GRID_W >= NA_KEY_ROWS
    dims = (batch, seq, ctx_len)

    mla_heads, mla_rope, mla_v = 8, 32, 64
    mla_nope = mla_w_uq.shape[2] // mla_heads - mla_rope
    q_lora, kv_lora = mla_g_q.shape[1], mla_g_kv.shape[1]
    ml_width = ml_conv_w.shape[2]
    mla_width = mla_heads * mla_v
    gqa_heads, gqa_dim = 8, gqa_g_q.shape[1]
    gqa_kv = (cd_w_in.shape[2] - 4 * 512 - 2 * gqa_heads * gqa_dim) // (2 * gqa_dim)
    na_width = na_rpb.shape[1] * 64

    mod_rows = -(-(batch + 1) // 8) * 8
    cvec = jnp.concatenate([c, c_ctx[None], jnp.zeros((mod_rows - batch - 1, d), F32)], axis=0)
    mod = _modulation(cvec, w_mod, b_mod)
    mod0 = mod[0].reshape(mod_rows, 1, 3 * d)
    mod1 = mod[1].reshape(mod_rows, 1, 3 * d)

    tok0 = _Tokens((x.reshape(batch * seq, d), ctx.reshape(batch * ctx_len, d)), *dims)
    w_in = ab_w_in[0]
    s1 = q_lora + kv_lora
    zcol = lambda n: jnp.zeros((d, n), w_in.dtype)
    w0 = jnp.concatenate([w_in[:, :s1], zcol(mla_nope), w_in[:, s1:s1 + mla_rope],
                          zcol(LANES - mla_nope - mla_rope), w_in[:, s1 + mla_rope:]], axis=1).astype(BF16)
    o_pa = s1 + LANES
    outs0 = ((0, o_pa, F32, 1.0, False), (o_pa, mla_width, BF16, 1.0, False),
             (o_pa + mla_width, ml_width, F32, 1.0, False), (o_pa + mla_width + ml_width, ml_width, BF16, 1.0, False))
    pa, gate_a, u, z = _in_proj(tok0, mod0, g_norm[0], w0, outs0, *dims)

    def mla_lanes(table, fill):
        n = table.shape[0]
        return jnp.concatenate([jnp.full((n, mla_nope), fill, F32), table,
                                jnp.full((n, LANES - mla_nope - mla_rope), fill, F32)], axis=-1)

    cos_a, sin_a = _rope_tables(seq, ctx_len, mla_rope)
    cos_a, sin_a = mla_lanes(cos_a, 1.0), mla_lanes(sin_a, 0.0)
    q_a, k_a, vt_a = _mla_prep(pa, cos_a, sin_a, mla_g_q[0], mla_g_kv[0], mla_w_uq[0], mla_w_ukv[0],
                               *dims, mla_heads, mla_nope, mla_rope, mla_v)
    mix_a = _flash(q_a, k_a, vt_a, gate_a, *dims, k_heads_per_pair=2, v_rows_per_pair=2 * mla_v,
                   pairs_per_kv=1, ctx_queries=True)

    xc, k_m, qt_m, vt_m, gate_pre = _mlstm_prep(u, ml_conv_w[0], ml_conv_b[0], ml_w_q[0], ml_w_k[0], ml_w_v[0],
                                                ml_w_gate[0], ml_b_gate[0], *dims)
    gates, gcols = _mlstm_gates(gate_pre)
    h_f, h_b = _mlstm_seq(k_m, qt_m, vt_m, gates, gcols, *dims)
    x1 = _out_proj(tok0, _Tokens(mix_a, *dims), (h_f, h_b, xc, z, ml_g_head[0], ml_skip[0]),
                   ab_w_out[0], mod0, batch, seq)

    tok1 = _Tokens((x1,), *dims)
    w1 = cd_w_in[0].astype(BF16)
    gq_w, gkv_w = gqa_heads * gqa_dim, gqa_kv * gqa_dim
    o_d = 4 * na_width
    outs1 = ((0, na_width, BF16, 64 ** -0.5 * LOG2E, False), (na_width, na_width, BF16, 1.0, False),
             (2 * na_width, na_width, BF16, 1.0, True), (3 * na_width, na_width, BF16, 1.0, False),
             (o_d, gq_w + gkv_w, F32, 1.0, True), (o_d + gq_w + gkv_w, gkv_w, BF16, 1.0, True),
             (o_d + gq_w + 2 * gkv_w, gq_w, BF16, 1.0, False))
    q_c, k_c, vt_c, gate_c, pd_t, vt_d, gate_d = _in_proj(tok1, mod1, g_norm[1], w1, outs1, *dims)

    mix_c = _na_attention(q_c, k_c, vt_c, _na_bias(na_rpb[0]), gate_c, *dims)

    cos_d, sin_d = _rope_tables(seq, ctx_len, gqa_dim)
    q_d, k_d = _gqa_prep(pd_t, cos_d, sin_d, gqa_g_q[0], gqa_g_k[0], *dims, gqa_heads, gqa_kv, gqa_dim)
    mix_d = _flash(q_d, k_d, vt_d, gate_d, *dims, k_heads_per_pair=1, v_rows_per_pair=gqa_dim,
                   pairs_per_kv=gqa_heads // (2 * gqa_kv), ctx_queries=False)

    out = _out_proj(tok1, _Tokens((mix_c,), *dims), mix_d[0], cd_w_out[0], mod1, batch, seq, g_final=g_final)
    return out.reshape(batch, seq, d)
```

```python
import functools

import jax
import jax.numpy as jnp
from jax import lax
from jax.experimental import pallas as pl
from jax.experimental.pallas import tpu as pltpu

F32 = jnp.float32
BF16 = jnp.bfloat16

LANES = 128
TOKEN_TILE = 256
KEY_CHUNK = 256
FLASH_TILE = 256
FLASH_STEP_TILES = 2
GRID_W = 64
WIN_R = 8
WIN_C = 16
ML_CHUNK = 128
ML_HEADS = 4
EPS = 1e-6
ROPE_BASE = 10000.0
LOG2E = 1.4426950408889634
NEG = -1e30
VMEM_LIMIT = 56 * 1024 * 1024

_NT = (((1,), (1,)), ((), ()))


def _dot(a, b):
    return jnp.dot(a, b, preferred_element_type=F32)


def _dot_nt(a, b):
    return lax.dot_general(a, b, _NT, preferred_element_type=F32)


def _silu(v):
    return v * (1.0 / (1.0 + jnp.exp(-v)))


def _log_sigmoid(v):
    return -(jnp.maximum(-v, 0.0) + jnp.log1p(jnp.exp(-jnp.abs(v))))


def _params(n_axes):
    return pltpu.CompilerParams(dimension_semantics=("arbitrary",) * n_axes, vmem_limit_bytes=VMEM_LIMIT)


def _full(shape):
    nd = len(shape)
    return pl.BlockSpec(shape, lambda *_: (0,) * nd)


def _mod_body(c_ref, w_ref, b_ref, o_ref):
    s = _silu(c_ref[...])
    o_ref[0] = _dot(s.astype(BF16), w_ref[0].astype(BF16)) + b_ref[0]


def _modulation(cvec, w_mod, b_mod):
    depth, d, n = w_mod.shape
    rows = cvec.shape[0]
    tn = n // 4
    return pl.pallas_call(
        _mod_body,
        out_shape=jax.ShapeDtypeStruct((depth, rows, n), F32),
        grid=(depth, n // tn),
        in_specs=[_full((rows, d)),
                  pl.BlockSpec((1, d, tn), lambda l, j: (l, 0, j)),
                  pl.BlockSpec((1, 1, tn), lambda l, j: (l, 0, j))],
        out_specs=pl.BlockSpec((1, rows, tn), lambda l, j: (l, 0, j)),
        compiler_params=_params(2), name="modulation",
    )(cvec, w_mod, b_mod.reshape(depth, 1, n))


class _Tokens:
    def __init__(self, arrays, batch, seq, ctx_len):
        self.arrays = arrays
        self.split = len(arrays) == 2
        self.n_lat = batch * seq // TOKEN_TILE
        self.n_ctx = batch * ctx_len // TOKEN_TILE
        self.d = arrays[0].shape[-1]

    def specs(self):
        blk = (TOKEN_TILE, self.d)
        if not self.split:
            return [pl.BlockSpec(blk, lambda i: (i, 0))]
        n_lat = self.n_lat
        return [pl.BlockSpec(blk, lambda i: (jnp.minimum(i, n_lat - 1), 0)),
                pl.BlockSpec(blk, lambda i: (jnp.maximum(i - n_lat, 0), 0))]

    def load(self, refs, i):
        if not self.split:
            return refs[0][...]
        return jnp.where(i < self.n_lat, refs[0][...], refs[1][...])


def _mod_spec(n_lat, nj, batch, width):
    return pl.BlockSpec((1, 1, width), lambda i: (jnp.where(i < n_lat, i // nj, batch), 0, 0))


def _in_proj_body(*refs, tok, outs, d):
    n_tok = len(tok.arrays)
    mod_ref, g_ref, w_ref = refs[n_tok:n_tok + 3]
    o_refs = refs[n_tok + 3:]
    i = pl.program_id(0)
    x = tok.load(refs[:n_tok], i)
    y = x * lax.rsqrt(jnp.mean(x * x, axis=-1, keepdims=True) + EPS) * g_ref[...]
    mod = mod_ref[0]
    h = y * (1.0 + mod[:, d:2 * d]) + mod[:, :d]
    acc = _dot(h.astype(BF16), w_ref[...])
    for o_ref, (c0, width, _, scale, transposed) in zip(o_refs, outs):
        v = acc[:, c0:c0 + width]
        if scale != 1.0:
            v = v * scale
        if transposed:
            o_ref[0] = v.T.astype(o_ref.dtype)
        else:
            o_ref[...] = v.astype(o_ref.dtype)


def _in_proj(tok, mod_l, g, w, outs, batch, seq, ctx_len):
    d = tok.d
    nj = seq // TOKEN_TILE
    n_lat, n_all = tok.n_lat, tok.n_lat + tok.n_ctx
    rows = n_all * TOKEN_TILE
    t_len = seq + ctx_len
    out_shape, out_specs = [], []
    for (_, width, dtype, _, transposed) in outs:
        if transposed:
            out_shape.append(jax.ShapeDtypeStruct((batch, width, t_len), dtype))
            out_specs.append(pl.BlockSpec(
                (1, width, TOKEN_TILE),
                lambda i: (jnp.where(i < n_lat, i // nj, i - n_lat), 0, jnp.where(i < n_lat, i % nj, nj))))
        else:
            out_shape.append(jax.ShapeDtypeStruct((rows, width), dtype))
            out_specs.append(pl.BlockSpec((TOKEN_TILE, width), lambda i: (i, 0)))
    body = functools.partial(_in_proj_body, tok=tok, outs=outs, d=d)
    return pl.pallas_call(
        body, out_shape=out_shape, grid=(n_all,),
        in_specs=tok.specs() + [_mod_spec(n_lat, nj, batch, 3 * d), _full((1, d)), _full(w.shape)],
        out_specs=out_specs, compiler_params=_params(1), name="in_proj",
    )(*tok.arrays, mod_l, g.reshape(1, d), w)


def _mlstm_mix(hf_ref, hb_ref, xc_ref, z_ref, gh_ref, sk_ref):
    ht = hf_ref[0].astype(F32) + hb_ref[0].astype(F32)
    L = ML_CHUNK
    normed = []
    for hd in range(ML_HEADS):
        hh = ht[hd * L:(hd + 1) * L]
        mu = jnp.mean(hh, axis=0, keepdims=True)
        var = jnp.mean(jnp.square(hh - mu), axis=0, keepdims=True)
        normed.append((hh - mu) * lax.rsqrt(var + EPS))
    hn = jnp.concatenate(normed, axis=0).T * gh_ref[...]
    return ((hn + sk_ref[...] * xc_ref[...].astype(F32)) * _silu(z_ref[...].astype(F32))).astype(BF16)


def _out_proj_body(*refs, tok, mix_a, n_b, d, final):
    n_tok, n_a = len(tok.arrays), len(mix_a.arrays)
    b_refs = refs[n_tok + n_a:n_tok + n_a + n_b]
    wa_ref, wb_ref, mod_ref = refs[n_tok + n_a + n_b:n_tok + n_a + n_b + 3]
    rest = refs[n_tok + n_a + n_b + 3:]
    i = pl.program_id(0)
    x = tok.load(refs[:n_tok], i)
    mb = b_refs[0][...] if n_b == 1 else _mlstm_mix(*b_refs)
    acc = _dot(mix_a.load(refs[n_tok:n_tok + n_a], i), wa_ref[...]) + _dot(mb, wb_ref[...])
    xn = x + mod_ref[0][:, 2 * d:] * acc
    if final:
        gf_ref, o_ref = rest
        xn = xn * lax.rsqrt(jnp.mean(xn * xn, axis=-1, keepdims=True) + EPS) * gf_ref[...]
    else:
        (o_ref,) = rest
    o_ref[...] = xn


def _out_proj(tok, mix_a, mix_b, w_out, mod_l, batch, seq, g_final=None):
    d = tok.d
    half = mix_a.d
    nj = seq // TOKEN_TILE
    n_lat = tok.n_lat
    final = g_final is not None
    n_tiles = n_lat if final else n_lat + tok.n_ctx
    wa, wb = w_out[:half].astype(BF16), w_out[half:].astype(BF16)
    tile = lambda width: pl.BlockSpec((TOKEN_TILE, width), lambda i: (i, 0))
    if isinstance(mix_b, tuple):
        hf, hb, xc, z, g_head, skip = mix_b
        width = xc.shape[1]
        feat = pl.BlockSpec((1, width, TOKEN_TILE),
                            lambda i: (jnp.where(i < n_lat, i // nj, i - n_lat), 0, jnp.where(i < n_lat, i % nj, nj)))
        b_specs = [feat, feat, tile(width), tile(width), _full((1, width)), _full((1, width))]
        b_args = [hf, hb, xc, z, g_head.reshape(1, width), skip.reshape(1, width)]
    else:
        b_specs, b_args = [tile(half)], [mix_b]
    in_specs = tok.specs() + mix_a.specs() + b_specs + [_full(wa.shape), _full(wb.shape),
                                                        _mod_spec(n_lat, nj, batch, 3 * d)]
    args = list(tok.arrays) + list(mix_a.arrays) + b_args + [wa, wb, mod_l]
    if final:
        in_specs.append(_full((1, d)))
        args.append(g_final.reshape(1, d))
    body = functools.partial(_out_proj_body, tok=tok, mix_a=mix_a, n_b=len(b_args), d=d, final=final)
    return pl.pallas_call(
        body, out_shape=jax.ShapeDtypeStruct((n_tiles * TOKEN_TILE, d), F32), grid=(n_tiles,),
        in_specs=in_specs, out_specs=tile(d), compiler_params=_params(1), name="out_proj",
    )(*args)


def _rope_tables(seq, ctx_len, rot_dim):
    t = jnp.arange(seq)
    pos = jnp.stack([t // GRID_W, t % GRID_W], axis=-1).astype(F32)
    n_freq = rot_dim // 4
    inv = ROPE_BASE ** (-jnp.arange(n_freq, dtype=F32) / n_freq)
    ang = pos[:, :, None] * inv
    cos, sin = jnp.cos(ang), jnp.sin(ang)
    cos_t = jnp.concatenate([cos[:, 0], cos[:, 0], cos[:, 1], cos[:, 1]], axis=-1)
    sin_t = jnp.concatenate([-sin[:, 0], sin[:, 0], -sin[:, 1], sin[:, 1]], axis=-1)
    cos_t = jnp.concatenate([cos_t, jnp.ones((ctx_len, rot_dim), F32)], axis=0)
    sin_t = jnp.concatenate([sin_t, jnp.zeros((ctx_len, rot_dim), F32)], axis=0)
    return cos_t, sin_t


def _rope(x, cos, sin, dist):
    lane = lax.broadcasted_iota(jnp.int32, x.shape, 1)
    first = (lane % (2 * dist)) < dist
    partner = jnp.where(first, pltpu.roll(x, LANES - dist, 1), pltpu.roll(x, dist, 1))
    return x * cos + partner * sin


def _rope_rows(x, cos, sin, dist):
    n = x.shape[0] // dist
    partner = jnp.concatenate([x[(i ^ 1) * dist:((i ^ 1) + 1) * dist] for i in range(n)], axis=0)
    return x * cos + partner * sin


def _tok_block(b, j, nj, batch):
    return jnp.where(j < nj, b * nj + j, batch * nj + b)


def _mla_prep_body(pa_ref, cos_ref, sin_ref, cost_ref, sint_ref, gq_ref, gkv_ref, wuqt_ref, wuk_ref, wuvt_ref,
                   q_ref, k_ref, vt_ref, *, heads, q_lora, kv_lora, qscale):
    pa = pa_ref[...]

    def norm(v, g_ref):
        return v * lax.rsqrt(jnp.mean(v * v, axis=-1, keepdims=True) + EPS) * g_ref[...]

    cq_t = norm(pa[:, :q_lora], gq_ref).T.astype(BF16)
    q_all = _dot(wuqt_ref[...], cq_t)
    cos_t, sin_t = cost_ref[...], sint_ref[...]
    for h in range(heads):
        qh = _rope_rows(q_all[h * LANES:(h + 1) * LANES], cos_t, sin_t, 8)
        q_ref[0, h] = (qh * qscale).astype(BF16)
    ckv = norm(pa[:, q_lora:q_lora + kv_lora], gkv_ref)
    k_nope = _dot(ckv.astype(BF16), wuk_ref[...])
    k_rope = _rope(pa[:, q_lora + kv_lora:], cos_ref[...], sin_ref[...], 8)
    for h in range(heads):
        k_ref[0, h] = (k_nope[:, h * LANES:(h + 1) * LANES] + k_rope).astype(BF16)
    vt_ref[0] = _dot(wuvt_ref[...], ckv.T.astype(BF16)).astype(BF16)


def _mla_prep(pa, cos, sin, g_q, g_kv, w_uq, w_ukv, batch, seq, ctx_len, heads, nope, rope, v_dim):
    q_lora, kv_lora = g_q.shape[0], g_kv.shape[0]
    nj = seq // TOKEN_TILE
    t_len = seq + ctx_len
    pad = LANES - nope - rope
    wuq = jnp.pad(w_uq.reshape(q_lora, heads, nope + rope), ((0, 0), (0, 0), (0, pad)))
    wuq_t = wuq.reshape(q_lora, heads * LANES).T.astype(BF16)
    wkv = w_ukv.reshape(kv_lora, heads, nope + v_dim)
    wuk = jnp.pad(wkv[..., :nope], ((0, 0), (0, 0), (0, LANES - nope))).reshape(kv_lora, heads * LANES).astype(BF16)
    wuv_t = wkv[..., nope:].reshape(kv_lora, heads * v_dim).T.astype(BF16)
    body = functools.partial(_mla_prep_body, heads=heads, q_lora=q_lora, kv_lora=kv_lora,
                             qscale=(nope + rope) ** -0.5 * LOG2E)
    head_major = pl.BlockSpec((1, heads, TOKEN_TILE, LANES), lambda j, b: (b, 0, j, 0))
    q_feature_major = pl.BlockSpec((1, heads, LANES, TOKEN_TILE), lambda j, b: (b, 0, 0, j))
    tok_table = pl.BlockSpec((TOKEN_TILE, LANES), lambda j, b: (j, 0))
    feat_table = pl.BlockSpec((LANES, TOKEN_TILE), lambda j, b: (0, j))
    return pl.pallas_call(
        body,
        out_shape=[jax.ShapeDtypeStruct((batch, heads, LANES, t_len), BF16),
                   jax.ShapeDtypeStruct((batch, heads, t_len, LANES), BF16),
                   jax.ShapeDtypeStruct((batch, heads * v_dim, t_len), BF16)],
        grid=(nj + 1, batch),
        in_specs=[pl.BlockSpec((TOKEN_TILE, pa.shape[1]), lambda j, b: (_tok_block(b, j, nj, batch), 0)),
                  tok_table, tok_table, feat_table, feat_table,
                  _full((1, q_lora)), _full((1, kv_lora)), _full(wuq_t.shape), _full(wuk.shape), _full(wuv_t.shape)],
        out_specs=[q_feature_major, head_major,
                   pl.BlockSpec((1, heads * v_dim, TOKEN_TILE), lambda j, b: (b, 0, j))],
        compiler_params=_params(2), name="mla_prep",
    )(pa, cos, sin, cos.T, sin.T, g_q.reshape(1, -1), g_kv.reshape(1, -1), wuq_t, wuk, wuv_t)


def _gqa_prep_body(pd_ref, cos_ref, sin_ref, gq_ref, gk_ref, q_ref, k_ref, *, heads, kv_heads, dim, qscale):
    cos, sin = cos_ref[...], sin_ref[...]

    def head(first_row, g_ref):
        x = pd_ref[0, first_row:first_row + dim, :]
        y = x * lax.rsqrt(jnp.mean(x * x, axis=0, keepdims=True) + EPS) * g_ref[...]
        return _rope_rows(y, cos, sin, dim // 4)

    zeros = jnp.zeros((dim, TOKEN_TILE), F32)
    for h in range(heads):
        parts = [zeros] * kv_heads
        parts[h // (heads // kv_heads)] = head(h * dim, gq_ref) * qscale
        q_ref[0, h] = jnp.concatenate(parts, axis=0).astype(BF16)
    keys = [head((heads + kv) * dim, gk_ref) for kv in range(kv_heads)]
    k_ref[0, 0] = jnp.concatenate(keys, axis=0).T.astype(BF16)


def _gqa_prep(pd_t, cos, sin, g_q, g_k, batch, seq, ctx_len, heads, kv_heads, head_dim):
    assert kv_heads * head_dim == LANES
    nj = seq // TOKEN_TILE
    t_len = seq + ctx_len
    body = functools.partial(_gqa_prep_body, heads=heads, kv_heads=kv_heads, dim=head_dim,
                             qscale=head_dim ** -0.5 * LOG2E)
    table = pl.BlockSpec((head_dim, TOKEN_TILE), lambda j, b: (0, j))
    gain = lambda g: jnp.broadcast_to(g[:, None], (head_dim, TOKEN_TILE))
    return pl.pallas_call(
        body,
        out_shape=[jax.ShapeDtypeStruct((batch, heads, LANES, t_len), BF16),
                   jax.ShapeDtypeStruct((batch, 1, t_len, LANES), BF16)],
        grid=(nj + 1, batch),
        in_specs=[pl.BlockSpec((1, pd_t.shape[1], TOKEN_TILE), lambda j, b: (b, 0, j)), table, table,
                  _full((head_dim, TOKEN_TILE)), _full((head_dim, TOKEN_TILE))],
        out_specs=[pl.BlockSpec((1, heads, LANES, TOKEN_TILE), lambda j, b: (b, 0, 0, j)),
                   pl.BlockSpec((1, 1, TOKEN_TILE, LANES), lambda j, b: (b, 0, j, 0))],
        compiler_params=_params(2), name="gqa_prep",
    )(pd_t, cos.T, sin.T, gain(g_q), gain(g_k))


def _score_pass(q, k_ref, k_head, chunks, s_ref):
    m = None
    for (st, sz) in chunks:
        s_t = _dot(k_ref[0, k_head, st:st + sz, :], q)
        s_ref[st:st + sz, :] = s_t
        cm = jnp.max(s_t, axis=0, keepdims=True)
        m = cm if m is None else jnp.maximum(m, cm)
    return m


def _value_pass(s_ref, m, vt_ref, v_rows, chunks):
    l = jnp.zeros(m.shape, F32)
    acc = jnp.zeros((v_rows.stop - v_rows.start, m.shape[1]), F32)
    for (st, sz) in chunks:
        p = jnp.exp2(s_ref[st:st + sz, :] - m)
        l = l + jnp.sum(p, axis=0, keepdims=True)
        acc = acc + _dot(vt_ref[0, v_rows, st:st + sz], p.astype(BF16))
    return acc / l


def _gated_store(outs, g_ref, o_ref):
    o2 = jnp.concatenate(outs, axis=0).T
    o_ref[...] = (o2 * _silu(g_ref[...].astype(F32))).astype(o_ref.dtype)


def _flash_body(q_ref, k_ref, vt_ref, g_ref, o_ref, s_scr, m_scr, *, k_sel, v_off, v_dim, chunks):
    j = pl.program_id(0)

    @pl.when(j == 0)
    def _():
        s_scr[...] = jnp.zeros(s_scr.shape, F32)
        m_scr[...] = jnp.zeros(m_scr.shape, F32)

    def step(slot, prev):
        n_sub = q_ref.shape[3] // FLASH_TILE
        units = [(sub, a) for sub in range(n_sub) for a in (0, 1)]
        q = [q_ref[0, a, :, sub * FLASH_TILE:(sub + 1) * FLASH_TILE] for sub, a in units]
        m_prev = [m_scr[prev, u] for u in range(len(units))]
        m = [None] * len(units)
        l = [jnp.zeros((1, FLASH_TILE), F32) for _ in units]
        acc = [jnp.zeros((v_dim, FLASH_TILE), F32) for _ in units]
        for (st, sz) in chunks:
            for u, (sub, a) in enumerate(units):
                s_t = _dot(k_ref[0, k_sel[a], st:st + sz, :], q[u])
                s_scr[slot, u, st:st + sz, :] = s_t
                cm = jnp.max(s_t, axis=0, keepdims=True)
                m[u] = cm if m[u] is None else jnp.maximum(m[u], cm)
                p = jnp.exp2(s_scr[prev, u, st:st + sz, :] - m_prev[u])
                l[u] = l[u] + jnp.sum(p, axis=0, keepdims=True)
                acc[u] = acc[u] + _dot(vt_ref[0, v_off[a]:v_off[a] + v_dim, st:st + sz], p.astype(BF16))
        for u in range(len(units)):
            m_scr[slot, u] = m[u]
        for sub in range(n_sub):
            toks = slice(sub * FLASH_TILE, (sub + 1) * FLASH_TILE)
            _gated_store([acc[2 * sub + a] / l[2 * sub + a] for a in (0, 1)], g_ref.at[toks, :], o_ref.at[toks, :])

    for parity in (0, 1):
        pl.when(j % 2 == parity)(functools.partial(step, parity, 1 - parity))


def _flash_ctx_body(q_ref, k_ref, vt_ref, g_ref, o_ref, s_scr, *, k_sel, v_off, v_dim, chunks):
    outs = []
    for a in range(2):
        m = _score_pass(q_ref[0, a], k_ref, k_sel[a], chunks, s_scr.at[a])
        outs.append(_value_pass(s_scr.at[a], m, vt_ref, slice(v_off[a], v_off[a] + v_dim), chunks))
    _gated_store(outs, g_ref, o_ref)


def _flash(q, k, vt, gate, batch, seq, ctx_len, k_heads_per_pair, v_rows_per_pair, pairs_per_kv, ctx_queries):
    heads = q.shape[1]
    t_len = seq + ctx_len
    nj = seq // TOKEN_TILE
    v_dim = LANES // 2
    k_sel = (0, 1) if k_heads_per_pair == 2 else (0, 0)
    v_off = (0, v_dim) if v_rows_per_pair == 2 * v_dim else (0, 0)
    k_blocks = k.shape[1] // k_heads_per_pair
    k_block = lambda p: (p // pairs_per_kv) % k_blocks
    chunks = tuple((c * KEY_CHUNK, KEY_CHUNK) for c in range(t_len // KEY_CHUNK))
    static = dict(k_sel=k_sel, v_off=v_off, v_dim=v_dim)

    width = gate.shape[1]
    tq = FLASH_TILE * FLASH_STEP_TILES
    n_q = seq // tq
    n_units = 2 * FLASH_STEP_TILES
    pairs = heads // 2
    n_tiles = batch * pairs * n_q

    def unravel(g):
        return g // (pairs * n_q), (g // n_q) % pairs, g % n_q

    scored = lambda g: unravel(jnp.minimum(g, n_tiles - 1))
    finished = lambda g: unravel(jnp.maximum(g - 1, 0))

    def q_map(g):
        b, p, j = scored(g)
        return b, p, 0, j

    def k_map(g):
        b, p, _ = scored(g)
        return b, k_block(p), 0, 0

    def v_map(g):
        b, p, _ = finished(g)
        return b, p // pairs_per_kv, 0

    def done(g):
        b, p, j = finished(g)
        return b * n_q + j, p

    out = pl.pallas_call(
        functools.partial(_flash_body, chunks=chunks, **static),
        out_shape=jax.ShapeDtypeStruct((batch * seq, width), BF16),
        grid=(n_tiles + 1,),
        in_specs=[pl.BlockSpec((1, 2, LANES, tq), q_map),
                  pl.BlockSpec((1, k_heads_per_pair, t_len, LANES), k_map),
                  pl.BlockSpec((1, v_rows_per_pair, t_len), v_map),
                  pl.BlockSpec((tq, LANES), done)],
        out_specs=pl.BlockSpec((tq, LANES), done),
        scratch_shapes=[pltpu.VMEM((2, n_units, t_len, FLASH_TILE), F32),
                        pltpu.VMEM((2, n_units, 1, FLASH_TILE), F32)],
        compiler_params=_params(1), name="flash_attention",
    )(q, k, vt, gate)
    if not ctx_queries:
        return (out,)

    out_ctx = pl.pallas_call(
        functools.partial(_flash_ctx_body, chunks=((0, ctx_len),), **static),
        out_shape=jax.ShapeDtypeStruct((batch * ctx_len, width), BF16),
        grid=(batch, heads // 2),
        in_specs=[pl.BlockSpec((1, 2, LANES, TOKEN_TILE), lambda b, p: (b, p, 0, nj)),
                  pl.BlockSpec((1, k_heads_per_pair, ctx_len, LANES), lambda b, p: (b, k_block(p), seq // ctx_len, 0)),
                  pl.BlockSpec((1, v_rows_per_pair, ctx_len), lambda b, p: (b, p // pairs_per_kv, seq // ctx_len)),
                  pl.BlockSpec((TOKEN_TILE, LANES), lambda b, p: (batch * nj + b, p))],
        out_specs=pl.BlockSpec((TOKEN_TILE, LANES), lambda b, p: (b, p)),
        scratch_shapes=[pltpu.VMEM((2, ctx_len, TOKEN_TILE), F32)],
        compiler_params=_params(2), name="flash_attention_ctx",
    )(q, k, vt, gate)
    return out, out_ctx


NA_KEY_ROWS = 12
NA_STEP_ROWS = TOKEN_TILE // GRID_W
NA_VARIANTS = 3


def _na_rel_row(variant, rr, a):
    if variant == 0:
        valid, dr = a < WIN_R, a - rr
    elif variant == 1:
        dr = a - WIN_R // 2 - rr
        valid = -(WIN_R // 2) <= dr < WIN_R // 2
    else:
        valid, dr = a >= NA_KEY_ROWS - WIN_R, a - (NA_KEY_ROWS - NA_STEP_ROWS) - rr
    return dr if valid else None


def _na_bias_body(rpb_ref, o_ref, blk_scr):
    p = pl.program_id(0)
    shape = (GRID_W, LANES)
    kc = lax.broadcasted_iota(jnp.int32, shape, 0)
    lane = lax.broadcasted_iota(jnp.int32, shape, 1)
    qc = lane % GRID_W
    upper = lane >= GRID_W
    rel = kc - qc + (WIN_C - 1)
    c0 = jnp.clip(qc - WIN_C // 2, 0, GRID_W - WIN_C)
    col_ok = (kc >= c0) & (kc < c0 + WIN_C)
    n_rel_r, n_rel_c = 2 * WIN_R - 1, 2 * WIN_C - 1

    def block(dd, carry):
        base0 = (2 * p) * (n_rel_r * n_rel_c) + dd * n_rel_c
        base1 = base0 + n_rel_r * n_rel_c
        val = jnp.zeros(shape, F32)
        for jj in range(n_rel_c):
            val = jnp.where(rel == jj, jnp.where(upper, rpb_ref[base1 + jj], rpb_ref[base0 + jj]), val)
        blk_scr[dd] = jnp.where(col_ok, val * LOG2E, NEG)
        return carry

    lax.fori_loop(0, n_rel_r, block, 0)
    outside = jnp.full(shape, NEG, F32)
    for variant in range(NA_VARIANTS):
        for rr in range(NA_STEP_ROWS):
            for a in range(NA_KEY_ROWS):
                dr = _na_rel_row(variant, rr, a)
                o_ref[0, variant, rr, a * GRID_W:(a + 1) * GRID_W, :] = (
                    outside if dr is None else blk_scr[dr + WIN_R - 1])


def _na_bias(rpb):
    heads = rpb.shape[0]
    tab = (NA_VARIANTS, NA_STEP_ROWS, NA_KEY_ROWS * GRID_W, LANES)
    return pl.pallas_call(
        _na_bias_body,
        out_shape=jax.ShapeDtypeStruct((heads // 2,) + tab, F32),
        grid=(heads // 2,),
        in_specs=[pl.BlockSpec(memory_space=pltpu.SMEM)],
        out_specs=pl.BlockSpec((1,) + tab, lambda p: (p, 0, 0, 0, 0)),
        scratch_shapes=[pltpu.VMEM((2 * WIN_R - 1, GRID_W, LANES), F32)],
        compiler_params=_params(1), name="na_bias",
    )(rpb.reshape(-1))


NA_CHUNK = 256
NA_STEP_BLOCKS = 2


def _na_body(q_ref, kl_ref, kc_ref, vt_ref, bias_ref, g_ref, o_ref, s_scr, m_scr, *, seq, ctx_len, n_steps, n_total):
    g = pl.program_id(0)
    j_scored = jnp.minimum(g, n_total - 1) % n_steps
    j_finished = jnp.maximum(g - 1, 0) % n_steps
    n_rows = seq // GRID_W
    n_loc = NA_KEY_ROWS * GRID_W // NA_CHUNK
    cols = NA_STEP_ROWS * LANES
    lower = lax.broadcasted_iota(jnp.int32, (GRID_W, LANES), 1) < GRID_W

    @pl.when(g == 0)
    def _():
        s_scr[...] = jnp.zeros(s_scr.shape, F32)
        m_scr[...] = jnp.zeros(m_scr.shape, F32)

    def span_start(step):
        first_row = jnp.clip(NA_STEP_ROWS * step - WIN_R // 2, 0, n_rows - NA_KEY_ROWS)
        return first_row * GRID_W

    def step(slot, prev):
        for sub in range(NA_STEP_BLOCKS):
            toks = slice(sub * TOKEN_TILE, (sub + 1) * TOKEN_TILE)
            blk = NA_STEP_BLOCKS * j_scored + sub
            blk_prev = NA_STEP_BLOCKS * j_finished + sub
            variant = jnp.where(blk == 0, 0, jnp.where(blk == NA_STEP_BLOCKS * n_steps - 1, 2, 1))
            k0, v0 = span_start(blk), span_start(blk_prev)
            q = q_ref[toks, :]
            zero = jnp.zeros((GRID_W, LANES), q.dtype)
            parts = []
            for rr in range(NA_STEP_ROWS):
                q_r = q[rr * GRID_W:(rr + 1) * GRID_W]
                parts += [jnp.where(lower, q_r, zero), jnp.where(lower, zero, q_r)]
            q2 = jnp.concatenate(parts, axis=0)
            m_prev = m_scr[prev, sub]
            m = None
            l = jnp.zeros((1, cols), F32)
            acc = jnp.zeros((LANES, cols), F32)
            for c in range(n_loc + ctx_len // NA_CHUNK):
                rows = slice((c % n_loc) * NA_CHUNK, (c % n_loc + 1) * NA_CHUNK)
                if c < n_loc:
                    s = _dot_nt(kl_ref[pl.ds(pl.multiple_of(k0 + c * NA_CHUNK, NA_CHUNK), NA_CHUNK), :], q2)
                    s = jnp.concatenate([s[:, rr * LANES:(rr + 1) * LANES] + bias_ref[0, variant, rr, rows, :]
                                         for rr in range(NA_STEP_ROWS)], axis=1)
                    v_chunk = vt_ref[0, :, pl.ds(pl.multiple_of(v0 + c * NA_CHUNK, NA_CHUNK), NA_CHUNK)]
                else:
                    s = _dot_nt(kc_ref[rows, :], q2)
                    v_chunk = vt_ref[0, :, seq + rows.start:seq + rows.stop]
                s_scr[slot, sub, c] = s
                cm = jnp.max(s, axis=0, keepdims=True)
                m = cm if m is None else jnp.maximum(m, cm)
                p = jnp.exp2(s_scr[prev, sub, c] - m_prev)
                l = l + jnp.sum(p, axis=0, keepdims=True)
                acc = acc + _dot(v_chunk, p.astype(BF16))
            m_scr[slot, sub] = m
            r_n = (acc / l).T
            outs = [jnp.where(lower, r_n[rr * LANES:rr * LANES + GRID_W], r_n[rr * LANES + GRID_W:(rr + 1) * LANES])
                    for rr in range(NA_STEP_ROWS)]
            o = jnp.concatenate(outs, axis=0)
            o_ref[toks, :] = (o * _silu(g_ref[toks, :].astype(F32))).astype(o_ref.dtype)

    for parity in (0, 1):
        pl.when(g % 2 == parity)(functools.partial(step, parity, 1 - parity))


def _na_attention(q, k, vt, bias, gate, batch, seq, ctx_len):
    pairs = q.shape[1] // LANES
    t_len = seq + ctx_len
    step_tokens = NA_STEP_BLOCKS * TOKEN_TILE
    n_steps = seq // step_tokens
    n_chunks = (NA_KEY_ROWS * GRID_W + ctx_len) // NA_CHUNK
    cols = NA_STEP_ROWS * LANES
    n_total = batch * pairs * n_steps

    def unravel(g):
        return g // (pairs * n_steps), (g // n_steps) % pairs, g % n_steps

    scored = lambda g: unravel(jnp.minimum(g, n_total - 1))
    finished = lambda g: unravel(jnp.maximum(g - 1, 0))

    def q_map(g):
        b, p, j = scored(g)
        return b * n_steps + j, p

    def k_map(g):
        b, p, _ = scored(g)
        return b, p

    def k_ctx_map(g):
        b, p, _ = scored(g)
        return batch * seq // ctx_len + b, p

    def v_map(g):
        b, p, _ = finished(g)
        return b, p, 0

    def done(g):
        b, p, j = finished(g)
        return b * n_steps + j, p

    body = functools.partial(_na_body, seq=seq, ctx_len=ctx_len, n_steps=n_steps, n_total=n_total)
    return pl.pallas_call(
        body, out_shape=jax.ShapeDtypeStruct((batch * seq, q.shape[1]), BF16),
        grid=(n_total + 1,),
        in_specs=[pl.BlockSpec((step_tokens, LANES), q_map),
                  pl.BlockSpec((seq, LANES), k_map),
                  pl.BlockSpec((ctx_len, LANES), k_ctx_map),
                  pl.BlockSpec((1, LANES, t_len), v_map),
                  pl.BlockSpec((1,) + bias.shape[1:], lambda g: (scored(g)[1], 0, 0, 0, 0)),
                  pl.BlockSpec((step_tokens, LANES), done)],
        out_specs=pl.BlockSpec((step_tokens, LANES), done),
        scratch_shapes=[pltpu.VMEM((2, NA_STEP_BLOCKS, n_chunks, NA_CHUNK, cols), F32),
                        pltpu.VMEM((2, NA_STEP_BLOCKS, 1, cols), F32)],
        compiler_params=_params(1), name="neighborhood_attention",
    )(q, k, k, vt, bias, gate)


def _seg_scans(jobs, use_max=False):
    vals = [v for v, _ in jobs]
    n = vals[0].shape[1]
    lane = lax.broadcasted_iota(jnp.int32, vals[0].shape, 1) % ML_CHUNK
    k = 1
    while k < ML_CHUNK:
        for i, (_, reverse) in enumerate(jobs):
            v = vals[i]
            if reverse:
                ok, shifted = lane < ML_CHUNK - k, pltpu.roll(v, n - k, 1)
            else:
                ok, shifted = lane >= k, pltpu.roll(v, k, 1)
            vals[i] = jnp.maximum(v, jnp.where(ok, shifted, NEG)) if use_max else v + jnp.where(ok, shifted, 0.0)
        k *= 2
    return vals


ML_GATE_ROWS = 40


def _mlstm_prep_body(u_ref, up_ref, un_ref, cw_ref, cb_ref, wqk_ref, wv_ref, wg_ref, bg_ref,
                     xc_ref, k_ref, qt_ref, vt_ref, pre_ref, *, nj, width, kscale):
    j = pl.program_id(1)
    u = u_ref[...]
    row = lax.broadcasted_iota(jnp.int32, u.shape, 0)
    prev = jnp.where((j > 0) & (j < nj), up_ref[7:8, :], 0.0)
    nxt = jnp.where(j < nj - 1, un_ref[0:1, :], 0.0)
    u_m1 = jnp.where(row == 0, prev, pltpu.roll(u, 1, 0))
    u_p1 = jnp.where(row == TOKEN_TILE - 1, nxt, pltpu.roll(u, TOKEN_TILE - 1, 0))
    cw = cw_ref[...]
    xc = _silu(u_m1 * cw[0:1] + u * cw[1:2] + u_p1 * cw[2:3] + cb_ref[...])
    xcb = xc.astype(BF16)
    xc_ref[...] = xcb
    qk = _dot(xcb, wqk_ref[...])
    v = _dot(u.astype(BF16), wv_ref[...])
    qb, kb, vb = qk[:, :width].astype(BF16), qk[:, width:].astype(BF16), v.astype(BF16)
    k_ref[...] = (qk[:, width:] * kscale).astype(BF16)
    qt_ref[0] = qk[:, :width].T.astype(BF16)
    vt_ref[0] = v.T.astype(BF16)
    pre_ref[0] = _dot_nt(wg_ref[0], qb) + _dot_nt(wg_ref[1], kb) + _dot_nt(wg_ref[2], vb) + bg_ref[...]


def _mlstm_gates_body(pre_ref, g_ref, gc_ref):
    pre = pre_ref[0]
    i8 = [pre[16 * d:16 * d + 8] for d in range(2)]
    f8 = [_log_sigmoid(pre[16 * d + 8:16 * d + 16]) for d in range(2)]
    b0, b0_rev, b1, b1_rev = _seg_scans([(f8[0], False), (f8[0], True), (f8[1], True), (f8[1], False)])
    b8, b_last = [b0, b1], [b0 + b0_rev - f8[0], b1 + b1_rev - f8[1]]
    r8 = [i8[d] - b8[d] for d in range(2)]
    c0, c0_rev, c1, c1_rev = _seg_scans([(r8[0], False), (r8[0], True), (r8[1], True), (r8[1], False)], use_max=True)
    c8, r_max = [c0, c1], [jnp.maximum(c0, c0_rev), jnp.maximum(c1, c1_rev)]
    for d in range(2):
        g_ref[0, d * ML_GATE_ROWS:(d + 1) * ML_GATE_ROWS] = jnp.concatenate(
            [-c8[d], b8[d] + c8[d], jnp.exp(r8[d] - r_max[d]), b_last[d], b_last[d] + r_max[d]], axis=0)
    pad = jnp.zeros((LANES - 16, pre.shape[1]), F32)
    gc_ref[0] = jnp.concatenate(r8 + [pad], axis=0).T


def _mlstm_gates(pre):
    batch, _, t_len = pre.shape
    return pl.pallas_call(
        _mlstm_gates_body,
        out_shape=[jax.ShapeDtypeStruct((batch, 2 * ML_GATE_ROWS, t_len), F32),
                   jax.ShapeDtypeStruct((batch, t_len, LANES), F32)],
        grid=(batch,),
        in_specs=[pl.BlockSpec((1,) + pre.shape[1:], lambda b: (b, 0, 0))],
        out_specs=[pl.BlockSpec((1, 2 * ML_GATE_ROWS, t_len), lambda b: (b, 0, 0)),
                   pl.BlockSpec((1, t_len, LANES), lambda b: (b, 0, 0))],
        compiler_params=_params(1), name="mlstm_gates",
    )(pre)


def _mlstm_prep(u, conv_w, conv_b, w_q, w_k, w_v, w_gate, b_gate, batch, seq, ctx_len):
    heads, hd = w_q.shape[0], w_q.shape[1]
    assert heads == ML_HEADS and hd == ML_CHUNK
    width = heads * hd
    nj = seq // TOKEN_TILE
    t_len = seq + ctx_len
    rows = u.shape[0]

    def block_diag(w):
        eye = jnp.eye(heads, dtype=w.dtype)
        return (eye[:, None, :, None] * w[:, :, None, :]).reshape(width, width)

    wqk = jnp.concatenate([block_diag(w_q), block_diag(w_k)], axis=1).astype(BF16)
    wv = block_diag(w_v).astype(BF16)
    wg = w_gate.reshape(2, heads, 3, hd, 2, heads).transpose(2, 0, 4, 5, 1, 3).reshape(3, 2, 2, heads, width)
    wg = jnp.pad(wg, ((0, 0), (0, 0), (0, 0), (0, 8 - heads), (0, 0))).reshape(3, 32, width).astype(BF16)
    bg = jnp.pad(b_gate.reshape(2, 2, heads), ((0, 0), (0, 0), (0, 8 - heads))).reshape(32, 1)
    n_halo = rows // 8
    tokb = lambda b, j: _tok_block(b, j, nj, batch)
    tile = pl.BlockSpec((TOKEN_TILE, width), lambda b, j: (tokb(b, j), 0))
    feat = pl.BlockSpec((1, width, TOKEN_TILE), lambda b, j: (b, 0, j))
    body = functools.partial(_mlstm_prep_body, nj=nj, width=width, kscale=hd ** -0.5)
    per_tile = TOKEN_TILE // 8
    return pl.pallas_call(
        body,
        out_shape=[jax.ShapeDtypeStruct((rows, width), BF16)] * 2
        + [jax.ShapeDtypeStruct((batch, width, t_len), BF16)] * 2
        + [jax.ShapeDtypeStruct((batch, 32, t_len), F32)],
        grid=(batch, nj + 1),
        in_specs=[tile,
                  pl.BlockSpec((8, width), lambda b, j: (jnp.maximum(tokb(b, j) * per_tile - 1, 0), 0)),
                  pl.BlockSpec((8, width), lambda b, j: (jnp.minimum((tokb(b, j) + 1) * per_tile, n_halo - 1), 0)),
                  _full((3, width)), _full((1, width)), _full(wqk.shape), _full(wv.shape),
                  _full(wg.shape), _full((32, 1))],
        out_specs=[tile, tile, feat, feat, pl.BlockSpec((1, 32, TOKEN_TILE), lambda b, j: (b, 0, j))],
        compiler_params=_params(2), name="mlstm_prep",
    )(u, u, u, conv_w, conv_b.reshape(1, width), wqk, wv, wg, bg)


def _mlstm_seq_body(kf, qtf, vtf, gf, gcf, kb, qtb, vtb, gb, gcb, hf_ref, hb_ref, c_s, n_s, m_s):
    t = pl.program_id(1)
    L = ML_CHUNK
    subs = TOKEN_TILE // L

    @pl.when(t == 0)
    def _():
        c_s[...] = jnp.zeros(c_s.shape, F32)
        n_s[...] = jnp.zeros(n_s.shape, F32)
        m_s[...] = jnp.zeros(m_s.shape, F32)

    si = lax.broadcasted_iota(jnp.int32, (L, L), 0)
    li = lax.broadcasted_iota(jnp.int32, (L, L), 1)
    streams = ((kf, qtf, vtf, gf, gcf, hf_ref, si <= li), (kb, qtb, vtb, gb, gcb, hb_ref, si >= li))
    units = {}
    for d, (k_ref, qt_ref, vt_ref, g_ref, gc_ref, h_ref, incl) in enumerate(streams):
        for sub in range(subs):
            toks = slice(sub * L, (sub + 1) * L)
            g = g_ref[0, :, toks]
            gc = gc_ref[0, toks, :]
            for h in range(ML_HEADS):
                cols = slice(h * L, (h + 1) * L)
                k, qt, vt = k_ref[toks, cols], qt_ref[0, cols, toks], vt_ref[0, cols, toks]
                neg_c, m_loc, w0, b_last, g_max = (g[8 * i + h:8 * i + h + 1] for i in range(5))
                r_col = gc[:, 8 * d + h:8 * d + h + 1]
                p0 = jnp.where(incl, jnp.exp(r_col + neg_c), 0.0) * _dot(k, qt)
                s_sum = jnp.sum(p0, axis=0, keepdims=True)
                intra = _dot(vt, p0.astype(BF16))
                c_inc = _dot((vt.astype(F32) * w0).astype(BF16), k)
                n_inc = _dot(jnp.broadcast_to(w0, (8, L)).astype(BF16), k)
                units[d, sub, h] = (h_ref, cols, toks, qt, neg_c, m_loc, b_last, g_max, s_sum, intra, c_inc, n_inc)
    for stage in range(subs):
        for d in range(2):
            sub = stage if d == 0 else subs - 1 - stage
            for h in range(ML_HEADS):
                idx = d * ML_HEADS + h
                h_ref, cols, toks, qt, neg_c, m_loc, b_last, g_max, s_sum, intra, c_inc, n_inc = units[d, sub, h]
                c_st, n_st, m_st = c_s[idx], n_s[idx], m_s[idx]
                cn = _dot(jnp.concatenate([c_st, n_st], axis=0).astype(BF16), qt)
                delta = jnp.maximum(m_st + neg_c, 0.0)
                e_intra = jnp.exp(-delta)
                w_inter = jnp.exp(m_st + neg_c - delta)
                num = w_inter * cn[:L] + e_intra * intra
                den = w_inter * cn[L:L + 1] + e_intra * s_sum
                h_out = num / jnp.maximum(jnp.abs(den), jnp.exp(-(m_loc + delta)))
                h_ref[0, cols, toks] = h_out.astype(h_ref.dtype)
                m_new = jnp.maximum(b_last + m_st, g_max)
                decay = jnp.exp(b_last + m_st - m_new)
                gain = jnp.exp(g_max - m_new)
                c_s[idx] = decay * c_st + gain * c_inc
                n_s[idx] = decay * n_st + gain * n_inc
                m_s[idx] = m_new


def _mlstm_seq(k, qt, vt, gates, gcols, batch, seq, ctx_len):
    width = k.shape[1]
    L = ML_CHUNK
    assert ctx_len == TOKEN_TILE
    nj = seq // TOKEN_TILE
    fwd = lambda t: (t + nj) % (nj + 1)
    bwd = lambda t: nj - t

    def stream(tile_of, d):
        feat = pl.BlockSpec((1, width, TOKEN_TILE), lambda b, t: (b, 0, tile_of(t)))
        return [pl.BlockSpec((TOKEN_TILE, width), lambda b, t: (_tok_block(b, tile_of(t), nj, batch), 0)), feat, feat,
                pl.BlockSpec((1, ML_GATE_ROWS, TOKEN_TILE), lambda b, t: (b, d, tile_of(t))),
                pl.BlockSpec((1, TOKEN_TILE, LANES), lambda b, t: (b, tile_of(t), 0))]

    out_f = pl.BlockSpec((1, width, TOKEN_TILE), lambda b, t: (b, 0, fwd(t)))
    out_b = pl.BlockSpec((1, width, TOKEN_TILE), lambda b, t: (b, 0, bwd(t)))
    n_state = 2 * ML_HEADS
    return pl.pallas_call(
        _mlstm_seq_body,
        out_shape=[jax.ShapeDtypeStruct(qt.shape, BF16)] * 2,
        grid=(batch, nj + 1),
        in_specs=stream(fwd, 0) + stream(bwd, 1),
        out_specs=[out_f, out_b],
        scratch_shapes=[pltpu.VMEM((n_state, L, L), F32), pltpu.VMEM((n_state, 8, L), F32),
                        pltpu.VMEM((n_state, 1, L), F32)],
        compiler_params=_params(2), name="mlstm_recurrence",
    )(k, qt, vt, gates, gcols, k, qt, vt, gates, gcols)


def kernel(x, c, ctx, c_ctx, w_mod, b_mod, g_norm, ab_w_in, ab_w_out, mla_g_q, mla_w_uq, mla_g_kv, mla_w_ukv,
           ml_conv_w, ml_conv_b, ml_w_q, ml_w_k, ml_w_v, ml_w_gate, ml_b_gate, ml_g_head, ml_skip,
           cd_w_in, cd_w_out, na_rpb, gqa_g_q, gqa_g_k, g_final):
    batch, seq, d = x.shape
    ctx_len = ctx.shape[1]
    assert ctx_len == TOKEN_TILE and seq % KEY_CHUNK == 0 and seq // GRID_W >= NA_KEY_ROWS
    dims = (batch, seq, ctx_len)

    mla_heads, mla_rope, mla_v = 8, 32, 64
    mla_nope = mla_w_uq.shape[2] // mla_heads - mla_rope
    q_lora, kv_lora = mla_g_q.shape[1], mla_g_kv.shape[1]
    ml_width = ml_conv_w.shape[2]
    mla_width = mla_heads * mla_v
    gqa_heads, gqa_dim = 8, gqa_g_q.shape[1]
    gqa_kv = (cd_w_in.shape[2] - 4 * 512 - 2 * gqa_heads * gqa_dim) // (2 * gqa_dim)
    na_width = na_rpb.shape[1] * 64

    mod_rows = -(-(batch + 1) // 8) * 8
    cvec = jnp.concatenate([c, c_ctx[None], jnp.zeros((mod_rows - batch - 1, d), F32)], axis=0)
    mod = _modulation(cvec, w_mod, b_mod)
    mod0 = mod[0].reshape(mod_rows, 1, 3 * d)
    mod1 = mod[1].reshape(mod_rows, 1, 3 * d)

    tok0 = _Tokens((x.reshape(batch * seq, d), ctx.reshape(batch * ctx_len, d)), *dims)
    w_in = ab_w_in[0]
    s1 = q_lora + kv_lora
    zcol = lambda n: jnp.zeros((d, n), w_in.dtype)
    w0 = jnp.concatenate([w_in[:, :s1], zcol(mla_nope), w_in[:, s1:s1 + mla_rope],
                          zcol(LANES - mla_nope - mla_rope), w_in[:, s1 + mla_rope:]], axis=1).astype(BF16)
    o_pa = s1 + LANES
    outs0 = ((0, o_pa, F32, 1.0, False), (o_pa, mla_width, BF16, 1.0, False),
             (o_pa + mla_width, ml_width, F32, 1.0, False), (o_pa + mla_width + ml_width, ml_width, BF16, 1.0, False))
    pa, gate_a, u, z = _in_proj(tok0, mod0, g_norm[0], w0, outs0, *dims)

    def mla_lanes(table, fill):
        n = table.shape[0]
        return jnp.concatenate([jnp.full((n, mla_nope), fill, F32), table,
                                jnp.full((n, LANES - mla_nope - mla_rope), fill, F32)], axis=-1)

    cos_a, sin_a = _rope_tables(seq, ctx_len, mla_rope)
    cos_a, sin_a = mla_lanes(cos_a, 1.0), mla_lanes(sin_a, 0.0)
    q_a, k_a, vt_a = _mla_prep(pa, cos_a, sin_a, mla_g_q[0], mla_g_kv[0], mla_w_uq[0], mla_w_ukv[0],
                               *dims, mla_heads, mla_nope, mla_rope, mla_v)
    mix_a = _flash(q_a, k_a, vt_a, gate_a, *dims, k_heads_per_pair=2, v_rows_per_pair=2 * mla_v,
                   pairs_per_kv=1, ctx_queries=True)

    xc, k_m, qt_m, vt_m, gate_pre = _mlstm_prep(u, ml_conv_w[0], ml_conv_b[0], ml_w_q[0], ml_w_k[0], ml_w_v[0],
                                                ml_w_gate[0], ml_b_gate[0], *dims)
    gates, gcols = _mlstm_gates(gate_pre)
    h_f, h_b = _mlstm_seq(k_m, qt_m, vt_m, gates, gcols, *dims)
    x1 = _out_proj(tok0, _Tokens(mix_a, *dims), (h_f, h_b, xc, z, ml_g_head[0], ml_skip[0]),
                   ab_w_out[0], mod0, batch, seq)

    tok1 = _Tokens((x1,), *dims)
    w1 = cd_w_in[0].astype(BF16)
    gq_w, gkv_w = gqa_heads * gqa_dim, gqa_kv * gqa_dim
    o_d = 4 * na_width
    outs1 = ((0, na_width, BF16, 64 ** -0.5 * LOG2E, False), (na_width, na_width, BF16, 1.0, False),
             (2 * na_width, na_width, BF16, 1.0, True), (3 * na_width, na_width, BF16, 1.0, False),
             (o_d, gq_w + gkv_w, F32, 1.0, True), (o_d + gq_w + gkv_w, gkv_w, BF16, 1.0, True),
             (o_d + gq_w + 2 * gkv_w, gq_w, BF16, 1.0, False))
    q_c, k_c, vt_c, gate_c, pd_t, vt_d, gate_d = _in_proj(tok1, mod1, g_norm[1], w1, outs1, *dims)

    mix_c = _na_attention(q_c, k_c, vt_c, _na_bias(na_rpb[0]), gate_c, *dims)

    cos_d, sin_d = _rope_tables(seq, ctx_len, gqa_dim)
    q_d, k_d = _gqa_prep(pd_t, cos_d, sin_d, gqa_g_q[0], gqa_g_k[0], *dims, gqa_heads, gqa_kv, gqa_dim)
    mix_d = _flash(q_d, k_d, vt_d, gate_d, *dims, k_heads_per_pair=1, v_rows_per_pair=gqa_dim,
                   pairs_per_kv=gqa_heads // (2 * gqa_kv), ctx_queries=False)

    out = _out_proj(tok1, _Tokens((mix_c,), *dims), mix_d[0], cd_w_out[0], mod1, batch, seq, g_final=g_final)
    return out.reshape(batch, seq, d)
```

```python
import functools

import jax
import jax.numpy as jnp
from jax import lax
from jax.experimental import pallas as pl
from jax.experimental.pallas import tpu as pltpu

F32 = jnp.float32
BF16 = jnp.bfloat16

LANES = 128
TOKEN_TILE = 256
KEY_CHUNK = 256
FLASH_TILE = 256
FLASH_STEP_TILES = 2
GRID_W = 64
WIN_R = 8
WIN_C = 16
ML_CHUNK = 128
ML_HEADS = 4
EPS = 1e-6
ROPE_BASE = 10000.0
LOG2E = 1.4426950408889634
NEG = -1e30
VMEM_LIMIT = 56 * 1024 * 1024

_NT = (((1,), (1,)), ((), ()))


def _dot(a, b):
    return jnp.dot(a, b, preferred_element_type=F32)


def _dot_nt(a, b):
    return lax.dot_general(a, b, _NT, preferred_element_type=F32)


def _silu(v):
    return v * (1.0 / (1.0 + jnp.exp(-v)))


def _log_sigmoid(v):
    return -(jnp.maximum(-v, 0.0) + jnp.log1p(jnp.exp(-jnp.abs(v))))


def _params(n_axes):
    return pltpu.CompilerParams(dimension_semantics=("arbitrary",) * n_axes, vmem_limit_bytes=VMEM_LIMIT)


def _full(shape):
    nd = len(shape)
    return pl.BlockSpec(shape, lambda *_: (0,) * nd)


def _mod_body(c_ref, w_ref, b_ref, o_ref):
    s = _silu(c_ref[...])
    o_ref[0] = _dot(s.astype(BF16), w_ref[0].astype(BF16)) + b_ref[0]


def _modulation(cvec, w_mod, b_mod):
    depth, d, n = w_mod.shape
    rows = cvec.shape[0]
    tn = n // 4
    return pl.pallas_call(
        _mod_body,
        out_shape=jax.ShapeDtypeStruct((depth, rows, n), F32),
        grid=(depth, n // tn),
        in_specs=[_full((rows, d)),
                  pl.BlockSpec((1, d, tn), lambda l, j: (l, 0, j)),
                  pl.BlockSpec((1, 1, tn), lambda l, j: (l, 0, j))],
        out_specs=pl.BlockSpec((1, rows, tn), lambda l, j: (l, 0, j)),
        compiler_params=_params(2), name="modulation",
    )(cvec, w_mod, b_mod.reshape(depth, 1, n))


class _Tokens:
    def __init__(self, arrays, batch, seq, ctx_len):
        self.arrays = arrays
        self.split = len(arrays) == 2
        self.n_lat = batch * seq // TOKEN_TILE
        self.n_ctx = batch * ctx_len // TOKEN_TILE
        self.d = arrays[0].shape[-1]

    def specs(self):
        blk = (TOKEN_TILE, self.d)
        if not self.split:
            return [pl.BlockSpec(blk, lambda i: (i, 0))]
        n_lat = self.n_lat
        return [pl.BlockSpec(blk, lambda i: (jnp.minimum(i, n_lat - 1), 0)),
                pl.BlockSpec(blk, lambda i: (jnp.maximum(i - n_lat, 0), 0))]

    def load(self, refs, i):
        if not self.split:
            return refs[0][...]
        return jnp.where(i < self.n_lat, refs[0][...], refs[1][...])


def _mod_spec(n_lat, nj, batch, width):
    return pl.BlockSpec((1, 1, width), lambda i: (jnp.where(i < n_lat, i // nj, batch), 0, 0))


def _in_proj_body(*refs, tok, outs, d):
    n_tok = len(tok.arrays)
    mod_ref, g_ref, w_ref = refs[n_tok:n_tok + 3]
    o_refs = refs[n_tok + 3:]
    i = pl.program_id(0)
    x = tok.load(refs[:n_tok], i)
    y = x * lax.rsqrt(jnp.mean(x * x, axis=-1, keepdims=True) + EPS) * g_ref[...]
    mod = mod_ref[0]
    h = y * (1.0 + mod[:, d:2 * d]) + mod[:, :d]
    acc = _dot(h.astype(BF16), w_ref[...])
    for o_ref, (c0, width, _, scale, transposed) in zip(o_refs, outs):
        v = acc[:, c0:c0 + width]
        if scale != 1.0:
            v = v * scale
        if transposed:
            o_ref[0] = v.T.astype(o_ref.dtype)
        else:
            o_ref[...] = v.astype(o_ref.dtype)


def _in_proj(tok, mod_l, g, w, outs, batch, seq, ctx_len):
    d = tok.d
    nj = seq // TOKEN_TILE
    n_lat, n_all = tok.n_lat, tok.n_lat + tok.n_ctx
    rows = n_all * TOKEN_TILE
    t_len = seq + ctx_len
    out_shape, out_specs = [], []
    for (_, width, dtype, _, transposed) in outs:
        if transposed:
            out_shape.append(jax.ShapeDtypeStruct((batch, width, t_len), dtype))
            out_specs.append(pl.BlockSpec(
                (1, width, TOKEN_TILE),
                lambda i: (jnp.where(i < n_lat, i // nj, i - n_lat), 0, jnp.where(i < n_lat, i % nj, nj))))
        else:
            out_shape.append(jax.ShapeDtypeStruct((rows, width), dtype))
            out_specs.append(pl.BlockSpec((TOKEN_TILE, width), lambda i: (i, 0)))
    body = functools.partial(_in_proj_body, tok=tok, outs=outs, d=d)
    return pl.pallas_call(
        body, out_shape=out_shape, grid=(n_all,),
        in_specs=tok.specs() + [_mod_spec(n_lat, nj, batch, 3 * d), _full((1, d)), _full(w.shape)],
        out_specs=out_specs, compiler_params=_params(1), name="in_proj",
    )(*tok.arrays, mod_l, g.reshape(1, d), w)


def _mlstm_mix(hf_ref, hb_ref, xc_ref, z_ref, gh_ref, sk_ref):
    ht = hf_ref[0].astype(F32) + hb_ref[0].astype(F32)
    L = ML_CHUNK
    normed = []
    for hd in range(ML_HEADS):
        hh = ht[hd * L:(hd + 1) * L]
        mu = jnp.mean(hh, axis=0, keepdims=True)
        var = jnp.mean(jnp.square(hh - mu), axis=0, keepdims=True)
        normed.append((hh - mu) * lax.rsqrt(var + EPS))
    hn = jnp.concatenate(normed, axis=0).T * gh_ref[...]
    return ((hn + sk_ref[...] * xc_ref[...].astype(F32)) * _silu(z_ref[...].astype(F32))).astype(BF16)


def _out_proj_body(*refs, tok, mix_a, n_b, d, final):
    n_tok, n_a = len(tok.arrays), len(mix_a.arrays)
    b_refs = refs[n_tok + n_a:n_tok + n_a + n_b]
    wa_ref, wb_ref, mod_ref = refs[n_tok + n_a + n_b:n_tok + n_a + n_b + 3]
    rest = refs[n_tok + n_a + n_b + 3:]
    i = pl.program_id(0)
    x = tok.load(refs[:n_tok], i)
    mb = b_refs[0][...] if n_b == 1 else _mlstm_mix(*b_refs)
    acc = _dot(mix_a.load(refs[n_tok:n_tok + n_a], i), wa_ref[...]) + _dot(mb, wb_ref[...])
    xn = x + mod_ref[0][:, 2 * d:] * acc
    if final:
        gf_ref, o_ref = rest
        xn = xn * lax.rsqrt(jnp.mean(xn * xn, axis=-1, keepdims=True) + EPS) * gf_ref[...]
    else:
        (o_ref,) = rest
    o_ref[...] = xn


def _out_proj(tok, mix_a, mix_b, w_out, mod_l, batch, seq, g_final=None):
    d = tok.d
    half = mix_a.d
    nj = seq // TOKEN_TILE
    n_lat = tok.n_lat
    final = g_final is not None
    n_tiles = n_lat if final else n_lat + tok.n_ctx
    wa, wb = w_out[:half].astype(BF16), w_out[half:].astype(BF16)
    tile = lambda width: pl.BlockSpec((TOKEN_TILE, width), lambda i: (i, 0))
    if isinstance(mix_b, tuple):
        hf, hb, xc, z, g_head, skip = mix_b
        width = xc.shape[1]
        feat = pl.BlockSpec((1, width, TOKEN_TILE),
                            lambda i: (jnp.where(i < n_lat, i // nj, i - n_lat), 0, jnp.where(i < n_lat, i % nj, nj)))
        b_specs = [feat, feat, tile(width), tile(width), _full((1, width)), _full((1, width))]
        b_args = [hf, hb, xc, z, g_head.reshape(1, width), skip.reshape(1, width)]
    else:
        b_specs, b_args = [tile(half)], [mix_b]
    in_specs = tok.specs() + mix_a.specs() + b_specs + [_full(wa.shape), _full(wb.shape),
                                                        _mod_spec(n_lat, nj, batch, 3 * d)]
    args = list(tok.arrays) + list(mix_a.arrays) + b_args + [wa, wb, mod_l]
    if final:
        in_specs.append(_full((1, d)))
        args.append(g_final.reshape(1, d))
    body = functools.partial(_out_proj_body, tok=tok, mix_a=mix_a, n_b=len(b_args), d=d, final=final)
    return pl.pallas_call(
        body, out_shape=jax.ShapeDtypeStruct((n_tiles * TOKEN_TILE, d), F32), grid=(n_tiles,),
        in_specs=in_specs, out_specs=tile(d), compiler_params=_params(1), name="out_proj",
    )(*args)


def _rope_tables(seq, ctx_len, rot_dim):
    t = jnp.arange(seq)
    pos = jnp.stack([t // GRID_W, t % GRID_W], axis=-1).astype(F32)
    n_freq = rot_dim // 4
    inv = ROPE_BASE ** (-jnp.arange(n_freq, dtype=F32) / n_freq)
    ang = pos[:, :, None] * inv
    cos, sin = jnp.cos(ang), jnp.sin(ang)
    cos_t = jnp.concatenate([cos[:, 0], cos[:, 0], cos[:, 1], cos[:, 1]], axis=-1)
    sin_t = jnp.concatenate([-sin[:, 0], sin[:, 0], -sin[:, 1], sin[:, 1]], axis=-1)
    cos_t = jnp.concatenate([cos_t, jnp.ones((ctx_len, rot_dim), F32)], axis=0)
    sin_t = jnp.concatenate([sin_t, jnp.zeros((ctx_len, rot_dim), F32)], axis=0)
    return cos_t, sin_t


def _rope(x, cos, sin, dist):
    lane = lax.broadcasted_iota(jnp.int32, x.shape, 1)
    first = (lane % (2 * dist)) < dist
    partner = jnp.where(first, pltpu.roll(x, LANES - dist, 1), pltpu.roll(x, dist, 1))
    return x * cos + partner * sin


def _rope_rows(x, cos, sin, dist):
    n = x.shape[0] // dist
    partner = jnp.concatenate([x[(i ^ 1) * dist:((i ^ 1) + 1) * dist] for i in range(n)], axis=0)
    return x * cos + partner * sin


def _tok_block(b, j, nj, batch):
    return jnp.where(j < nj, b * nj + j, batch * nj + b)


def _mla_prep_body(pa_ref, cos_ref, sin_ref, cost_ref, sint_ref, gq_ref, gkv_ref, wuqt_ref, wuk_ref, wuvt_ref,
                   q_ref, k_ref, vt_ref, *, heads, q_lora, kv_lora, qscale):
    pa = pa_ref[...]

    def norm(v, g_ref):
        return v * lax.rsqrt(jnp.mean(v * v, axis=-1, keepdims=True) + EPS) * g_ref[...]

    cq_t = norm(pa[:, :q_lora], gq_ref).T.astype(BF16)
    q_all = _dot(wuqt_ref[...], cq_t)
    cos_t, sin_t = cost_ref[...], sint_ref[...]
    for h in range(heads):
        qh = _rope_rows(q_all[h * LANES:(h + 1) * LANES], cos_t, sin_t, 8)
        q_ref[0, h] = (qh * qscale).astype(BF16)
    ckv = norm(pa[:, q_lora:q_lora + kv_lora], gkv_ref)
    k_nope = _dot(ckv.astype(BF16), wuk_ref[...])
    k_rope = _rope(pa[:, q_lora + kv_lora:], cos_ref[...], sin_ref[...], 8)
    for h in range(heads):
        k_ref[0, h] = (k_nope[:, h * LANES:(h + 1) * LANES] + k_rope).astype(BF16)
    vt_ref[0] = _dot(wuvt_ref[...], ckv.T.astype(BF16)).astype(BF16)


def _mla_prep(pa, cos, sin, g_q, g_kv, w_uq, w_ukv, batch, seq, ctx_len, heads, nope, rope, v_dim):
    q_lora, kv_lora = g_q.shape[0], g_kv.shape[0]
    nj = seq // TOKEN_TILE
    t_len = seq + ctx_len
    pad = LANES - nope - rope
    wuq = jnp.pad(w_uq.reshape(q_lora, heads, nope + rope), ((0, 0), (0, 0), (0, pad)))
    wuq_t = wuq.reshape(q_lora, heads * LANES).T.astype(BF16)
    wkv = w_ukv.reshape(kv_lora, heads, nope + v_dim)
    wuk = jnp.pad(wkv[..., :nope], ((0, 0), (0, 0), (0, LANES - nope))).reshape(kv_lora, heads * LANES).astype(BF16)
    wuv_t = wkv[..., nope:].reshape(kv_lora, heads * v_dim).T.astype(BF16)
    body = functools.partial(_mla_prep_body, heads=heads, q_lora=q_lora, kv_lora=kv_lora,
                             qscale=(nope + rope) ** -0.5 * LOG2E)
    head_major = pl.BlockSpec((1, heads, TOKEN_TILE, LANES), lambda j, b: (b, 0, j, 0))
    q_feature_major = pl.BlockSpec((1, heads, LANES, TOKEN_TILE), lambda j, b: (b, 0, 0, j))
    tok_table = pl.BlockSpec((TOKEN_TILE, LANES), lambda j, b: (j, 0))
    feat_table = pl.BlockSpec((LANES, TOKEN_TILE), lambda j, b: (0, j))
    return pl.pallas_call(
        body,
        out_shape=[jax.ShapeDtypeStruct((batch, heads, LANES, t_len), BF16),
                   jax.ShapeDtypeStruct((batch, heads, t_len, LANES), BF16),
                   jax.ShapeDtypeStruct((batch, heads * v_dim, t_len), BF16)],
        grid=(nj + 1, batch),
        in_specs=[pl.BlockSpec((TOKEN_TILE, pa.shape[1]), lambda j, b: (_tok_block(b, j, nj, batch), 0)),
                  tok_table, tok_table, feat_table, feat_table,
                  _full((1, q_lora)), _full((1, kv_lora)), _full(wuq_t.shape), _full(wuk.shape), _full(wuv_t.shape)],
        out_specs=[q_feature_major, head_major,
                   pl.BlockSpec((1, heads * v_dim, TOKEN_TILE), lambda j, b: (b, 0, j))],
        compiler_params=_params(2), name="mla_prep",
    )(pa, cos, sin, cos.T, sin.T, g_q.reshape(1, -1), g_kv.reshape(1, -1), wuq_t, wuk, wuv_t)


def _gqa_prep_body(pd_ref, cos_ref, sin_ref, gq_ref, gk_ref, q_ref, k_ref, *, heads, kv_heads, dim, qscale):
    cos, sin = cos_ref[...], sin_ref[...]

    def head(first_row, g_ref):
        x = pd_ref[0, first_row:first_row + dim, :]
        y = x * lax.rsqrt(jnp.mean(x * x, axis=0, keepdims=True) + EPS) * g_ref[...]
        return _rope_rows(y, cos, sin, dim // 4)

    zeros = jnp.zeros((dim, TOKEN_TILE), F32)
    for h in range(heads):
        parts = [zeros] * kv_heads
        parts[h // (heads // kv_heads)] = head(h * dim, gq_ref) * qscale
        q_ref[0, h] = jnp.concatenate(parts, axis=0).astype(BF16)
    keys = [head((heads + kv) * dim, gk_ref) for kv in range(kv_heads)]
    k_ref[0, 0] = jnp.concatenate(keys, axis=0).T.astype(BF16)


def _gqa_prep(pd_t, cos, sin, g_q, g_k, batch, seq, ctx_len, heads, kv_heads, head_dim):
    assert kv_heads * head_dim == LANES
    nj = seq // TOKEN_TILE
    t_len = seq + ctx_len
    body = functools.partial(_gqa_prep_body, heads=heads, kv_heads=kv_heads, dim=head_dim,
                             qscale=head_dim ** -0.5 * LOG2E)
    table = pl.BlockSpec((head_dim, TOKEN_TILE), lambda j, b: (0, j))
    gain = lambda g: jnp.broadcast_to(g[:, None], (head_dim, TOKEN_TILE))
    return pl.pallas_call(
        body,
        out_shape=[jax.ShapeDtypeStruct((batch, heads, LANES, t_len), BF16),
                   jax.ShapeDtypeStruct((batch, 1, t_len, LANES), BF16)],
        grid=(nj + 1, batch),
        in_specs=[pl.BlockSpec((1, pd_t.shape[1], TOKEN_TILE), lambda j, b: (b, 0, j)), table, table,
                  _full((head_dim, TOKEN_TILE)), _full((head_dim, TOKEN_TILE))],
        out_specs=[pl.BlockSpec((1, heads, LANES, TOKEN_TILE), lambda j, b: (b, 0, 0, j)),
                   pl.BlockSpec((1, 1, TOKEN_TILE, LANES), lambda j, b: (b, 0, j, 0))],
        compiler_params=_params(2), name="gqa_prep",
    )(pd_t, cos.T, sin.T, gain(g_q), gain(g_k))


def _score_pass(q, k_ref, k_head, chunks, s_ref):
    m = None
    for (st, sz) in chunks:
        s_t = _dot(k_ref[0, k_head, st:st + sz, :], q)
        s_ref[st:st + sz, :] = s_t
        cm = jnp.max(s_t, axis=0, keepdims=True)
        m = cm if m is None else jnp.maximum(m, cm)
    return m


def _value_pass(s_ref, m, vt_ref, v_rows, chunks):
    l = jnp.zeros(m.shape, F32)
    acc = jnp.zeros((v_rows.stop - v_rows.start, m.shape[1]), F32)
    for (st, sz) in chunks:
        p = jnp.exp2(s_ref[st:st + sz, :] - m)
        l = l + jnp.sum(p, axis=0, keepdims=True)
        acc = acc + _dot(vt_ref[0, v_rows, st:st + sz], p.astype(BF16))
    return acc / l


def _gated_store(outs, g_ref, o_ref):
    o2 = jnp.concatenate(outs, axis=0).T
    o_ref[...] = (o2 * _silu(g_ref[...].astype(F32))).astype(o_ref.dtype)


def _flash_body(q_ref, k_ref, vt_ref, g_ref, o_ref, s_scr, m_scr, *, k_sel, v_off, v_dim, chunks):
    j = pl.program_id(0)

    @pl.when(j == 0)
    def _():
        s_scr[...] = jnp.zeros(s_scr.shape, F32)
        m_scr[...] = jnp.zeros(m_scr.shape, F32)

    def step(slot, prev):
        n_sub = q_ref.shape[3] // FLASH_TILE
        units = [(sub, a) for sub in range(n_sub) for a in (0, 1)]
        q = [q_ref[0, a, :, sub * FLASH_TILE:(sub + 1) * FLASH_TILE] for sub, a in units]
        m_prev = [m_scr[prev, u] for u in range(len(units))]
        m = [None] * len(units)
        l = [jnp.zeros((1, FLASH_TILE), F32) for _ in units]
        acc = [jnp.zeros((v_dim, FLASH_TILE), F32) for _ in units]
        for (st, sz) in chunks:
            for u, (sub, a) in enumerate(units):
                s_t = _dot(k_ref[0, k_sel[a], st:st + sz, :], q[u])
                s_scr[slot, u, st:st + sz, :] = s_t
                cm = jnp.max(s_t, axis=0, keepdims=True)
                m[u] = cm if m[u] is None else jnp.maximum(m[u], cm)
                p = jnp.exp2(s_scr[prev, u, st:st + sz, :] - m_prev[u])
                l[u] = l[u] + jnp.sum(p, axis=0, keepdims=True)
                acc[u] = acc[u] + _dot(vt_ref[0, v_off[a]:v_off[a] + v_dim, st:st + sz], p.astype(BF16))
        for u in range(len(units)):
            m_scr[slot, u] = m[u]
        for sub in range(n_sub):
            toks = slice(sub * FLASH_TILE, (sub + 1) * FLASH_TILE)
            _gated_store([acc[2 * sub + a] / l[2 * sub + a] for a in (0, 1)], g_ref.at[toks, :], o_ref.at[toks, :])

    for parity in (0, 1):
        pl.when(j % 2 == parity)(functools.partial(step, parity, 1 - parity))


def _flash_ctx_body(q_ref, k_ref, vt_ref, g_ref, o_ref, s_scr, *, k_sel, v_off, v_dim, chunks):
    outs = []
    for a in range(2):
        m = _score_pass(q_ref[0, a], k_ref, k_sel[a], chunks, s_scr.at[a])
        outs.append(_value_pass(s_scr.at[a], m, vt_ref, slice(v_off[a], v_off[a] + v_dim), chunks))
    _gated_store(outs, g_ref, o_ref)


def _flash(q, k, vt, gate, batch, seq, ctx_len, k_heads_per_pair, v_rows_per_pair, pairs_per_kv, ctx_queries):
    heads = q.shape[1]
    t_len = seq + ctx_len
    nj = seq // TOKEN_TILE
    v_dim = LANES // 2
    k_sel = (0, 1) if k_heads_per_pair == 2 else (0, 0)
    v_off = (0, v_dim) if v_rows_per_pair == 2 * v_dim else (0, 0)
    k_blocks = k.shape[1] // k_heads_per_pair
    k_block = lambda p: (p // pairs_per_kv) % k_blocks
    chunks = tuple((c * KEY_CHUNK, KEY_CHUNK) for c in range(t_len // KEY_CHUNK))
    static = dict(k_sel=k_sel, v_off=v_off, v_dim=v_dim)

    width = gate.shape[1]
    tq = FLASH_TILE * FLASH_STEP_TILES
    n_q = seq // tq
    n_units = 2 * FLASH_STEP_TILES
    pairs = heads // 2
    n_tiles = batch * pairs * n_q

    def unravel(g):
        return g // (pairs * n_q), (g // n_q) % pairs, g % n_q

    scored = lambda g: unravel(jnp.minimum(g, n_tiles - 1))
    finished = lambda g: unravel(jnp.maximum(g - 1, 0))

    def q_map(g):
        b, p, j = scored(g)
        return b, p, 0, j

    def k_map(g):
        b, p, _ = scored(g)
        return b, k_block(p), 0, 0

    def v_map(g):
        b, p, _ = finished(g)
        return b, p // pairs_per_kv, 0

    def done(g):
        b, p, j = finished(g)
        return b * n_q + j, p

    out = pl.pallas_call(
        functools.partial(_flash_body, chunks=chunks, **static),
        out_shape=jax.ShapeDtypeStruct((batch * seq, width), BF16),
        grid=(n_tiles + 1,),
        in_specs=[pl.BlockSpec((1, 2, LANES, tq), q_map),
                  pl.BlockSpec((1, k_heads_per_pair, t_len, LANES), k_map),
                  pl.BlockSpec((1, v_rows_per_pair, t_len), v_map),
                  pl.BlockSpec((tq, LANES), done)],
        out_specs=pl.BlockSpec((tq, LANES), done),
        scratch_shapes=[pltpu.VMEM((2, n_units, t_len, FLASH_TILE), F32),
                        pltpu.VMEM((2, n_units, 1, FLASH_TILE), F32)],
        compiler_params=_params(1), name="flash_attention",
    )(q, k, vt, gate)
    if not ctx_queries:
        return (out,)

    out_ctx = pl.pallas_call(
        functools.partial(_flash_ctx_body, chunks=((0, ctx_len),), **static),
        out_shape=jax.ShapeDtypeStruct((batch * ctx_len, width), BF16),
        grid=(batch, heads // 2),
        in_specs=[pl.BlockSpec((1, 2, LANES, TOKEN_TILE), lambda b, p: (b, p, 0, nj)),
                  pl.BlockSpec((1, k_heads_per_pair, ctx_len, LANES), lambda b, p: (b, k_block(p), seq // ctx_len, 0)),
                  pl.BlockSpec((1, v_rows_per_pair, ctx_len), lambda b, p: (b, p // pairs_per_kv, seq // ctx_len)),
                  pl.BlockSpec((TOKEN_TILE, LANES), lambda b, p: (batch * nj + b, p))],
        out_specs=pl.BlockSpec((TOKEN_TILE, LANES), lambda b, p: (b, p)),
        scratch_shapes=[pltpu.VMEM((2, ctx_len, TOKEN_TILE), F32)],
        compiler_params=_params(2), name="flash_attention_ctx",
    )(q, k, vt, gate)
    return out, out_ctx


NA_KEY_ROWS = 12
NA_STEP_ROWS = TOKEN_TILE // GRID_W
NA_VARIANTS = 3


def _na_rel_row(variant, rr, a):
    if variant == 0:
        valid, dr = a < WIN_R, a - rr
    elif variant == 1:
        dr = a - WIN_R // 2 - rr
        valid = -(WIN_R // 2) <= dr < WIN_R // 2
    else:
        valid, dr = a >= NA_KEY_ROWS - WIN_R, a - (NA_KEY_ROWS - NA_STEP_ROWS) - rr
    return dr if valid else None


def _na_bias_body(rpb_ref, o_ref, blk_scr):
    p = pl.program_id(0)
    shape = (GRID_W, LANES)
    kc = lax.broadcasted_iota(jnp.int32, shape, 0)
    lane = lax.broadcasted_iota(jnp.int32, shape, 1)
    qc = lane % GRID_W
    upper = lane >= GRID_W
    rel = kc - qc + (WIN_C - 1)
    c0 = jnp.clip(qc - WIN_C // 2, 0, GRID_W - WIN_C)
    col_ok = (kc >= c0) & (kc < c0 + WIN_C)
    n_rel_r, n_rel_c = 2 * WIN_R - 1, 2 * WIN_C - 1

    def block(dd, carry):
        base0 = (2 * p) * (n_rel_r * n_rel_c) + dd * n_rel_c
        base1 = base0 + n_rel_r * n_rel_c
        val = jnp.zeros(shape, F32)
        for jj in range(n_rel_c):
            val = jnp.where(rel == jj, jnp.where(upper, rpb_ref[base1 + jj], rpb_ref[base0 + jj]), val)
        blk_scr[dd] = jnp.where(col_ok, val * LOG2E, NEG)
        return carry

    lax.fori_loop(0, n_rel_r, block, 0)
    outside = jnp.full(shape, NEG, F32)
    for variant in range(NA_VARIANTS):
        for rr in range(NA_STEP_ROWS):
            for a in range(NA_KEY_ROWS):
                dr = _na_rel_row(variant, rr, a)
                o_ref[0, variant, rr, a * GRID_W:(a + 1) * GRID_W, :] = (
                    outside if dr is None else blk_scr[dr + WIN_R - 1])


def _na_bias(rpb):
    heads = rpb.shape[0]
    tab = (NA_VARIANTS, NA_STEP_ROWS, NA_KEY_ROWS * GRID_W, LANES)
    return pl.pallas_call(
        _na_bias_body,
        out_shape=jax.ShapeDtypeStruct((heads // 2,) + tab, F32),
        grid=(heads // 2,),
        in_specs=[pl.BlockSpec(memory_space=pltpu.SMEM)],
        out_specs=pl.BlockSpec((1,) + tab, lambda p: (p, 0, 0, 0, 0)),
        scratch_shapes=[pltpu.VMEM((2 * WIN_R - 1, GRID_W, LANES), F32)],
        compiler_params=_params(1), name="na_bias",
    )(rpb.reshape(-1))


NA_CHUNK = 256
NA_STEP_BLOCKS = 4


def _na_body(q_ref, kl_ref, kc_ref, vt_ref, bias_ref, g_ref, o_ref, s_scr, m_scr, *, seq, ctx_len, n_steps, n_total):
    g = pl.program_id(0)
    j_scored = jnp.minimum(g, n_total - 1) % n_steps
    j_finished = jnp.maximum(g - 1, 0) % n_steps
    n_rows = seq // GRID_W
    n_loc = NA_KEY_ROWS * GRID_W // NA_CHUNK
    cols = NA_STEP_ROWS * LANES
    lower = lax.broadcasted_iota(jnp.int32, (GRID_W, LANES), 1) < GRID_W

    @pl.when(g == 0)
    def _():
        s_scr[...] = jnp.zeros(s_scr.shape, F32)
        m_scr[...] = jnp.zeros(m_scr.shape, F32)

    def span_start(step):
        first_row = jnp.clip(NA_STEP_ROWS * step - WIN_R // 2, 0, n_rows - NA_KEY_ROWS)
        return first_row * GRID_W

    def step(slot, prev):
        subs = range(NA_STEP_BLOCKS)
        variant, k0, v0, q2 = [], [], [], []
        for sub in subs:
            blk = NA_STEP_BLOCKS * j_scored + sub
            blk_prev = NA_STEP_BLOCKS * j_finished + sub
            variant.append(jnp.where(blk == 0, 0, jnp.where(blk == NA_STEP_BLOCKS * n_steps - 1, 2, 1)))
            k0.append(span_start(blk))
            v0.append(span_start(blk_prev))
            q = q_ref[sub * TOKEN_TILE:(sub + 1) * TOKEN_TILE, :]
            zero = jnp.zeros((GRID_W, LANES), q.dtype)
            parts = []
            for rr in range(NA_STEP_ROWS):
                q_r = q[rr * GRID_W:(rr + 1) * GRID_W]
                parts += [jnp.where(lower, q_r, zero), jnp.where(lower, zero, q_r)]
            q2.append(jnp.concatenate(parts, axis=0))
        m_prev = [m_scr[prev, sub] for sub in subs]
        m = [None for _ in subs]
        l = [jnp.zeros((1, cols), F32) for _ in subs]
        acc = [jnp.zeros((LANES, cols), F32) for _ in subs]
        for c in range(n_loc + ctx_len // NA_CHUNK):
            rows = slice((c % n_loc) * NA_CHUNK, (c % n_loc + 1) * NA_CHUNK)
            for sub in subs:
                if c < n_loc:
                    k_at = pl.multiple_of(k0[sub] + c * NA_CHUNK, NA_CHUNK)
                    v_at = pl.multiple_of(v0[sub] + c * NA_CHUNK, NA_CHUNK)
                    s = _dot_nt(kl_ref[pl.ds(k_at, NA_CHUNK), :], q2[sub])
                    s = jnp.concatenate([s[:, rr * LANES:(rr + 1) * LANES] + bias_ref[0, variant[sub], rr, rows, :]
                                         for rr in range(NA_STEP_ROWS)], axis=1)
                    v_chunk = vt_ref[0, :, pl.ds(v_at, NA_CHUNK)]
                else:
                    s = _dot_nt(kc_ref[rows, :], q2[sub])
                    v_chunk = vt_ref[0, :, seq + rows.start:seq + rows.stop]
                s_scr[slot, sub, c] = s
                cm = jnp.max(s, axis=0, keepdims=True)
                m[sub] = cm if m[sub] is None else jnp.maximum(m[sub], cm)
                p = jnp.exp2(s_scr[prev, sub, c] - m_prev[sub])
                l[sub] = l[sub] + jnp.sum(p, axis=0, keepdims=True)
                acc[sub] = acc[sub] + _dot(v_chunk, p.astype(BF16))
        for sub in subs:
            m_scr[slot, sub] = m[sub]
            r_n = (acc[sub] / l[sub]).T
            outs = [jnp.where(lower, r_n[rr * LANES:rr * LANES + GRID_W], r_n[rr * LANES + GRID_W:(rr + 1) * LANES])
                    for rr in range(NA_STEP_ROWS)]
            toks = slice(sub * TOKEN_TILE, (sub + 1) * TOKEN_TILE)
            o_ref[toks, :] = (jnp.concatenate(outs, axis=0) * _silu(g_ref[toks, :].astype(F32))).astype(o_ref.dtype)

    for parity in (0, 1):
        pl.when(g % 2 == parity)(functools.partial(step, parity, 1 - parity))


def _na_attention(q, k, vt, bias, gate, batch, seq, ctx_len):
    pairs = q.shape[1] // LANES
    t_len = seq + ctx_len
    step_tokens = NA_STEP_BLOCKS * TOKEN_TILE
    n_steps = seq // step_tokens
    n_chunks = (NA_KEY_ROWS * GRID_W + ctx_len) // NA_CHUNK
    cols = NA_STEP_ROWS * LANES
    n_total = batch * pairs * n_steps

    def unravel(g):
        return g // (pairs * n_steps), (g // n_steps) % pairs, g % n_steps

    scored = lambda g: unravel(jnp.minimum(g, n_total - 1))
    finished = lambda g: unravel(jnp.maximum(g - 1, 0))

    def q_map(g):
        b, p, j = scored(g)
        return b * n_steps + j, p

    def k_map(g):
        b, p, _ = scored(g)
        return b, p

    def k_ctx_map(g):
        b, p, _ = scored(g)
        return batch * seq // ctx_len + b, p

    def v_map(g):
        b, p, _ = finished(g)
        return b, p, 0

    def done(g):
        b, p, j = finished(g)
        return b * n_steps + j, p

    body = functools.partial(_na_body, seq=seq, ctx_len=ctx_len, n_steps=n_steps, n_total=n_total)
    return pl.pallas_call(
        body, out_shape=jax.ShapeDtypeStruct((batch * seq, q.shape[1]), BF16),
        grid=(n_total + 1,),
        in_specs=[pl.BlockSpec((step_tokens, LANES), q_map),
                  pl.BlockSpec((seq, LANES), k_map),
                  pl.BlockSpec((ctx_len, LANES), k_ctx_map),
                  pl.BlockSpec((1, LANES, t_len), v_map),
                  pl.BlockSpec((1,) + bias.shape[1:], lambda g: (scored(g)[1], 0, 0, 0, 0)),
                  pl.BlockSpec((step_tokens, LANES), done)],
        out_specs=pl.BlockSpec((step_tokens, LANES), done),
        scratch_shapes=[pltpu.VMEM((2, NA_STEP_BLOCKS, n_chunks, NA_CHUNK, cols), F32),
                        pltpu.VMEM((2, NA_STEP_BLOCKS, 1, cols), F32)],
        compiler_params=_params(1), name="neighborhood_attention",
    )(q, k, k, vt, bias, gate)


def _seg_scans(jobs, use_max=False):
    vals = [v for v, _ in jobs]
    n = vals[0].shape[1]
    lane = lax.broadcasted_iota(jnp.int32, vals[0].shape, 1) % ML_CHUNK
    k = 1
    while k < ML_CHUNK:
        for i, (_, reverse) in enumerate(jobs):
            v = vals[i]
            if reverse:
                ok, shifted = lane < ML_CHUNK - k, pltpu.roll(v, n - k, 1)
            else:
                ok, shifted = lane >= k, pltpu.roll(v, k, 1)
            vals[i] = jnp.maximum(v, jnp.where(ok, shifted, NEG)) if use_max else v + jnp.where(ok, shifted, 0.0)
        k *= 2
    return vals


ML_GATE_ROWS = 40


def _mlstm_prep_body(u_ref, up_ref, un_ref, cw_ref, cb_ref, wqk_ref, wv_ref, wg_ref, bg_ref,
                     xc_ref, k_ref, qt_ref, vt_ref, pre_ref, *, nj, width, kscale):
    j = pl.program_id(1)
    u = u_ref[...]
    row = lax.broadcasted_iota(jnp.int32, u.shape, 0)
    prev = jnp.where((j > 0) & (j < nj), up_ref[7:8, :], 0.0)
    nxt = jnp.where(j < nj - 1, un_ref[0:1, :], 0.0)
    u_m1 = jnp.where(row == 0, prev, pltpu.roll(u, 1, 0))
    u_p1 = jnp.where(row == TOKEN_TILE - 1, nxt, pltpu.roll(u, TOKEN_TILE - 1, 0))
    cw = cw_ref[...]
    xc = _silu(u_m1 * cw[0:1] + u * cw[1:2] + u_p1 * cw[2:3] + cb_ref[...])
    xcb = xc.astype(BF16)
    xc_ref[...] = xcb
    qk = _dot(xcb, wqk_ref[...])
    v = _dot(u.astype(BF16), wv_ref[...])
    qb, kb, vb = qk[:, :width].astype(BF16), qk[:, width:].astype(BF16), v.astype(BF16)
    k_ref[...] = (qk[:, width:] * kscale).astype(BF16)
    qt_ref[0] = qk[:, :width].T.astype(BF16)
    vt_ref[0] = v.T.astype(BF16)
    pre_ref[0] = _dot_nt(wg_ref[0], qb) + _dot_nt(wg_ref[1], kb) + _dot_nt(wg_ref[2], vb) + bg_ref[...]


def _mlstm_gates_body(pre_ref, g_ref, gc_ref):
    pre = pre_ref[0]
    i8 = [pre[16 * d:16 * d + 8] for d in range(2)]
    f8 = [_log_sigmoid(pre[16 * d + 8:16 * d + 16]) for d in range(2)]
    b0, b0_rev, b1, b1_rev = _seg_scans([(f8[0], False), (f8[0], True), (f8[1], True), (f8[1], False)])
    b8, b_last = [b0, b1], [b0 + b0_rev - f8[0], b1 + b1_rev - f8[1]]
    r8 = [i8[d] - b8[d] for d in range(2)]
    c0, c0_rev, c1, c1_rev = _seg_scans([(r8[0], False), (r8[0], True), (r8[1], True), (r8[1], False)], use_max=True)
    c8, r_max = [c0, c1], [jnp.maximum(c0, c0_rev), jnp.maximum(c1, c1_rev)]
    for d in range(2):
        g_ref[0, d * ML_GATE_ROWS:(d + 1) * ML_GATE_ROWS] = jnp.concatenate(
            [-c8[d], b8[d] + c8[d], jnp.exp(r8[d] - r_max[d]), b_last[d], b_last[d] + r_max[d]], axis=0)
    pad = jnp.zeros((LANES - 16, pre.shape[1]), F32)
    gc_ref[0] = jnp.concatenate(r8 + [pad], axis=0).T


def _mlstm_gates(pre):
    batch, _, t_len = pre.shape
    return pl.pallas_call(
        _mlstm_gates_body,
        out_shape=[jax.ShapeDtypeStruct((batch, 2 * ML_GATE_ROWS, t_len), F32),
                   jax.ShapeDtypeStruct((batch, t_len, LANES), F32)],
        grid=(batch,),
        in_specs=[pl.BlockSpec((1,) + pre.shape[1:], lambda b: (b, 0, 0))],
        out_specs=[pl.BlockSpec((1, 2 * ML_GATE_ROWS, t_len), lambda b: (b, 0, 0)),
                   pl.BlockSpec((1, t_len, LANES), lambda b: (b, 0, 0))],
        compiler_params=_params(1), name="mlstm_gates",
    )(pre)


def _mlstm_prep(u, conv_w, conv_b, w_q, w_k, w_v, w_gate, b_gate, batch, seq, ctx_len):
    heads, hd = w_q.shape[0], w_q.shape[1]
    assert heads == ML_HEADS and hd == ML_CHUNK
    width = heads * hd
    nj = seq // TOKEN_TILE
    t_len = seq + ctx_len
    rows = u.shape[0]

    def block_diag(w):
        eye = jnp.eye(heads, dtype=w.dtype)
        return (eye[:, None, :, None] * w[:, :, None, :]).reshape(width, width)

    wqk = jnp.concatenate([block_diag(w_q), block_diag(w_k)], axis=1).astype(BF16)
    wv = block_diag(w_v).astype(BF16)
    wg = w_gate.reshape(2, heads, 3, hd, 2, heads).transpose(2, 0, 4, 5, 1, 3).reshape(3, 2, 2, heads, width)
    wg = jnp.pad(wg, ((0, 0), (0, 0), (0, 0), (0, 8 - heads), (0, 0))).reshape(3, 32, width).astype(BF16)
    bg = jnp.pad(b_gate.reshape(2, 2, heads), ((0, 0), (0, 0), (0, 8 - heads))).reshape(32, 1)
    n_halo = rows // 8
    tokb = lambda b, j: _tok_block(b, j, nj, batch)
    tile = pl.BlockSpec((TOKEN_TILE, width), lambda b, j: (tokb(b, j), 0))
    feat = pl.BlockSpec((1, width, TOKEN_TILE), lambda b, j: (b, 0, j))
    body = functools.partial(_mlstm_prep_body, nj=nj, width=width, kscale=hd ** -0.5)
    per_tile = TOKEN_TILE // 8
    return pl.pallas_call(
        body,
        out_shape=[jax.ShapeDtypeStruct((rows, width), BF16)] * 2
        + [jax.ShapeDtypeStruct((batch, width, t_len), BF16)] * 2
        + [jax.ShapeDtypeStruct((batch, 32, t_len), F32)],
        grid=(batch, nj + 1),
        in_specs=[tile,
                  pl.BlockSpec((8, width), lambda b, j: (jnp.maximum(tokb(b, j) * per_tile - 1, 0), 0)),
                  pl.BlockSpec((8, width), lambda b, j: (jnp.minimum((tokb(b, j) + 1) * per_tile, n_halo - 1), 0)),
                  _full((3, width)), _full((1, width)), _full(wqk.shape), _full(wv.shape),
                  _full(wg.shape), _full((32, 1))],
        out_specs=[tile, tile, feat, feat, pl.BlockSpec((1, 32, TOKEN_TILE), lambda b, j: (b, 0, j))],
        compiler_params=_params(2), name="mlstm_prep",
    )(u, u, u, conv_w, conv_b.reshape(1, width), wqk, wv, wg, bg)


def _mlstm_seq_body(kf, qtf, vtf, gf, gcf, kb, qtb, vtb, gb, gcb, hf_ref, hb_ref, c_s, n_s, m_s):
    t = pl.program_id(1)
    L = ML_CHUNK
    subs = TOKEN_TILE // L

    @pl.when(t == 0)
    def _():
        c_s[...] = jnp.zeros(c_s.shape, F32)
        n_s[...] = jnp.zeros(n_s.shape, F32)
        m_s[...] = jnp.zeros(m_s.shape, F32)

    si = lax.broadcasted_iota(jnp.int32, (L, L), 0)
    li = lax.broadcasted_iota(jnp.int32, (L, L), 1)
    streams = ((kf, qtf, vtf, gf, gcf, hf_ref, si <= li), (kb, qtb, vtb, gb, gcb, hb_ref, si >= li))
    units = {}
    for d, (k_ref, qt_ref, vt_ref, g_ref, gc_ref, h_ref, incl) in enumerate(streams):
        for sub in range(subs):
            toks = slice(sub * L, (sub + 1) * L)
            g = g_ref[0, :, toks]
            gc = gc_ref[0, toks, :]
            for h in range(ML_HEADS):
                cols = slice(h * L, (h + 1) * L)
                k, qt, vt = k_ref[toks, cols], qt_ref[0, cols, toks], vt_ref[0, cols, toks]
                neg_c, m_loc, w0, b_last, g_max = (g[8 * i + h:8 * i + h + 1] for i in range(5))
                r_col = gc[:, 8 * d + h:8 * d + h + 1]
                p0 = jnp.where(incl, jnp.exp(r_col + neg_c), 0.0) * _dot(k, qt)
                s_sum = jnp.sum(p0, axis=0, keepdims=True)
                intra = _dot(vt, p0.astype(BF16))
                c_inc = _dot((vt.astype(F32) * w0).astype(BF16), k)
                n_inc = _dot(jnp.broadcast_to(w0, (8, L)).astype(BF16), k)
                units[d, sub, h] = (h_ref, cols, toks, qt, neg_c, m_loc, b_last, g_max, s_sum, intra, c_inc, n_inc)
    for stage in range(subs):
        for d in range(2):
            sub = stage if d == 0 else subs - 1 - stage
            for h in range(ML_HEADS):
                idx = d * ML_HEADS + h
                h_ref, cols, toks, qt, neg_c, m_loc, b_last, g_max, s_sum, intra, c_inc, n_inc = units[d, sub, h]
                c_st, n_st, m_st = c_s[idx], n_s[idx], m_s[idx]
                cn = _dot(jnp.concatenate([c_st, n_st], axis=0).astype(BF16), qt)
                delta = jnp.maximum(m_st + neg_c, 0.0)
                e_intra = jnp.exp(-delta)
                w_inter = jnp.exp(m_st + neg_c - delta)
                num = w_inter * cn[:L] + e_intra * intra
                den = w_inter * cn[L:L + 1] + e_intra * s_sum
                h_out = num / jnp.maximum(jnp.abs(den), jnp.exp(-(m_loc + delta)))
                h_ref[0, cols, toks] = h_out.astype(h_ref.dtype)
                m_new = jnp.maximum(b_last + m_st, g_max)
                decay = jnp.exp(b_last + m_st - m_new)
                gain = jnp.exp(g_max - m_new)
                c_s[idx] = decay * c_st + gain * c_inc
                n_s[idx] = decay * n_st + gain * n_inc
                m_s[idx] = m_new


def _mlstm_seq(k, qt, vt, gates, gcols, batch, seq, ctx_len):
    width = k.shape[1]
    L = ML_CHUNK
    assert ctx_len == TOKEN_TILE
    nj = seq // TOKEN_TILE
    fwd = lambda t: (t + nj) % (nj + 1)
    bwd = lambda t: nj - t

    def stream(tile_of, d):
        feat = pl.BlockSpec((1, width, TOKEN_TILE), lambda b, t: (b, 0, tile_of(t)))
        return [pl.BlockSpec((TOKEN_TILE, width), lambda b, t: (_tok_block(b, tile_of(t), nj, batch), 0)), feat, feat,
                pl.BlockSpec((1, ML_GATE_ROWS, TOKEN_TILE), lambda b, t: (b, d, tile_of(t))),
                pl.BlockSpec((1, TOKEN_TILE, LANES), lambda b, t: (b, tile_of(t), 0))]

    out_f = pl.BlockSpec((1, width, TOKEN_TILE), lambda b, t: (b, 0, fwd(t)))
    out_b = pl.BlockSpec((1, width, TOKEN_TILE), lambda b, t: (b, 0, bwd(t)))
    n_state = 2 * ML_HEADS
    return pl.pallas_call(
        _mlstm_seq_body,
        out_shape=[jax.ShapeDtypeStruct(qt.shape, BF16)] * 2,
        grid=(batch, nj + 1),
        in_specs=stream(fwd, 0) + stream(bwd, 1),
        out_specs=[out_f, out_b],
        scratch_shapes=[pltpu.VMEM((n_state, L, L), F32), pltpu.VMEM((n_state, 8, L), F32),
                        pltpu.VMEM((n_state, 1, L), F32)],
        compiler_params=_params(2), name="mlstm_recurrence",
    )(k, qt, vt, gates, gcols, k, qt, vt, gates, gcols)


def kernel(x, c, ctx, c_ctx, w_mod, b_mod, g_norm, ab_w_in, ab_w_out, mla_g_q, mla_w_uq, mla_g_kv, mla_w_ukv,
           ml_conv_w, ml_conv_b, ml_w_q, ml_w_k, ml_w_v, ml_w_gate, ml_b_gate, ml_g_head, ml_skip,
           cd_w_in, cd_w_out, na_rpb, gqa_g_q, gqa_g_k, g_final):
    batch, seq, d = x.shape
    ctx_len = ctx.shape[1]
    assert ctx_len == TOKEN_TILE and seq % KEY_CHUNK == 0 and seq // GRID_W >= NA_KEY_ROWS
    dims = (batch, seq, ctx_len)

    mla_heads, mla_rope, mla_v = 8, 32, 64
    mla_nope = mla_w_uq.shape[2] // mla_heads - mla_rope
    q_lora, kv_lora = mla_g_q.shape[1], mla_g_kv.shape[1]
    ml_width = ml_conv_w.shape[2]
    mla_width = mla_heads * mla_v
    gqa_heads, gqa_dim = 8, gqa_g_q.shape[1]
    gqa_kv = (cd_w_in.shape[2] - 4 * 512 - 2 * gqa_heads * gqa_dim) // (2 * gqa_dim)
    na_width = na_rpb.shape[1] * 64

    mod_rows = -(-(batch + 1) // 8) * 8
    cvec = jnp.concatenate([c, c_ctx[None], jnp.zeros((mod_rows - batch - 1, d), F32)], axis=0)
    mod = _modulation(cvec, w_mod, b_mod)
    mod0 = mod[0].reshape(mod_rows, 1, 3 * d)
    mod1 = mod[1].reshape(mod_rows, 1, 3 * d)

    tok0 = _Tokens((x.reshape(batch * seq, d), ctx.reshape(batch * ctx_len, d)), *dims)
    w_in = ab_w_in[0]
    s1 = q_lora + kv_lora
    zcol = lambda n: jnp.zeros((d, n), w_in.dtype)
    w0 = jnp.concatenate([w_in[:, :s1], zcol(mla_nope), w_in[:, s1:s1 + mla_rope],
                          zcol(LANES - mla_nope - mla_rope), w_in[:, s1 + mla_rope:]], axis=1).astype(BF16)
    o_pa = s1 + LANES
    outs0 = ((0, o_pa, F32, 1.0, False), (o_pa, mla_width, BF16, 1.0, False),
             (o_pa + mla_width, ml_width, F32, 1.0, False), (o_pa + mla_width + ml_width, ml_width, BF16, 1.0, False))
    pa, gate_a, u, z = _in_proj(tok0, mod0, g_norm[0], w0, outs0, *dims)

    def mla_lanes(table, fill):
        n = table.shape[0]
        return jnp.concatenate([jnp.full((n, mla_nope), fill, F32), table,
                                jnp.full((n, LANES - mla_nope - mla_rope), fill, F32)], axis=-1)

    cos_a, sin_a = _rope_tables(seq, ctx_len, mla_rope)
    cos_a, sin_a = mla_lanes(cos_a, 1.0), mla_lanes(sin_a, 0.0)
    q_a, k_a, vt_a = _mla_prep(pa, cos_a, sin_a, mla_g_q[0], mla_g_kv[0], mla_w_uq[0], mla_w_ukv[0],
                               *dims, mla_heads, mla_nope, mla_rope, mla_v)
    mix_a = _flash(q_a, k_a, vt_a, gate_a, *dims, k_heads_per_pair=2, v_rows_per_pair=2 * mla_v,
                   pairs_per_kv=1, ctx_queries=True)

    xc, k_m, qt_m, vt_m, gate_pre = _mlstm_prep(u, ml_conv_w[0], ml_conv_b[0], ml_w_q[0], ml_w_k[0], ml_w_v[0],
                                                ml_w_gate[0], ml_b_gate[0], *dims)
    gates, gcols = _mlstm_gates(gate_pre)
    h_f, h_b = _mlstm_seq(k_m, qt_m, vt_m, gates, gcols, *dims)
    x1 = _out_proj(tok0, _Tokens(mix_a, *dims), (h_f, h_b, xc, z, ml_g_head[0], ml_skip[0]),
                   ab_w_out[0], mod0, batch, seq)

    tok1 = _Tokens((x1,), *dims)
    w1 = cd_w_in[0].astype(BF16)
    gq_w, gkv_w = gqa_heads * gqa_dim, gqa_kv * gqa_dim
    o_d = 4 * na_width
    outs1 = ((0, na_width, BF16, 64 ** -0.5 * LOG2E, False), (na_width, na_width, BF16, 1.0, False),
             (2 * na_width, na_width, BF16, 1.0, True), (3 * na_width, na_width, BF16, 1.0, False),
             (o_d, gq_w + gkv_w, F32, 1.0, True), (o_d + gq_w + gkv_w, gkv_w, BF16, 1.0, True),
             (o_d + gq_w + 2 * gkv_w, gq_w, BF16, 1.0, False))
    q_c, k_c, vt_c, gate_c, pd_t, vt_d, gate_d = _in_proj(tok1, mod1, g_norm[1], w1, outs1, *dims)

    mix_c = _na_attention(q_c, k_c, vt_c, _na_bias(na_rpb[0]), gate_c, *dims)

    cos_d, sin_d = _rope_tables(seq, ctx_len, gqa_dim)
    q_d, k_d = _gqa_prep(pd_t, cos_d, sin_d, gqa_g_q[0], gqa_g_k[0], *dims, gqa_heads, gqa_kv, gqa_dim)
    mix_d = _flash(q_d, k_d, vt_d, gate_d, *dims, k_heads_per_pair=1, v_rows_per_pair=gqa_dim,
                   pairs_per_kv=gqa_heads // (2 * gqa_kv), ctx_queries=False)

    out = _out_proj(tok1, _Tokens((mix_c,), *dims), mix_d[0], cd_w_out[0], mod1, batch, seq, g_final=g_final)
    return out.reshape(batch, seq, d)
```

```python
import functools

import jax
import jax.numpy as jnp
from jax import lax
from jax.experimental import pallas as pl
from jax.experimental.pallas import tpu as pltpu

F32 = jnp.float32
BF16 = jnp.bfloat16

LANES = 128
TOKEN_TILE = 256
KEY_CHUNK = 256
FLASH_TILE = 256
FLASH_STEP_TILES = 2
GRID_W = 64
WIN_R = 8
WIN_C = 16
ML_CHUNK = 128
ML_HEADS = 4
EPS = 1e-6
ROPE_BASE = 10000.0
LOG2E = 1.4426950408889634
NEG = -1e30
VMEM_LIMIT = 56 * 1024 * 1024

_NT = (((1,), (1,)), ((), ()))


def _dot(a, b):
    return jnp.dot(a, b, preferred_element_type=F32)


def _dot_nt(a, b):
    return lax.dot_general(a, b, _NT, preferred_element_type=F32)


def _silu(v):
    return v * (1.0 / (1.0 + jnp.exp(-v)))


def _log_sigmoid(v):
    return -(jnp.maximum(-v, 0.0) + jnp.log1p(jnp.exp(-jnp.abs(v))))


def _params(n_axes):
    return pltpu.CompilerParams(dimension_semantics=("arbitrary",) * n_axes, vmem_limit_bytes=VMEM_LIMIT)


def _full(shape):
    nd = len(shape)
    return pl.BlockSpec(shape, lambda *_: (0,) * nd)


def _mod_body(c_ref, w_ref, b_ref, o_ref):
    s = _silu(c_ref[...])
    o_ref[0] = _dot(s.astype(BF16), w_ref[0].astype(BF16)) + b_ref[0]


def _modulation(cvec, w_mod, b_mod):
    depth, d, n = w_mod.shape
    rows = cvec.shape[0]
    tn = n // 4
    return pl.pallas_call(
        _mod_body,
        out_shape=jax.ShapeDtypeStruct((depth, rows, n), F32),
        grid=(depth, n // tn),
        in_specs=[_full((rows, d)),
                  pl.BlockSpec((1, d, tn), lambda l, j: (l, 0, j)),
                  pl.BlockSpec((1, 1, tn), lambda l, j: (l, 0, j))],
        out_specs=pl.BlockSpec((1, rows, tn), lambda l, j: (l, 0, j)),
        compiler_params=_params(2), name="modulation",
    )(cvec, w_mod, b_mod.reshape(depth, 1, n))


class _Tokens:
    def __init__(self, arrays, batch, seq, ctx_len):
        self.arrays = arrays
        self.split = len(arrays) == 2
        self.n_lat = batch * seq // TOKEN_TILE
        self.n_ctx = batch * ctx_len // TOKEN_TILE
        self.d = arrays[0].shape[-1]

    def specs(self):
        blk = (TOKEN_TILE, self.d)
        if not self.split:
            return [pl.BlockSpec(blk, lambda i: (i, 0))]
        n_lat = self.n_lat
        return [pl.BlockSpec(blk, lambda i: (jnp.minimum(i, n_lat - 1), 0)),
                pl.BlockSpec(blk, lambda i: (jnp.maximum(i - n_lat, 0), 0))]

    def load(self, refs, i):
        if not self.split:
            return refs[0][...]
        return jnp.where(i < self.n_lat, refs[0][...], refs[1][...])


def _mod_spec(n_lat, nj, batch, width):
    return pl.BlockSpec((1, 1, width), lambda i: (jnp.where(i < n_lat, i // nj, batch), 0, 0))


def _in_proj_body(*refs, tok, outs, d):
    n_tok = len(tok.arrays)
    mod_ref, g_ref, w_ref = refs[n_tok:n_tok + 3]
    o_refs = refs[n_tok + 3:]
    i = pl.program_id(0)
    x = tok.load(refs[:n_tok], i)
    y = x * lax.rsqrt(jnp.mean(x * x, axis=-1, keepdims=True) + EPS) * g_ref[...]
    mod = mod_ref[0]
    h = y * (1.0 + mod[:, d:2 * d]) + mod[:, :d]
    acc = _dot(h.astype(BF16), w_ref[...])
    for o_ref, (c0, width, _, scale, transposed) in zip(o_refs, outs):
        v = acc[:, c0:c0 + width]
        if scale != 1.0:
            v = v * scale
        if transposed:
            o_ref[0] = v.T.astype(o_ref.dtype)
        else:
            o_ref[...] = v.astype(o_ref.dtype)


def _in_proj(tok, mod_l, g, w, outs, batch, seq, ctx_len):
    d = tok.d
    nj = seq // TOKEN_TILE
    n_lat, n_all = tok.n_lat, tok.n_lat + tok.n_ctx
    rows = n_all * TOKEN_TILE
    t_len = seq + ctx_len
    out_shape, out_specs = [], []
    for (_, width, dtype, _, transposed) in outs:
        if transposed:
            out_shape.append(jax.ShapeDtypeStruct((batch, width, t_len), dtype))
            out_specs.append(pl.BlockSpec(
                (1, width, TOKEN_TILE),
                lambda i: (jnp.where(i < n_lat, i // nj, i - n_lat), 0, jnp.where(i < n_lat, i % nj, nj))))
        else:
            out_shape.append(jax.ShapeDtypeStruct((rows, width), dtype))
            out_specs.append(pl.BlockSpec((TOKEN_TILE, width), lambda i: (i, 0)))
    body = functools.partial(_in_proj_body, tok=tok, outs=outs, d=d)
    return pl.pallas_call(
        body, out_shape=out_shape, grid=(n_all,),
        in_specs=tok.specs() + [_mod_spec(n_lat, nj, batch, 3 * d), _full((1, d)), _full(w.shape)],
        out_specs=out_specs, compiler_params=_params(1), name="in_proj",
    )(*tok.arrays, mod_l, g.reshape(1, d), w)


def _mlstm_mix(hf_ref, hb_ref, xc_ref, z_ref, gh_ref, sk_ref):
    ht = hf_ref[0].astype(F32) + hb_ref[0].astype(F32)
    L = ML_CHUNK
    normed = []
    for hd in range(ML_HEADS):
        hh = ht[hd * L:(hd + 1) * L]
        mu = jnp.mean(hh, axis=0, keepdims=True)
        var = jnp.mean(jnp.square(hh - mu), axis=0, keepdims=True)
        normed.append((hh - mu) * lax.rsqrt(var + EPS))
    hn = jnp.concatenate(normed, axis=0).T * gh_ref[...]
    return ((hn + sk_ref[...] * xc_ref[...].astype(F32)) * _silu(z_ref[...].astype(F32))).astype(BF16)


def _out_proj_body(*refs, tok, mix_a, n_b, d, final):
    n_tok, n_a = len(tok.arrays), len(mix_a.arrays)
    b_refs = refs[n_tok + n_a:n_tok + n_a + n_b]
    wa_ref, wb_ref, mod_ref = refs[n_tok + n_a + n_b:n_tok + n_a + n_b + 3]
    rest = refs[n_tok + n_a + n_b + 3:]
    i = pl.program_id(0)
    x = tok.load(refs[:n_tok], i)
    mb = b_refs[0][...] if n_b == 1 else _mlstm_mix(*b_refs)
    acc = _dot(mix_a.load(refs[n_tok:n_tok + n_a], i), wa_ref[...]) + _dot(mb, wb_ref[...])
    xn = x + mod_ref[0][:, 2 * d:] * acc
    if final:
        gf_ref, o_ref = rest
        xn = xn * lax.rsqrt(jnp.mean(xn * xn, axis=-1, keepdims=True) + EPS) * gf_ref[...]
    else:
        (o_ref,) = rest
    o_ref[...] = xn


def _out_proj(tok, mix_a, mix_b, w_out, mod_l, batch, seq, g_final=None):
    d = tok.d
    half = mix_a.d
    nj = seq // TOKEN_TILE
    n_lat = tok.n_lat
    final = g_final is not None
    n_tiles = n_lat if final else n_lat + tok.n_ctx
    wa, wb = w_out[:half].astype(BF16), w_out[half:].astype(BF16)
    tile = lambda width: pl.BlockSpec((TOKEN_TILE, width), lambda i: (i, 0))
    if isinstance(mix_b, tuple):
        hf, hb, xc, z, g_head, skip = mix_b
        width = xc.shape[1]
        feat = pl.BlockSpec((1, width, TOKEN_TILE),
                            lambda i: (jnp.where(i < n_lat, i // nj, i - n_lat), 0, jnp.where(i < n_lat, i % nj, nj)))
        b_specs = [feat, feat, tile(width), tile(width), _full((1, width)), _full((1, width))]
        b_args = [hf, hb, xc, z, g_head.reshape(1, width), skip.reshape(1, width)]
    else:
        b_specs, b_args = [tile(half)], [mix_b]
    in_specs = tok.specs() + mix_a.specs() + b_specs + [_full(wa.shape), _full(wb.shape),
                                                        _mod_spec(n_lat, nj, batch, 3 * d)]
    args = list(tok.arrays) + list(mix_a.arrays) + b_args + [wa, wb, mod_l]
    if final:
        in_specs.append(_full((1, d)))
        args.append(g_final.reshape(1, d))
    body = functools.partial(_out_proj_body, tok=tok, mix_a=mix_a, n_b=len(b_args), d=d, final=final)
    return pl.pallas_call(
        body, out_shape=jax.ShapeDtypeStruct((n_tiles * TOKEN_TILE, d), F32), grid=(n_tiles,),
        in_specs=in_specs, out_specs=tile(d), compiler_params=_params(1), name="out_proj",
    )(*args)


def _rope_tables(seq, ctx_len, rot_dim):
    t = jnp.arange(seq)
    pos = jnp.stack([t // GRID_W, t % GRID_W], axis=-1).astype(F32)
    n_freq = rot_dim // 4
    inv = ROPE_BASE ** (-jnp.arange(n_freq, dtype=F32) / n_freq)
    ang = pos[:, :, None] * inv
    cos, sin = jnp.cos(ang), jnp.sin(ang)
    cos_t = jnp.concatenate([cos[:, 0], cos[:, 0], cos[:, 1], cos[:, 1]], axis=-1)
    sin_t = jnp.concatenate([-sin[:, 0], sin[:, 0], -sin[:, 1], sin[:, 1]], axis=-1)
    cos_t = jnp.concatenate([cos_t, jnp.ones((ctx_len, rot_dim), F32)], axis=0)
    sin_t = jnp.concatenate([sin_t, jnp.zeros((ctx_len, rot_dim), F32)], axis=0)
    return cos_t, sin_t


def _rope(x, cos, sin, dist):
    lane = lax.broadcasted_iota(jnp.int32, x.shape, 1)
    first = (lane % (2 * dist)) < dist
    partner = jnp.where(first, pltpu.roll(x, LANES - dist, 1), pltpu.roll(x, dist, 1))
    return x * cos + partner * sin


def _rope_rows(x, cos, sin, dist):
    n = x.shape[0] // dist
    partner = jnp.concatenate([x[(i ^ 1) * dist:((i ^ 1) + 1) * dist] for i in range(n)], axis=0)
    return x * cos + partner * sin


def _tok_block(b, j, nj, batch):
    return jnp.where(j < nj, b * nj + j, batch * nj + b)


def _mla_prep_body(pa_ref, cos_ref, sin_ref, cost_ref, sint_ref, gq_ref, gkv_ref, wuqt_ref, wuk_ref, wuvt_ref,
                   q_ref, k_ref, vt_ref, *, heads, q_lora, kv_lora, qscale):
    pa = pa_ref[...]

    def norm(v, g_ref):
        return v * lax.rsqrt(jnp.mean(v * v, axis=-1, keepdims=True) + EPS) * g_ref[...]

    cq_t = norm(pa[:, :q_lora], gq_ref).T.astype(BF16)
    q_all = _dot(wuqt_ref[...], cq_t)
    cos_t, sin_t = cost_ref[...], sint_ref[...]
    for h in range(heads):
        qh = _rope_rows(q_all[h * LANES:(h + 1) * LANES], cos_t, sin_t, 8)
        q_ref[0, h] = (qh * qscale).astype(BF16)
    ckv = norm(pa[:, q_lora:q_lora + kv_lora], gkv_ref)
    k_nope = _dot(ckv.astype(BF16), wuk_ref[...])
    k_rope = _rope(pa[:, q_lora + kv_lora:], cos_ref[...], sin_ref[...], 8)
    for h in range(heads):
        k_ref[0, h] = (k_nope[:, h * LANES:(h + 1) * LANES] + k_rope).astype(BF16)
    vt_ref[0] = _dot(wuvt_ref[...], ckv.T.astype(BF16)).astype(BF16)


def _mla_prep(pa, cos, sin, g_q, g_kv, w_uq, w_ukv, batch, seq, ctx_len, heads, nope, rope, v_dim):
    q_lora, kv_lora = g_q.shape[0], g_kv.shape[0]
    nj = seq // TOKEN_TILE
    t_len = seq + ctx_len
    pad = LANES - nope - rope
    wuq = jnp.pad(w_uq.reshape(q_lora, heads, nope + rope), ((0, 0), (0, 0), (0, pad)))
    wuq_t = wuq.reshape(q_lora, heads * LANES).T.astype(BF16)
    wkv = w_ukv.reshape(kv_lora, heads, nope + v_dim)
    wuk = jnp.pad(wkv[..., :nope], ((0, 0), (0, 0), (0, LANES - nope))).reshape(kv_lora, heads * LANES).astype(BF16)
    wuv_t = wkv[..., nope:].reshape(kv_lora, heads * v_dim).T.astype(BF16)
    body = functools.partial(_mla_prep_body, heads=heads, q_lora=q_lora, kv_lora=kv_lora,
                             qscale=(nope + rope) ** -0.5 * LOG2E)
    head_major = pl.BlockSpec((1, heads, TOKEN_TILE, LANES), lambda j, b: (b, 0, j, 0))
    q_feature_major = pl.BlockSpec((1, heads, LANES, TOKEN_TILE), lambda j, b: (b, 0, 0, j))
    tok_table = pl.BlockSpec((TOKEN_TILE, LANES), lambda j, b: (j, 0))
    feat_table = pl.BlockSpec((LANES, TOKEN_TILE), lambda j, b: (0, j))
    return pl.pallas_call(
        body,
        out_shape=[jax.ShapeDtypeStruct((batch, heads, LANES, t_len), BF16),
                   jax.ShapeDtypeStruct((batch, heads, t_len, LANES), BF16),
                   jax.ShapeDtypeStruct((batch, heads * v_dim, t_len), BF16)],
        grid=(nj + 1, batch),
        in_specs=[pl.BlockSpec((TOKEN_TILE, pa.shape[1]), lambda j, b: (_tok_block(b, j, nj, batch), 0)),
                  tok_table, tok_table, feat_table, feat_table,
                  _full((1, q_lora)), _full((1, kv_lora)), _full(wuq_t.shape), _full(wuk.shape), _full(wuv_t.shape)],
        out_specs=[q_feature_major, head_major,
                   pl.BlockSpec((1, heads * v_dim, TOKEN_TILE), lambda j, b: (b, 0, j))],
        compiler_params=_params(2), name="mla_prep",
    )(pa, cos, sin, cos.T, sin.T, g_q.reshape(1, -1), g_kv.reshape(1, -1), wuq_t, wuk, wuv_t)


def _gqa_prep_body(pd_ref, cos_ref, sin_ref, gq_ref, gk_ref, q_ref, k_ref, *, heads, kv_heads, dim, qscale):
    cos, sin = cos_ref[...], sin_ref[...]
    n_b = pd_ref.shape[0]

    def head(s, first_row, g_ref):
        x = pd_ref[s, first_row:first_row + dim, :]
        y = x * lax.rsqrt(jnp.mean(x * x, axis=0, keepdims=True) + EPS) * g_ref[...]
        return _rope_rows(y, cos, sin, dim // 4)

    zeros = jnp.zeros((dim, TOKEN_TILE), F32)
    for h in range(heads):
        for s in range(n_b):
            parts = [zeros] * kv_heads
            parts[h // (heads // kv_heads)] = head(s, h * dim, gq_ref) * qscale
            q_ref[s, h] = jnp.concatenate(parts, axis=0).astype(BF16)
    for s in range(n_b):
        keys = [head(s, (heads + kv) * dim, gk_ref) for kv in range(kv_heads)]
        k_ref[s, 0] = jnp.concatenate(keys, axis=0).T.astype(BF16)


def _gqa_prep(pd_t, cos, sin, g_q, g_k, batch, seq, ctx_len, heads, kv_heads, head_dim):
    assert kv_heads * head_dim == LANES
    nj = seq // TOKEN_TILE
    t_len = seq + ctx_len
    n_b = 2 if batch % 2 == 0 else 1
    body = functools.partial(_gqa_prep_body, heads=heads, kv_heads=kv_heads, dim=head_dim,
                             qscale=head_dim ** -0.5 * LOG2E)
    table = pl.BlockSpec((head_dim, TOKEN_TILE), lambda j, b: (0, j))
    gain = lambda g: jnp.broadcast_to(g[:, None], (head_dim, TOKEN_TILE))
    return pl.pallas_call(
        body,
        out_shape=[jax.ShapeDtypeStruct((batch, heads, LANES, t_len), BF16),
                   jax.ShapeDtypeStruct((batch, 1, t_len, LANES), BF16)],
        grid=(nj + 1, batch // n_b),
        in_specs=[pl.BlockSpec((n_b, pd_t.shape[1], TOKEN_TILE), lambda j, b: (b, 0, j)), table, table,
                  _full((head_dim, TOKEN_TILE)), _full((head_dim, TOKEN_TILE))],
        out_specs=[pl.BlockSpec((n_b, heads, LANES, TOKEN_TILE), lambda j, b: (b, 0, 0, j)),
                   pl.BlockSpec((n_b, 1, TOKEN_TILE, LANES), lambda j, b: (b, 0, j, 0))],
        compiler_params=_params(2), name="gqa_prep",
    )(pd_t, cos.T, sin.T, gain(g_q), gain(g_k))


def _score_pass(q, k_ref, k_head, chunks, s_ref):
    m = None
    for (st, sz) in chunks:
        s_t = _dot(k_ref[0, k_head, st:st + sz, :], q)
        s_ref[st:st + sz, :] = s_t
        cm = jnp.max(s_t, axis=0, keepdims=True)
        m = cm if m is None else jnp.maximum(m, cm)
    return m


def _value_pass(s_ref, m, vt_ref, v_rows, chunks):
    l = jnp.zeros(m.shape, F32)
    acc = jnp.zeros((v_rows.stop - v_rows.start, m.shape[1]), F32)
    for (st, sz) in chunks:
        p = jnp.exp2(s_ref[st:st + sz, :] - m)
        l = l + jnp.sum(p, axis=0, keepdims=True)
        acc = acc + _dot(vt_ref[0, v_rows, st:st + sz], p.astype(BF16))
    return acc / l


def _gated_store(outs, g_ref, o_ref):
    o2 = jnp.concatenate(outs, axis=0).T
    o_ref[...] = (o2 * _silu(g_ref[...].astype(F32))).astype(o_ref.dtype)


def _flash_body(q_ref, k_ref, vt_ref, g_ref, o_ref, s_scr, m_scr, *, k_sel, v_off, v_dim, chunks):
    j = pl.program_id(0)

    @pl.when(j == 0)
    def _():
        s_scr[...] = jnp.zeros(s_scr.shape, F32)
        m_scr[...] = jnp.zeros(m_scr.shape, F32)

    def step(slot, prev):
        n_sub = q_ref.shape[3] // FLASH_TILE
        units = [(sub, a) for sub in range(n_sub) for a in (0, 1)]
        q = [q_ref[0, a, :, sub * FLASH_TILE:(sub + 1) * FLASH_TILE] for sub, a in units]
        m_prev = [m_scr[prev, u] for u in range(len(units))]
        m = [None] * len(units)
        l = [jnp.zeros((1, FLASH_TILE), F32) for _ in units]
        acc = [jnp.zeros((v_dim, FLASH_TILE), F32) for _ in units]
        for (st, sz) in chunks:
            for u, (sub, a) in enumerate(units):
                s_t = _dot(k_ref[0, k_sel[a], st:st + sz, :], q[u])
                s_scr[slot, u, st:st + sz, :] = s_t
                cm = jnp.max(s_t, axis=0, keepdims=True)
                m[u] = cm if m[u] is None else jnp.maximum(m[u], cm)
                p = jnp.exp2(s_scr[prev, u, st:st + sz, :] - m_prev[u])
                l[u] = l[u] + jnp.sum(p, axis=0, keepdims=True)
                acc[u] = acc[u] + _dot(vt_ref[0, v_off[a]:v_off[a] + v_dim, st:st + sz], p.astype(BF16))
        for u in range(len(units)):
            m_scr[slot, u] = m[u]
        for sub in range(n_sub):
            toks = slice(sub * FLASH_TILE, (sub + 1) * FLASH_TILE)
            _gated_store([acc[2 * sub + a] / l[2 * sub + a] for a in (0, 1)], g_ref.at[toks, :], o_ref.at[toks, :])

    for parity in (0, 1):
        pl.when(j % 2 == parity)(functools.partial(step, parity, 1 - parity))


def _flash_ctx_body(q_ref, k_ref, vt_ref, g_ref, o_ref, s_scr, *, k_sel, v_off, v_dim, chunks):
    outs = []
    for a in range(2):
        m = _score_pass(q_ref[0, a], k_ref, k_sel[a], chunks, s_scr.at[a])
        outs.append(_value_pass(s_scr.at[a], m, vt_ref, slice(v_off[a], v_off[a] + v_dim), chunks))
    _gated_store(outs, g_ref, o_ref)


def _flash(q, k, vt, gate, batch, seq, ctx_len, k_heads_per_pair, v_rows_per_pair, pairs_per_kv, ctx_queries):
    heads = q.shape[1]
    t_len = seq + ctx_len
    nj = seq // TOKEN_TILE
    v_dim = LANES // 2
    k_sel = (0, 1) if k_heads_per_pair == 2 else (0, 0)
    v_off = (0, v_dim) if v_rows_per_pair == 2 * v_dim else (0, 0)
    k_blocks = k.shape[1] // k_heads_per_pair
    k_block = lambda p: (p // pairs_per_kv) % k_blocks
    chunks = tuple((c * KEY_CHUNK, KEY_CHUNK) for c in range(t_len // KEY_CHUNK))
    static = dict(k_sel=k_sel, v_off=v_off, v_dim=v_dim)

    width = gate.shape[1]
    tq = FLASH_TILE * FLASH_STEP_TILES
    n_q = seq // tq
    n_units = 2 * FLASH_STEP_TILES
    pairs = heads // 2
    n_tiles = batch * pairs * n_q

    def unravel(g):
        return g // (pairs * n_q), (g // n_q) % pairs, g % n_q

    scored = lambda g: unravel(jnp.minimum(g, n_tiles - 1))
    finished = lambda g: unravel(jnp.maximum(g - 1, 0))

    def q_map(g):
        b, p, j = scored(g)
        return b, p, 0, j

    def k_map(g):
        b, p, _ = scored(g)
        return b, k_block(p), 0, 0

    def v_map(g):
        b, p, _ = finished(g)
        return b, p // pairs_per_kv, 0

    def done(g):
        b, p, j = finished(g)
        return b * n_q + j, p

    out = pl.pallas_call(
        functools.partial(_flash_body, chunks=chunks, **static),
        out_shape=jax.ShapeDtypeStruct((batch * seq, width), BF16),
        grid=(n_tiles + 1,),
        in_specs=[pl.BlockSpec((1, 2, LANES, tq), q_map),
                  pl.BlockSpec((1, k_heads_per_pair, t_len, LANES), k_map),
                  pl.BlockSpec((1, v_rows_per_pair, t_len), v_map),
                  pl.BlockSpec((tq, LANES), done)],
        out_specs=pl.BlockSpec((tq, LANES), done),
        scratch_shapes=[pltpu.VMEM((2, n_units, t_len, FLASH_TILE), F32),
                        pltpu.VMEM((2, n_units, 1, FLASH_TILE), F32)],
        compiler_params=_params(1), name="flash_attention",
    )(q, k, vt, gate)
    if not ctx_queries:
        return (out,)

    out_ctx = pl.pallas_call(
        functools.partial(_flash_ctx_body, chunks=((0, ctx_len),), **static),
        out_shape=jax.ShapeDtypeStruct((batch * ctx_len, width), BF16),
        grid=(batch, heads // 2),
        in_specs=[pl.BlockSpec((1, 2, LANES, TOKEN_TILE), lambda b, p: (b, p, 0, nj)),
                  pl.BlockSpec((1, k_heads_per_pair, ctx_len, LANES), lambda b, p: (b, k_block(p), seq // ctx_len, 0)),
                  pl.BlockSpec((1, v_rows_per_pair, ctx_len), lambda b, p: (b, p // pairs_per_kv, seq // ctx_len)),
                  pl.BlockSpec((TOKEN_TILE, LANES), lambda b, p: (batch * nj + b, p))],
        out_specs=pl.BlockSpec((TOKEN_TILE, LANES), lambda b, p: (b, p)),
        scratch_shapes=[pltpu.VMEM((2, ctx_len, TOKEN_TILE), F32)],
        compiler_params=_params(2), name="flash_attention_ctx",
    )(q, k, vt, gate)
    return out, out_ctx


NA_KEY_ROWS = 12
NA_STEP_ROWS = TOKEN_TILE // GRID_W
NA_VARIANTS = 3


def _na_rel_row(variant, rr, a):
    if variant == 0:
        valid, dr = a < WIN_R, a - rr
    elif variant == 1:
        dr = a - WIN_R // 2 - rr
        valid = -(WIN_R // 2) <= dr < WIN_R // 2
    else:
        valid, dr = a >= NA_KEY_ROWS - WIN_R, a - (NA_KEY_ROWS - NA_STEP_ROWS) - rr
    return dr if valid else None


def _na_bias_body(rpb_ref, o_ref, blk_scr):
    p = pl.program_id(0)
    shape = (GRID_W, LANES)
    kc = lax.broadcasted_iota(jnp.int32, shape, 0)
    lane = lax.broadcasted_iota(jnp.int32, shape, 1)
    qc = lane % GRID_W
    upper = lane >= GRID_W
    rel = kc - qc + (WIN_C - 1)
    c0 = jnp.clip(qc - WIN_C // 2, 0, GRID_W - WIN_C)
    col_ok = (kc >= c0) & (kc < c0 + WIN_C)
    n_rel_r, n_rel_c = 2 * WIN_R - 1, 2 * WIN_C - 1

    def block(dd, carry):
        base0 = (2 * p) * (n_rel_r * n_rel_c) + dd * n_rel_c
        base1 = base0 + n_rel_r * n_rel_c
        val = jnp.zeros(shape, F32)
        for jj in range(n_rel_c):
            val = jnp.where(rel == jj, jnp.where(upper, rpb_ref[base1 + jj], rpb_ref[base0 + jj]), val)
        blk_scr[dd] = jnp.where(col_ok, val * LOG2E, NEG)
        return carry

    lax.fori_loop(0, n_rel_r, block, 0)
    outside = jnp.full(shape, NEG, F32)
    for variant in range(NA_VARIANTS):
        for rr in range(NA_STEP_ROWS):
            for a in range(NA_KEY_ROWS):
                dr = _na_rel_row(variant, rr, a)
                o_ref[0, variant, rr, a * GRID_W:(a + 1) * GRID_W, :] = (
                    outside if dr is None else blk_scr[dr + WIN_R - 1])


def _na_bias(rpb):
    heads = rpb.shape[0]
    tab = (NA_VARIANTS, NA_STEP_ROWS, NA_KEY_ROWS * GRID_W, LANES)
    return pl.pallas_call(
        _na_bias_body,
        out_shape=jax.ShapeDtypeStruct((heads // 2,) + tab, F32),
        grid=(heads // 2,),
        in_specs=[pl.BlockSpec(memory_space=pltpu.SMEM)],
        out_specs=pl.BlockSpec((1,) + tab, lambda p: (p, 0, 0, 0, 0)),
        scratch_shapes=[pltpu.VMEM((2 * WIN_R - 1, GRID_W, LANES), F32)],
        compiler_params=_params(1), name="na_bias",
    )(rpb.reshape(-1))


NA_CHUNK = 256
NA_STEP_BLOCKS = 8


def _na_body(q_ref, kl_ref, kc_ref, vt_ref, bias_ref, g_ref, o_ref, s_scr, m_scr, *, seq, ctx_len, n_steps, n_total):
    g = pl.program_id(0)
    j_scored = jnp.minimum(g, n_total - 1) % n_steps
    j_finished = jnp.maximum(g - 1, 0) % n_steps
    n_rows = seq // GRID_W
    n_loc = NA_KEY_ROWS * GRID_W // NA_CHUNK
    cols = NA_STEP_ROWS * LANES
    lower = lax.broadcasted_iota(jnp.int32, (GRID_W, LANES), 1) < GRID_W

    @pl.when(g == 0)
    def _():
        s_scr[...] = jnp.zeros(s_scr.shape, F32)
        m_scr[...] = jnp.zeros(m_scr.shape, F32)

    def span_start(step):
        first_row = jnp.clip(NA_STEP_ROWS * step - WIN_R // 2, 0, n_rows - NA_KEY_ROWS)
        return first_row * GRID_W

    def step(slot, prev):
        subs = range(NA_STEP_BLOCKS)
        variant, k0, v0, q2 = [], [], [], []
        for sub in subs:
            blk = NA_STEP_BLOCKS * j_scored + sub
            blk_prev = NA_STEP_BLOCKS * j_finished + sub
            variant.append(jnp.where(blk == 0, 0, jnp.where(blk == NA_STEP_BLOCKS * n_steps - 1, 2, 1)))
            k0.append(span_start(blk))
            v0.append(span_start(blk_prev))
            q = q_ref[sub * TOKEN_TILE:(sub + 1) * TOKEN_TILE, :]
            zero = jnp.zeros((GRID_W, LANES), q.dtype)
            parts = []
            for rr in range(NA_STEP_ROWS):
                q_r = q[rr * GRID_W:(rr + 1) * GRID_W]
                parts += [jnp.where(lower, q_r, zero), jnp.where(lower, zero, q_r)]
            q2.append(jnp.concatenate(parts, axis=0))
        m_prev = [m_scr[prev, sub] for sub in subs]
        m = [None for _ in subs]
        l = [jnp.zeros((1, cols), F32) for _ in subs]
        acc = [jnp.zeros((LANES, cols), F32) for _ in subs]
        for c in range(n_loc + ctx_len // NA_CHUNK):
            rows = slice((c % n_loc) * NA_CHUNK, (c % n_loc + 1) * NA_CHUNK)
            for sub in subs:
                if c < n_loc:
                    k_at = pl.multiple_of(k0[sub] + c * NA_CHUNK, NA_CHUNK)
                    v_at = pl.multiple_of(v0[sub] + c * NA_CHUNK, NA_CHUNK)
                    s = _dot_nt(kl_ref[pl.ds(k_at, NA_CHUNK), :], q2[sub])
                    s = jnp.concatenate([s[:, rr * LANES:(rr + 1) * LANES] + bias_ref[0, variant[sub], rr, rows, :]
                                         for rr in range(NA_STEP_ROWS)], axis=1)
                    v_chunk = vt_ref[0, :, pl.ds(v_at, NA_CHUNK)]
                else:
                    s = _dot_nt(kc_ref[rows, :], q2[sub])
                    v_chunk = vt_ref[0, :, seq + rows.start:seq + rows.stop]
                s_scr[slot, sub, c] = s
                cm = jnp.max(s, axis=0, keepdims=True)
                m[sub] = cm if m[sub] is None else jnp.maximum(m[sub], cm)
                p = jnp.exp2(s_scr[prev, sub, c] - m_prev[sub])
                l[sub] = l[sub] + jnp.sum(p, axis=0, keepdims=True)
                acc[sub] = acc[sub] + _dot(v_chunk, p.astype(BF16))
        for sub in subs:
            m_scr[slot, sub] = m[sub]
            r_n = (acc[sub] / l[sub]).T
            outs = [jnp.where(lower, r_n[rr * LANES:rr * LANES + GRID_W], r_n[rr * LANES + GRID_W:(rr + 1) * LANES])
                    for rr in range(NA_STEP_ROWS)]
            toks = slice(sub * TOKEN_TILE, (sub + 1) * TOKEN_TILE)
            o_ref[toks, :] = (jnp.concatenate(outs, axis=0) * _silu(g_ref[toks, :].astype(F32))).astype(o_ref.dtype)

    for parity in (0, 1):
        pl.when(g % 2 == parity)(functools.partial(step, parity, 1 - parity))


def _na_attention(q, k, vt, bias, gate, batch, seq, ctx_len):
    pairs = q.shape[1] // LANES
    t_len = seq + ctx_len
    step_tokens = NA_STEP_BLOCKS * TOKEN_TILE
    assert seq % step_tokens == 0
    n_steps = seq // step_tokens
    n_chunks = (NA_KEY_ROWS * GRID_W + ctx_len) // NA_CHUNK
    cols = NA_STEP_ROWS * LANES
    n_total = batch * pairs * n_steps

    def unravel(g):
        return g // (pairs * n_steps), (g // n_steps) % pairs, g % n_steps

    scored = lambda g: unravel(jnp.minimum(g, n_total - 1))
    finished = lambda g: unravel(jnp.maximum(g - 1, 0))

    def q_map(g):
        b, p, j = scored(g)
        return b * n_steps + j, p

    def k_map(g):
        b, p, _ = scored(g)
        return b, p

    def k_ctx_map(g):
        b, p, _ = scored(g)
        return batch * seq // ctx_len + b, p

    def v_map(g):
        b, p, _ = finished(g)
        return b, p, 0

    def done(g):
        b, p, j = finished(g)
        return b * n_steps + j, p

    body = functools.partial(_na_body, seq=seq, ctx_len=ctx_len, n_steps=n_steps, n_total=n_total)
    return pl.pallas_call(
        body, out_shape=jax.ShapeDtypeStruct((batch * seq, q.shape[1]), BF16),
        grid=(n_total + 1,),
        in_specs=[pl.BlockSpec((step_tokens, LANES), q_map),
                  pl.BlockSpec((seq, LANES), k_map),
                  pl.BlockSpec((ctx_len, LANES), k_ctx_map),
                  pl.BlockSpec((1, LANES, t_len), v_map),
                  pl.BlockSpec((1,) + bias.shape[1:], lambda g: (scored(g)[1], 0, 0, 0, 0)),
                  pl.BlockSpec((step_tokens, LANES), done)],
        out_specs=pl.BlockSpec((step_tokens, LANES), done),
        scratch_shapes=[pltpu.VMEM((2, NA_STEP_BLOCKS, n_chunks, NA_CHUNK, cols), F32),
                        pltpu.VMEM((2, NA_STEP_BLOCKS, 1, cols), F32)],
        compiler_params=_params(1), name="neighborhood_attention",
    )(q, k, k, vt, bias, gate)


def _seg_scans(jobs, use_max=False):
    vals = [v for v, _ in jobs]
    n = vals[0].shape[1]
    lane = lax.broadcasted_iota(jnp.int32, vals[0].shape, 1) % ML_CHUNK
    k = 1
    while k < ML_CHUNK:
        for i, (_, reverse) in enumerate(jobs):
            v = vals[i]
            if reverse:
                ok, shifted = lane < ML_CHUNK - k, pltpu.roll(v, n - k, 1)
            else:
                ok, shifted = lane >= k, pltpu.roll(v, k, 1)
            vals[i] = jnp.maximum(v, jnp.where(ok, shifted, NEG)) if use_max else v + jnp.where(ok, shifted, 0.0)
        k *= 2
    return vals


ML_GATE_ROWS = 40


def _mlstm_prep_body(u_ref, up_ref, un_ref, cw_ref, cb_ref, wqk_ref, wv_ref, wg_ref, bg_ref,
                     xc_ref, k_ref, qt_ref, vt_ref, pre_ref, *, nj, width, kscale):
    j = pl.program_id(1)
    u = u_ref[...]
    row = lax.broadcasted_iota(jnp.int32, u.shape, 0)
    prev = jnp.where((j > 0) & (j < nj), up_ref[7:8, :], 0.0)
    nxt = jnp.where(j < nj - 1, un_ref[0:1, :], 0.0)
    u_m1 = jnp.where(row == 0, prev, pltpu.roll(u, 1, 0))
    u_p1 = jnp.where(row == TOKEN_TILE - 1, nxt, pltpu.roll(u, TOKEN_TILE - 1, 0))
    cw = cw_ref[...]
    xc = _silu(u_m1 * cw[0:1] + u * cw[1:2] + u_p1 * cw[2:3] + cb_ref[...])
    xcb = xc.astype(BF16)
    xc_ref[...] = xcb
    qk = _dot(xcb, wqk_ref[...])
    v = _dot(u.astype(BF16), wv_ref[...])
    qb, kb, vb = qk[:, :width].astype(BF16), qk[:, width:].astype(BF16), v.astype(BF16)
    k_ref[...] = (qk[:, width:] * kscale).astype(BF16)
    qt_ref[0] = qk[:, :width].T.astype(BF16)
    vt_ref[0] = v.T.astype(BF16)
    pre_ref[0] = _dot_nt(wg_ref[0], qb) + _dot_nt(wg_ref[1], kb) + _dot_nt(wg_ref[2], vb) + bg_ref[...]


def _mlstm_gates_body(pre_ref, g_ref, gc_ref):
    pre = pre_ref[0]
    i8 = [pre[16 * d:16 * d + 8] for d in range(2)]
    f8 = [_log_sigmoid(pre[16 * d + 8:16 * d + 16]) for d in range(2)]
    b0, b0_rev, b1, b1_rev = _seg_scans([(f8[0], False), (f8[0], True), (f8[1], True), (f8[1], False)])
    b8, b_last = [b0, b1], [b0 + b0_rev - f8[0], b1 + b1_rev - f8[1]]
    r8 = [i8[d] - b8[d] for d in range(2)]
    c0, c0_rev, c1, c1_rev = _seg_scans([(r8[0], False), (r8[0], True), (r8[1], True), (r8[1], False)], use_max=True)
    c8, r_max = [c0, c1], [jnp.maximum(c0, c0_rev), jnp.maximum(c1, c1_rev)]
    for d in range(2):
        g_ref[0, d * ML_GATE_ROWS:(d + 1) * ML_GATE_ROWS] = jnp.concatenate(
            [-c8[d], b8[d] + c8[d], jnp.exp(r8[d] - r_max[d]), b_last[d], b_last[d] + r_max[d]], axis=0)
    pad = jnp.zeros((LANES - 16, pre.shape[1]), F32)
    gc_ref[0] = jnp.concatenate(r8 + [pad], axis=0).T


def _mlstm_gates(pre):
    batch, _, t_len = pre.shape
    return pl.pallas_call(
        _mlstm_gates_body,
        out_shape=[jax.ShapeDtypeStruct((batch, 2 * ML_GATE_ROWS, t_len), F32),
                   jax.ShapeDtypeStruct((batch, t_len, LANES), F32)],
        grid=(batch,),
        in_specs=[pl.BlockSpec((1,) + pre.shape[1:], lambda b: (b, 0, 0))],
        out_specs=[pl.BlockSpec((1, 2 * ML_GATE_ROWS, t_len), lambda b: (b, 0, 0)),
                   pl.BlockSpec((1, t_len, LANES), lambda b: (b, 0, 0))],
        compiler_params=_params(1), name="mlstm_gates",
    )(pre)


def _mlstm_prep(u, conv_w, conv_b, w_q, w_k, w_v, w_gate, b_gate, batch, seq, ctx_len):
    heads, hd = w_q.shape[0], w_q.shape[1]
    assert heads == ML_HEADS and hd == ML_CHUNK
    width = heads * hd
    nj = seq // TOKEN_TILE
    t_len = seq + ctx_len
    rows = u.shape[0]

    def block_diag(w):
        eye = jnp.eye(heads, dtype=w.dtype)
        return (eye[:, None, :, None] * w[:, :, None, :]).reshape(width, width)

    wqk = jnp.concatenate([block_diag(w_q), block_diag(w_k)], axis=1).astype(BF16)
    wv = block_diag(w_v).astype(BF16)
    wg = w_gate.reshape(2, heads, 3, hd, 2, heads).transpose(2, 0, 4, 5, 1, 3).reshape(3, 2, 2, heads, width)
    wg = jnp.pad(wg, ((0, 0), (0, 0), (0, 0), (0, 8 - heads), (0, 0))).reshape(3, 32, width).astype(BF16)
    bg = jnp.pad(b_gate.reshape(2, 2, heads), ((0, 0), (0, 0), (0, 8 - heads))).reshape(32, 1)
    n_halo = rows // 8
    tokb = lambda b, j: _tok_block(b, j, nj, batch)
    tile = pl.BlockSpec((TOKEN_TILE, width), lambda b, j: (tokb(b, j), 0))
    feat = pl.BlockSpec((1, width, TOKEN_TILE), lambda b, j: (b, 0, j))
    body = functools.partial(_mlstm_prep_body, nj=nj, width=width, kscale=hd ** -0.5)
    per_tile = TOKEN_TILE // 8
    return pl.pallas_call(
        body,
        out_shape=[jax.ShapeDtypeStruct((rows, width), BF16)] * 2
        + [jax.ShapeDtypeStruct((batch, width, t_len), BF16)] * 2
        + [jax.ShapeDtypeStruct((batch, 32, t_len), F32)],
        grid=(batch, nj + 1),
        in_specs=[tile,
                  pl.BlockSpec((8, width), lambda b, j: (jnp.maximum(tokb(b, j) * per_tile - 1, 0), 0)),
                  pl.BlockSpec((8, width), lambda b, j: (jnp.minimum((tokb(b, j) + 1) * per_tile, n_halo - 1), 0)),
                  _full((3, width)), _full((1, width)), _full(wqk.shape), _full(wv.shape),
                  _full(wg.shape), _full((32, 1))],
        out_specs=[tile, tile, feat, feat, pl.BlockSpec((1, 32, TOKEN_TILE), lambda b, j: (b, 0, j))],
        compiler_params=_params(2), name="mlstm_prep",
    )(u, u, u, conv_w, conv_b.reshape(1, width), wqk, wv, wg, bg)


def _mlstm_seq_body(kf, qtf, vtf, gf, gcf, kb, qtb, vtb, gb, gcb, hf_ref, hb_ref, c_s, n_s, m_s):
    t = pl.program_id(1)
    L = ML_CHUNK
    subs = TOKEN_TILE // L

    @pl.when(t == 0)
    def _():
        c_s[...] = jnp.zeros(c_s.shape, F32)
        n_s[...] = jnp.zeros(n_s.shape, F32)
        m_s[...] = jnp.zeros(m_s.shape, F32)

    si = lax.broadcasted_iota(jnp.int32, (L, L), 0)
    li = lax.broadcasted_iota(jnp.int32, (L, L), 1)
    streams = ((kf, qtf, vtf, gf, gcf, hf_ref, si <= li), (kb, qtb, vtb, gb, gcb, hb_ref, si >= li))
    units = {}
    for d, (k_ref, qt_ref, vt_ref, g_ref, gc_ref, h_ref, incl) in enumerate(streams):
        for sub in range(subs):
            toks = slice(sub * L, (sub + 1) * L)
            g = g_ref[0, :, toks]
            gc = gc_ref[0, toks, :]
            for h in range(ML_HEADS):
                cols = slice(h * L, (h + 1) * L)
                k, qt, vt = k_ref[toks, cols], qt_ref[0, cols, toks], vt_ref[0, cols, toks]
                neg_c, m_loc, w0, b_last, g_max = (g[8 * i + h:8 * i + h + 1] for i in range(5))
                r_col = gc[:, 8 * d + h:8 * d + h + 1]
                p0 = jnp.where(incl, jnp.exp(r_col + neg_c), 0.0) * _dot(k, qt)
                s_sum = jnp.sum(p0, axis=0, keepdims=True)
                intra = _dot(vt, p0.astype(BF16))
                c_inc = _dot((vt.astype(F32) * w0).astype(BF16), k)
                n_inc = _dot(jnp.broadcast_to(w0, (8, L)).astype(BF16), k)
                units[d, sub, h] = (h_ref, cols, toks, qt, neg_c, m_loc, b_last, g_max, s_sum, intra, c_inc, n_inc)
    for stage in range(subs):
        for d in range(2):
            sub = stage if d == 0 else subs - 1 - stage
            for h in range(ML_HEADS):
                idx = d * ML_HEADS + h
                h_ref, cols, toks, qt, neg_c, m_loc, b_last, g_max, s_sum, intra, c_inc, n_inc = units[d, sub, h]
                c_st, n_st, m_st = c_s[idx], n_s[idx], m_s[idx]
                cn = _dot(jnp.concatenate([c_st, n_st], axis=0).astype(BF16), qt)
                delta = jnp.maximum(m_st + neg_c, 0.0)
                e_intra = jnp.exp(-delta)
                w_inter = jnp.exp(m_st + neg_c - delta)
                num = w_inter * cn[:L] + e_intra * intra
                den = w_inter * cn[L:L + 1] + e_intra * s_sum
                h_out = num / jnp.maximum(jnp.abs(den), jnp.exp(-(m_loc + delta)))
                h_ref[0, cols, toks] = h_out.astype(h_ref.dtype)
                m_new = jnp.maximum(b_last + m_st, g_max)
                decay = jnp.exp(b_last + m_st - m_new)
                gain = jnp.exp(g_max - m_new)
                c_s[idx] = decay * c_st + gain * c_inc
                n_s[idx] = decay * n_st + gain * n_inc
                m_s[idx] = m_new


def _mlstm_seq(k, qt, vt, gates, gcols, batch, seq, ctx_len):
    width = k.shape[1]
    L = ML_CHUNK
    assert ctx_len == TOKEN_TILE
    nj = seq // TOKEN_TILE
    fwd = lambda t: (t + nj) % (nj + 1)
    bwd = lambda t: nj - t

    def stream(tile_of, d):
        feat = pl.BlockSpec((1, width, TOKEN_TILE), lambda b, t: (b, 0, tile_of(t)))
        return [pl.BlockSpec((TOKEN_TILE, width), lambda b, t: (_tok_block(b, tile_of(t), nj, batch), 0)), feat, feat,
                pl.BlockSpec((1, ML_GATE_ROWS, TOKEN_TILE), lambda b, t: (b, d, tile_of(t))),
                pl.BlockSpec((1, TOKEN_TILE, LANES), lambda b, t: (b, tile_of(t), 0))]

    out_f = pl.BlockSpec((1, width, TOKEN_TILE), lambda b, t: (b, 0, fwd(t)))
    out_b = pl.BlockSpec((1, width, TOKEN_TILE), lambda b, t: (b, 0, bwd(t)))
    n_state = 2 * ML_HEADS
    return pl.pallas_call(
        _mlstm_seq_body,
        out_shape=[jax.ShapeDtypeStruct(qt.shape, BF16)] * 2,
        grid=(batch, nj + 1),
        in_specs=stream(fwd, 0) + stream(bwd, 1),
        out_specs=[out_f, out_b],
        scratch_shapes=[pltpu.VMEM((n_state, L, L), F32), pltpu.VMEM((n_state, 8, L), F32),
                        pltpu.VMEM((n_state, 1, L), F32)],
        compiler_params=_params(2), name="mlstm_recurrence",
    )(k, qt, vt, gates, gcols, k, qt, vt, gates, gcols)


def kernel(x, c, ctx, c_ctx, w_mod, b_mod, g_norm, ab_w_in, ab_w_out, mla_g_q, mla_w_uq, mla_g_kv, mla_w_ukv,
           ml_conv_w, ml_conv_b, ml_w_q, ml_w_k, ml_w_v, ml_w_gate, ml_b_gate, ml_g_head, ml_skip,
           cd_w_in, cd_w_out, na_rpb, gqa_g_q, gqa_g_k, g_final):
    batch, seq, d = x.shape
    ctx_len = ctx.shape[1]
    assert ctx_len == TOKEN_TILE and seq % KEY_CHUNK == 0 and seq // GRID_W >= NA_KEY_ROWS
    dims = (batch, seq, ctx_len)

    mla_heads, mla_rope, mla_v = 8, 32, 64
    mla_nope = mla_w_uq.shape[2] // mla_heads - mla_rope
    q_lora, kv_lora = mla_g_q.shape[1], mla_g_kv.shape[1]
    ml_width = ml_conv_w.shape[2]
    mla_width = mla_heads * mla_v
    gqa_heads, gqa_dim = 8, gqa_g_q.shape[1]
    gqa_kv = (cd_w_in.shape[2] - 4 * 512 - 2 * gqa_heads * gqa_dim) // (2 * gqa_dim)
    na_width = na_rpb.shape[1] * 64

    mod_rows = -(-(batch + 1) // 8) * 8
    cvec = jnp.concatenate([c, c_ctx[None], jnp.zeros((mod_rows - batch - 1, d), F32)], axis=0)
    mod = _modulation(cvec, w_mod, b_mod)
    mod0 = mod[0].reshape(mod_rows, 1, 3 * d)
    mod1 = mod[1].reshape(mod_rows, 1, 3 * d)

    tok0 = _Tokens((x.reshape(batch * seq, d), ctx.reshape(batch * ctx_len, d)), *dims)
    w_in = ab_w_in[0]
    s1 = q_lora + kv_lora
    zcol = lambda n: jnp.zeros((d, n), w_in.dtype)
    w0 = jnp.concatenate([w_in[:, :s1], zcol(mla_nope), w_in[:, s1:s1 + mla_rope],
                          zcol(LANES - mla_nope - mla_rope), w_in[:, s1 + mla_rope:]], axis=1).astype(BF16)
    o_pa = s1 + LANES
    outs0 = ((0, o_pa, F32, 1.0, False), (o_pa, mla_width, BF16, 1.0, False),
             (o_pa + mla_width, ml_width, F32, 1.0, False), (o_pa + mla_width + ml_width, ml_width, BF16, 1.0, False))
    pa, gate_a, u, z = _in_proj(tok0, mod0, g_norm[0], w0, outs0, *dims)

    def mla_lanes(table, fill):
        n = table.shape[0]
        return jnp.concatenate([jnp.full((n, mla_nope), fill, F32), table,
                                jnp.full((n, LANES - mla_nope - mla_rope), fill, F32)], axis=-1)

    cos_a, sin_a = _rope_tables(seq, ctx_len, mla_rope)
    cos_a, sin_a = mla_lanes(cos_a, 1.0), mla_lanes(sin_a, 0.0)
    q_a, k_a, vt_a = _mla_prep(pa, cos_a, sin_a, mla_g_q[0], mla_g_kv[0], mla_w_uq[0], mla_w_ukv[0],
                               *dims, mla_heads, mla_nope, mla_rope, mla_v)
    mix_a = _flash(q_a, k_a, vt_a, gate_a, *dims, k_heads_per_pair=2, v_rows_per_pair=2 * mla_v,
                   pairs_per_kv=1, ctx_queries=True)

    xc, k_m, qt_m, vt_m, gate_pre = _mlstm_prep(u, ml_conv_w[0], ml_conv_b[0], ml_w_q[0], ml_w_k[0], ml_w_v[0],
                                                ml_w_gate[0], ml_b_gate[0], *dims)
    gates, gcols = _mlstm_gates(gate_pre)
    h_f, h_b = _mlstm_seq(k_m, qt_m, vt_m, gates, gcols, *dims)
    x1 = _out_proj(tok0, _Tokens(mix_a, *dims), (h_f, h_b, xc, z, ml_g_head[0], ml_skip[0]),
                   ab_w_out[0], mod0, batch, seq)

    tok1 = _Tokens((x1,), *dims)
    w1 = cd_w_in[0].astype(BF16)
    gq_w, gkv_w = gqa_heads * gqa_dim, gqa_kv * gqa_dim
    o_d = 4 * na_width
    outs1 = ((0, na_width, BF16, 64 ** -0.5 * LOG2E, False), (na_width, na_width, BF16, 1.0, False),
             (2 * na_width, na_width, BF16, 1.0, True), (3 * na_width, na_width, BF16, 1.0, False),
             (o_d, gq_w + gkv_w, F32, 1.0, True), (o_d + gq_w + gkv_w, gkv_w, BF16, 1.0, True),
             (o_d + gq_w + 2 * gkv_w, gq_w, BF16, 1.0, False))
    q_c, k_c, vt_c, gate_c, pd_t, vt_d, gate_d = _in_proj(tok1, mod1, g_norm[1], w1, outs1, *dims)

    mix_c = _na_attention(q_c, k_c, vt_c, _na_bias(na_rpb[0]), gate_c, *dims)

    cos_d, sin_d = _rope_tables(seq, ctx_len, gqa_dim)
    q_d, k_d = _gqa_prep(pd_t, cos_d, sin_d, gqa_g_q[0], gqa_g_k[0], *dims, gqa_heads, gqa_kv, gqa_dim)
    mix_d = _flash(q_d, k_d, vt_d, gate_d, *dims, k_heads_per_pair=1, v_rows_per_pair=gqa_dim,
                   pairs_per_kv=gqa_heads // (2 * gqa_kv), ctx_queries=False)

    out = _out_proj(tok1, _Tokens((mix_c,), *dims), mix_d[0], cd_w_out[0], mod1, batch, seq, g_final=g_final)
    return out.reshape(batch, seq, d)
```

```python
import functools

import jax
import jax.numpy as jnp
from jax import lax
from jax.experimental import pallas as pl
from jax.experimental.pallas import tpu as pltpu

F32 = jnp.float32
BF16 = jnp.bfloat16

LANES = 128
TOKEN_TILE = 256
KEY_CHUNK = 256
FLASH_TILE = 256
FLASH_STEP_TILES = 2
GRID_W = 64
WIN_R = 8
WIN_C = 16
ML_CHUNK = 128
ML_HEADS = 4
EPS = 1e-6
ROPE_BASE = 10000.0
LOG2E = 1.4426950408889634
NEG = -1e30
VMEM_LIMIT = 56 * 1024 * 1024

_NT = (((1,), (1,)), ((), ()))


def _dot(a, b):
    return jnp.dot(a, b, preferred_element_type=F32)


def _dot_nt(a, b):
    return lax.dot_general(a, b, _NT, preferred_element_type=F32)


def _silu(v):
    return v * (1.0 / (1.0 + jnp.exp(-v)))


def _log_sigmoid(v):
    return -(jnp.maximum(-v, 0.0) + jnp.log1p(jnp.exp(-jnp.abs(v))))


def _params(n_axes):
    return pltpu.CompilerParams(dimension_semantics=("arbitrary",) * n_axes, vmem_limit_bytes=VMEM_LIMIT)


def _full(shape):
    nd = len(shape)
    return pl.BlockSpec(shape, lambda *_: (0,) * nd)


def _mod_body(c_ref, w_ref, b_ref, o_ref):
    s = _silu(c_ref[...])
    o_ref[0] = _dot(s.astype(BF16), w_ref[0].astype(BF16)) + b_ref[0]


def _modulation(cvec, w_mod, b_mod):
    depth, d, n = w_mod.shape
    rows = cvec.shape[0]
    tn = n // 4
    return pl.pallas_call(
        _mod_body,
        out_shape=jax.ShapeDtypeStruct((depth, rows, n), F32),
        grid=(depth, n // tn),
        in_specs=[_full((rows, d)),
                  pl.BlockSpec((1, d, tn), lambda l, j: (l, 0, j)),
                  pl.BlockSpec((1, 1, tn), lambda l, j: (l, 0, j))],
        out_specs=pl.BlockSpec((1, rows, tn), lambda l, j: (l, 0, j)),
        compiler_params=_params(2), name="modulation",
    )(cvec, w_mod, b_mod.reshape(depth, 1, n))


class _Tokens:
    def __init__(self, arrays, batch, seq, ctx_len):
        self.arrays = arrays
        self.split = len(arrays) == 2
        self.n_lat = batch * seq // TOKEN_TILE
        self.n_ctx = batch * ctx_len // TOKEN_TILE
        self.d = arrays[0].shape[-1]

    def specs(self):
        blk = (TOKEN_TILE, self.d)
        if not self.split:
            return [pl.BlockSpec(blk, lambda i: (i, 0))]
        n_lat = self.n_lat
        return [pl.BlockSpec(blk, lambda i: (jnp.minimum(i, n_lat - 1), 0)),
                pl.BlockSpec(blk, lambda i: (jnp.maximum(i - n_lat, 0), 0))]

    def load(self, refs, i):
        if not self.split:
            return refs[0][...]
        return jnp.where(i < self.n_lat, refs[0][...], refs[1][...])


def _mod_spec(n_lat, nj, batch, width):
    return pl.BlockSpec((1, 1, width), lambda i: (jnp.where(i < n_lat, i // nj, batch), 0, 0))


def _in_proj_body(*refs, tok, outs, d):
    n_tok = len(tok.arrays)
    mod_ref, g_ref, w_ref = refs[n_tok:n_tok + 3]
    o_refs = refs[n_tok + 3:]
    i = pl.program_id(0)
    x = tok.load(refs[:n_tok], i)
    y = x * lax.rsqrt(jnp.mean(x * x, axis=-1, keepdims=True) + EPS) * g_ref[...]
    mod = mod_ref[0]
    h = y * (1.0 + mod[:, d:2 * d]) + mod[:, :d]
    acc = _dot(h.astype(BF16), w_ref[...])
    for o_ref, (c0, width, _, scale, transposed) in zip(o_refs, outs):
        v = acc[:, c0:c0 + width]
        if scale != 1.0:
            v = v * scale
        if transposed:
            o_ref[0] = v.T.astype(o_ref.dtype)
        else:
            o_ref[...] = v.astype(o_ref.dtype)


def _in_proj(tok, mod_l, g, w, outs, batch, seq, ctx_len):
    d = tok.d
    nj = seq // TOKEN_TILE
    n_lat, n_all = tok.n_lat, tok.n_lat + tok.n_ctx
    rows = n_all * TOKEN_TILE
    t_len = seq + ctx_len
    out_shape, out_specs = [], []
    for (_, width, dtype, _, transposed) in outs:
        if transposed:
            out_shape.append(jax.ShapeDtypeStruct((batch, width, t_len), dtype))
            out_specs.append(pl.BlockSpec(
                (1, width, TOKEN_TILE),
                lambda i: (jnp.where(i < n_lat, i // nj, i - n_lat), 0, jnp.where(i < n_lat, i % nj, nj))))
        else:
            out_shape.append(jax.ShapeDtypeStruct((rows, width), dtype))
            out_specs.append(pl.BlockSpec((TOKEN_TILE, width), lambda i: (i, 0)))
    body = functools.partial(_in_proj_body, tok=tok, outs=outs, d=d)
    return pl.pallas_call(
        body, out_shape=out_shape, grid=(n_all,),
        in_specs=tok.specs() + [_mod_spec(n_lat, nj, batch, 3 * d), _full((1, d)), _full(w.shape)],
        out_specs=out_specs, compiler_params=_params(1), name="in_proj",
    )(*tok.arrays, mod_l, g.reshape(1, d), w)


def _mlstm_mix(hf_ref, hb_ref, xc_ref, z_ref, gh_ref, sk_ref):
    ht = hf_ref[0].astype(F32) + hb_ref[0].astype(F32)
    L = ML_CHUNK
    normed = []
    for hd in range(ML_HEADS):
        hh = ht[hd * L:(hd + 1) * L]
        mu = jnp.mean(hh, axis=0, keepdims=True)
        var = jnp.mean(jnp.square(hh - mu), axis=0, keepdims=True)
        normed.append((hh - mu) * lax.rsqrt(var + EPS))
    hn = jnp.concatenate(normed, axis=0).T * gh_ref[...]
    return ((hn + sk_ref[...] * xc_ref[...].astype(F32)) * _silu(z_ref[...].astype(F32))).astype(BF16)


def _out_proj_body(*refs, tok, mix_a, n_b, d, final):
    n_tok, n_a = len(tok.arrays), len(mix_a.arrays)
    b_refs = refs[n_tok + n_a:n_tok + n_a + n_b]
    wa_ref, wb_ref, mod_ref = refs[n_tok + n_a + n_b:n_tok + n_a + n_b + 3]
    rest = refs[n_tok + n_a + n_b + 3:]
    i = pl.program_id(0)
    x = tok.load(refs[:n_tok], i)
    mb = b_refs[0][...] if n_b == 1 else _mlstm_mix(*b_refs)
    acc = _dot(mix_a.load(refs[n_tok:n_tok + n_a], i), wa_ref[...]) + _dot(mb, wb_ref[...])
    xn = x + mod_ref[0][:, 2 * d:] * acc
    if final:
        gf_ref, o_ref = rest
        xn = xn * lax.rsqrt(jnp.mean(xn * xn, axis=-1, keepdims=True) + EPS) * gf_ref[...]
    else:
        (o_ref,) = rest
    o_ref[...] = xn


def _out_proj(tok, mix_a, mix_b, w_out, mod_l, batch, seq, g_final=None):
    d = tok.d
    half = mix_a.d
    nj = seq // TOKEN_TILE
    n_lat = tok.n_lat
    final = g_final is not None
    n_tiles = n_lat if final else n_lat + tok.n_ctx
    wa, wb = w_out[:half].astype(BF16), w_out[half:].astype(BF16)
    tile = lambda width: pl.BlockSpec((TOKEN_TILE, width), lambda i: (i, 0))
    if isinstance(mix_b, tuple):
        hf, hb, xc, z, g_head, skip = mix_b
        width = xc.shape[1]
        feat = pl.BlockSpec((1, width, TOKEN_TILE),
                            lambda i: (jnp.where(i < n_lat, i // nj, i - n_lat), 0, jnp.where(i < n_lat, i % nj, nj)))
        b_specs = [feat, feat, tile(width), tile(width), _full((1, width)), _full((1, width))]
        b_args = [hf, hb, xc, z, g_head.reshape(1, width), skip.reshape(1, width)]
    else:
        b_specs, b_args = [tile(half)], [mix_b]
    in_specs = tok.specs() + mix_a.specs() + b_specs + [_full(wa.shape), _full(wb.shape),
                                                        _mod_spec(n_lat, nj, batch, 3 * d)]
    args = list(tok.arrays) + list(mix_a.arrays) + b_args + [wa, wb, mod_l]
    if final:
        in_specs.append(_full((1, d)))
        args.append(g_final.reshape(1, d))
    body = functools.partial(_out_proj_body, tok=tok, mix_a=mix_a, n_b=len(b_args), d=d, final=final)
    return pl.pallas_call(
        body, out_shape=jax.ShapeDtypeStruct((n_tiles * TOKEN_TILE, d), F32), grid=(n_tiles,),
        in_specs=in_specs, out_specs=tile(d), compiler_params=_params(1), name="out_proj",
    )(*args)


def _rope_tables(seq, ctx_len, rot_dim):
    t = jnp.arange(seq)
    pos = jnp.stack([t // GRID_W, t % GRID_W], axis=-1).astype(F32)
    n_freq = rot_dim // 4
    inv = ROPE_BASE ** (-jnp.arange(n_freq, dtype=F32) / n_freq)
    ang = pos[:, :, None] * inv
    cos, sin = jnp.cos(ang), jnp.sin(ang)
    cos_t = jnp.concatenate([cos[:, 0], cos[:, 0], cos[:, 1], cos[:, 1]], axis=-1)
    sin_t = jnp.concatenate([-sin[:, 0], sin[:, 0], -sin[:, 1], sin[:, 1]], axis=-1)
    cos_t = jnp.concatenate([cos_t, jnp.ones((ctx_len, rot_dim), F32)], axis=0)
    sin_t = jnp.concatenate([sin_t, jnp.zeros((ctx_len, rot_dim), F32)], axis=0)
    return cos_t, sin_t


def _rope(x, cos, sin, dist):
    lane = lax.broadcasted_iota(jnp.int32, x.shape, 1)
    first = (lane % (2 * dist)) < dist
    partner = jnp.where(first, pltpu.roll(x, LANES - dist, 1), pltpu.roll(x, dist, 1))
    return x * cos + partner * sin


def _rope_rows(x, cos, sin, dist):
    n = x.shape[0] // dist
    partner = jnp.concatenate([x[(i ^ 1) * dist:((i ^ 1) + 1) * dist] for i in range(n)], axis=0)
    return x * cos + partner * sin


def _tok_block(b, j, nj, batch):
    return jnp.where(j < nj, b * nj + j, batch * nj + b)


def _mla_prep_body(*refs, n_b, heads, q_lora, kv_lora, qscale):
    pa_refs = refs[:n_b]
    cos_ref, sin_ref, cost_ref, sint_ref, gq_ref, gkv_ref, wuqt_ref, wuk_ref, wuvt_ref = refs[n_b:n_b + 9]
    q_ref, k_ref, vt_ref = refs[n_b + 9:]

    def norm(v, g_ref):
        return v * lax.rsqrt(jnp.mean(v * v, axis=-1, keepdims=True) + EPS) * g_ref[...]

    cos_t, sin_t = cost_ref[...], sint_ref[...]
    for s in range(n_b):
        pa = pa_refs[s][...]
        cq_t = norm(pa[:, :q_lora], gq_ref).T.astype(BF16)
        q_all = _dot(wuqt_ref[...], cq_t)
        for h in range(heads):
            qh = _rope_rows(q_all[h * LANES:(h + 1) * LANES], cos_t, sin_t, 8)
            q_ref[s, h] = (qh * qscale).astype(BF16)
        ckv = norm(pa[:, q_lora:q_lora + kv_lora], gkv_ref)
        k_nope = _dot(ckv.astype(BF16), wuk_ref[...])
        k_rope = _rope(pa[:, q_lora + kv_lora:], cos_ref[...], sin_ref[...], 8)
        for h in range(heads):
            k_ref[s, h] = (k_nope[:, h * LANES:(h + 1) * LANES] + k_rope).astype(BF16)
        vt_ref[s] = _dot(wuvt_ref[...], ckv.T.astype(BF16)).astype(BF16)


def _mla_prep(pa, cos, sin, g_q, g_kv, w_uq, w_ukv, batch, seq, ctx_len, heads, nope, rope, v_dim):
    q_lora, kv_lora = g_q.shape[0], g_kv.shape[0]
    nj = seq // TOKEN_TILE
    t_len = seq + ctx_len
    pad = LANES - nope - rope
    wuq = jnp.pad(w_uq.reshape(q_lora, heads, nope + rope), ((0, 0), (0, 0), (0, pad)))
    wuq_t = wuq.reshape(q_lora, heads * LANES).T.astype(BF16)
    wkv = w_ukv.reshape(kv_lora, heads, nope + v_dim)
    wuk = jnp.pad(wkv[..., :nope], ((0, 0), (0, 0), (0, LANES - nope))).reshape(kv_lora, heads * LANES).astype(BF16)
    wuv_t = wkv[..., nope:].reshape(kv_lora, heads * v_dim).T.astype(BF16)
    n_b = 2 if batch % 2 == 0 else 1
    body = functools.partial(_mla_prep_body, n_b=n_b, heads=heads, q_lora=q_lora, kv_lora=kv_lora,
                             qscale=(nope + rope) ** -0.5 * LOG2E)
    head_major = pl.BlockSpec((n_b, heads, TOKEN_TILE, LANES), lambda j, b: (b, 0, j, 0))
    q_feature_major = pl.BlockSpec((n_b, heads, LANES, TOKEN_TILE), lambda j, b: (b, 0, 0, j))
    tok_table = pl.BlockSpec((TOKEN_TILE, LANES), lambda j, b: (j, 0))
    feat_table = pl.BlockSpec((LANES, TOKEN_TILE), lambda j, b: (0, j))
    pa_specs = [pl.BlockSpec((TOKEN_TILE, pa.shape[1]), lambda j, b, s=s: (_tok_block(n_b * b + s, j, nj, batch), 0))
                for s in range(n_b)]
    return pl.pallas_call(
        body,
        out_shape=[jax.ShapeDtypeStruct((batch, heads, LANES, t_len), BF16),
                   jax.ShapeDtypeStruct((batch, heads, t_len, LANES), BF16),
                   jax.ShapeDtypeStruct((batch, heads * v_dim, t_len), BF16)],
        grid=(nj + 1, batch // n_b),
        in_specs=pa_specs + [tok_table, tok_table, feat_table, feat_table,
                             _full((1, q_lora)), _full((1, kv_lora)), _full(wuq_t.shape), _full(wuk.shape),
                             _full(wuv_t.shape)],
        out_specs=[q_feature_major, head_major,
                   pl.BlockSpec((n_b, heads * v_dim, TOKEN_TILE), lambda j, b: (b, 0, j))],
        compiler_params=_params(2), name="mla_prep",
    )(*([pa] * n_b), cos, sin, cos.T, sin.T, g_q.reshape(1, -1), g_kv.reshape(1, -1), wuq_t, wuk, wuv_t)


def _gqa_prep_body(pd_ref, cos_ref, sin_ref, gq_ref, gk_ref, q_ref, k_ref, *, heads, kv_heads, dim, qscale):
    cos, sin = cos_ref[...], sin_ref[...]
    n_b = pd_ref.shape[0]

    def head(s, first_row, g_ref):
        x = pd_ref[s, first_row:first_row + dim, :]
        y = x * lax.rsqrt(jnp.mean(x * x, axis=0, keepdims=True) + EPS) * g_ref[...]
        return _rope_rows(y, cos, sin, dim // 4)

    zeros = jnp.zeros((dim, TOKEN_TILE), F32)
    for h in range(heads):
        for s in range(n_b):
            parts = [zeros] * kv_heads
            parts[h // (heads // kv_heads)] = head(s, h * dim, gq_ref) * qscale
            q_ref[s, h] = jnp.concatenate(parts, axis=0).astype(BF16)
    for s in range(n_b):
        keys = [head(s, (heads + kv) * dim, gk_ref) for kv in range(kv_heads)]
        k_ref[s, 0] = jnp.concatenate(keys, axis=0).T.astype(BF16)


def _gqa_prep(pd_t, cos, sin, g_q, g_k, batch, seq, ctx_len, heads, kv_heads, head_dim):
    assert kv_heads * head_dim == LANES
    nj = seq // TOKEN_TILE
    t_len = seq + ctx_len
    n_b = 2 if batch % 2 == 0 else 1
    body = functools.partial(_gqa_prep_body, heads=heads, kv_heads=kv_heads, dim=head_dim,
                             qscale=head_dim ** -0.5 * LOG2E)
    table = pl.BlockSpec((head_dim, TOKEN_TILE), lambda j, b: (0, j))
    gain = lambda g: jnp.broadcast_to(g[:, None], (head_dim, TOKEN_TILE))
    return pl.pallas_call(
        body,
        out_shape=[jax.ShapeDtypeStruct((batch, heads, LANES, t_len), BF16),
                   jax.ShapeDtypeStruct((batch, 1, t_len, LANES), BF16)],
        grid=(nj + 1, batch // n_b),
        in_specs=[pl.BlockSpec((n_b, pd_t.shape[1], TOKEN_TILE), lambda j, b: (b, 0, j)), table, table,
                  _full((head_dim, TOKEN_TILE)), _full((head_dim, TOKEN_TILE))],
        out_specs=[pl.BlockSpec((n_b, heads, LANES, TOKEN_TILE), lambda j, b: (b, 0, 0, j)),
                   pl.BlockSpec((n_b, 1, TOKEN_TILE, LANES), lambda j, b: (b, 0, j, 0))],
        compiler_params=_params(2), name="gqa_prep",
    )(pd_t, cos.T, sin.T, gain(g_q), gain(g_k))


def _score_pass(q, k_ref, k_head, chunks, s_ref):
    m = None
    for (st, sz) in chunks:
        s_t = _dot(k_ref[0, k_head, st:st + sz, :], q)
        s_ref[st:st + sz, :] = s_t
        cm = jnp.max(s_t, axis=0, keepdims=True)
        m = cm if m is None else jnp.maximum(m, cm)
    return m


def _value_pass(s_ref, m, vt_ref, v_rows, chunks):
    l = jnp.zeros(m.shape, F32)
    acc = jnp.zeros((v_rows.stop - v_rows.start, m.shape[1]), F32)
    for (st, sz) in chunks:
        p = jnp.exp2(s_ref[st:st + sz, :] - m)
        l = l + jnp.sum(p, axis=0, keepdims=True)
        acc = acc + _dot(vt_ref[0, v_rows, st:st + sz], p.astype(BF16))
    return acc / l


def _gated_store(outs, g_ref, o_ref):
    o2 = jnp.concatenate(outs, axis=0).T
    o_ref[...] = (o2 * _silu(g_ref[...].astype(F32))).astype(o_ref.dtype)


def _flash_body(q_ref, k_ref, vt_ref, g_ref, o_ref, s_scr, m_scr, *, k_sel, v_off, v_dim, chunks):
    j = pl.program_id(0)

    @pl.when(j == 0)
    def _():
        s_scr[...] = jnp.zeros(s_scr.shape, F32)
        m_scr[...] = jnp.zeros(m_scr.shape, F32)

    def step(slot, prev):
        n_sub = q_ref.shape[3] // FLASH_TILE
        units = [(sub, a) for sub in range(n_sub) for a in (0, 1)]
        q = [q_ref[0, a, :, sub * FLASH_TILE:(sub + 1) * FLASH_TILE] for sub, a in units]
        m_prev = [m_scr[prev, u] for u in range(len(units))]
        m = [None] * len(units)
        l = [jnp.zeros((1, FLASH_TILE), F32) for _ in units]
        acc = [jnp.zeros((v_dim, FLASH_TILE), F32) for _ in units]
        for (st, sz) in chunks:
            for u, (sub, a) in enumerate(units):
                s_t = _dot(k_ref[0, k_sel[a], st:st + sz, :], q[u])
                s_scr[slot, u, st:st + sz, :] = s_t
                cm = jnp.max(s_t, axis=0, keepdims=True)
                m[u] = cm if m[u] is None else jnp.maximum(m[u], cm)
                p = jnp.exp2(s_scr[prev, u, st:st + sz, :] - m_prev[u])
                l[u] = l[u] + jnp.sum(p, axis=0, keepdims=True)
                acc[u] = acc[u] + _dot(vt_ref[0, v_off[a]:v_off[a] + v_dim, st:st + sz], p.astype(BF16))
        for u in range(len(units)):
            m_scr[slot, u] = m[u]
        for sub in range(n_sub):
            toks = slice(sub * FLASH_TILE, (sub + 1) * FLASH_TILE)
            _gated_store([acc[2 * sub + a] / l[2 * sub + a] for a in (0, 1)], g_ref.at[toks, :], o_ref.at[toks, :])

    for parity in (0, 1):
        pl.when(j % 2 == parity)(functools.partial(step, parity, 1 - parity))


def _flash_ctx_body(q_ref, k_ref, vt_ref, g_ref, o_ref, s_scr, *, k_sel, v_off, v_dim, chunks):
    outs = []
    for a in range(2):
        m = _score_pass(q_ref[0, a], k_ref, k_sel[a], chunks, s_scr.at[a])
        outs.append(_value_pass(s_scr.at[a], m, vt_ref, slice(v_off[a], v_off[a] + v_dim), chunks))
    _gated_store(outs, g_ref, o_ref)


def _flash(q, k, vt, gate, batch, seq, ctx_len, k_heads_per_pair, v_rows_per_pair, pairs_per_kv, ctx_queries):
    heads = q.shape[1]
    t_len = seq + ctx_len
    nj = seq // TOKEN_TILE
    v_dim = LANES // 2
    k_sel = (0, 1) if k_heads_per_pair == 2 else (0, 0)
    v_off = (0, v_dim) if v_rows_per_pair == 2 * v_dim else (0, 0)
    k_blocks = k.shape[1] // k_heads_per_pair
    k_block = lambda p: (p // pairs_per_kv) % k_blocks
    chunks = tuple((c * KEY_CHUNK, KEY_CHUNK) for c in range(t_len // KEY_CHUNK))
    static = dict(k_sel=k_sel, v_off=v_off, v_dim=v_dim)

    width = gate.shape[1]
    tq = FLASH_TILE * FLASH_STEP_TILES
    n_q = seq // tq
    n_units = 2 * FLASH_STEP_TILES
    pairs = heads // 2
    n_tiles = batch * pairs * n_q

    def unravel(g):
        return g // (pairs * n_q), (g // n_q) % pairs, g % n_q

    scored = lambda g: unravel(jnp.minimum(g, n_tiles - 1))
    finished = lambda g: unravel(jnp.maximum(g - 1, 0))

    def q_map(g):
        b, p, j = scored(g)
        return b, p, 0, j

    def k_map(g):
        b, p, _ = scored(g)
        return b, k_block(p), 0, 0

    def v_map(g):
        b, p, _ = finished(g)
        return b, p // pairs_per_kv, 0

    def done(g):
        b, p, j = finished(g)
        return b * n_q + j, p

    out = pl.pallas_call(
        functools.partial(_flash_body, chunks=chunks, **static),
        out_shape=jax.ShapeDtypeStruct((batch * seq, width), BF16),
        grid=(n_tiles + 1,),
        in_specs=[pl.BlockSpec((1, 2, LANES, tq), q_map),
                  pl.BlockSpec((1, k_heads_per_pair, t_len, LANES), k_map),
                  pl.BlockSpec((1, v_rows_per_pair, t_len), v_map),
                  pl.BlockSpec((tq, LANES), done)],
        out_specs=pl.BlockSpec((tq, LANES), done),
        scratch_shapes=[pltpu.VMEM((2, n_units, t_len, FLASH_TILE), F32),
                        pltpu.VMEM((2, n_units, 1, FLASH_TILE), F32)],
        compiler_params=_params(1), name="flash_attention",
    )(q, k, vt, gate)
    if not ctx_queries:
        return (out,)

    out_ctx = pl.pallas_call(
        functools.partial(_flash_ctx_body, chunks=((0, ctx_len),), **static),
        out_shape=jax.ShapeDtypeStruct((batch * ctx_len, width), BF16),
        grid=(batch, heads // 2),
        in_specs=[pl.BlockSpec((1, 2, LANES, TOKEN_TILE), lambda b, p: (b, p, 0, nj)),
                  pl.BlockSpec((1, k_heads_per_pair, ctx_len, LANES), lambda b, p: (b, k_block(p), seq // ctx_len, 0)),
                  pl.BlockSpec((1, v_rows_per_pair, ctx_len), lambda b, p: (b, p // pairs_per_kv, seq // ctx_len)),
                  pl.BlockSpec((TOKEN_TILE, LANES), lambda b, p: (batch * nj + b, p))],
        out_specs=pl.BlockSpec((TOKEN_TILE, LANES), lambda b, p: (b, p)),
        scratch_shapes=[pltpu.VMEM((2, ctx_len, TOKEN_TILE), F32)],
        compiler_params=_params(2), name="flash_attention_ctx",
    )(q, k, vt, gate)
    return out, out_ctx


NA_KEY_ROWS = 12
NA_STEP_ROWS = TOKEN_TILE // GRID_W
NA_VARIANTS = 3


def _na_rel_row(variant, rr, a):
    if variant == 0:
        valid, dr = a < WIN_R, a - rr
    elif variant == 1:
        dr = a - WIN_R // 2 - rr
        valid = -(WIN_R // 2) <= dr < WIN_R // 2
    else:
        valid, dr = a >= NA_KEY_ROWS - WIN_R, a - (NA_KEY_ROWS - NA_STEP_ROWS) - rr
    return dr if valid else None


def _na_bias_body(rpb_ref, o_ref, blk_scr):
    p = pl.program_id(0)
    shape = (GRID_W, LANES)
    kc = lax.broadcasted_iota(jnp.int32, shape, 0)
    lane = lax.broadcasted_iota(jnp.int32, shape, 1)
    qc = lane % GRID_W
    upper = lane >= GRID_W
    rel = kc - qc + (WIN_C - 1)
    c0 = jnp.clip(qc - WIN_C // 2, 0, GRID_W - WIN_C)
    col_ok = (kc >= c0) & (kc < c0 + WIN_C)
    n_rel_r, n_rel_c = 2 * WIN_R - 1, 2 * WIN_C - 1

    def block(dd, carry):
        base0 = (2 * p) * (n_rel_r * n_rel_c) + dd * n_rel_c
        base1 = base0 + n_rel_r * n_rel_c
        val = jnp.zeros(shape, F32)
        for jj in range(n_rel_c):
            val = jnp.where(rel == jj, jnp.where(upper, rpb_ref[base1 + jj], rpb_ref[base0 + jj]), val)
        blk_scr[dd] = jnp.where(col_ok, val * LOG2E, NEG)
        return carry

    lax.fori_loop(0, n_rel_r, block, 0)
    outside = jnp.full(shape, NEG, F32)
    for variant in range(NA_VARIANTS):
        for rr in range(NA_STEP_ROWS):
            for a in range(NA_KEY_ROWS):
                dr = _na_rel_row(variant, rr, a)
                o_ref[0, variant, rr, a * GRID_W:(a + 1) * GRID_W, :] = (
                    outside if dr is None else blk_scr[dr + WIN_R - 1])


def _na_bias(rpb):
    heads = rpb.shape[0]
    tab = (NA_VARIANTS, NA_STEP_ROWS, NA_KEY_ROWS * GRID_W, LANES)
    return pl.pallas_call(
        _na_bias_body,
        out_shape=jax.ShapeDtypeStruct((heads // 2,) + tab, F32),
        grid=(heads // 2,),
        in_specs=[pl.BlockSpec(memory_space=pltpu.SMEM)],
        out_specs=pl.BlockSpec((1,) + tab, lambda p: (p, 0, 0, 0, 0)),
        scratch_shapes=[pltpu.VMEM((2 * WIN_R - 1, GRID_W, LANES), F32)],
        compiler_params=_params(1), name="na_bias",
    )(rpb.reshape(-1))


NA_CHUNK = 256
NA_STEP_BLOCKS = 8


def _na_body(q_ref, kl_ref, kc_ref, vt_ref, bias_ref, g_ref, o_ref, s_scr, m_scr, *, seq, ctx_len, n_steps, n_total):
    g = pl.program_id(0)
    j_scored = jnp.minimum(g, n_total - 1) % n_steps
    j_finished = jnp.maximum(g - 1, 0) % n_steps
    n_rows = seq // GRID_W
    n_loc = NA_KEY_ROWS * GRID_W // NA_CHUNK
    cols = NA_STEP_ROWS * LANES
    lower = lax.broadcasted_iota(jnp.int32, (GRID_W, LANES), 1) < GRID_W

    @pl.when(g == 0)
    def _():
        s_scr[...] = jnp.zeros(s_scr.shape, F32)
        m_scr[...] = jnp.zeros(m_scr.shape, F32)

    def span_start(step):
        first_row = jnp.clip(NA_STEP_ROWS * step - WIN_R // 2, 0, n_rows - NA_KEY_ROWS)
        return first_row * GRID_W

    def step(slot, prev):
        subs = range(NA_STEP_BLOCKS)
        variant, k0, v0, q2 = [], [], [], []
        for sub in subs:
            blk = NA_STEP_BLOCKS * j_scored + sub
            blk_prev = NA_STEP_BLOCKS * j_finished + sub
            variant.append(jnp.where(blk == 0, 0, jnp.where(blk == NA_STEP_BLOCKS * n_steps - 1, 2, 1)))
            k0.append(span_start(blk))
            v0.append(span_start(blk_prev))
            q = q_ref[sub * TOKEN_TILE:(sub + 1) * TOKEN_TILE, :]
            zero = jnp.zeros((GRID_W, LANES), q.dtype)
            parts = []
            for rr in range(NA_STEP_ROWS):
                q_r = q[rr * GRID_W:(rr + 1) * GRID_W]
                parts += [jnp.where(lower, q_r, zero), jnp.where(lower, zero, q_r)]
            q2.append(jnp.concatenate(parts, axis=0))
        m_prev = [m_scr[prev, sub] for sub in subs]
        m = [None for _ in subs]
        l = [jnp.zeros((1, cols), F32) for _ in subs]
        acc = [jnp.zeros((LANES, cols), F32) for _ in subs]
        for c in range(n_loc + ctx_len // NA_CHUNK):
            rows = slice((c % n_loc) * NA_CHUNK, (c % n_loc + 1) * NA_CHUNK)
            for sub in subs:
                if c < n_loc:
                    k_at = pl.multiple_of(k0[sub] + c * NA_CHUNK, NA_CHUNK)
                    v_at = pl.multiple_of(v0[sub] + c * NA_CHUNK, NA_CHUNK)
                    s = _dot_nt(kl_ref[pl.ds(k_at, NA_CHUNK), :], q2[sub])
                    s = jnp.concatenate([s[:, rr * LANES:(rr + 1) * LANES] + bias_ref[0, variant[sub], rr, rows, :]
                                         for rr in range(NA_STEP_ROWS)], axis=1)
                    v_chunk = vt_ref[0, :, pl.ds(v_at, NA_CHUNK)]
                else:
                    s = _dot_nt(kc_ref[rows, :], q2[sub])
                    v_chunk = vt_ref[0, :, seq + rows.start:seq + rows.stop]
                s_scr[slot, sub, c] = s
                cm = jnp.max(s, axis=0, keepdims=True)
                m[sub] = cm if m[sub] is None else jnp.maximum(m[sub], cm)
                p = jnp.exp2(s_scr[prev, sub, c] - m_prev[sub])
                l[sub] = l[sub] + jnp.sum(p, axis=0, keepdims=True)
                acc[sub] = acc[sub] + _dot(v_chunk, p.astype(BF16))
        for sub in subs:
            m_scr[slot, sub] = m[sub]
            r_n = (acc[sub] / l[sub]).T
            outs = [jnp.where(lower, r_n[rr * LANES:rr * LANES + GRID_W], r_n[rr * LANES + GRID_W:(rr + 1) * LANES])
                    for rr in range(NA_STEP_ROWS)]
            toks = slice(sub * TOKEN_TILE, (sub + 1) * TOKEN_TILE)
            o_ref[toks, :] = (jnp.concatenate(outs, axis=0) * _silu(g_ref[toks, :].astype(F32))).astype(o_ref.dtype)

    for parity in (0, 1):
        pl.when(g % 2 == parity)(functools.partial(step, parity, 1 - parity))


def _na_attention(q, k, vt, bias, gate, batch, seq, ctx_len):
    pairs = q.shape[1] // LANES
    t_len = seq + ctx_len
    step_tokens = NA_STEP_BLOCKS * TOKEN_TILE
    assert seq % step_tokens == 0
    n_steps = seq // step_tokens
    n_chunks = (NA_KEY_ROWS * GRID_W + ctx_len) // NA_CHUNK
    cols = NA_STEP_ROWS * LANES
    n_total = batch * pairs * n_steps

    def unravel(g):
        return g // (pairs * n_steps), (g // n_steps) % pairs, g % n_steps

    scored = lambda g: unravel(jnp.minimum(g, n_total - 1))
    finished = lambda g: unravel(jnp.maximum(g - 1, 0))

    def q_map(g):
        b, p, j = scored(g)
        return b * n_steps + j, p

    def k_map(g):
        b, p, _ = scored(g)
        return b, p

    def k_ctx_map(g):
        b, p, _ = scored(g)
        return batch * seq // ctx_len + b, p

    def v_map(g):
        b, p, _ = finished(g)
        return b, p, 0

    def done(g):
        b, p, j = finished(g)
        return b * n_steps + j, p

    body = functools.partial(_na_body, seq=seq, ctx_len=ctx_len, n_steps=n_steps, n_total=n_total)
    return pl.pallas_call(
        body, out_shape=jax.ShapeDtypeStruct((batch * seq, q.shape[1]), BF16),
        grid=(n_total + 1,),
        in_specs=[pl.BlockSpec((step_tokens, LANES), q_map),
                  pl.BlockSpec((seq, LANES), k_map),
                  pl.BlockSpec((ctx_len, LANES), k_ctx_map),
                  pl.BlockSpec((1, LANES, t_len), v_map),
                  pl.BlockSpec((1,) + bias.shape[1:], lambda g: (scored(g)[1], 0, 0, 0, 0)),
                  pl.BlockSpec((step_tokens, LANES), done)],
        out_specs=pl.BlockSpec((step_tokens, LANES), done),
        scratch_shapes=[pltpu.VMEM((2, NA_STEP_BLOCKS, n_chunks, NA_CHUNK, cols), F32),
                        pltpu.VMEM((2, NA_STEP_BLOCKS, 1, cols), F32)],
        compiler_params=_params(1), name="neighborhood_attention",
    )(q, k, k, vt, bias, gate)


def _seg_scans(jobs, use_max=False):
    vals = [v for v, _ in jobs]
    n = vals[0].shape[1]
    lane = lax.broadcasted_iota(jnp.int32, vals[0].shape, 1) % ML_CHUNK
    k = 1
    while k < ML_CHUNK:
        for i, (_, reverse) in enumerate(jobs):
            v = vals[i]
            if reverse:
                ok, shifted = lane < ML_CHUNK - k, pltpu.roll(v, n - k, 1)
            else:
                ok, shifted = lane >= k, pltpu.roll(v, k, 1)
            vals[i] = jnp.maximum(v, jnp.where(ok, shifted, NEG)) if use_max else v + jnp.where(ok, shifted, 0.0)
        k *= 2
    return vals


ML_GATE_ROWS = 40


def _mlstm_prep_body(u_ref, up_ref, un_ref, cw_ref, cb_ref, wqk_ref, wv_ref, wg_ref, bg_ref,
                     xc_ref, k_ref, qt_ref, vt_ref, pre_ref, *, nj, width, kscale):
    j = pl.program_id(1)
    u = u_ref[...]
    row = lax.broadcasted_iota(jnp.int32, u.shape, 0)
    prev = jnp.where((j > 0) & (j < nj), up_ref[7:8, :], 0.0)
    nxt = jnp.where(j < nj - 1, un_ref[0:1, :], 0.0)
    u_m1 = jnp.where(row == 0, prev, pltpu.roll(u, 1, 0))
    u_p1 = jnp.where(row == TOKEN_TILE - 1, nxt, pltpu.roll(u, TOKEN_TILE - 1, 0))
    cw = cw_ref[...]
    xc = _silu(u_m1 * cw[0:1] + u * cw[1:2] + u_p1 * cw[2:3] + cb_ref[...])
    xcb = xc.astype(BF16)
    xc_ref[...] = xcb
    qk = _dot(xcb, wqk_ref[...])
    v = _dot(u.astype(BF16), wv_ref[...])
    qb, kb, vb = qk[:, :width].astype(BF16), qk[:, width:].astype(BF16), v.astype(BF16)
    k_ref[...] = (qk[:, width:] * kscale).astype(BF16)
    qt_ref[0] = qk[:, :width].T.astype(BF16)
    vt_ref[0] = v.T.astype(BF16)
    pre_ref[0] = _dot_nt(wg_ref[0], qb) + _dot_nt(wg_ref[1], kb) + _dot_nt(wg_ref[2], vb) + bg_ref[...]


def _mlstm_gates_body(pre_ref, g_ref, gc_ref):
    pre = pre_ref[0]
    i8 = [pre[16 * d:16 * d + 8] for d in range(2)]
    f8 = [_log_sigmoid(pre[16 * d + 8:16 * d + 16]) for d in range(2)]
    b0, b0_rev, b1, b1_rev = _seg_scans([(f8[0], False), (f8[0], True), (f8[1], True), (f8[1], False)])
    b8, b_last = [b0, b1], [b0 + b0_rev - f8[0], b1 + b1_rev - f8[1]]
    r8 = [i8[d] - b8[d] for d in range(2)]
    c0, c0_rev, c1, c1_rev = _seg_scans([(r8[0], False), (r8[0], True), (r8[1], True), (r8[1], False)], use_max=True)
    c8, r_max = [c0, c1], [jnp.maximum(c0, c0_rev), jnp.maximum(c1, c1_rev)]
    for d in range(2):
        g_ref[0, d * ML_GATE_ROWS:(d + 1) * ML_GATE_ROWS] = jnp.concatenate(
            [-c8[d], b8[d] + c8[d], jnp.exp(r8[d] - r_max[d]), b_last[d], b_last[d] + r_max[d]], axis=0)
    pad = jnp.zeros((LANES - 16, pre.shape[1]), F32)
    gc_ref[0] = jnp.concatenate(r8 + [pad], axis=0).T


def _mlstm_gates(pre):
    batch, _, t_len = pre.shape
    return pl.pallas_call(
        _mlstm_gates_body,
        out_shape=[jax.ShapeDtypeStruct((batch, 2 * ML_GATE_ROWS, t_len), F32),
                   jax.ShapeDtypeStruct((batch, t_len, LANES), F32)],
        grid=(batch,),
        in_specs=[pl.BlockSpec((1,) + pre.shape[1:], lambda b: (b, 0, 0))],
        out_specs=[pl.BlockSpec((1, 2 * ML_GATE_ROWS, t_len), lambda b: (b, 0, 0)),
                   pl.BlockSpec((1, t_len, LANES), lambda b: (b, 0, 0))],
        compiler_params=_params(1), name="mlstm_gates",
    )(pre)


def _mlstm_prep(u, conv_w, conv_b, w_q, w_k, w_v, w_gate, b_gate, batch, seq, ctx_len):
    heads, hd = w_q.shape[0], w_q.shape[1]
    assert heads == ML_HEADS and hd == ML_CHUNK
    width = heads * hd
    nj = seq // TOKEN_TILE
    t_len = seq + ctx_len
    rows = u.shape[0]

    def block_diag(w):
        eye = jnp.eye(heads, dtype=w.dtype)
        return (eye[:, None, :, None] * w[:, :, None, :]).reshape(width, width)

    wqk = jnp.concatenate([block_diag(w_q), block_diag(w_k)], axis=1).astype(BF16)
    wv = block_diag(w_v).astype(BF16)
    wg = w_gate.reshape(2, heads, 3, hd, 2, heads).transpose(2, 0, 4, 5, 1, 3).reshape(3, 2, 2, heads, width)
    wg = jnp.pad(wg, ((0, 0), (0, 0), (0, 0), (0, 8 - heads), (0, 0))).reshape(3, 32, width).astype(BF16)
    bg = jnp.pad(b_gate.reshape(2, 2, heads), ((0, 0), (0, 0), (0, 8 - heads))).reshape(32, 1)
    n_halo = rows // 8
    tokb = lambda b, j: _tok_block(b, j, nj, batch)
    tile = pl.BlockSpec((TOKEN_TILE, width), lambda b, j: (tokb(b, j), 0))
    feat = pl.BlockSpec((1, width, TOKEN_TILE), lambda b, j: (b, 0, j))
    body = functools.partial(_mlstm_prep_body, nj=nj, width=width, kscale=hd ** -0.5)
    per_tile = TOKEN_TILE // 8
    return pl.pallas_call(
        body,
        out_shape=[jax.ShapeDtypeStruct((rows, width), BF16)] * 2
        + [jax.ShapeDtypeStruct((batch, width, t_len), BF16)] * 2
        + [jax.ShapeDtypeStruct((batch, 32, t_len), F32)],
        grid=(batch, nj + 1),
        in_specs=[tile,
                  pl.BlockSpec((8, width), lambda b, j: (jnp.maximum(tokb(b, j) * per_tile - 1, 0), 0)),
                  pl.BlockSpec((8, width), lambda b, j: (jnp.minimum((tokb(b, j) + 1) * per_tile, n_halo - 1), 0)),
                  _full((3, width)), _full((1, width)), _full(wqk.shape), _full(wv.shape),
                  _full(wg.shape), _full((32, 1))],
        out_specs=[tile, tile, feat, feat, pl.BlockSpec((1, 32, TOKEN_TILE), lambda b, j: (b, 0, j))],
        compiler_params=_params(2), name="mlstm_prep",
    )(u, u, u, conv_w, conv_b.reshape(1, width), wqk, wv, wg, bg)


def _mlstm_seq_body(kf, qtf, vtf, gf, gcf, kb, qtb, vtb, gb, gcb, hf_ref, hb_ref, c_s, n_s, m_s):
    t = pl.program_id(1)
    L = ML_CHUNK
    subs = TOKEN_TILE // L

    @pl.when(t == 0)
    def _():
        c_s[...] = jnp.zeros(c_s.shape, F32)
        n_s[...] = jnp.zeros(n_s.shape, F32)
        m_s[...] = jnp.zeros(m_s.shape, F32)

    si = lax.broadcasted_iota(jnp.int32, (L, L), 0)
    li = lax.broadcasted_iota(jnp.int32, (L, L), 1)
    streams = ((kf, qtf, vtf, gf, gcf, hf_ref, si <= li), (kb, qtb, vtb, gb, gcb, hb_ref, si >= li))
    units = {}
    for d, (k_ref, qt_ref, vt_ref, g_ref, gc_ref, h_ref, incl) in enumerate(streams):
        for sub in range(subs):
            toks = slice(sub * L, (sub + 1) * L)
            g = g_ref[0, :, toks]
            gc = gc_ref[0, toks, :]
            for h in range(ML_HEADS):
                cols = slice(h * L, (h + 1) * L)
                k, qt, vt = k_ref[toks, cols], qt_ref[0, cols, toks], vt_ref[0, cols, toks]
                neg_c, m_loc, w0, b_last, g_max = (g[8 * i + h:8 * i + h + 1] for i in range(5))
                r_col = gc[:, 8 * d + h:8 * d + h + 1]
                p0 = jnp.where(incl, jnp.exp(r_col + neg_c), 0.0) * _dot(k, qt)
                s_sum = jnp.sum(p0, axis=0, keepdims=True)
                intra = _dot(vt, p0.astype(BF16))
                c_inc = _dot((vt.astype(F32) * w0).astype(BF16), k)
                n_inc = _dot(jnp.broadcast_to(w0, (8, L)).astype(BF16), k)
                units[d, sub, h] = (h_ref, cols, toks, qt, neg_c, m_loc, b_last, g_max, s_sum, intra, c_inc, n_inc)
    for stage in range(subs):
        for d in range(2):
            sub = stage if d == 0 else subs - 1 - stage
            for h in range(ML_HEADS):
                idx = d * ML_HEADS + h
                h_ref, cols, toks, qt, neg_c, m_loc, b_last, g_max, s_sum, intra, c_inc, n_inc = units[d, sub, h]
                c_st, n_st, m_st = c_s[idx], n_s[idx], m_s[idx]
                cn = _dot(jnp.concatenate([c_st, n_st], axis=0).astype(BF16), qt)
                delta = jnp.maximum(m_st + neg_c, 0.0)
                e_intra = jnp.exp(-delta)
                w_inter = jnp.exp(m_st + neg_c - delta)
                num = w_inter * cn[:L] + e_intra * intra
                den = w_inter * cn[L:L + 1] + e_intra * s_sum
                h_out = num / jnp.maximum(jnp.abs(den), jnp.exp(-(m_loc + delta)))
                h_ref[0, cols, toks] = h_out.astype(h_ref.dtype)
                m_new = jnp.maximum(b_last + m_st, g_max)
                decay = jnp.exp(b_last + m_st - m_new)
                gain = jnp.exp(g_max - m_new)
                c_s[idx] = decay * c_st + gain * c_inc
                n_s[idx] = decay * n_st + gain * n_inc
                m_s[idx] = m_new


def _mlstm_seq(k, qt, vt, gates, gcols, batch, seq, ctx_len):
    width = k.shape[1]
    L = ML_CHUNK
    assert ctx_len == TOKEN_TILE
    nj = seq // TOKEN_TILE
    fwd = lambda t: (t + nj) % (nj + 1)
    bwd = lambda t: nj - t

    def stream(tile_of, d):
        feat = pl.BlockSpec((1, width, TOKEN_TILE), lambda b, t: (b, 0, tile_of(t)))
        return [pl.BlockSpec((TOKEN_TILE, width), lambda b, t: (_tok_block(b, tile_of(t), nj, batch), 0)), feat, feat,
                pl.BlockSpec((1, ML_GATE_ROWS, TOKEN_TILE), lambda b, t: (b, d, tile_of(t))),
                pl.BlockSpec((1, TOKEN_TILE, LANES), lambda b, t: (b, tile_of(t), 0))]

    out_f = pl.BlockSpec((1, width, TOKEN_TILE), lambda b, t: (b, 0, fwd(t)))
    out_b = pl.BlockSpec((1, width, TOKEN_TILE), lambda b, t: (b, 0, bwd(t)))
    n_state = 2 * ML_HEADS
    return pl.pallas_call(
        _mlstm_seq_body,
        out_shape=[jax.ShapeDtypeStruct(qt.shape, BF16)] * 2,
        grid=(batch, nj + 1),
        in_specs=stream(fwd, 0) + stream(bwd, 1),
        out_specs=[out_f, out_b],
        scratch_shapes=[pltpu.VMEM((n_state, L, L), F32), pltpu.VMEM((n_state, 8, L), F32),
                        pltpu.VMEM((n_state, 1, L), F32)],
        compiler_params=_params(2), name="mlstm_recurrence",
    )(k, qt, vt, gates, gcols, k, qt, vt, gates, gcols)


def kernel(x, c, ctx, c_ctx, w_mod, b_mod, g_norm, ab_w_in, ab_w_out, mla_g_q, mla_w_uq, mla_g_kv, mla_w_ukv,
           ml_conv_w, ml_conv_b, ml_w_q, ml_w_k, ml_w_v, ml_w_gate, ml_b_gate, ml_g_head, ml_skip,
           cd_w_in, cd_w_out, na_rpb, gqa_g_q, gqa_g_k, g_final):
    batch, seq, d = x.shape
    ctx_len = ctx.shape[1]
    assert ctx_len == TOKEN_TILE and seq % KEY_CHUNK == 0 and seq // GRID_W >= NA_KEY_ROWS
    dims = (batch, seq, ctx_len)

    mla_heads, mla_rope, mla_v = 8, 32, 64
    mla_nope = mla_w_uq.shape[2] // mla_heads - mla_rope
    q_lora, kv_lora = mla_g_q.shape[1], mla_g_kv.shape[1]
    ml_width = ml_conv_w.shape[2]
    mla_width = mla_heads * mla_v
    gqa_heads, gqa_dim = 8, gqa_g_q.shape[1]
    gqa_kv = (cd_w_in.shape[2] - 4 * 512 - 2 * gqa_heads * gqa_dim) // (2 * gqa_dim)
    na_width = na_rpb.shape[1] * 64

    mod_rows = -(-(batch + 1) // 8) * 8
    cvec = jnp.concatenate([c, c_ctx[None], jnp.zeros((mod_rows - batch - 1, d), F32)], axis=0)
    mod = _modulation(cvec, w_mod, b_mod)
    mod0 = mod[0].reshape(mod_rows, 1, 3 * d)
    mod1 = mod[1].reshape(mod_rows, 1, 3 * d)

    tok0 = _Tokens((x.reshape(batch * seq, d), ctx.reshape(batch * ctx_len, d)), *dims)
    w_in = ab_w_in[0]
    s1 = q_lora + kv_lora
    zcol = lambda n: jnp.zeros((d, n), w_in.dtype)
    w0 = jnp.concatenate([w_in[:, :s1], zcol(mla_nope), w_in[:, s1:s1 + mla_rope],
                          zcol(LANES - mla_nope - mla_rope), w_in[:, s1 + mla_rope:]], axis=1).astype(BF16)
    o_pa = s1 + LANES
    outs0 = ((0, o_pa, F32, 1.0, False), (o_pa, mla_width, BF16, 1.0, False),
             (o_pa + mla_width, ml_width, F32, 1.0, False), (o_pa + mla_width + ml_width, ml_width, BF16, 1.0, False))
    pa, gate_a, u, z = _in_proj(tok0, mod0, g_norm[0], w0, outs0, *dims)

    def mla_lanes(table, fill):
        n = table.shape[0]
        return jnp.concatenate([jnp.full((n, mla_nope), fill, F32), table,
                                jnp.full((n, LANES - mla_nope - mla_rope), fill, F32)], axis=-1)

    cos_a, sin_a = _rope_tables(seq, ctx_len, mla_rope)
    cos_a, sin_a = mla_lanes(cos_a, 1.0), mla_lanes(sin_a, 0.0)
    q_a, k_a, vt_a = _mla_prep(pa, cos_a, sin_a, mla_g_q[0], mla_g_kv[0], mla_w_uq[0], mla_w_ukv[0],
                               *dims, mla_heads, mla_nope, mla_rope, mla_v)
    mix_a = _flash(q_a, k_a, vt_a, gate_a, *dims, k_heads_per_pair=2, v_rows_per_pair=2 * mla_v,
                   pairs_per_kv=1, ctx_queries=True)

    xc, k_m, qt_m, vt_m, gate_pre = _mlstm_prep(u, ml_conv_w[0], ml_conv_b[0], ml_w_q[0], ml_w_k[0], ml_w_v[0],
                                                ml_w_gate[0], ml_b_gate[0], *dims)
    gates, gcols = _mlstm_gates(gate_pre)
    h_f, h_b = _mlstm_seq(k_m, qt_m, vt_m, gates, gcols, *dims)
    x1 = _out_proj(tok0, _Tokens(mix_a, *dims), (h_f, h_b, xc, z, ml_g_head[0], ml_skip[0]),
                   ab_w_out[0], mod0, batch, seq)

    tok1 = _Tokens((x1,), *dims)
    w1 = cd_w_in[0].astype(BF16)
    gq_w, gkv_w = gqa_heads * gqa_dim, gqa_kv * gqa_dim
    o_d = 4 * na_width
    outs1 = ((0, na_width, BF16, 64 ** -0.5 * LOG2E, False), (na_width, na_width, BF16, 1.0, False),
             (2 * na_width, na_width, BF16, 1.0, True), (3 * na_width, na_width, BF16, 1.0, False),
             (o_d, gq_w + gkv_w, F32, 1.0, True), (o_d + gq_w + gkv_w, gkv_w, BF16, 1.0, True),
             (o_d + gq_w + 2 * gkv_w, gq_w, BF16, 1.0, False))
    q_c, k_c, vt_c, gate_c, pd_t, vt_d, gate_d = _in_proj(tok1, mod1, g_norm[1], w1, outs1, *dims)

    mix_c = _na_attention(q_c, k_c, vt_c, _na_bias(na_rpb[0]), gate_c, *dims)

    cos_d, sin_d = _rope_tables(seq, ctx_len, gqa_dim)
    q_d, k_d = _gqa_prep(pd_t, cos_d, sin_d, gqa_g_q[0], gqa_g_k[0], *dims, gqa_heads, gqa_kv, gqa_dim)
    mix_d = _flash(q_d, k_d, vt_d, gate_d, *dims, k_heads_per_pair=1, v_rows_per_pair=gqa_dim,
                   pairs_per_kv=gqa_heads // (2 * gqa_kv), ctx_queries=False)

    out = _out_proj(tok1, _Tokens((mix_c,), *dims), mix_d[0], cd_w_out[0], mod1, batch, seq, g_final=g_final)
    return out.reshape(batch, seq, d)
```
